```python
import jax
import jax.numpy as jnp
from jax import lax
import numpy as np

D_MODEL = 1024
BATCH = 1
SEQ = 16384
DEPTH = 4
DEC_BATCH = 16
DEC_SEQ = 32
PAST_LEN = 1024

CHUNK = 64
N_MIXERS = 3
N_A = (DEPTH + 2) // 3
N_B = (DEPTH + 1) // 3
N_C = DEPTH // 3

A_HEADS = 8
A_Q_LORA = 384
A_KV_LORA = 256
A_NOPE = 128
A_ROPE = 64
A_V = 128
ROPE_THETA = 10000.0

B_HEADS = 16
B_HEAD_DIM = D_MODEL // B_HEADS
B_LEFT_CHUNKS = 8
B_WIN = B_LEFT_CHUNKS * CHUNK
B_REL_CLIP = 128

C_HEADS = 16
C_HEAD_DIM = D_MODEL // C_HEADS
C_FORGET_BIAS = 4.0

D_FF = 2816
CONV_W = 3

ALPHA = (2.0 * DEPTH) ** 0.25
BETA = (8.0 * DEPTH) ** -0.25
LN_EPS = 1e-5
RMS_EPS = 1e-6
Q_BLOCK = 128
NEG_INF = -1e30

kernel_name = 'streaming_hybrid_mla_band_fox_convffn'


def layer_norm(x, g, b):
    xf = x.astype(jnp.float32)
    mu = jnp.mean(xf, axis=-1, keepdims=True)
    var = jnp.mean(jnp.square(xf - mu), axis=-1, keepdims=True)
    y = (xf - mu) * lax.rsqrt(var + LN_EPS) * g.astype(jnp.float32) + b.astype(jnp.float32)
    return y.astype(x.dtype)


def rms_norm(x, g):
    xf = x.astype(jnp.float32)
    y = xf * lax.rsqrt(jnp.mean(xf * xf, axis=-1, keepdims=True) + RMS_EPS) * g.astype(jnp.float32)
    return y.astype(x.dtype)


def rope(x, pos):
    half = x.shape[-1] // 2
    inv_freq = ROPE_THETA ** (-jnp.arange(half, dtype=jnp.float32) / half)
    ang = pos.astype(jnp.float32)[:, None] * inv_freq
    shape = (pos.shape[0],) + (1,) * (x.ndim - 3) + (half,)
    cos = jnp.cos(ang).reshape(shape)
    sin = jnp.sin(ang).reshape(shape)
    x1 = x[..., :half].astype(jnp.float32)
    x2 = x[..., half:].astype(jnp.float32)
    return jnp.concatenate([x1 * cos - x2 * sin, x2 * cos + x1 * sin], axis=-1).astype(x.dtype)


def blocked_attention(q, k, v, q_pos, k_pos, frame_causal, fq=None, fk=None):
    b, tq = q.shape[:2]
    qb = Q_BLOCK if tq % Q_BLOCK == 0 else tq
    nb = tq // qb
    scale = q.shape[-1] ** -0.5

    def to_blocks(a):
        return a.reshape((b, nb, qb) + a.shape[2:]).swapaxes(0, 1)

    has_forget = fk is not None
    fk_t = fk.transpose(0, 2, 1)[:, :, None, :] if has_forget else None
    k_chunk = k_pos // CHUNK

    def one_block(args):
        q_blk, pos_blk = args[0], args[1]
        s = jnp.einsum('bqhd,bkhd->bhqk', q_blk, k, preferred_element_type=jnp.float32) * scale
        if has_forget:
            s = s + args[2].transpose(0, 2, 1)[..., None] - fk_t
        if frame_causal:
            mask = k_pos[None, :] <= pos_blk[:, None]
        else:
            mask = k_chunk[None, :] <= (pos_blk // CHUNK)[:, None]
        s = jnp.where(mask, s, NEG_INF)
        p = jax.nn.softmax(s, axis=-1).astype(v.dtype)
        return jnp.einsum('bhqk,bkhd->bqhd', p, v)

    xs = (to_blocks(q), q_pos.reshape(nb, qb))
    if has_forget:
        xs = xs + (to_blocks(fq),)
    out = lax.map(one_block, xs)
    return out.swapaxes(0, 1).reshape((b, tq) + out.shape[3:])


def mla_mixer(x, pos, ckv_past, kpe_past, w_dq, g_q, w_uq, w_dkv, g_kv, w_kr, w_uk, w_uv, w_o):
    b, t, _ = x.shape
    q = (rms_norm(x @ w_dq, g_q) @ w_uq).reshape(b, t, A_HEADS, A_NOPE + A_ROPE)
    q = jnp.concatenate([q[..., :A_NOPE], rope(q[..., A_NOPE:], pos)], axis=-1)
    ckv = rms_norm(x @ w_dkv, g_kv)
    kpe = rope(x @ w_kr, pos)
    if ckv_past is None:
        ckv_all, kpe_all, k_pos = ckv, kpe, pos
    else:
        ckv_all = jnp.concatenate([ckv_past.astype(ckv.dtype), ckv], axis=1)
        kpe_all = jnp.concatenate([kpe_past.astype(kpe.dtype), kpe], axis=1)
        k_pos = jnp.concatenate([jnp.arange(PAST_LEN), pos])
    tk = ckv_all.shape[1]
    k = jnp.concatenate([jnp.einsum('bsl,lhn->bshn', ckv_all, w_uk),
                         jnp.broadcast_to(kpe_all[:, :, None, :], (b, tk, A_HEADS, A_ROPE))], axis=-1)
    v = jnp.einsum('bsl,lhd->bshd', ckv_all, w_uv)
    o = blocked_attention(q, k, v, pos, k_pos, False)
    return o.reshape(b, t, A_HEADS * A_V) @ w_o, ckv, kpe


def band_attend(q, k, v, q_pos, k_pos, rel_bias):
    s = jnp.einsum('bqhd,bkhd->bhqk', q, k, preferred_element_type=jnp.float32) * (B_HEAD_DIM ** -0.5)
    rel = jnp.clip(q_pos[:, None] - k_pos[None, :], -B_REL_CLIP, B_REL_CLIP) + B_REL_CLIP
    s = s + rel_bias[:, rel].astype(jnp.float32)
    qc = (q_pos // CHUNK)[:, None]
    kc = (k_pos // CHUNK)[None, :]
    mask = (kc <= qc) & (kc >= qc - B_LEFT_CHUNKS) & (k_pos[None, :] >= 0)
    s = jnp.where(mask, s, NEG_INF)
    p = jax.nn.softmax(s, axis=-1).astype(v.dtype)
    return jnp.einsum('bhqk,bkhd->bqhd', p, v)


def band_mixer(x, pos, k_past, v_past, w_qkv, rel_bias, w_o):
    b, t, _ = x.shape
    qkv = (x @ w_qkv).reshape(b, t, 3, B_HEADS, B_HEAD_DIM)
    q, k, v = qkv[:, :, 0], qkv[:, :, 1], qkv[:, :, 2]
    if k_past is None:
        nc = t // CHUNK
        pad = ((0, 0), (B_WIN, 0), (0, 0), (0, 0))
        kp, vp = jnp.pad(k, pad), jnp.pad(v, pad)
        q_chunks = q.reshape(b, nc, CHUNK, B_HEADS, B_HEAD_DIM).swapaxes(0, 1)

        def one_chunk(args):
            qc, c = args
            start = c * CHUNK
            kc = lax.dynamic_slice_in_dim(kp, start, B_WIN + CHUNK, axis=1)
            vc = lax.dynamic_slice_in_dim(vp, start, B_WIN + CHUNK, axis=1)
            q_pos = start + jnp.arange(CHUNK)
            k_pos = start - B_WIN + jnp.arange(B_WIN + CHUNK)
            return band_attend(qc, kc, vc, q_pos, k_pos, rel_bias)

        o = lax.map(one_chunk, (q_chunks, jnp.arange(nc)))
        o = o.swapaxes(0, 1).reshape(b, t, D_MODEL)
        keep = min(B_WIN, t)
        k_new, v_new = k[:, t - keep:], v[:, t - keep:]
    else:
        p_len = k_past.shape[1]
        k_all = jnp.concatenate([k_past.astype(k.dtype), k], axis=1)
        v_all = jnp.concatenate([v_past.astype(v.dtype), v], axis=1)
        k_pos = jnp.concatenate([jnp.arange(PAST_LEN - p_len, PAST_LEN), pos])
        o = band_attend(q, k_all, v_all, pos, k_pos, rel_bias).reshape(b, t, D_MODEL)
        k_new, v_new = k, v
    return o @ w_o, k_new, v_new


def fox_mixer(x, pos, k_past, v_past, lf_past, w_qkv, w_f, b_f, w_o):
    b, t, _ = x.shape
    qkv = (x @ w_qkv).reshape(b, t, 3, C_HEADS, C_HEAD_DIM)
    q, k, v = qkv[:, :, 0], qkv[:, :, 1], qkv[:, :, 2]
    log_f = jax.nn.log_sigmoid((x @ w_f).astype(jnp.float32) + b_f.astype(jnp.float32))
    if k_past is None:
        k_all, v_all, lf_all, k_pos = k, v, log_f, pos
    else:
        k_all = jnp.concatenate([k_past.astype(k.dtype), k], axis=1)
        v_all = jnp.concatenate([v_past.astype(v.dtype), v], axis=1)
        lf_all = jnp.concatenate([lf_past.astype(jnp.float32), log_f], axis=1)
        k_pos = jnp.concatenate([jnp.arange(PAST_LEN), pos])
    f_cum = jnp.cumsum(lf_all, axis=1)
    o = blocked_attention(q, k_all, v_all, pos, k_pos, True, f_cum[:, -t:], f_cum)
    return o.reshape(b, t, D_MODEL) @ w_o, k, v, log_f


def conv_ffn(x, conv_past, w_up, conv_w, conv_b, w_down):
    t = x.shape[1]
    h = x @ w_up
    if conv_past is None:
        hp = jnp.pad(h, ((0, 0), (CONV_W - 1, 0), (0, 0)))
    else:
        hp = jnp.concatenate([conv_past.astype(h.dtype), h], axis=1)
    hc = conv_b + sum(hp[:, j:j + t] * conv_w[j] for j in range(CONV_W))
    a, g = hc[..., :D_FF], hc[..., D_FF:]
    return (jax.nn.silu(g) * a) @ w_down, hp[:, -(CONV_W - 1):]


def trunk(x, pos, past, w):
    outs = {n: [] for n in ('a_ckv', 'a_kpe', 'b_k', 'b_v', 'c_k', 'c_v', 'c_logf', 'ffn_conv')}
    ia = ib = ic = 0
    for i in range(DEPTH):
        kind = i % N_MIXERS
        if kind == 0:
            h, ckv, kpe = mla_mixer(
                x, pos,
                None if past is None else past['a_ckv'][ia],
                None if past is None else past['a_kpe'][ia],
                w['a_w_dq'][ia], w['a_g_q'][ia], w['a_w_uq'][ia], w['a_w_dkv'][ia], w['a_g_kv'][ia],
                w['a_w_kr'][ia], w['a_w_uk'][ia], w['a_w_uv'][ia], w['a_w_o'][ia])
            outs['a_ckv'].append(ckv)
            outs['a_kpe'].append(kpe)
            ia += 1
        elif kind == 1:
            h, kb, vb = band_mixer(
                x, pos,
                None if past is None else past['b_k'][ib],
                None if past is None else past['b_v'][ib],
                w['b_w_qkv'][ib], w['b_rel_bias'][ib], w['b_w_o'][ib])
            outs['b_k'].append(kb)
            outs['b_v'].append(vb)
            ib += 1
        else:
            h, kc, vc, lf = fox_mixer(
                x, pos,
                None if past is None else past['c_k'][ic],
                None if past is None else past['c_v'][ic],
                None if past is None else past['c_logf'][ic],
                w['c_w_qkv'][ic], w['c_w_f'][ic], w['c_b_f'][ic], w['c_w_o'][ic])
            outs['c_k'].append(kc)
            outs['c_v'].append(vc)
            outs['c_logf'].append(lf)
            ic += 1
        x = layer_norm(ALPHA * x + h, w['ln1_g'][i], w['ln1_b'][i])
        f, conv_state = conv_ffn(x, None if past is None else past['ffn_conv'][i],
                                 w['f_w_up'][i], w['f_conv_w'][i], w['f_conv_b'][i], w['f_w_down'][i])
        outs['ffn_conv'].append(conv_state)
        x = layer_norm(ALPHA * x + f, w['ln2_g'][i], w['ln2_b'][i])
    return x, {n: jnp.stack(v) for n, v in outs.items()}


def setup_inputs(seed: int = 0) -> dict:
    key = jax.random.key(seed)
    keys = iter(jax.random.split(key, 40))

    def nrm(shape, scale=1.0):
        return scale * jax.random.normal(next(keys), shape, jnp.float32)

    d = D_MODEL
    bc = min(B_WIN, PAST_LEN)
    return {
        'x_prompt': nrm((BATCH, SEQ, d)),
        'x_sample': nrm((DEC_BATCH, DEC_SEQ, d)),
        'cache_a_ckv': nrm((N_A, DEC_BATCH, PAST_LEN, A_KV_LORA)),
        'cache_a_kpe': nrm((N_A, DEC_BATCH, PAST_LEN, A_ROPE)),
        'cache_b_k': nrm((N_B, DEC_BATCH, bc, B_HEADS, B_HEAD_DIM)),
        'cache_b_v': nrm((N_B, DEC_BATCH, bc, B_HEADS, B_HEAD_DIM)),
        'cache_c_k': nrm((N_C, DEC_BATCH, PAST_LEN, C_HEADS, C_HEAD_DIM)),
        'cache_c_v': nrm((N_C, DEC_BATCH, PAST_LEN, C_HEADS, C_HEAD_DIM)),
        'cache_c_logf': jax.nn.log_sigmoid(C_FORGET_BIAS + nrm((N_C, DEC_BATCH, PAST_LEN, C_HEADS))),
        'state_ffn_conv': nrm((DEPTH, DEC_BATCH, CONV_W - 1, 2 * D_FF)),
        'a_w_dq': nrm((N_A, d, A_Q_LORA), d ** -0.5),
        'a_g_q': 1.0 + nrm((N_A, A_Q_LORA), 0.02),
        'a_w_uq': nrm((N_A, A_Q_LORA, A_HEADS * (A_NOPE + A_ROPE)), A_Q_LORA ** -0.5),
        'a_w_dkv': nrm((N_A, d, A_KV_LORA), d ** -0.5),
        'a_g_kv': 1.0 + nrm((N_A, A_KV_LORA), 0.02),
        'a_w_kr': nrm((N_A, d, A_ROPE), d ** -0.5),
        'a_w_uk': nrm((N_A, A_KV_LORA, A_HEADS, A_NOPE), A_KV_LORA ** -0.5),
        'a_w_uv': nrm((N_A, A_KV_LORA, A_HEADS, A_V), A_KV_LORA ** -0.5),
        'a_w_o': nrm((N_A, A_HEADS * A_V, d), BETA * (A_HEADS * A_V) ** -0.5),
        'b_w_qkv': nrm((N_B, d, 3 * d), d ** -0.5),
        'b_rel_bias': nrm((N_B, B_HEADS, 2 * B_REL_CLIP + 1), 0.1),
        'b_w_o': nrm((N_B, d, d), BETA * d ** -0.5),
        'c_w_qkv': nrm((N_C, d, 3 * d), d ** -0.5),
        'c_w_f': nrm((N_C, d, C_HEADS), d ** -0.5),
        'c_b_f': C_FORGET_BIAS + nrm((N_C, C_HEADS), 0.5),
        'c_w_o': nrm((N_C, d, d), BETA * d ** -0.5),
        'f_w_up': nrm((DEPTH, d, 2 * D_FF), d ** -0.5),
        'f_conv_w': nrm((DEPTH, CONV_W, 2 * D_FF), CONV_W ** -0.5),
        'f_conv_b': nrm((DEPTH, 2 * D_FF), 0.02),
        'f_w_down': nrm((DEPTH, D_FF, d), BETA * D_FF ** -0.5),
        'ln1_g': 1.0 + nrm((DEPTH, d), 0.02),
        'ln1_b': nrm((DEPTH, d), 0.02),
        'ln2_g': 1.0 + nrm((DEPTH, d), 0.02),
        'ln2_b': nrm((DEPTH, d), 0.02),
    }


def reference(x_prompt, x_sample, cache_a_ckv, cache_a_kpe, cache_b_k, cache_b_v, cache_c_k, cache_c_v,
              cache_c_logf, state_ffn_conv, a_w_dq, a_g_q, a_w_uq, a_w_dkv, a_g_kv, a_w_kr, a_w_uk, a_w_uv,
              a_w_o, b_w_qkv, b_rel_bias, b_w_o, c_w_qkv, c_w_f, c_b_f, c_w_o, f_w_up, f_conv_w, f_conv_b,
              f_w_down, ln1_g, ln1_b, ln2_g, ln2_b):
    w = dict(a_w_dq=a_w_dq, a_g_q=a_g_q, a_w_uq=a_w_uq, a_w_dkv=a_w_dkv, a_g_kv=a_g_kv, a_w_kr=a_w_kr,
             a_w_uk=a_w_uk, a_w_uv=a_w_uv, a_w_o=a_w_o, b_w_qkv=b_w_qkv, b_rel_bias=b_rel_bias, b_w_o=b_w_o,
             c_w_qkv=c_w_qkv, c_w_f=c_w_f, c_b_f=c_b_f, c_w_o=c_w_o, f_w_up=f_w_up, f_conv_w=f_conv_w,
             f_conv_b=f_conv_b, f_w_down=f_w_down, ln1_g=ln1_g, ln1_b=ln1_b, ln2_g=ln2_g, ln2_b=ln2_b)
    past = dict(a_ckv=cache_a_ckv, a_kpe=cache_a_kpe, b_k=cache_b_k, b_v=cache_b_v, c_k=cache_c_k,
                c_v=cache_c_v, c_logf=cache_c_logf, ffn_conv=state_ffn_conv)
    y_prompt, p = trunk(x_prompt, jnp.arange(x_prompt.shape[1]), None, w)
    y_sample, s = trunk(x_sample, PAST_LEN + jnp.arange(x_sample.shape[1]), past, w)
    return (y_prompt, y_sample,
            p['a_ckv'], p['a_kpe'], p['b_k'], p['b_v'], p['c_k'], p['c_v'], p['c_logf'], p['ffn_conv'],
            s['a_ckv'], s['a_kpe'], s['b_k'], s['b_v'], s['c_k'], s['c_v'], s['c_logf'], s['ffn_conv'])
```

```python
import functools

import numpy as np
import jax
import jax.numpy as jnp
from jax import lax
from jax.experimental import pallas as pl
from jax.experimental.pallas import tpu as pltpu

F32 = jnp.float32
BF16 = jnp.bfloat16

D_MODEL = 1024
DEPTH = 4
CHUNK = 64
N_MIXERS = 3

A_HEADS = 8
A_Q_LORA = 384
A_KV_LORA = 256
A_NOPE = 128
A_ROPE = 64
A_V = 128
A_QK_PAD = 256
A_LAT_PAD = 384
ROPE_THETA = 10000.0

B_HEADS = 16
B_HEAD_DIM = D_MODEL // B_HEADS
B_LEFT_CHUNKS = 8
B_WIN = B_LEFT_CHUNKS * CHUNK
B_REL_CLIP = 128

C_HEADS = 16
C_HEAD_DIM = D_MODEL // C_HEADS

D_FF = 2816
CONV_W = 3

ALPHA = (2.0 * DEPTH) ** 0.25
LN_EPS = 1e-5
RMS_EPS = 1e-6
NEG_INF = -1e30

LANES = 128
SUBLANES = 8
VMEM_LIMIT_BYTES = 48 * 2 ** 20


def _params(*sem):
    return pltpu.CompilerParams(dimension_semantics=sem, vmem_limit_bytes=VMEM_LIMIT_BYTES)


def _row_tile(m, tm):
    while m % tm:
        tm //= 2
    assert tm % SUBLANES == 0, (m, tm)
    return tm


def _mm_kernel(x_ref, w_ref, *o_refs):
    acc = jnp.dot(x_ref[...], w_ref[...], preferred_element_type=F32)
    for o_ref in o_refs:
        o_ref[...] = acc.astype(o_ref.dtype)


def _mm(x, w, out_dtypes, tm=512, tn=1024):
    m, k = x.shape
    n = w.shape[1]
    tm, tn = _row_tile(m, tm), min(tn, n)
    return pl.pallas_call(
        _mm_kernel,
        grid=(m // tm, n // tn),
        in_specs=[pl.BlockSpec((tm, k), lambda i, j: (i, 0)),
                  pl.BlockSpec((k, tn), lambda i, j: (0, j))],
        out_specs=[pl.BlockSpec((tm, tn), lambda i, j: (i, j)) for _ in out_dtypes],
        out_shape=[jax.ShapeDtypeStruct((m, n), d) for d in out_dtypes],
        compiler_params=_params("parallel", "parallel"),
        name="mm",
    )(x, w)


def _mm_res_ln_kernel(a_ref, w_ref, x_ref, g_ref, b_ref, of_ref, ob_ref):
    y = ALPHA * x_ref[...] + jnp.dot(a_ref[...], w_ref[...], preferred_element_type=F32)
    mu = jnp.mean(y, axis=-1, keepdims=True)
    d = y - mu
    var = jnp.mean(d * d, axis=-1, keepdims=True)
    out = d * lax.rsqrt(var + LN_EPS) * g_ref[...] + b_ref[...]
    of_ref[...] = out
    ob_ref[...] = out.astype(BF16)


def _mm_res_ln(a, w, x, g, b, tm=512):
    m, k = a.shape
    n = w.shape[1]
    tm = _row_tile(m, tm)
    row = lambda i: (i, 0)
    fixed = lambda i: (0, 0)
    return pl.pallas_call(
        _mm_res_ln_kernel,
        grid=(m // tm,),
        in_specs=[pl.BlockSpec((tm, k), row), pl.BlockSpec((k, n), fixed),
                  pl.BlockSpec((tm, n), row), pl.BlockSpec((1, n), fixed),
                  pl.BlockSpec((1, n), fixed)],
        out_specs=[pl.BlockSpec((tm, n), row), pl.BlockSpec((tm, n), row)],
        out_shape=[jax.ShapeDtypeStruct((m, n), F32), jax.ShapeDtypeStruct((m, n), BF16)],
        compiler_params=_params("parallel"),
        name="mm_res_ln",
    )(a, w, x, g.reshape(1, n), b.reshape(1, n))


_W1_CQ = (0, A_Q_LORA)
_W1_CKV = (A_Q_LORA, A_Q_LORA + A_KV_LORA)
_W1_KR = (_W1_CKV[1], _W1_CKV[1] + A_ROPE)
_W1_KRS = (_W1_CKV[1] + LANES, _W1_CKV[1] + LANES + A_ROPE)
_W1_COLS = _W1_CKV[1] + 2 * LANES


def _rms(v, g):
    return v * lax.rsqrt(jnp.mean(v * v, axis=-1, keepdims=True) + RMS_EPS) * g


def _mla_proj_kernel(x_ref, w1_ref, gq_ref, gkv_ref, wq_ref, cq_ref, sq_ref, ck_ref, sk_ref,
                     q_ref, lat_ref, ckv_ref, kpe_ref):
    y = jnp.dot(x_ref[...], w1_ref[...], preferred_element_type=F32)
    cq = _rms(y[:, _W1_CQ[0]:_W1_CQ[1]], gq_ref[...]).astype(BF16)
    ckv = _rms(y[:, _W1_CKV[0]:_W1_CKV[1]], gkv_ref[...])
    kpe = y[:, _W1_KR[0]:_W1_KR[1]] * ck_ref[...] + y[:, _W1_KRS[0]:_W1_KRS[1]] * sk_ref[...]
    ckv_ref[...] = ckv
    kpe_ref[...] = kpe
    lat_ref[:, 0:A_KV_LORA] = ckv.astype(BF16)
    lat_ref[:, A_KV_LORA:A_KV_LORA + A_ROPE] = kpe.astype(BF16)
    lat_ref[:, A_KV_LORA + A_ROPE:] = jnp.zeros(
        (lat_ref.shape[0], A_LAT_PAD - A_KV_LORA - A_ROPE), BF16)
    sw0 = A_HEADS * A_QK_PAD
    for h in range(A_HEADS):
        lo, hi = h * A_QK_PAD, (h + 1) * A_QK_PAD
        qp = jnp.dot(cq, wq_ref[:, lo:hi], preferred_element_type=F32)
        qs = jnp.dot(cq, wq_ref[:, sw0 + lo:sw0 + hi], preferred_element_type=F32)
        q_ref[:, lo:hi] = (qp * cq_ref[...] + qs * sq_ref[...]).astype(BF16)


def _mla_proj(xb, w1, gq, gkv, wq, cosq, sinq, cosk, sink, tm=512):
    m = xb.shape[0]
    tm = _row_tile(m, tm)
    row = lambda i: (i, 0)
    fixed = lambda i: (0, 0)
    nq = A_HEADS * A_QK_PAD
    return pl.pallas_call(
        _mla_proj_kernel,
        grid=(m // tm,),
        in_specs=[pl.BlockSpec((tm, D_MODEL), row), pl.BlockSpec(w1.shape, fixed),
                  pl.BlockSpec((1, A_Q_LORA), fixed), pl.BlockSpec((1, A_KV_LORA), fixed),
                  pl.BlockSpec(wq.shape, fixed),
                  pl.BlockSpec((tm, A_QK_PAD), row), pl.BlockSpec((tm, A_QK_PAD), row),
                  pl.BlockSpec((tm, A_ROPE), row), pl.BlockSpec((tm, A_ROPE), row)],
        out_specs=[pl.BlockSpec((tm, nq), row), pl.BlockSpec((tm, A_LAT_PAD), row),
                   pl.BlockSpec((tm, A_KV_LORA), row), pl.BlockSpec((tm, A_ROPE), row)],
        out_shape=[jax.ShapeDtypeStruct((m, nq), BF16), jax.ShapeDtypeStruct((m, A_LAT_PAD), BF16),
                   jax.ShapeDtypeStruct((m, A_KV_LORA), F32), jax.ShapeDtypeStruct((m, A_ROPE), F32)],
        compiler_params=_params("parallel"),
        name="mla_proj",
    )(xb, w1, gq.reshape(1, -1), gkv.reshape(1, -1), wq, cosq, sinq, cosk, sink)


def _flash_kernel(*refs, nh, dq, dv, tq, tk, nk, q_off, chunk_causal, forget):
    if forget:
        q_ref, k_ref, v_ref, fq_ref, fk_ref, o_ref, m_s, l_s, acc_s = refs
    else:
        q_ref, k_ref, v_ref, o_ref, m_s, l_s, acc_s = refs
    iq = pl.program_id(2)
    ik = pl.program_id(3)
    q_lo = q_off + iq * tq
    q_hi = q_lo + tq - 1
    if chunk_causal:
        vis_lo = (q_lo // CHUNK) * CHUNK + CHUNK - 1
        vis_hi = (q_hi // CHUNK) * CHUNK + CHUNK - 1
    else:
        vis_lo, vis_hi = q_lo, q_hi
    k_lo = ik * tk
    needed = k_lo <= vis_hi
    unmasked = k_lo + tk - 1 <= vis_lo

    @pl.when(ik == 0)
    def _init():
        m_s[...] = jnp.full(m_s.shape, -jnp.inf, F32)
        l_s[...] = jnp.zeros(l_s.shape, F32)
        acc_s[...] = jnp.zeros(acc_s.shape, F32)

    def step(masked):
        if masked:
            q_pos = q_lo + lax.broadcasted_iota(jnp.int32, (tq, tk), 0)
            k_pos = k_lo + lax.broadcasted_iota(jnp.int32, (tq, tk), 1)
            if chunk_causal:
                shift = CHUNK.bit_length() - 1
                mask = jnp.right_shift(k_pos, shift) <= jnp.right_shift(q_pos, shift)
            else:
                mask = k_pos <= q_pos
        for h in range(nh):
            q = q_ref[:, h * dq:(h + 1) * dq]
            k = k_ref[:, h * dq:(h + 1) * dq]
            v = v_ref[:, h * dv:(h + 1) * dv]
            s = lax.dot_general(q, k, (((1,), (1,)), ((), ())), preferred_element_type=F32)
            if forget:
                s = s + fq_ref[:, h:h + 1] - fk_ref[h:h + 1, :]
            if masked:
                s = jnp.where(mask, s, NEG_INF)
            m_prev = m_s[h]
            m_new = jnp.maximum(m_prev, jnp.max(s, axis=-1, keepdims=True))
            alpha = jnp.exp(m_prev - m_new)
            p = jnp.exp(s - m_new)
            l_s[h] = alpha * l_s[h] + jnp.sum(p, axis=-1, keepdims=True)
            acc_s[h] = alpha * acc_s[h] + jnp.dot(p.astype(BF16), v, preferred_element_type=F32)
            m_s[h] = m_new

    @pl.when(needed & unmasked)
    def _plain():
        step(False)

    @pl.when(needed & jnp.logical_not(unmasked))
    def _masked():
        step(True)

    @pl.when(ik == nk - 1)
    def _fin():
        for h in range(nh):
            o_ref[:, h * dv:(h + 1) * dv] = (acc_s[h] / l_s[h]).astype(o_ref.dtype)


def _flash(q, k, v, *, nh, dq, dv, n_hblk, k_col0, v_col0, tq, tk, q_off, chunk_causal,
           fq=None, fk=None):
    b, t_q = q.shape[0], q.shape[1]
    t_k = k.shape[1]
    tq = tq if t_q % tq == 0 else t_q
    tk = tk if t_k % tk == 0 else t_k
    nq, nk = t_q // tq, t_k // tk
    forget = fq is not None

    def last_blk(i):
        q_hi = q_off + (i + 1) * tq - 1
        vis = (q_hi // CHUNK) * CHUNK + CHUNK - 1 if chunk_causal else q_hi
        return jnp.minimum(vis // tk, nk - 1)

    in_specs = [
        pl.BlockSpec((None, tq, nh * dq), lambda bi, h, i, j: (bi, i, h)),
        pl.BlockSpec((None, tk, nh * dq), lambda bi, h, i, j: (bi, jnp.minimum(j, last_blk(i)), k_col0 + h)),
        pl.BlockSpec((None, tk, nh * dv), lambda bi, h, i, j: (bi, jnp.minimum(j, last_blk(i)), v_col0 + h)),
    ]
    args = [q, k, v]
    if forget:
        in_specs += [
            pl.BlockSpec((None, None, tq, LANES), lambda bi, h, i, j: (bi, h, i, 0)),
            pl.BlockSpec((None, None, SUBLANES, tk),
                         lambda bi, h, i, j: (bi, h, 0, jnp.minimum(j, last_blk(i)))),
        ]
        args += [fq, fk]
    kern = functools.partial(_flash_kernel, nh=nh, dq=dq, dv=dv, tq=tq, tk=tk, nk=nk, q_off=q_off,
                             chunk_causal=chunk_causal, forget=forget)
    return pl.pallas_call(
        kern,
        grid=(b, n_hblk, nq, nk),
        in_specs=in_specs,
        out_specs=pl.BlockSpec((None, tq, nh * dv), lambda bi, h, i, j: (bi, i, h)),
        out_shape=jax.ShapeDtypeStruct((b, t_q, n_hblk * nh * dv), BF16),
        scratch_shapes=[pltpu.VMEM((nh, tq, 1), F32), pltpu.VMEM((nh, tq, 1), F32),
                        pltpu.VMEM((nh, tq, dv), F32)],
        compiler_params=_params("parallel", "parallel", "parallel", "arbitrary"),
        name="flash_fox" if forget else "flash_mla",
    )(*args)


B_HEADS_PER_STEP = LANES // B_HEAD_DIM


def _band_kernel(*refs, nkb, tkb, clamp_front):
    q_ref = refs[0]
    k_refs = refs[1:1 + nkb]
    v_refs = refs[1 + nkb:1 + 2 * nkb]
    bias_ref, o_ref = refs[1 + 2 * nkb], refs[2 + 2 * nkb]
    iq = pl.program_id(2)
    dh = B_HEAD_DIM
    for h in range(B_HEADS_PER_STEP):
        q = q_ref[:, h * dh:(h + 1) * dh]
        ss = []
        for j in range(nkb):
            s = lax.dot_general(q, k_refs[j][:, h * dh:(h + 1) * dh], (((1,), (1,)), ((), ())),
                                preferred_element_type=F32)
            s = s + bias_ref[h, :, j * tkb:(j + 1) * tkb]
            if clamp_front and j < nkb - 1:
                s = jnp.where(iq >= nkb - 1 - j, s, NEG_INF)
            ss.append(s)
        m = functools.reduce(jnp.maximum, [jnp.max(s, axis=-1, keepdims=True) for s in ss])
        ps = [jnp.exp(s - m) for s in ss]
        l = functools.reduce(lambda a, c: a + c, [jnp.sum(p, axis=-1, keepdims=True) for p in ps])
        o = functools.reduce(lambda a, c: a + c, [
            jnp.dot(ps[j].astype(BF16), v_refs[j][:, h * dh:(h + 1) * dh], preferred_element_type=F32)
            for j in range(nkb)])
        o_ref[:, h * dh:(h + 1) * dh] = (o / l).astype(o_ref.dtype)


def _band_bias(rel_bias, q_pos, k_pos):
    rel = np.clip(q_pos[:, None] - k_pos[None, :], -B_REL_CLIP, B_REL_CLIP) + B_REL_CLIP
    qc = q_pos[:, None] // CHUNK
    kc = k_pos[None, :] // CHUNK
    mask = (kc <= qc) & (kc >= qc - B_LEFT_CHUNKS) & (k_pos[None, :] >= 0)
    return jnp.where(jnp.asarray(mask)[None], rel_bias[:, jnp.asarray(rel)], NEG_INF).astype(F32)


def _band(q, k, v, k_col0, v_col0, bias, *, tq, tkb, nkb, clamp_front):
    b, t_q = q.shape[0], q.shape[1]
    nq = t_q // tq
    n_hblk = B_HEADS // B_HEADS_PER_STEP

    def kv_spec(j, col0):
        back = nkb - 1 - j
        return pl.BlockSpec((None, tkb, LANES),
                            lambda bi, h, i: (bi, jnp.maximum(i - back, 0), col0 + h))

    in_specs = ([pl.BlockSpec((None, tq, LANES), lambda bi, h, i: (bi, i, h))]
                + [kv_spec(j, k_col0) for j in range(nkb)]
                + [kv_spec(j, v_col0) for j in range(nkb)]
                + [pl.BlockSpec((B_HEADS_PER_STEP, tq, nkb * tkb), lambda bi, h, i: (h, 0, 0))])
    kern = functools.partial(_band_kernel, nkb=nkb, tkb=tkb, clamp_front=clamp_front)
    return pl.pallas_call(
        kern,
        grid=(b, n_hblk, nq),
        in_specs=in_specs,
        out_specs=pl.BlockSpec((None, tq, LANES), lambda bi, h, i: (bi, i, h)),
        out_shape=jax.ShapeDtypeStruct((b, t_q, D_MODEL), BF16),
        compiler_params=_params("parallel", "parallel", "parallel"),
        name="band_attn",
    )(q, *([k] * nkb), *([v] * nkb), bias)


def _logf_kernel(x_ref, w_ref, b_ref, o_ref):
    z = jnp.dot(x_ref[...], w_ref[...], preferred_element_type=F32) + b_ref[...]
    o_ref[...] = -(jnp.maximum(-z, 0.0) + jnp.log1p(jnp.exp(-jnp.abs(z))))


def _logf(xb, w_pad, b_pad, tm=512):
    m = xb.shape[0]
    tm = _row_tile(m, tm)
    return pl.pallas_call(
        _logf_kernel,
        grid=(m // tm,),
        in_specs=[pl.BlockSpec((tm, D_MODEL), lambda i: (i, 0)),
                  pl.BlockSpec((D_MODEL, LANES), lambda i: (0, 0)),
                  pl.BlockSpec((1, LANES), lambda i: (0, 0))],
        out_specs=pl.BlockSpec((tm, LANES), lambda i: (i, 0)),
        out_shape=jax.ShapeDtypeStruct((m, LANES), F32),
        compiler_params=_params("parallel"),
        name="logf",
    )(xb, w_pad, b_pad)


def _cumsum_kernel(x_ref, o_ref, carry, *, tc):
    @pl.when(pl.program_id(1) == 0)
    def _():
        carry[...] = jnp.zeros(carry.shape, F32)

    x = x_ref[...]
    tri = (lax.broadcasted_iota(jnp.int32, (tc, tc), 0)
           >= lax.broadcasted_iota(jnp.int32, (tc, tc), 1)).astype(BF16)
    hi = x.astype(BF16)
    r1 = x - hi.astype(F32)
    mid = r1.astype(BF16)
    lo = (r1 - mid.astype(F32)).astype(BF16)
    c = (jnp.dot(tri, hi, preferred_element_type=F32) + jnp.dot(tri, mid, preferred_element_type=F32)
         + jnp.dot(tri, lo, preferred_element_type=F32))
    out = c + carry[0:1, :]
    o_ref[...] = out
    carry[...] = jnp.broadcast_to(out[tc - 1:tc, :], carry.shape)


def _cumsum(x, tc=256):
    b, t, _ = x.shape
    return pl.pallas_call(
        functools.partial(_cumsum_kernel, tc=tc),
        grid=(b, t // tc),
        in_specs=[pl.BlockSpec((None, tc, LANES), lambda bi, i: (bi, i, 0))],
        out_specs=pl.BlockSpec((None, tc, LANES), lambda bi, i: (bi, i, 0)),
        out_shape=jax.ShapeDtypeStruct(x.shape, F32),
        scratch_shapes=[pltpu.VMEM((SUBLANES, LANES), F32)],
        compiler_params=_params("parallel", "arbitrary"),
        name="cumsum",
    )(x)


FFN_TN = D_FF // 2
CONV_ROWS = SUBLANES


def _ffn_up_kernel(x_ref, wa_ref, wg_ref, ca_ref, cg_ref, pa_ref, pg_ref,
                   act_ref, sa_ref, sg_ref, hpa, hpg, *, tm):
    @pl.when(pl.program_id(2) == 0)
    def _():
        hpa[0:CONV_ROWS] = pa_ref[...]
        hpg[0:CONV_ROWS] = pg_ref[...]

    x = x_ref[...]

    def conv(w_ref, c_ref, s_ref, hp):
        hp[CONV_ROWS:CONV_ROWS + tm] = jnp.dot(x, w_ref[...], preferred_element_type=F32)
        hc = c_ref[CONV_W:CONV_W + 1]
        for j in range(CONV_W):
            off = CONV_ROWS - (CONV_W - 1) + j
            hc = hc + c_ref[j:j + 1] * hp[off:off + tm]
        tail = hp[tm:tm + CONV_ROWS]
        s_ref[...] = tail
        hp[0:CONV_ROWS] = tail
        return hc

    a = conv(wa_ref, ca_ref, sa_ref, hpa)
    g = conv(wg_ref, cg_ref, sg_ref, hpg)
    act_ref[...] = (g * jax.nn.sigmoid(g) * a).astype(BF16)


def _ffn_up(xb, w_up, conv_tab, past, tm=512):
    b, t, _ = xb.shape
    tm = _row_tile(t, tm)
    tn = FFN_TN
    nn = D_FF // tn
    a_col = lambda n, bi, ti: (0, n)
    g_col = lambda n, bi, ti: (0, n + nn)
    state = jax.ShapeDtypeStruct((b, CONV_ROWS, D_FF), F32)
    return pl.pallas_call(
        functools.partial(_ffn_up_kernel, tm=tm),
        grid=(nn, b, t // tm),
        in_specs=[pl.BlockSpec((None, tm, D_MODEL), lambda n, bi, ti: (bi, ti, 0)),
                  pl.BlockSpec((D_MODEL, tn), a_col), pl.BlockSpec((D_MODEL, tn), g_col),
                  pl.BlockSpec((CONV_ROWS, tn), a_col), pl.BlockSpec((CONV_ROWS, tn), g_col),
                  pl.BlockSpec((None, CONV_ROWS, tn), lambda n, bi, ti: (bi, 0, n)),
                  pl.BlockSpec((None, CONV_ROWS, tn), lambda n, bi, ti: (bi, 0, n + nn))],
        out_specs=[pl.BlockSpec((None, tm, tn), lambda n, bi, ti: (bi, ti, n)),
                   pl.BlockSpec((None, CONV_ROWS, tn), lambda n, bi, ti: (bi, 0, n)),
                   pl.BlockSpec((None, CONV_ROWS, tn), lambda n, bi, ti: (bi, 0, n))],
        out_shape=[jax.ShapeDtypeStruct((b, t, D_FF), BF16), state, state],
        scratch_shapes=[pltpu.VMEM((tm + CONV_ROWS, tn), F32), pltpu.VMEM((tm + CONV_ROWS, tn), F32)],
        compiler_params=_params("parallel", "parallel", "arbitrary"),
        name="ffn_up",
    )(xb, w_up, w_up, conv_tab, conv_tab, past, past)


def _rope_tables(pos, batch):
    half = A_ROPE // 2
    inv_freq = ROPE_THETA ** (-jnp.arange(half, dtype=F32) / half)
    ang = pos.astype(F32)[:, None] * inv_freq
    cos, sin = jnp.cos(ang), jnp.sin(ang)
    cosk = jnp.concatenate([cos, cos], axis=-1)
    sink = jnp.concatenate([sin, sin], axis=-1)
    t = pos.shape[0]
    scale = (A_NOPE + A_ROPE) ** -0.5
    pad = jnp.zeros((t, A_QK_PAD - A_NOPE - A_ROPE), F32)
    cosq = scale * jnp.concatenate([jnp.ones((t, A_NOPE), F32), cosk, pad], axis=-1)
    sinq = scale * jnp.concatenate([jnp.zeros((t, A_NOPE), F32), sink, pad], axis=-1)
    return tuple(jnp.tile(a, (batch, 1)) for a in (cosq, sinq, cosk, sink))


def _swap_halves(w):
    half = w.shape[-1] // 2
    return jnp.concatenate([-w[..., half:], w[..., :half]], axis=-1)


def _mla_weights(w_dq, w_dkv, w_kr, w_uq, w_uk, w_uv):
    zc = jnp.zeros((D_MODEL, LANES - A_ROPE), F32)
    w1 = jnp.concatenate([w_dq, w_dkv, w_kr, zc, _swap_halves(w_kr), zc], axis=1).astype(BF16)
    wq = w_uq.reshape(A_Q_LORA, A_HEADS, A_NOPE + A_ROPE)
    nope, rope = wq[..., :A_NOPE], wq[..., A_NOPE:]
    zpad = jnp.zeros((A_Q_LORA, A_HEADS, A_QK_PAD - A_NOPE - A_ROPE), F32)
    w_cat = jnp.concatenate([nope, rope, zpad], axis=-1).reshape(A_Q_LORA, -1)
    w_sw = jnp.concatenate([jnp.zeros_like(nope), _swap_halves(rope), zpad], axis=-1).reshape(A_Q_LORA, -1)
    wq2 = jnp.concatenate([w_cat, w_sw], axis=1).astype(BF16)
    wk = jnp.zeros((A_LAT_PAD, A_HEADS, A_QK_PAD), F32)
    wk = wk.at[:A_KV_LORA, :, :A_NOPE].set(w_uk)
    eye = jnp.broadcast_to(jnp.eye(A_ROPE, dtype=F32)[:, None, :], (A_ROPE, A_HEADS, A_ROPE))
    wk = wk.at[A_KV_LORA:A_KV_LORA + A_ROPE, :, A_NOPE:A_NOPE + A_ROPE].set(eye)
    wv = jnp.zeros((A_LAT_PAD, A_HEADS * A_V), F32).at[:A_KV_LORA].set(w_uv.reshape(A_KV_LORA, -1))
    wkv = jnp.concatenate([wk.reshape(A_LAT_PAD, -1), wv], axis=1).astype(BF16)
    return w1, wq2, wkv


def _mla_mixer(xb, b, t, pos, ckv_past, kpe_past, w, i):
    w1, wq2, wkv = _mla_weights(w['a_w_dq'][i], w['a_w_dkv'][i], w['a_w_kr'][i], w['a_w_uq'][i],
                                w['a_w_uk'][i], w['a_w_uv'][i])
    q, lat, ckv, kpe = _mla_proj(xb, w1, w['a_g_q'][i], w['a_g_kv'][i], wq2, *_rope_tables(pos, b))
    lat = lat.reshape(b, t, A_LAT_PAD)
    q_off = 0
    if ckv_past is not None:
        p_len = ckv_past.shape[1]
        past = jnp.concatenate(
            [ckv_past, kpe_past, jnp.zeros((b, p_len, A_LAT_PAD - A_KV_LORA - A_ROPE), F32)], axis=-1)
        lat = jnp.concatenate([past.astype(BF16), lat], axis=1)
        q_off = p_len
    t_k = lat.shape[1]
    (kv,) = _mm(lat.reshape(b * t_k, A_LAT_PAD), wkv, [BF16])
    kv = kv.reshape(b, t_k, -1)
    o = _flash(q.reshape(b, t, -1), kv, kv, nh=1, dq=A_QK_PAD, dv=A_V, n_hblk=A_HEADS,
               k_col0=0, v_col0=A_HEADS * A_QK_PAD // A_V, tq=512, tk=512, q_off=q_off,
               chunk_causal=True)
    return o.reshape(b * t, A_HEADS * A_V), ckv.reshape(b, t, -1), kpe.reshape(b, t, -1)


def _qkv(xb, w_qkv, head_dim):
    wq = (w_qkv[:, :D_MODEL] * head_dim ** -0.5).astype(BF16)
    (q,) = _mm(xb, wq, [BF16])
    kv32, kvb = _mm(xb, w_qkv[:, D_MODEL:].astype(BF16), [F32, BF16])
    return q, kv32, kvb


def _band_mixer(xb, b, t, pos0, k_past, v_past, w, i):
    q, kv32, kvb = _qkv(xb, w['b_w_qkv'][i], B_HEAD_DIM)
    q = q.reshape(b, t, D_MODEL)
    k32 = kv32[:, :D_MODEL].reshape(b, t, B_HEADS, B_HEAD_DIM)
    v32 = kv32[:, D_MODEL:].reshape(b, t, B_HEADS, B_HEAD_DIM)
    n_cols = D_MODEL // LANES
    if k_past is None:
        tq = 4 * CHUNK
        nkb = B_WIN // tq + 1
        bias = _band_bias(w['b_rel_bias'][i], B_WIN + np.arange(tq), np.arange(B_WIN + tq))
        kvb = kvb.reshape(b, t, 2 * D_MODEL)
        o = _band(q, kvb, kvb, 0, n_cols, bias, tq=tq, tkb=tq, nkb=nkb, clamp_front=True)
        keep = min(B_WIN, t)
        k_new, v_new = k32[:, t - keep:], v32[:, t - keep:]
    else:
        p_len = k_past.shape[1]
        kvb = kvb.reshape(b, t, 2 * D_MODEL)
        past = jnp.concatenate([k_past.reshape(b, p_len, D_MODEL), v_past.reshape(b, p_len, D_MODEL)],
                               axis=-1).astype(BF16)
        kv_all = jnp.concatenate([past, kvb], axis=1)
        q_pos = pos0 + np.arange(t)
        k_pos = np.concatenate([np.arange(pos0 - p_len, pos0), q_pos])
        bias = _band_bias(w['b_rel_bias'][i], q_pos, k_pos)
        o = _band(q, kv_all, kv_all, 0, n_cols, bias, tq=t, tkb=p_len + t, nkb=1, clamp_front=False)
        k_new, v_new = k32, v32
    return o.reshape(b * t, D_MODEL), k_new, v_new


def _fox_mixer(xb, b, t, k_past, v_past, lf_past, w, i):
    q, kv32, kvb = _qkv(xb, w['c_w_qkv'][i], C_HEAD_DIM)
    k32 = kv32[:, :D_MODEL].reshape(b, t, C_HEADS, C_HEAD_DIM)
    v32 = kv32[:, D_MODEL:].reshape(b, t, C_HEADS, C_HEAD_DIM)
    w_f = jnp.zeros((D_MODEL, LANES), F32).at[:, :C_HEADS].set(w['c_w_f'][i]).astype(BF16)
    b_f = jnp.zeros((1, LANES), F32).at[0, :C_HEADS].set(w['c_b_f'][i])
    log_f = _logf(xb, w_f, b_f).reshape(b, t, LANES)
    kvb = kvb.reshape(b, t, 2 * D_MODEL)
    lf_all = log_f
    q_off = 0
    if k_past is not None:
        p_len = k_past.shape[1]
        past = jnp.concatenate([k_past.reshape(b, p_len, D_MODEL), v_past.reshape(b, p_len, D_MODEL)],
                               axis=-1).astype(BF16)
        kvb = jnp.concatenate([past, kvb], axis=1)
        lf_all = jnp.concatenate([jnp.pad(lf_past, ((0, 0), (0, 0), (0, LANES - C_HEADS))), log_f], axis=1)
        q_off = p_len
    t_k = lf_all.shape[1]
    tc = 256
    t_pad = -(-t_k // tc) * tc
    f_cum = _cumsum(jnp.pad(lf_all, ((0, 0), (0, t_pad - t_k), (0, 0))), tc)[:, :t_k, :C_HEADS]
    nh = LANES // C_HEAD_DIM
    n_hblk = C_HEADS // nh
    f_blk = f_cum.reshape(b, t_k, n_hblk, nh).transpose(0, 2, 1, 3)
    fq = jnp.pad(f_blk[:, :, t_k - t:], ((0, 0), (0, 0), (0, 0), (0, LANES - nh)))
    fk = jnp.pad(f_blk.transpose(0, 1, 3, 2), ((0, 0), (0, 0), (0, SUBLANES - nh), (0, 0)))
    o = _flash(q.reshape(b, t, D_MODEL), kvb, kvb, nh=nh, dq=C_HEAD_DIM, dv=C_HEAD_DIM, n_hblk=n_hblk,
               k_col0=0, v_col0=n_hblk, tq=512, tk=512, q_off=q_off, chunk_causal=False, fq=fq, fk=fk)
    return o.reshape(b * t, D_MODEL), k32, v32, log_f[:, :, :C_HEADS]


def _conv_ffn(xb, b, t, conv_past, w, i):
    tab = jnp.concatenate([w['f_conv_w'][i], w['f_conv_b'][i][None],
                           jnp.zeros((CONV_ROWS - CONV_W - 1, 2 * D_FF), F32)], axis=0)
    if conv_past is None:
        past = jnp.zeros((b, CONV_ROWS, 2 * D_FF), F32)
    else:
        past = jnp.pad(conv_past, ((0, 0), (CONV_ROWS - (CONV_W - 1), 0), (0, 0)))
    act, sa, sg = _ffn_up(xb.reshape(b, t, D_MODEL), w['f_w_up'][i].astype(BF16), tab, past)
    state = jnp.concatenate([sa, sg], axis=-1)[:, CONV_ROWS - (CONV_W - 1):]
    return act.reshape(b * t, D_FF), state


def _trunk(x, pos0, past, w):
    b, t, _ = x.shape
    pos = pos0 + jnp.arange(t)
    xf = x.reshape(b * t, D_MODEL)
    xb = xf.astype(BF16)
    outs = {n: [] for n in ('a_ckv', 'a_kpe', 'b_k', 'b_v', 'c_k', 'c_v', 'c_logf', 'ffn_conv')}
    ia = ib = ic = 0
    get = lambda name, j: None if past is None else past[name][j]
    for i in range(DEPTH):
        kind = i % N_MIXERS
        if kind == 0:
            o, ckv, kpe = _mla_mixer(xb, b, t, pos, get('a_ckv', ia), get('a_kpe', ia), w, ia)
            outs['a_ckv'].append(ckv)
            outs['a_kpe'].append(kpe)
            w_o = w['a_w_o'][ia]
            ia += 1
        elif kind == 1:
            o, kb, vb = _band_mixer(xb, b, t, pos0, get('b_k', ib), get('b_v', ib), w, ib)
            outs['b_k'].append(kb)
            outs['b_v'].append(vb)
            w_o = w['b_w_o'][ib]
            ib += 1
        else:
            o, kc, vc, lf = _fox_mixer(xb, b, t, get('c_k', ic), get('c_v', ic), get('c_logf', ic), w, ic)
            outs['c_k'].append(kc)
            outs['c_v'].append(vc)
            outs['c_logf'].append(lf)
            w_o = w['c_w_o'][ic]
            ic += 1
        xf, xb = _mm_res_ln(o, w_o.astype(BF16), xf, w['ln1_g'][i], w['ln1_b'][i])
        act, conv_state = _conv_ffn(xb, b, t, get('ffn_conv', i), w, i)
        outs['ffn_conv'].append(conv_state)
        xf, xb = _mm_res_ln(act, w['f_w_down'][i].astype(BF16), xf, w['ln2_g'][i], w['ln2_b'][i])
    return xf.reshape(b, t, D_MODEL), {n: jnp.stack(v) for n, v in outs.items()}


def kernel(x_prompt, x_sample, cache_a_ckv, cache_a_kpe, cache_b_k, cache_b_v, cache_c_k, cache_c_v,
           cache_c_logf, state_ffn_conv, a_w_dq, a_g_q, a_w_uq, a_w_dkv, a_g_kv, a_w_kr, a_w_uk, a_w_uv,
           a_w_o, b_w_qkv, b_rel_bias, b_w_o, c_w_qkv, c_w_f, c_b_f, c_w_o, f_w_up, f_conv_w, f_conv_b,
           f_w_down, ln1_g, ln1_b, ln2_g, ln2_b):
    w = dict(a_w_dq=a_w_dq, a_g_q=a_g_q, a_w_uq=a_w_uq, a_w_dkv=a_w_dkv, a_g_kv=a_g_kv, a_w_kr=a_w_kr,
             a_w_uk=a_w_uk, a_w_uv=a_w_uv, a_w_o=a_w_o, b_w_qkv=b_w_qkv, b_rel_bias=b_rel_bias, b_w_o=b_w_o,
             c_w_qkv=c_w_qkv, c_w_f=c_w_f, c_b_f=c_b_f, c_w_o=c_w_o, f_w_up=f_w_up, f_conv_w=f_conv_w,
             f_conv_b=f_conv_b, f_w_down=f_w_down, ln1_g=ln1_g, ln1_b=ln1_b, ln2_g=ln2_g, ln2_b=ln2_b)
    past = dict(a_ckv=cache_a_ckv, a_kpe=cache_a_kpe, b_k=cache_b_k, b_v=cache_b_v, c_k=cache_c_k,
                c_v=cache_c_v, c_logf=cache_c_logf, ffn_conv=state_ffn_conv)
    past_len = cache_a_ckv.shape[2]
    y_prompt, p = _trunk(x_prompt, 0, None, w)
    y_sample, s = _trunk(x_sample, past_len, past, w)
    names = ('a_ckv', 'a_kpe', 'b_k', 'b_v', 'c_k', 'c_v', 'c_logf', 'ffn_conv')
    return (y_prompt, y_sample) + tuple(p[n] for n in names) + tuple(s[n] for n in names)
```

```python
import functools

import numpy as np
import jax
import jax.numpy as jnp
from jax import lax
from jax.experimental import pallas as pl
from jax.experimental.pallas import tpu as pltpu

F32 = jnp.float32
BF16 = jnp.bfloat16

D_MODEL = 1024
DEPTH = 4
CHUNK = 64
N_MIXERS = 3

A_HEADS = 8
A_Q_LORA = 384
A_KV_LORA = 256
A_NOPE = 128
A_ROPE = 64
A_V = 128
A_QK_PAD = 256
A_LAT_PAD = 384
ROPE_THETA = 10000.0

B_HEADS = 16
B_HEAD_DIM = D_MODEL // B_HEADS
B_LEFT_CHUNKS = 8
B_WIN = B_LEFT_CHUNKS * CHUNK
B_REL_CLIP = 128

C_HEADS = 16
C_HEAD_DIM = D_MODEL // C_HEADS

D_FF = 2816
CONV_W = 3

ALPHA = (2.0 * DEPTH) ** 0.25
LN_EPS = 1e-5
RMS_EPS = 1e-6
NEG_INF = -1e30

LANES = 128
SUBLANES = 8
VMEM_LIMIT_BYTES = 48 * 2 ** 20


def _params(*sem):
    return pltpu.CompilerParams(dimension_semantics=sem, vmem_limit_bytes=VMEM_LIMIT_BYTES)


def _row_tile(m, tm):
    while m % tm:
        tm //= 2
    assert tm % SUBLANES == 0, (m, tm)
    return tm


def _mm_kernel(x_ref, w_ref, *o_refs):
    acc = jnp.dot(x_ref[...], w_ref[...], preferred_element_type=F32)
    for o_ref in o_refs:
        o_ref[...] = acc.astype(o_ref.dtype)


def _mm(x, w, out_dtypes, tm=512, tn=1024):
    m, k = x.shape
    n = w.shape[1]
    tm, tn = _row_tile(m, tm), min(tn, n)
    return pl.pallas_call(
        _mm_kernel,
        grid=(m // tm, n // tn),
        in_specs=[pl.BlockSpec((tm, k), lambda i, j: (i, 0)),
                  pl.BlockSpec((k, tn), lambda i, j: (0, j))],
        out_specs=[pl.BlockSpec((tm, tn), lambda i, j: (i, j)) for _ in out_dtypes],
        out_shape=[jax.ShapeDtypeStruct((m, n), d) for d in out_dtypes],
        compiler_params=_params("parallel", "parallel"),
        name="mm",
    )(x, w)


def _mm_res_ln_kernel(a_ref, w_ref, x_ref, g_ref, b_ref, of_ref, ob_ref):
    y = ALPHA * x_ref[...] + jnp.dot(a_ref[...], w_ref[...], preferred_element_type=F32)
    mu = jnp.mean(y, axis=-1, keepdims=True)
    d = y - mu
    var = jnp.mean(d * d, axis=-1, keepdims=True)
    out = d * lax.rsqrt(var + LN_EPS) * g_ref[...] + b_ref[...]
    of_ref[...] = out
    ob_ref[...] = out.astype(BF16)


def _mm_res_ln(a, w, x, g, b, tm=512):
    m, k = a.shape
    n = w.shape[1]
    tm = _row_tile(m, tm)
    row = lambda i: (i, 0)
    fixed = lambda i: (0, 0)
    return pl.pallas_call(
        _mm_res_ln_kernel,
        grid=(m // tm,),
        in_specs=[pl.BlockSpec((tm, k), row), pl.BlockSpec((k, n), fixed),
                  pl.BlockSpec((tm, n), row), pl.BlockSpec((1, n), fixed),
                  pl.BlockSpec((1, n), fixed)],
        out_specs=[pl.BlockSpec((tm, n), row), pl.BlockSpec((tm, n), row)],
        out_shape=[jax.ShapeDtypeStruct((m, n), F32), jax.ShapeDtypeStruct((m, n), BF16)],
        compiler_params=_params("parallel"),
        name="mm_res_ln",
    )(a, w, x, g.reshape(1, n), b.reshape(1, n))


_W1_CQ = (0, A_Q_LORA)
_W1_CKV = (A_Q_LORA, A_Q_LORA + A_KV_LORA)
_W1_KR = (_W1_CKV[1], _W1_CKV[1] + A_ROPE)
_W1_KRS = (_W1_CKV[1] + LANES, _W1_CKV[1] + LANES + A_ROPE)
_W1_COLS = _W1_CKV[1] + 2 * LANES


def _rms(v, g):
    return v * lax.rsqrt(jnp.mean(v * v, axis=-1, keepdims=True) + RMS_EPS) * g


def _mla_proj_kernel(x_ref, w1_ref, gq_ref, gkv_ref, wq_ref, cq_ref, sq_ref, ck_ref, sk_ref,
                     q_ref, lat_ref, ckv_ref, kpe_ref):
    y = jnp.dot(x_ref[...], w1_ref[...], preferred_element_type=F32)
    cq = _rms(y[:, _W1_CQ[0]:_W1_CQ[1]], gq_ref[...]).astype(BF16)
    ckv = _rms(y[:, _W1_CKV[0]:_W1_CKV[1]], gkv_ref[...])
    kpe = y[:, _W1_KR[0]:_W1_KR[1]] * ck_ref[...] + y[:, _W1_KRS[0]:_W1_KRS[1]] * sk_ref[...]
    ckv_ref[...] = ckv
    kpe_ref[...] = kpe
    lat_ref[:, 0:A_KV_LORA] = ckv.astype(BF16)
    lat_ref[:, A_KV_LORA:A_KV_LORA + A_ROPE] = kpe.astype(BF16)
    lat_ref[:, A_KV_LORA + A_ROPE:] = jnp.zeros(
        (lat_ref.shape[0], A_LAT_PAD - A_KV_LORA - A_ROPE), BF16)
    sw0 = A_HEADS * A_QK_PAD
    for h in range(A_HEADS):
        lo, hi = h * A_QK_PAD, (h + 1) * A_QK_PAD
        qp = jnp.dot(cq, wq_ref[:, lo:hi], preferred_element_type=F32)
        qs = jnp.dot(cq, wq_ref[:, sw0 + lo:sw0 + hi], preferred_element_type=F32)
        q_ref[:, lo:hi] = (qp * cq_ref[...] + qs * sq_ref[...]).astype(BF16)


def _mla_proj(xb, w1, gq, gkv, wq, cosq, sinq, cosk, sink, tm=512):
    m = xb.shape[0]
    tm = _row_tile(m, tm)
    row = lambda i: (i, 0)
    fixed = lambda i: (0, 0)
    nq = A_HEADS * A_QK_PAD
    return pl.pallas_call(
        _mla_proj_kernel,
        grid=(m // tm,),
        in_specs=[pl.BlockSpec((tm, D_MODEL), row), pl.BlockSpec(w1.shape, fixed),
                  pl.BlockSpec((1, A_Q_LORA), fixed), pl.BlockSpec((1, A_KV_LORA), fixed),
                  pl.BlockSpec(wq.shape, fixed),
                  pl.BlockSpec((tm, A_QK_PAD), row), pl.BlockSpec((tm, A_QK_PAD), row),
                  pl.BlockSpec((tm, A_ROPE), row), pl.BlockSpec((tm, A_ROPE), row)],
        out_specs=[pl.BlockSpec((tm, nq), row), pl.BlockSpec((tm, A_LAT_PAD), row),
                   pl.BlockSpec((tm, A_KV_LORA), row), pl.BlockSpec((tm, A_ROPE), row)],
        out_shape=[jax.ShapeDtypeStruct((m, nq), BF16), jax.ShapeDtypeStruct((m, A_LAT_PAD), BF16),
                   jax.ShapeDtypeStruct((m, A_KV_LORA), F32), jax.ShapeDtypeStruct((m, A_ROPE), F32)],
        compiler_params=_params("parallel"),
        name="mla_proj",
    )(xb, w1, gq.reshape(1, -1), gkv.reshape(1, -1), wq, cosq, sinq, cosk, sink)


def _flash_kernel(*refs, nh, dq, dv, tq, tk, nk, q_off, chunk_causal, forget):
    if forget:
        q_ref, k_ref, v_ref, fq_ref, fk_ref, o_ref, m_s, l_s, acc_s = refs
    else:
        q_ref, k_ref, v_ref, o_ref, m_s, l_s, acc_s = refs
    iq = pl.program_id(2)
    ik = pl.program_id(3)
    q_lo = q_off + iq * tq
    q_hi = q_lo + tq - 1
    if chunk_causal:
        vis_lo = (q_lo // CHUNK) * CHUNK + CHUNK - 1
        vis_hi = (q_hi // CHUNK) * CHUNK + CHUNK - 1
    else:
        vis_lo, vis_hi = q_lo, q_hi
    k_lo = ik * tk
    needed = k_lo <= vis_hi
    unmasked = k_lo + tk - 1 <= vis_lo

    @pl.when(ik == 0)
    def _init():
        m_s[...] = jnp.full(m_s.shape, -jnp.inf, F32)
        l_s[...] = jnp.zeros(l_s.shape, F32)
        acc_s[...] = jnp.zeros(acc_s.shape, F32)

    def step(masked):
        if masked:
            q_pos = q_lo + lax.broadcasted_iota(jnp.int32, (tq, tk), 0)
            k_pos = k_lo + lax.broadcasted_iota(jnp.int32, (tq, tk), 1)
            if chunk_causal:
                shift = CHUNK.bit_length() - 1
                mask = jnp.right_shift(k_pos, shift) <= jnp.right_shift(q_pos, shift)
            else:
                mask = k_pos <= q_pos
        for h in range(nh):
            q = q_ref[:, h * dq:(h + 1) * dq]
            k = k_ref[:, h * dq:(h + 1) * dq]
            v = v_ref[:, h * dv:(h + 1) * dv]
            s = lax.dot_general(q, k, (((1,), (1,)), ((), ())), preferred_element_type=F32)
            if forget:
                s = s + fq_ref[:, h:h + 1] - fk_ref[h:h + 1, :]
            if masked:
                s = jnp.where(mask, s, NEG_INF)
            m_prev = m_s[h]
            m_new = jnp.maximum(m_prev, jnp.max(s, axis=-1, keepdims=True))
            alpha = jnp.exp(m_prev - m_new)
            p = jnp.exp(s - m_new)
            l_s[h] = alpha * l_s[h] + jnp.sum(p, axis=-1, keepdims=True)
            acc_s[h] = alpha * acc_s[h] + jnp.dot(p.astype(BF16), v, preferred_element_type=F32)
            m_s[h] = m_new

    @pl.when(needed & unmasked)
    def _plain():
        step(False)

    @pl.when(needed & jnp.logical_not(unmasked))
    def _masked():
        step(True)

    @pl.when(ik == nk - 1)
    def _fin():
        for h in range(nh):
            o_ref[:, h * dv:(h + 1) * dv] = (acc_s[h] / l_s[h]).astype(o_ref.dtype)


def _flash(q, k, v, *, nh, dq, dv, n_hblk, k_col0, v_col0, tq, tk, q_off, chunk_causal,
           fq=None, fk=None):
    b, t_q = q.shape[0], q.shape[1]
    t_k = k.shape[1]
    tq = tq if t_q % tq == 0 else t_q
    tk = tk if t_k % tk == 0 else t_k
    nq, nk = t_q // tq, t_k // tk
    forget = fq is not None

    def last_blk(i):
        q_hi = q_off + (i + 1) * tq - 1
        vis = (q_hi // CHUNK) * CHUNK + CHUNK - 1 if chunk_causal else q_hi
        return jnp.minimum(vis // tk, nk - 1)

    in_specs = [
        pl.BlockSpec((None, tq, nh * dq), lambda bi, h, i, j: (bi, i, h)),
        pl.BlockSpec((None, tk, nh * dq), lambda bi, h, i, j: (bi, jnp.minimum(j, last_blk(i)), k_col0 + h)),
        pl.BlockSpec((None, tk, nh * dv), lambda bi, h, i, j: (bi, jnp.minimum(j, last_blk(i)), v_col0 + h)),
    ]
    args = [q, k, v]
    if forget:
        in_specs += [
            pl.BlockSpec((None, None, tq, LANES), lambda bi, h, i, j: (bi, h, i, 0)),
            pl.BlockSpec((None, None, SUBLANES, tk),
                         lambda bi, h, i, j: (bi, h, 0, jnp.minimum(j, last_blk(i)))),
        ]
        args += [fq, fk]
    kern = functools.partial(_flash_kernel, nh=nh, dq=dq, dv=dv, tq=tq, tk=tk, nk=nk, q_off=q_off,
                             chunk_causal=chunk_causal, forget=forget)
    return pl.pallas_call(
        kern,
        grid=(b, n_hblk, nq, nk),
        in_specs=in_specs,
        out_specs=pl.BlockSpec((None, tq, nh * dv), lambda bi, h, i, j: (bi, i, h)),
        out_shape=jax.ShapeDtypeStruct((b, t_q, n_hblk * nh * dv), BF16),
        scratch_shapes=[pltpu.VMEM((nh, tq, 1), F32), pltpu.VMEM((nh, tq, 1), F32),
                        pltpu.VMEM((nh, tq, dv), F32)],
        compiler_params=_params("parallel", "parallel", "parallel", "arbitrary"),
        name="flash_fox" if forget else "flash_mla",
    )(*args)


FLASH_T = 512


def _flash_t_kernel(q_ref, k_ref, vt_ref, o_ref, m_s, l_s, acc_s, *, nh, dq, dv, t, chunk_causal):
    iq = pl.program_id(2)
    m_s[...] = jnp.full(m_s.shape, -jnp.inf, F32)
    l_s[...] = jnp.zeros(l_s.shape, F32)
    acc_s[...] = jnp.zeros(acc_s.shape, F32)

    def block(j, masked):
        row0 = pl.multiple_of(j * t, t)
        for h in range(nh):
            s = lax.dot_general(k_ref[pl.ds(row0, t), h * dq:(h + 1) * dq], q_ref[:, h * dq:(h + 1) * dq],
                                (((1,), (1,)), ((), ())), preferred_element_type=F32)
            if masked:
                k_pos = lax.broadcasted_iota(jnp.int32, (t, t), 0)
                q_pos = lax.broadcasted_iota(jnp.int32, (t, t), 1)
                if chunk_causal:
                    shift = CHUNK.bit_length() - 1
                    mask = jnp.right_shift(k_pos, shift) <= jnp.right_shift(q_pos, shift)
                else:
                    mask = k_pos <= q_pos
                s = jnp.where(mask, s, NEG_INF)
            m_prev = m_s[h]
            m_new = jnp.maximum(m_prev, jnp.max(s, axis=0, keepdims=True))
            alpha = jnp.exp(m_prev - m_new)
            p = jnp.exp(s - m_new)
            l_s[h] = alpha * l_s[h] + jnp.sum(p, axis=0, keepdims=True)
            acc_s[h] = alpha * acc_s[h] + jnp.dot(vt_ref[j, h * dv:(h + 1) * dv, :], p.astype(BF16),
                                                  preferred_element_type=F32)
            m_s[h] = m_new

    def full_block(j, carry):
        block(j, False)
        return carry

    lax.fori_loop(0, iq, full_block, 0)
    block(iq, True)
    out = jnp.concatenate([acc_s[h] / l_s[h] for h in range(nh)], axis=0)
    o_ref[...] = out.T.astype(o_ref.dtype)


def _flash_t(q, k, k_col0, vt, *, nh, dq, dv, n_hblk, chunk_causal):
    b, t_all = q.shape[0], q.shape[1]
    t = FLASH_T
    nblk = t_all // t
    kern = functools.partial(_flash_t_kernel, nh=nh, dq=dq, dv=dv, t=t, chunk_causal=chunk_causal)
    return pl.pallas_call(
        kern,
        grid=(b, n_hblk, nblk),
        in_specs=[pl.BlockSpec((None, t, nh * dq), lambda bi, h, i: (bi, i, h)),
                  pl.BlockSpec((None, t_all, nh * dq), lambda bi, h, i: (bi, 0, k_col0 + h)),
                  pl.BlockSpec((None, nblk, nh * dv, t), lambda bi, h, i: (bi, 0, h, 0))],
        out_specs=pl.BlockSpec((None, t, nh * dv), lambda bi, h, i: (bi, i, h)),
        out_shape=jax.ShapeDtypeStruct((b, t_all, n_hblk * nh * dv), BF16),
        scratch_shapes=[pltpu.VMEM((nh, 1, t), F32), pltpu.VMEM((nh, 1, t), F32),
                        pltpu.VMEM((nh, dv, t), F32)],
        compiler_params=_params("parallel", "parallel", "parallel"),
        name="flash_t_chunk" if chunk_causal else "flash_t_frame",
    )(q, k, vt)


def _key_block_transpose(v, t):
    b, t_all, c = v.shape
    return v.reshape(b, t_all // t, t, c).transpose(0, 1, 3, 2)


B_HEADS_PER_STEP = LANES // B_HEAD_DIM


def _band_kernel(*refs, nkb, tkb, clamp_front):
    q_ref = refs[0]
    k_refs = refs[1:1 + nkb]
    v_refs = refs[1 + nkb:1 + 2 * nkb]
    bias_ref, o_ref = refs[1 + 2 * nkb], refs[2 + 2 * nkb]
    iq = pl.program_id(2)
    dh = B_HEAD_DIM
    for h in range(B_HEADS_PER_STEP):
        q = q_ref[:, h * dh:(h + 1) * dh]
        ss = []
        for j in range(nkb):
            s = lax.dot_general(q, k_refs[j][:, h * dh:(h + 1) * dh], (((1,), (1,)), ((), ())),
                                preferred_element_type=F32)
            s = s + bias_ref[h, :, j * tkb:(j + 1) * tkb]
            if clamp_front and j < nkb - 1:
                s = jnp.where(iq >= nkb - 1 - j, s, NEG_INF)
            ss.append(s)
        m = functools.reduce(jnp.maximum, [jnp.max(s, axis=-1, keepdims=True) for s in ss])
        ps = [jnp.exp(s - m) for s in ss]
        l = functools.reduce(lambda a, c: a + c, [jnp.sum(p, axis=-1, keepdims=True) for p in ps])
        o = functools.reduce(lambda a, c: a + c, [
            jnp.dot(ps[j].astype(BF16), v_refs[j][:, h * dh:(h + 1) * dh], preferred_element_type=F32)
            for j in range(nkb)])
        o_ref[:, h * dh:(h + 1) * dh] = (o / l).astype(o_ref.dtype)


def _band_bias(rel_bias, q_pos, k_pos):
    rel = np.clip(q_pos[:, None] - k_pos[None, :], -B_REL_CLIP, B_REL_CLIP) + B_REL_CLIP
    qc = q_pos[:, None] // CHUNK
    kc = k_pos[None, :] // CHUNK
    mask = (kc <= qc) & (kc >= qc - B_LEFT_CHUNKS) & (k_pos[None, :] >= 0)
    return jnp.where(jnp.asarray(mask)[None], rel_bias[:, jnp.asarray(rel)], NEG_INF).astype(F32)


def _band(q, k, v, k_col0, v_col0, bias, *, tq, tkb, nkb, clamp_front):
    b, t_q = q.shape[0], q.shape[1]
    nq = t_q // tq
    n_hblk = B_HEADS // B_HEADS_PER_STEP

    def kv_spec(j, col0):
        back = nkb - 1 - j
        return pl.BlockSpec((None, tkb, LANES),
                            lambda bi, h, i: (bi, jnp.maximum(i - back, 0), col0 + h))

    in_specs = ([pl.BlockSpec((None, tq, LANES), lambda bi, h, i: (bi, i, h))]
                + [kv_spec(j, k_col0) for j in range(nkb)]
                + [kv_spec(j, v_col0) for j in range(nkb)]
                + [pl.BlockSpec((B_HEADS_PER_STEP, tq, nkb * tkb), lambda bi, h, i: (h, 0, 0))])
    kern = functools.partial(_band_kernel, nkb=nkb, tkb=tkb, clamp_front=clamp_front)
    return pl.pallas_call(
        kern,
        grid=(b, n_hblk, nq),
        in_specs=in_specs,
        out_specs=pl.BlockSpec((None, tq, LANES), lambda bi, h, i: (bi, i, h)),
        out_shape=jax.ShapeDtypeStruct((b, t_q, D_MODEL), BF16),
        compiler_params=_params("parallel", "parallel", "parallel"),
        name="band_attn",
    )(q, *([k] * nkb), *([v] * nkb), bias)


def _logf_kernel(x_ref, w_ref, b_ref, o_ref):
    z = jnp.dot(x_ref[...], w_ref[...], preferred_element_type=F32) + b_ref[...]
    o_ref[...] = -(jnp.maximum(-z, 0.0) + jnp.log1p(jnp.exp(-jnp.abs(z))))


def _logf(xb, w_pad, b_pad, tm=512):
    m = xb.shape[0]
    tm = _row_tile(m, tm)
    return pl.pallas_call(
        _logf_kernel,
        grid=(m // tm,),
        in_specs=[pl.BlockSpec((tm, D_MODEL), lambda i: (i, 0)),
                  pl.BlockSpec((D_MODEL, LANES), lambda i: (0, 0)),
                  pl.BlockSpec((1, LANES), lambda i: (0, 0))],
        out_specs=pl.BlockSpec((tm, LANES), lambda i: (i, 0)),
        out_shape=jax.ShapeDtypeStruct((m, LANES), F32),
        compiler_params=_params("parallel"),
        name="logf",
    )(xb, w_pad, b_pad)


def _split3(x):
    hi = x.astype(BF16)
    r = x - hi.astype(F32)
    mid = r.astype(BF16)
    lo = (r - mid.astype(F32)).astype(BF16)
    return hi, mid, lo


def _cumsum_kernel(x_ref, o_ref, hi_ref, mid_ref, lo_ref, carry, *, tc):
    @pl.when(pl.program_id(1) == 0)
    def _():
        carry[...] = jnp.zeros(carry.shape, F32)

    tri = (lax.broadcasted_iota(jnp.int32, (tc, tc), 0)
           >= lax.broadcasted_iota(jnp.int32, (tc, tc), 1)).astype(BF16)
    c = functools.reduce(lambda a, b: a + b, [jnp.dot(tri, piece, preferred_element_type=F32)
                                              for piece in _split3(x_ref[...])])
    out = c + carry[0:1, :]
    o_ref[...] = out
    hi_ref[...], mid_ref[...], lo_ref[...] = _split3(out)
    carry[...] = jnp.broadcast_to(out[tc - 1:tc, :], carry.shape)


def _cumsum(x, tc=256):
    b, t, _ = x.shape
    spec = pl.BlockSpec((None, tc, LANES), lambda bi, i: (bi, i, 0))
    piece = jax.ShapeDtypeStruct(x.shape, BF16)
    return pl.pallas_call(
        functools.partial(_cumsum_kernel, tc=tc),
        grid=(b, t // tc),
        in_specs=[spec],
        out_specs=[spec] * 4,
        out_shape=[jax.ShapeDtypeStruct(x.shape, F32), piece, piece, piece],
        scratch_shapes=[pltpu.VMEM((SUBLANES, LANES), F32)],
        compiler_params=_params("parallel", "arbitrary"),
        name="cumsum",
    )(x)


FFN_TN = D_FF // 2
CONV_ROWS = SUBLANES


def _ffn_up_kernel(x_ref, wa_ref, wg_ref, ca_ref, cg_ref, pa_ref, pg_ref,
                   act_ref, sa_ref, sg_ref, hpa, hpg, *, tm):
    @pl.when(pl.program_id(2) == 0)
    def _():
        hpa[0:CONV_ROWS] = pa_ref[...]
        hpg[0:CONV_ROWS] = pg_ref[...]

    x = x_ref[...]

    def conv(w_ref, c_ref, s_ref, hp):
        hp[CONV_ROWS:CONV_ROWS + tm] = jnp.dot(x, w_ref[...], preferred_element_type=F32)
        hc = c_ref[CONV_W:CONV_W + 1]
        for j in range(CONV_W):
            off = CONV_ROWS - (CONV_W - 1) + j
            hc = hc + c_ref[j:j + 1] * hp[off:off + tm]
        tail = hp[tm:tm + CONV_ROWS]
        s_ref[...] = tail
        hp[0:CONV_ROWS] = tail
        return hc

    a = conv(wa_ref, ca_ref, sa_ref, hpa)
    g = conv(wg_ref, cg_ref, sg_ref, hpg)
    act_ref[...] = (g * jax.nn.sigmoid(g) * a).astype(BF16)


def _ffn_up(xb, w_up, conv_tab, past, tm=512):
    b, t, _ = xb.shape
    tm = _row_tile(t, tm)
    tn = FFN_TN
    nn = D_FF // tn
    a_col = lambda n, bi, ti: (0, n)
    g_col = lambda n, bi, ti: (0, n + nn)
    state = jax.ShapeDtypeStruct((b, CONV_ROWS, D_FF), F32)
    return pl.pallas_call(
        functools.partial(_ffn_up_kernel, tm=tm),
        grid=(nn, b, t // tm),
        in_specs=[pl.BlockSpec((None, tm, D_MODEL), lambda n, bi, ti: (bi, ti, 0)),
                  pl.BlockSpec((D_MODEL, tn), a_col), pl.BlockSpec((D_MODEL, tn), g_col),
                  pl.BlockSpec((CONV_ROWS, tn), a_col), pl.BlockSpec((CONV_ROWS, tn), g_col),
                  pl.BlockSpec((None, CONV_ROWS, tn), lambda n, bi, ti: (bi, 0, n)),
                  pl.BlockSpec((None, CONV_ROWS, tn), lambda n, bi, ti: (bi, 0, n + nn))],
        out_specs=[pl.BlockSpec((None, tm, tn), lambda n, bi, ti: (bi, ti, n)),
                   pl.BlockSpec((None, CONV_ROWS, tn), lambda n, bi, ti: (bi, 0, n)),
                   pl.BlockSpec((None, CONV_ROWS, tn), lambda n, bi, ti: (bi, 0, n))],
        out_shape=[jax.ShapeDtypeStruct((b, t, D_FF), BF16), state, state],
        scratch_shapes=[pltpu.VMEM((tm + CONV_ROWS, tn), F32), pltpu.VMEM((tm + CONV_ROWS, tn), F32)],
        compiler_params=_params("parallel", "parallel", "arbitrary"),
        name="ffn_up",
    )(xb, w_up, w_up, conv_tab, conv_tab, past, past)


def _rope_tables(pos, batch):
    half = A_ROPE // 2
    inv_freq = ROPE_THETA ** (-jnp.arange(half, dtype=F32) / half)
    ang = pos.astype(F32)[:, None] * inv_freq
    cos, sin = jnp.cos(ang), jnp.sin(ang)
    cosk = jnp.concatenate([cos, cos], axis=-1)
    sink = jnp.concatenate([sin, sin], axis=-1)
    t = pos.shape[0]
    scale = (A_NOPE + A_ROPE) ** -0.5
    pad = jnp.zeros((t, A_QK_PAD - A_NOPE - A_ROPE), F32)
    cosq = scale * jnp.concatenate([jnp.ones((t, A_NOPE), F32), cosk, pad], axis=-1)
    sinq = scale * jnp.concatenate([jnp.zeros((t, A_NOPE), F32), sink, pad], axis=-1)
    return tuple(jnp.tile(a, (batch, 1)) for a in (cosq, sinq, cosk, sink))


def _swap_halves(w):
    half = w.shape[-1] // 2
    return jnp.concatenate([-w[..., half:], w[..., :half]], axis=-1)


def _mla_weights(w_dq, w_dkv, w_kr, w_uq, w_uk, w_uv):
    zc = jnp.zeros((D_MODEL, LANES - A_ROPE), F32)
    w1 = jnp.concatenate([w_dq, w_dkv, w_kr, zc, _swap_halves(w_kr), zc], axis=1).astype(BF16)
    wq = w_uq.reshape(A_Q_LORA, A_HEADS, A_NOPE + A_ROPE)
    nope, rope = wq[..., :A_NOPE], wq[..., A_NOPE:]
    zpad = jnp.zeros((A_Q_LORA, A_HEADS, A_QK_PAD - A_NOPE - A_ROPE), F32)
    w_cat = jnp.concatenate([nope, rope, zpad], axis=-1).reshape(A_Q_LORA, -1)
    w_sw = jnp.concatenate([jnp.zeros_like(nope), _swap_halves(rope), zpad], axis=-1).reshape(A_Q_LORA, -1)
    wq2 = jnp.concatenate([w_cat, w_sw], axis=1).astype(BF16)
    wk = jnp.zeros((A_LAT_PAD, A_HEADS, A_QK_PAD), F32)
    wk = wk.at[:A_KV_LORA, :, :A_NOPE].set(w_uk)
    eye = jnp.broadcast_to(jnp.eye(A_ROPE, dtype=F32)[:, None, :], (A_ROPE, A_HEADS, A_ROPE))
    wk = wk.at[A_KV_LORA:A_KV_LORA + A_ROPE, :, A_NOPE:A_NOPE + A_ROPE].set(eye)
    wv = jnp.zeros((A_LAT_PAD, A_HEADS * A_V), F32).at[:A_KV_LORA].set(w_uv.reshape(A_KV_LORA, -1))
    wkv = jnp.concatenate([wk.reshape(A_LAT_PAD, -1), wv], axis=1).astype(BF16)
    return w1, wq2, wkv


def _mla_mixer(xb, b, t, pos, ckv_past, kpe_past, w, i):
    w1, wq2, wkv = _mla_weights(w['a_w_dq'][i], w['a_w_dkv'][i], w['a_w_kr'][i], w['a_w_uq'][i],
                                w['a_w_uk'][i], w['a_w_uv'][i])
    q, lat, ckv, kpe = _mla_proj(xb, w1, w['a_g_q'][i], w['a_g_kv'][i], wq2, *_rope_tables(pos, b))
    lat = lat.reshape(b, t, A_LAT_PAD)
    q_off = 0
    if ckv_past is not None:
        p_len = ckv_past.shape[1]
        past = jnp.concatenate(
            [ckv_past, kpe_past, jnp.zeros((b, p_len, A_LAT_PAD - A_KV_LORA - A_ROPE), F32)], axis=-1)
        lat = jnp.concatenate([past.astype(BF16), lat], axis=1)
        q_off = p_len
    t_k = lat.shape[1]
    (kv,) = _mm(lat.reshape(b * t_k, A_LAT_PAD), wkv, [BF16])
    kv = kv.reshape(b, t_k, -1)
    if ckv_past is None and t % FLASH_T == 0:
        vt = _key_block_transpose(kv[:, :, A_HEADS * A_QK_PAD:], FLASH_T)
        o = _flash_t(q.reshape(b, t, -1), kv, 0, vt, nh=1, dq=A_QK_PAD, dv=A_V, n_hblk=A_HEADS,
                     chunk_causal=True)
    else:
        o = _flash(q.reshape(b, t, -1), kv, kv, nh=1, dq=A_QK_PAD, dv=A_V, n_hblk=A_HEADS,
                   k_col0=0, v_col0=A_HEADS * A_QK_PAD // A_V, tq=512, tk=512, q_off=q_off,
                   chunk_causal=True)
    return o.reshape(b * t, A_HEADS * A_V), ckv.reshape(b, t, -1), kpe.reshape(b, t, -1)


def _qkv(xb, w_qkv, head_dim):
    wq = (w_qkv[:, :D_MODEL] * head_dim ** -0.5).astype(BF16)
    (q,) = _mm(xb, wq, [BF16])
    kv32, kvb = _mm(xb, w_qkv[:, D_MODEL:].astype(BF16), [F32, BF16])
    return q, kv32, kvb


def _band_mixer(xb, b, t, pos0, k_past, v_past, w, i):
    q, kv32, kvb = _qkv(xb, w['b_w_qkv'][i], B_HEAD_DIM)
    q = q.reshape(b, t, D_MODEL)
    k32 = kv32[:, :D_MODEL].reshape(b, t, B_HEADS, B_HEAD_DIM)
    v32 = kv32[:, D_MODEL:].reshape(b, t, B_HEADS, B_HEAD_DIM)
    n_cols = D_MODEL // LANES
    if k_past is None:
        tq = 4 * CHUNK
        nkb = B_WIN // tq + 1
        bias = _band_bias(w['b_rel_bias'][i], B_WIN + np.arange(tq), np.arange(B_WIN + tq))
        kvb = kvb.reshape(b, t, 2 * D_MODEL)
        o = _band(q, kvb, kvb, 0, n_cols, bias, tq=tq, tkb=tq, nkb=nkb, clamp_front=True)
        keep = min(B_WIN, t)
        k_new, v_new = k32[:, t - keep:], v32[:, t - keep:]
    else:
        p_len = k_past.shape[1]
        kvb = kvb.reshape(b, t, 2 * D_MODEL)
        past = jnp.concatenate([k_past.reshape(b, p_len, D_MODEL), v_past.reshape(b, p_len, D_MODEL)],
                               axis=-1).astype(BF16)
        kv_all = jnp.concatenate([past, kvb], axis=1)
        q_pos = pos0 + np.arange(t)
        k_pos = np.concatenate([np.arange(pos0 - p_len, pos0), q_pos])
        bias = _band_bias(w['b_rel_bias'][i], q_pos, k_pos)
        o = _band(q, kv_all, kv_all, 0, n_cols, bias, tq=t, tkb=p_len + t, nkb=1, clamp_front=False)
        k_new, v_new = k32, v32
    return o.reshape(b * t, D_MODEL), k_new, v_new


def _fox_mixer(xb, b, t, k_past, v_past, lf_past, w, i):
    q, kv32, kvb = _qkv(xb, w['c_w_qkv'][i], C_HEAD_DIM)
    k32 = kv32[:, :D_MODEL].reshape(b, t, C_HEADS, C_HEAD_DIM)
    v32 = kv32[:, D_MODEL:].reshape(b, t, C_HEADS, C_HEAD_DIM)
    w_f = jnp.zeros((D_MODEL, LANES), F32).at[:, :C_HEADS].set(w['c_w_f'][i]).astype(BF16)
    b_f = jnp.zeros((1, LANES), F32).at[0, :C_HEADS].set(w['c_b_f'][i])
    log_f = _logf(xb, w_f, b_f).reshape(b, t, LANES)
    kvb = kvb.reshape(b, t, 2 * D_MODEL)
    lf_all = log_f
    q_off = 0
    if k_past is not None:
        p_len = k_past.shape[1]
        past = jnp.concatenate([k_past.reshape(b, p_len, D_MODEL), v_past.reshape(b, p_len, D_MODEL)],
                               axis=-1).astype(BF16)
        kvb = jnp.concatenate([past, kvb], axis=1)
        lf_all = jnp.concatenate([jnp.pad(lf_past, ((0, 0), (0, 0), (0, LANES - C_HEADS))), log_f], axis=1)
        q_off = p_len
    t_k = lf_all.shape[1]
    tc = 256
    t_pad = -(-t_k // tc) * tc
    f_cum, *f_pieces = _cumsum(jnp.pad(lf_all, ((0, 0), (0, t_pad - t_k), (0, 0))), tc)
    f_cum = f_cum[:, :t_k, :C_HEADS]
    nh = LANES // C_HEAD_DIM
    n_hblk = C_HEADS // nh
    if k_past is None and t % FLASH_T == 0:
        f3 = jnp.stack([p[:, :t_k, :C_HEADS] for p in f_pieces], axis=-1)
        ones = jnp.ones_like(f3)
        pad = jnp.zeros((b, t, C_HEADS, LANES - C_HEAD_DIM - 2 * f3.shape[-1]), BF16)
        q_aug = jnp.concatenate([q.reshape(b, t, C_HEADS, C_HEAD_DIM), ones, f3, pad], axis=-1)
        k_aug = jnp.concatenate([kvb[:, :, :D_MODEL].reshape(b, t, C_HEADS, C_HEAD_DIM), -f3, ones, pad],
                                axis=-1)
        vt = _key_block_transpose(kvb[:, :, D_MODEL:], FLASH_T)
        o = _flash_t(q_aug.reshape(b, t, C_HEADS * LANES), k_aug.reshape(b, t, C_HEADS * LANES), 0, vt,
                     nh=nh, dq=LANES, dv=C_HEAD_DIM, n_hblk=n_hblk, chunk_causal=False)
        return o.reshape(b * t, D_MODEL), k32, v32, log_f[:, :, :C_HEADS]
    f_blk = f_cum.reshape(b, t_k, n_hblk, nh).transpose(0, 2, 1, 3)
    fq = jnp.pad(f_blk[:, :, t_k - t:], ((0, 0), (0, 0), (0, 0), (0, LANES - nh)))
    fk = jnp.pad(f_blk.transpose(0, 1, 3, 2), ((0, 0), (0, 0), (0, SUBLANES - nh), (0, 0)))
    o = _flash(q.reshape(b, t, D_MODEL), kvb, kvb, nh=nh, dq=C_HEAD_DIM, dv=C_HEAD_DIM, n_hblk=n_hblk,
               k_col0=0, v_col0=n_hblk, tq=512, tk=512, q_off=q_off, chunk_causal=False, fq=fq, fk=fk)
    return o.reshape(b * t, D_MODEL), k32, v32, log_f[:, :, :C_HEADS]


def _conv_ffn(xb, b, t, conv_past, w, i):
    tab = jnp.concatenate([w['f_conv_w'][i], w['f_conv_b'][i][None],
                           jnp.zeros((CONV_ROWS - CONV_W - 1, 2 * D_FF), F32)], axis=0)
    if conv_past is None:
        past = jnp.zeros((b, CONV_ROWS, 2 * D_FF), F32)
    else:
        past = jnp.pad(conv_past, ((0, 0), (CONV_ROWS - (CONV_W - 1), 0), (0, 0)))
    act, sa, sg = _ffn_up(xb.reshape(b, t, D_MODEL), w['f_w_up'][i].astype(BF16), tab, past)
    state = jnp.concatenate([sa, sg], axis=-1)[:, CONV_ROWS - (CONV_W - 1):]
    return act.reshape(b * t, D_FF), state


def _trunk(x, pos0, past, w):
    b, t, _ = x.shape
    pos = pos0 + jnp.arange(t)
    xf = x.reshape(b * t, D_MODEL)
    xb = xf.astype(BF16)
    outs = {n: [] for n in ('a_ckv', 'a_kpe', 'b_k', 'b_v', 'c_k', 'c_v', 'c_logf', 'ffn_conv')}
    ia = ib = ic = 0
    get = lambda name, j: None if past is None else past[name][j]
    for i in range(DEPTH):
        kind = i % N_MIXERS
        if kind == 0:
            o, ckv, kpe = _mla_mixer(xb, b, t, pos, get('a_ckv', ia), get('a_kpe', ia), w, ia)
            outs['a_ckv'].append(ckv)
            outs['a_kpe'].append(kpe)
            w_o = w['a_w_o'][ia]
            ia += 1
        elif kind == 1:
            o, kb, vb = _band_mixer(xb, b, t, pos0, get('b_k', ib), get('b_v', ib), w, ib)
            outs['b_k'].append(kb)
            outs['b_v'].append(vb)
            w_o = w['b_w_o'][ib]
            ib += 1
        else:
            o, kc, vc, lf = _fox_mixer(xb, b, t, get('c_k', ic), get('c_v', ic), get('c_logf', ic), w, ic)
            outs['c_k'].append(kc)
            outs['c_v'].append(vc)
            outs['c_logf'].append(lf)
            w_o = w['c_w_o'][ic]
            ic += 1
        xf, xb = _mm_res_ln(o, w_o.astype(BF16), xf, w['ln1_g'][i], w['ln1_b'][i])
        act, conv_state = _conv_ffn(xb, b, t, get('ffn_conv', i), w, i)
        outs['ffn_conv'].append(conv_state)
        xf, xb = _mm_res_ln(act, w['f_w_down'][i].astype(BF16), xf, w['ln2_g'][i], w['ln2_b'][i])
    return xf.reshape(b, t, D_MODEL), {n: jnp.stack(v) for n, v in outs.items()}


def kernel(x_prompt, x_sample, cache_a_ckv, cache_a_kpe, cache_b_k, cache_b_v, cache_c_k, cache_c_v,
           cache_c_logf, state_ffn_conv, a_w_dq, a_g_q, a_w_uq, a_w_dkv, a_g_kv, a_w_kr, a_w_uk, a_w_uv,
           a_w_o, b_w_qkv, b_rel_bias, b_w_o, c_w_qkv, c_w_f, c_b_f, c_w_o, f_w_up, f_conv_w, f_conv_b,
           f_w_down, ln1_g, ln1_b, ln2_g, ln2_b):
    w = dict(a_w_dq=a_w_dq, a_g_q=a_g_q, a_w_uq=a_w_uq, a_w_dkv=a_w_dkv, a_g_kv=a_g_kv, a_w_kr=a_w_kr,
             a_w_uk=a_w_uk, a_w_uv=a_w_uv, a_w_o=a_w_o, b_w_qkv=b_w_qkv, b_rel_bias=b_rel_bias, b_w_o=b_w_o,
             c_w_qkv=c_w_qkv, c_w_f=c_w_f, c_b_f=c_b_f, c_w_o=c_w_o, f_w_up=f_w_up, f_conv_w=f_conv_w,
             f_conv_b=f_conv_b, f_w_down=f_w_down, ln1_g=ln1_g, ln1_b=ln1_b, ln2_g=ln2_g, ln2_b=ln2_b)
    past = dict(a_ckv=cache_a_ckv, a_kpe=cache_a_kpe, b_k=cache_b_k, b_v=cache_b_v, c_k=cache_c_k,
                c_v=cache_c_v, c_logf=cache_c_logf, ffn_conv=state_ffn_conv)
    past_len = cache_a_ckv.shape[2]
    y_prompt, p = _trunk(x_prompt, 0, None, w)
    y_sample, s = _trunk(x_sample, past_len, past, w)
    names = ('a_ckv', 'a_kpe', 'b_k', 'b_v', 'c_k', 'c_v', 'c_logf', 'ffn_conv')
    return (y_prompt, y_sample) + tuple(p[n] for n in names) + tuple(s[n] for n in names)
```

```python
import functools
import math

import numpy as np
import jax
import jax.numpy as jnp
from jax import lax
from jax.experimental import pallas as pl
from jax.experimental.pallas import tpu as pltpu

F32 = jnp.float32
BF16 = jnp.bfloat16

D_MODEL = 1024
DEPTH = 4
CHUNK = 64
N_MIXERS = 3

A_HEADS = 8
A_Q_LORA = 384
A_KV_LORA = 256
A_NOPE = 128
A_ROPE = 64
A_V = 128
A_QK_PAD = 256
A_LAT_PAD = 384
ROPE_THETA = 10000.0

B_HEADS = 16
B_HEAD_DIM = D_MODEL // B_HEADS
B_LEFT_CHUNKS = 8
B_WIN = B_LEFT_CHUNKS * CHUNK
B_REL_CLIP = 128

C_HEADS = 16
C_HEAD_DIM = D_MODEL // C_HEADS

D_FF = 2816
CONV_W = 3

ALPHA = (2.0 * DEPTH) ** 0.25
LN_EPS = 1e-5
RMS_EPS = 1e-6
NEG_INF = -1e30

LANES = 128
SUBLANES = 8
VMEM_LIMIT_BYTES = 48 * 2 ** 20


def _params(*sem):
    return pltpu.CompilerParams(dimension_semantics=sem, vmem_limit_bytes=VMEM_LIMIT_BYTES)


def _row_tile(m, tm):
    while m % tm:
        tm //= 2
    assert tm % SUBLANES == 0, (m, tm)
    return tm


def _mm_kernel(x_ref, w_ref, *o_refs):
    acc = jnp.dot(x_ref[...], w_ref[...], preferred_element_type=F32)
    for o_ref in o_refs:
        o_ref[...] = acc.astype(o_ref.dtype)


def _mm(x, w, out_dtypes, tm=512, tn=1024):
    m, k = x.shape
    n = w.shape[1]
    tm, tn = _row_tile(m, tm), min(tn, n)
    return pl.pallas_call(
        _mm_kernel,
        grid=(m // tm, n // tn),
        in_specs=[pl.BlockSpec((tm, k), lambda i, j: (i, 0)),
                  pl.BlockSpec((k, tn), lambda i, j: (0, j))],
        out_specs=[pl.BlockSpec((tm, tn), lambda i, j: (i, j)) for _ in out_dtypes],
        out_shape=[jax.ShapeDtypeStruct((m, n), d) for d in out_dtypes],
        compiler_params=_params("parallel", "parallel"),
        name="mm",
    )(x, w)


def _mm_res_ln_kernel(a_ref, w_ref, x_ref, g_ref, b_ref, of_ref, ob_ref):
    y = ALPHA * x_ref[...] + jnp.dot(a_ref[...], w_ref[...], preferred_element_type=F32)
    mu = jnp.mean(y, axis=-1, keepdims=True)
    d = y - mu
    var = jnp.mean(d * d, axis=-1, keepdims=True)
    out = d * lax.rsqrt(var + LN_EPS) * g_ref[...] + b_ref[...]
    of_ref[...] = out
    ob_ref[...] = out.astype(BF16)


def _mm_res_ln(a, w, x, g, b, tm=512):
    m, k = a.shape
    n = w.shape[1]
    tm = _row_tile(m, tm)
    row = lambda i: (i, 0)
    fixed = lambda i: (0, 0)
    return pl.pallas_call(
        _mm_res_ln_kernel,
        grid=(m // tm,),
        in_specs=[pl.BlockSpec((tm, k), row), pl.BlockSpec((k, n), fixed),
                  pl.BlockSpec((tm, n), row), pl.BlockSpec((1, n), fixed),
                  pl.BlockSpec((1, n), fixed)],
        out_specs=[pl.BlockSpec((tm, n), row), pl.BlockSpec((tm, n), row)],
        out_shape=[jax.ShapeDtypeStruct((m, n), F32), jax.ShapeDtypeStruct((m, n), BF16)],
        compiler_params=_params("parallel"),
        name="mm_res_ln",
    )(a, w, x, g.reshape(1, n), b.reshape(1, n))


_W1_CQ = (0, A_Q_LORA)
_W1_CKV = (A_Q_LORA, A_Q_LORA + A_KV_LORA)
_W1_KR = (_W1_CKV[1], _W1_CKV[1] + A_ROPE)
_W1_KRS = (_W1_CKV[1] + LANES, _W1_CKV[1] + LANES + A_ROPE)
_W1_COLS = _W1_CKV[1] + 2 * LANES


def _rms(v, g):
    return v * lax.rsqrt(jnp.mean(v * v, axis=-1, keepdims=True) + RMS_EPS) * g


def _mla_proj_kernel(x_ref, w1_ref, gq_ref, gkv_ref, wq_ref, cq_ref, sq_ref, ck_ref, sk_ref,
                     q_ref, lat_ref, ckv_ref, kpe_ref):
    y = jnp.dot(x_ref[...], w1_ref[...], preferred_element_type=F32)
    cq = _rms(y[:, _W1_CQ[0]:_W1_CQ[1]], gq_ref[...]).astype(BF16)
    ckv = _rms(y[:, _W1_CKV[0]:_W1_CKV[1]], gkv_ref[...])
    kpe = y[:, _W1_KR[0]:_W1_KR[1]] * ck_ref[...] + y[:, _W1_KRS[0]:_W1_KRS[1]] * sk_ref[...]
    ckv_ref[...] = ckv
    kpe_ref[...] = kpe
    lat_ref[:, 0:A_KV_LORA] = ckv.astype(BF16)
    lat_ref[:, A_KV_LORA:A_KV_LORA + A_ROPE] = kpe.astype(BF16)
    lat_ref[:, A_KV_LORA + A_ROPE:] = jnp.zeros(
        (lat_ref.shape[0], A_LAT_PAD - A_KV_LORA - A_ROPE), BF16)
    sw0 = A_HEADS * A_QK_PAD
    for h in range(A_HEADS):
        lo, hi = h * A_QK_PAD, (h + 1) * A_QK_PAD
        qp = jnp.dot(cq, wq_ref[:, lo:hi], preferred_element_type=F32)
        qs = jnp.dot(cq, wq_ref[:, sw0 + lo:sw0 + hi], preferred_element_type=F32)
        q_ref[:, lo:hi] = (qp * cq_ref[...] + qs * sq_ref[...]).astype(BF16)


def _mla_proj(xb, w1, gq, gkv, wq, cosq, sinq, cosk, sink, tm=512):
    m = xb.shape[0]
    tm = _row_tile(m, tm)
    row = lambda i: (i, 0)
    fixed = lambda i: (0, 0)
    nq = A_HEADS * A_QK_PAD
    return pl.pallas_call(
        _mla_proj_kernel,
        grid=(m // tm,),
        in_specs=[pl.BlockSpec((tm, D_MODEL), row), pl.BlockSpec(w1.shape, fixed),
                  pl.BlockSpec((1, A_Q_LORA), fixed), pl.BlockSpec((1, A_KV_LORA), fixed),
                  pl.BlockSpec(wq.shape, fixed),
                  pl.BlockSpec((tm, A_QK_PAD), row), pl.BlockSpec((tm, A_QK_PAD), row),
                  pl.BlockSpec((tm, A_ROPE), row), pl.BlockSpec((tm, A_ROPE), row)],
        out_specs=[pl.BlockSpec((tm, nq), row), pl.BlockSpec((tm, A_LAT_PAD), row),
                   pl.BlockSpec((tm, A_KV_LORA), row), pl.BlockSpec((tm, A_ROPE), row)],
        out_shape=[jax.ShapeDtypeStruct((m, nq), BF16), jax.ShapeDtypeStruct((m, A_LAT_PAD), BF16),
                   jax.ShapeDtypeStruct((m, A_KV_LORA), F32), jax.ShapeDtypeStruct((m, A_ROPE), F32)],
        compiler_params=_params("parallel"),
        name="mla_proj",
    )(xb, w1, gq.reshape(1, -1), gkv.reshape(1, -1), wq, cosq, sinq, cosk, sink)


def _flash_kernel(*refs, nh, dq, dv, tq, tk, nk, q_off, chunk_causal, forget):
    if forget:
        q_ref, k_ref, v_ref, fq_ref, fk_ref, o_ref, m_s, l_s, acc_s = refs
    else:
        q_ref, k_ref, v_ref, o_ref, m_s, l_s, acc_s = refs
    iq = pl.program_id(2)
    ik = pl.program_id(3)
    q_lo = q_off + iq * tq
    q_hi = q_lo + tq - 1
    if chunk_causal:
        vis_lo = (q_lo // CHUNK) * CHUNK + CHUNK - 1
        vis_hi = (q_hi // CHUNK) * CHUNK + CHUNK - 1
    else:
        vis_lo, vis_hi = q_lo, q_hi
    k_lo = ik * tk
    needed = k_lo <= vis_hi
    unmasked = k_lo + tk - 1 <= vis_lo

    @pl.when(ik == 0)
    def _init():
        m_s[...] = jnp.full(m_s.shape, -jnp.inf, F32)
        l_s[...] = jnp.zeros(l_s.shape, F32)
        acc_s[...] = jnp.zeros(acc_s.shape, F32)

    def step(masked):
        if masked:
            q_pos = q_lo + lax.broadcasted_iota(jnp.int32, (tq, tk), 0)
            k_pos = k_lo + lax.broadcasted_iota(jnp.int32, (tq, tk), 1)
            if chunk_causal:
                shift = CHUNK.bit_length() - 1
                mask = jnp.right_shift(k_pos, shift) <= jnp.right_shift(q_pos, shift)
            else:
                mask = k_pos <= q_pos
        for h in range(nh):
            q = q_ref[:, h * dq:(h + 1) * dq]
            k = k_ref[:, h * dq:(h + 1) * dq]
            v = v_ref[:, h * dv:(h + 1) * dv]
            s = lax.dot_general(q, k, (((1,), (1,)), ((), ())), preferred_element_type=F32)
            if forget:
                s = s + fq_ref[:, h:h + 1] - fk_ref[h:h + 1, :]
            if masked:
                s = jnp.where(mask, s, NEG_INF)
            m_prev = m_s[h]
            m_new = jnp.maximum(m_prev, jnp.max(s, axis=-1, keepdims=True))
            alpha = jnp.exp(m_prev - m_new)
            p = jnp.exp(s - m_new)
            l_s[h] = alpha * l_s[h] + jnp.sum(p, axis=-1, keepdims=True)
            acc_s[h] = alpha * acc_s[h] + jnp.dot(p.astype(BF16), v, preferred_element_type=F32)
            m_s[h] = m_new

    @pl.when(needed & unmasked)
    def _plain():
        step(False)

    @pl.when(needed & jnp.logical_not(unmasked))
    def _masked():
        step(True)

    @pl.when(ik == nk - 1)
    def _fin():
        for h in range(nh):
            o_ref[:, h * dv:(h + 1) * dv] = (acc_s[h] / l_s[h]).astype(o_ref.dtype)


def _flash(q, k, v, *, nh, dq, dv, n_hblk, k_col0, v_col0, tq, tk, q_off, chunk_causal,
           fq=None, fk=None):
    b, t_q = q.shape[0], q.shape[1]
    t_k = k.shape[1]
    tq = tq if t_q % tq == 0 else t_q
    tk = tk if t_k % tk == 0 else t_k
    nq, nk = t_q // tq, t_k // tk
    forget = fq is not None

    def last_blk(i):
        q_hi = q_off + (i + 1) * tq - 1
        vis = (q_hi // CHUNK) * CHUNK + CHUNK - 1 if chunk_causal else q_hi
        return jnp.minimum(vis // tk, nk - 1)

    in_specs = [
        pl.BlockSpec((None, tq, nh * dq), lambda bi, h, i, j: (bi, i, h)),
        pl.BlockSpec((None, tk, nh * dq), lambda bi, h, i, j: (bi, jnp.minimum(j, last_blk(i)), k_col0 + h)),
        pl.BlockSpec((None, tk, nh * dv), lambda bi, h, i, j: (bi, jnp.minimum(j, last_blk(i)), v_col0 + h)),
    ]
    args = [q, k, v]
    if forget:
        in_specs += [
            pl.BlockSpec((None, None, tq, LANES), lambda bi, h, i, j: (bi, h, i, 0)),
            pl.BlockSpec((None, None, SUBLANES, tk),
                         lambda bi, h, i, j: (bi, h, 0, jnp.minimum(j, last_blk(i)))),
        ]
        args += [fq, fk]
    kern = functools.partial(_flash_kernel, nh=nh, dq=dq, dv=dv, tq=tq, tk=tk, nk=nk, q_off=q_off,
                             chunk_causal=chunk_causal, forget=forget)
    return pl.pallas_call(
        kern,
        grid=(b, n_hblk, nq, nk),
        in_specs=in_specs,
        out_specs=pl.BlockSpec((None, tq, nh * dv), lambda bi, h, i, j: (bi, i, h)),
        out_shape=jax.ShapeDtypeStruct((b, t_q, n_hblk * nh * dv), BF16),
        scratch_shapes=[pltpu.VMEM((nh, tq, 1), F32), pltpu.VMEM((nh, tq, 1), F32),
                        pltpu.VMEM((nh, tq, dv), F32)],
        compiler_params=_params("parallel", "parallel", "parallel", "arbitrary"),
        name="flash_fox" if forget else "flash_mla",
    )(*args)


FLASH_T = 1024
FLASH_QC = 256
FLASH_AHEAD = 3
LOG2E = math.log2(math.e)


def _flash_t_kernel(qt_ref, k_ref, vt_ref, o_ref, m_s, l_s, acc_s, *, nh, dq, dv, t, chunk_causal):
    iq = pl.program_id(2)
    m_s[...] = jnp.full(m_s.shape, -jnp.inf, F32)
    l_s[...] = jnp.zeros(l_s.shape, F32)
    acc_s[...] = jnp.zeros(acc_s.shape, F32)

    def block(j, masked):
        row0 = pl.multiple_of(j * t, t)
        qc = FLASH_QC
        chains = [(h, c) for h in range(nh) for c in range(t // qc)]

        def qk(h, c):
            return jnp.dot(k_ref[pl.ds(row0, t), h * dq:(h + 1) * dq],
                           qt_ref[h * dq:(h + 1) * dq, c * qc:(c + 1) * qc], preferred_element_type=F32)

        def softmax_pv(h, c, s):
            cols = slice(c * qc, (c + 1) * qc)
            if masked:
                k_pos = lax.broadcasted_iota(jnp.int32, (t, qc), 0)
                q_pos = lax.broadcasted_iota(jnp.int32, (t, qc), 1) + c * qc
                if chunk_causal:
                    shift = CHUNK.bit_length() - 1
                    mask = jnp.right_shift(k_pos, shift) <= jnp.right_shift(q_pos, shift)
                else:
                    mask = k_pos <= q_pos
                s = jnp.where(mask, s, NEG_INF)
            m_prev = m_s[h, :, cols]
            m_new = jnp.maximum(m_prev, jnp.max(s, axis=0, keepdims=True))
            alpha = jnp.exp2(m_prev - m_new)
            p = jnp.exp2(s - m_new)
            l_s[h, :, cols] = alpha * l_s[h, :, cols] + jnp.sum(p, axis=0, keepdims=True)
            acc_s[h, :, cols] = alpha * acc_s[h, :, cols] + jnp.dot(
                vt_ref[j, h * dv:(h + 1) * dv, :], p.astype(BF16), preferred_element_type=F32)
            m_s[h, :, cols] = m_new

        pending = [qk(*chain) for chain in chains[:FLASH_AHEAD]]
        for i, chain in enumerate(chains):
            if i + FLASH_AHEAD < len(chains):
                pending.append(qk(*chains[i + FLASH_AHEAD]))
            softmax_pv(*chain, pending.pop(0))

    def full_block(j, carry):
        block(j, False)
        return carry

    lax.fori_loop(0, iq, full_block, 0)
    block(iq, True)
    out = jnp.concatenate([acc_s[h] / l_s[h] for h in range(nh)], axis=0)
    o_ref[...] = out.T.astype(o_ref.dtype)


def _flash_t(q, k, k_col0, vt, *, nh, dq, dv, n_hblk, chunk_causal):
    b, t_all = q.shape[0], q.shape[1]
    t = FLASH_T
    nblk = t_all // t
    kern = functools.partial(_flash_t_kernel, nh=nh, dq=dq, dv=dv, t=t, chunk_causal=chunk_causal)
    qt = q.transpose(0, 2, 1)
    return pl.pallas_call(
        kern,
        grid=(b, n_hblk, nblk),
        in_specs=[pl.BlockSpec((None, nh * dq, t), lambda bi, h, i: (bi, h, i)),
                  pl.BlockSpec((None, t_all, nh * dq), lambda bi, h, i: (bi, 0, k_col0 + h)),
                  pl.BlockSpec((None, nblk, nh * dv, t), lambda bi, h, i: (bi, 0, h, 0))],
        out_specs=pl.BlockSpec((None, t, nh * dv), lambda bi, h, i: (bi, i, h)),
        out_shape=jax.ShapeDtypeStruct((b, t_all, n_hblk * nh * dv), BF16),
        scratch_shapes=[pltpu.VMEM((nh, 1, t), F32), pltpu.VMEM((nh, 1, t), F32),
                        pltpu.VMEM((nh, dv, t), F32)],
        compiler_params=_params("parallel", "parallel", "parallel"),
        name="flash_t_chunk" if chunk_causal else "flash_t_frame",
    )(qt, k, vt)


def _key_block_transpose(v, t):
    b, t_all, c = v.shape
    return v.reshape(b, t_all // t, t, c).transpose(0, 1, 3, 2)


B_HEADS_PER_STEP = LANES // B_HEAD_DIM


def _band_kernel(*refs, nkb, tkb, clamp_front):
    q_ref = refs[0]
    k_refs = refs[1:1 + nkb]
    v_refs = refs[1 + nkb:1 + 2 * nkb]
    bias_ref, o_ref = refs[1 + 2 * nkb], refs[2 + 2 * nkb]
    iq = pl.program_id(2)
    dh = B_HEAD_DIM
    for h in range(B_HEADS_PER_STEP):
        q = q_ref[:, h * dh:(h + 1) * dh]
        ss = []
        for j in range(nkb):
            s = lax.dot_general(q, k_refs[j][:, h * dh:(h + 1) * dh], (((1,), (1,)), ((), ())),
                                preferred_element_type=F32)
            s = s + bias_ref[h, :, j * tkb:(j + 1) * tkb]
            if clamp_front and j < nkb - 1:
                s = jnp.where(iq >= nkb - 1 - j, s, NEG_INF)
            ss.append(s)
        m = functools.reduce(jnp.maximum, [jnp.max(s, axis=-1, keepdims=True) for s in ss])
        ps = [jnp.exp(s - m) for s in ss]
        l = functools.reduce(lambda a, c: a + c, [jnp.sum(p, axis=-1, keepdims=True) for p in ps])
        o = functools.reduce(lambda a, c: a + c, [
            jnp.dot(ps[j].astype(BF16), v_refs[j][:, h * dh:(h + 1) * dh], preferred_element_type=F32)
            for j in range(nkb)])
        o_ref[:, h * dh:(h + 1) * dh] = (o / l).astype(o_ref.dtype)


def _band_bias(rel_bias, q_pos, k_pos):
    nq, nk = len(q_pos), len(k_pos)
    assert (np.diff(q_pos) == 1).all() and (np.diff(k_pos) == 1).all()
    m = np.arange(nq + nk - 1)
    u = rel_bias[:, np.clip(q_pos[0] - k_pos[0] + nq - 1 - m, -B_REL_CLIP, B_REL_CLIP) + B_REL_CLIP]
    period = nq + nk
    w = jnp.concatenate([u[:, nq - 1:], jnp.zeros((u.shape[0], 1), u.dtype), u[:, :nq - 1]], axis=1)
    skew = jnp.tile(w, (1, nq))[:, :nq * (period - 1)].reshape(-1, nq, period - 1)[:, :, :nk]
    qc = q_pos[:, None] // CHUNK
    kc = k_pos[None, :] // CHUNK
    mask = (kc <= qc) & (kc >= qc - B_LEFT_CHUNKS) & (k_pos[None, :] >= 0)
    return jnp.where(jnp.asarray(mask)[None], skew, NEG_INF).astype(F32)


def _band(q, k, v, k_col0, v_col0, bias, *, tq, tkb, nkb, clamp_front):
    b, t_q = q.shape[0], q.shape[1]
    nq = t_q // tq
    n_hblk = B_HEADS // B_HEADS_PER_STEP

    def kv_spec(j, col0):
        back = nkb - 1 - j
        return pl.BlockSpec((None, tkb, LANES),
                            lambda bi, h, i: (bi, jnp.maximum(i - back, 0), col0 + h))

    in_specs = ([pl.BlockSpec((None, tq, LANES), lambda bi, h, i: (bi, i, h))]
                + [kv_spec(j, k_col0) for j in range(nkb)]
                + [kv_spec(j, v_col0) for j in range(nkb)]
                + [pl.BlockSpec((B_HEADS_PER_STEP, tq, nkb * tkb), lambda bi, h, i: (h, 0, 0))])
    kern = functools.partial(_band_kernel, nkb=nkb, tkb=tkb, clamp_front=clamp_front)
    return pl.pallas_call(
        kern,
        grid=(b, n_hblk, nq),
        in_specs=in_specs,
        out_specs=pl.BlockSpec((None, tq, LANES), lambda bi, h, i: (bi, i, h)),
        out_shape=jax.ShapeDtypeStruct((b, t_q, D_MODEL), BF16),
        compiler_params=_params("parallel", "parallel", "parallel"),
        name="band_attn",
    )(q, *([k] * nkb), *([v] * nkb), bias)


def _logf_kernel(x_ref, w_ref, b_ref, o_ref):
    z = jnp.dot(x_ref[...], w_ref[...], preferred_element_type=F32) + b_ref[...]
    o_ref[...] = -(jnp.maximum(-z, 0.0) + jnp.log1p(jnp.exp(-jnp.abs(z))))


def _logf(xb, w_pad, b_pad, tm=512):
    m = xb.shape[0]
    tm = _row_tile(m, tm)
    return pl.pallas_call(
        _logf_kernel,
        grid=(m // tm,),
        in_specs=[pl.BlockSpec((tm, D_MODEL), lambda i: (i, 0)),
                  pl.BlockSpec((D_MODEL, LANES), lambda i: (0, 0)),
                  pl.BlockSpec((1, LANES), lambda i: (0, 0))],
        out_specs=pl.BlockSpec((tm, LANES), lambda i: (i, 0)),
        out_shape=jax.ShapeDtypeStruct((m, LANES), F32),
        compiler_params=_params("parallel"),
        name="logf",
    )(xb, w_pad, b_pad)


def _split3(x):
    hi = x.astype(BF16)
    r = x - hi.astype(F32)
    mid = r.astype(BF16)
    lo = (r - mid.astype(F32)).astype(BF16)
    return hi, mid, lo


def _cumsum_kernel(x_ref, o_ref, hi_ref, mid_ref, lo_ref, carry, *, tc):
    @pl.when(pl.program_id(1) == 0)
    def _():
        carry[...] = jnp.zeros(carry.shape, F32)

    tri = (lax.broadcasted_iota(jnp.int32, (tc, tc), 0)
           >= lax.broadcasted_iota(jnp.int32, (tc, tc), 1)).astype(BF16)
    c = functools.reduce(lambda a, b: a + b, [jnp.dot(tri, piece, preferred_element_type=F32)
                                              for piece in _split3(x_ref[...])])
    out = c + carry[0:1, :]
    o_ref[...] = out
    hi_ref[...], mid_ref[...], lo_ref[...] = _split3(out * LOG2E)
    carry[...] = jnp.broadcast_to(out[tc - 1:tc, :], carry.shape)


def _cumsum(x, tc=256):
    b, t, _ = x.shape
    spec = pl.BlockSpec((None, tc, LANES), lambda bi, i: (bi, i, 0))
    piece = jax.ShapeDtypeStruct(x.shape, BF16)
    return pl.pallas_call(
        functools.partial(_cumsum_kernel, tc=tc),
        grid=(b, t // tc),
        in_specs=[spec],
        out_specs=[spec] * 4,
        out_shape=[jax.ShapeDtypeStruct(x.shape, F32), piece, piece, piece],
        scratch_shapes=[pltpu.VMEM((SUBLANES, LANES), F32)],
        compiler_params=_params("parallel", "arbitrary"),
        name="cumsum",
    )(x)


FFN_TN = D_FF // 2
CONV_ROWS = SUBLANES


def _ffn_up_kernel(x_ref, wa_ref, wg_ref, ca_ref, cg_ref, pa_ref, pg_ref,
                   act_ref, sa_ref, sg_ref, hpa, hpg, *, tm):
    @pl.when(pl.program_id(2) == 0)
    def _():
        hpa[0:CONV_ROWS] = pa_ref[...]
        hpg[0:CONV_ROWS] = pg_ref[...]

    x = x_ref[...]

    def conv(w_ref, c_ref, s_ref, hp):
        hp[CONV_ROWS:CONV_ROWS + tm] = jnp.dot(x, w_ref[...], preferred_element_type=F32)
        hc = c_ref[CONV_W:CONV_W + 1]
        for j in range(CONV_W):
            off = CONV_ROWS - (CONV_W - 1) + j
            hc = hc + c_ref[j:j + 1] * hp[off:off + tm]
        tail = hp[tm:tm + CONV_ROWS]
        s_ref[...] = tail
        hp[0:CONV_ROWS] = tail
        return hc

    a = conv(wa_ref, ca_ref, sa_ref, hpa)
    g = conv(wg_ref, cg_ref, sg_ref, hpg)
    act_ref[...] = (g * jax.nn.sigmoid(g) * a).astype(BF16)


def _ffn_up(xb, w_up, conv_tab, past, tm=512):
    b, t, _ = xb.shape
    tm = _row_tile(t, tm)
    tn = FFN_TN
    nn = D_FF // tn
    a_col = lambda n, bi, ti: (0, n)
    g_col = lambda n, bi, ti: (0, n + nn)
    state = jax.ShapeDtypeStruct((b, CONV_ROWS, D_FF), F32)
    return pl.pallas_call(
        functools.partial(_ffn_up_kernel, tm=tm),
        grid=(nn, b, t // tm),
        in_specs=[pl.BlockSpec((None, tm, D_MODEL), lambda n, bi, ti: (bi, ti, 0)),
                  pl.BlockSpec((D_MODEL, tn), a_col), pl.BlockSpec((D_MODEL, tn), g_col),
                  pl.BlockSpec((CONV_ROWS, tn), a_col), pl.BlockSpec((CONV_ROWS, tn), g_col),
                  pl.BlockSpec((None, CONV_ROWS, tn), lambda n, bi, ti: (bi, 0, n)),
                  pl.BlockSpec((None, CONV_ROWS, tn), lambda n, bi, ti: (bi, 0, n + nn))],
        out_specs=[pl.BlockSpec((None, tm, tn), lambda n, bi, ti: (bi, ti, n)),
                   pl.BlockSpec((None, CONV_ROWS, tn), lambda n, bi, ti: (bi, 0, n)),
                   pl.BlockSpec((None, CONV_ROWS, tn), lambda n, bi, ti: (bi, 0, n))],
        out_shape=[jax.ShapeDtypeStruct((b, t, D_FF), BF16), state, state],
        scratch_shapes=[pltpu.VMEM((tm + CONV_ROWS, tn), F32), pltpu.VMEM((tm + CONV_ROWS, tn), F32)],
        compiler_params=_params("parallel", "parallel", "arbitrary"),
        name="ffn_up",
    )(xb, w_up, w_up, conv_tab, conv_tab, past, past)


def _rope_tables(pos, batch, q_scale):
    half = A_ROPE // 2
    inv_freq = ROPE_THETA ** (-jnp.arange(half, dtype=F32) / half)
    ang = pos.astype(F32)[:, None] * inv_freq
    cos, sin = jnp.cos(ang), jnp.sin(ang)
    cosk = jnp.concatenate([cos, cos], axis=-1)
    sink = jnp.concatenate([sin, sin], axis=-1)
    t = pos.shape[0]
    pad = jnp.zeros((t, A_QK_PAD - A_NOPE - A_ROPE), F32)
    cosq = q_scale * jnp.concatenate([jnp.ones((t, A_NOPE), F32), cosk, pad], axis=-1)
    sinq = q_scale * jnp.concatenate([jnp.zeros((t, A_NOPE), F32), sink, pad], axis=-1)
    return tuple(jnp.tile(a, (batch, 1)) for a in (cosq, sinq, cosk, sink))


def _swap_halves(w):
    half = w.shape[-1] // 2
    return jnp.concatenate([-w[..., half:], w[..., :half]], axis=-1)


def _mla_weights(w_dq, w_dkv, w_kr, w_uq, w_uk, w_uv):
    zc = jnp.zeros((D_MODEL, LANES - A_ROPE), F32)
    w1 = jnp.concatenate([w_dq, w_dkv, w_kr, zc, _swap_halves(w_kr), zc], axis=1).astype(BF16)
    wq = w_uq.reshape(A_Q_LORA, A_HEADS, A_NOPE + A_ROPE)
    nope, rope = wq[..., :A_NOPE], wq[..., A_NOPE:]
    zpad = jnp.zeros((A_Q_LORA, A_HEADS, A_QK_PAD - A_NOPE - A_ROPE), F32)
    w_cat = jnp.concatenate([nope, rope, zpad], axis=-1).reshape(A_Q_LORA, -1)
    w_sw = jnp.concatenate([jnp.zeros_like(nope), _swap_halves(rope), zpad], axis=-1).reshape(A_Q_LORA, -1)
    wq2 = jnp.concatenate([w_cat, w_sw], axis=1).astype(BF16)
    wk = jnp.zeros((A_LAT_PAD, A_HEADS, A_QK_PAD), F32)
    wk = wk.at[:A_KV_LORA, :, :A_NOPE].set(w_uk)
    eye = jnp.broadcast_to(jnp.eye(A_ROPE, dtype=F32)[:, None, :], (A_ROPE, A_HEADS, A_ROPE))
    wk = wk.at[A_KV_LORA:A_KV_LORA + A_ROPE, :, A_NOPE:A_NOPE + A_ROPE].set(eye)
    wv = jnp.zeros((A_LAT_PAD, A_HEADS * A_V), F32).at[:A_KV_LORA].set(w_uv.reshape(A_KV_LORA, -1))
    wkv = jnp.concatenate([wk.reshape(A_LAT_PAD, -1), wv], axis=1).astype(BF16)
    return w1, wq2, wkv


def _mla_mixer(xb, b, t, pos, ckv_past, kpe_past, w, i):
    w1, wq2, wkv = _mla_weights(w['a_w_dq'][i], w['a_w_dkv'][i], w['a_w_kr'][i], w['a_w_uq'][i],
                                w['a_w_uk'][i], w['a_w_uv'][i])
    resident = ckv_past is None and t % FLASH_T == 0
    q_scale = (A_NOPE + A_ROPE) ** -0.5 * (LOG2E if resident else 1.0)
    q, lat, ckv, kpe = _mla_proj(xb, w1, w['a_g_q'][i], w['a_g_kv'][i], wq2,
                                 *_rope_tables(pos, b, q_scale))
    lat = lat.reshape(b, t, A_LAT_PAD)
    q_off = 0
    if ckv_past is not None:
        p_len = ckv_past.shape[1]
        past = jnp.concatenate(
            [ckv_past, kpe_past, jnp.zeros((b, p_len, A_LAT_PAD - A_KV_LORA - A_ROPE), F32)], axis=-1)
        lat = jnp.concatenate([past.astype(BF16), lat], axis=1)
        q_off = p_len
    t_k = lat.shape[1]
    (kv,) = _mm(lat.reshape(b * t_k, A_LAT_PAD), wkv, [BF16])
    kv = kv.reshape(b, t_k, -1)
    if resident:
        vt = _key_block_transpose(kv[:, :, A_HEADS * A_QK_PAD:], FLASH_T)
        o = _flash_t(q.reshape(b, t, -1), kv, 0, vt, nh=1, dq=A_QK_PAD, dv=A_V, n_hblk=A_HEADS,
                     chunk_causal=True)
    else:
        o = _flash(q.reshape(b, t, -1), kv, kv, nh=1, dq=A_QK_PAD, dv=A_V, n_hblk=A_HEADS,
                   k_col0=0, v_col0=A_HEADS * A_QK_PAD // A_V, tq=512, tk=512, q_off=q_off,
                   chunk_causal=True)
    return o.reshape(b * t, A_HEADS * A_V), ckv.reshape(b, t, -1), kpe.reshape(b, t, -1)


def _qkv(xb, w_qkv, q_scale):
    wq = (w_qkv[:, :D_MODEL] * q_scale).astype(BF16)
    (q,) = _mm(xb, wq, [BF16])
    kv32, kvb = _mm(xb, w_qkv[:, D_MODEL:].astype(BF16), [F32, BF16])
    return q, kv32, kvb


def _band_mixer(xb, b, t, pos0, k_past, v_past, w, i):
    q, kv32, kvb = _qkv(xb, w['b_w_qkv'][i], B_HEAD_DIM ** -0.5)
    q = q.reshape(b, t, D_MODEL)
    k32 = kv32[:, :D_MODEL].reshape(b, t, B_HEADS, B_HEAD_DIM)
    v32 = kv32[:, D_MODEL:].reshape(b, t, B_HEADS, B_HEAD_DIM)
    n_cols = D_MODEL // LANES
    if k_past is None:
        tq = 4 * CHUNK
        nkb = B_WIN // tq + 1
        bias = _band_bias(w['b_rel_bias'][i], B_WIN + np.arange(tq), np.arange(B_WIN + tq))
        kvb = kvb.reshape(b, t, 2 * D_MODEL)
        o = _band(q, kvb, kvb, 0, n_cols, bias, tq=tq, tkb=tq, nkb=nkb, clamp_front=True)
        keep = min(B_WIN, t)
        k_new, v_new = k32[:, t - keep:], v32[:, t - keep:]
    else:
        p_len = k_past.shape[1]
        kvb = kvb.reshape(b, t, 2 * D_MODEL)
        past = jnp.concatenate([k_past.reshape(b, p_len, D_MODEL), v_past.reshape(b, p_len, D_MODEL)],
                               axis=-1).astype(BF16)
        kv_all = jnp.concatenate([past, kvb], axis=1)
        q_pos = pos0 + np.arange(t)
        k_pos = np.concatenate([np.arange(pos0 - p_len, pos0), q_pos])
        bias = _band_bias(w['b_rel_bias'][i], q_pos, k_pos)
        o = _band(q, kv_all, kv_all, 0, n_cols, bias, tq=t, tkb=p_len + t, nkb=1, clamp_front=False)
        k_new, v_new = k32, v32
    return o.reshape(b * t, D_MODEL), k_new, v_new


def _fox_mixer(xb, b, t, k_past, v_past, lf_past, w, i):
    resident = k_past is None and t % FLASH_T == 0
    q, kv32, kvb = _qkv(xb, w['c_w_qkv'][i], C_HEAD_DIM ** -0.5 * (LOG2E if resident else 1.0))
    k32 = kv32[:, :D_MODEL].reshape(b, t, C_HEADS, C_HEAD_DIM)
    v32 = kv32[:, D_MODEL:].reshape(b, t, C_HEADS, C_HEAD_DIM)
    w_f = jnp.zeros((D_MODEL, LANES), F32).at[:, :C_HEADS].set(w['c_w_f'][i]).astype(BF16)
    b_f = jnp.zeros((1, LANES), F32).at[0, :C_HEADS].set(w['c_b_f'][i])
    log_f = _logf(xb, w_f, b_f).reshape(b, t, LANES)
    kvb = kvb.reshape(b, t, 2 * D_MODEL)
    lf_all = log_f
    q_off = 0
    if k_past is not None:
        p_len = k_past.shape[1]
        past = jnp.concatenate([k_past.reshape(b, p_len, D_MODEL), v_past.reshape(b, p_len, D_MODEL)],
                               axis=-1).astype(BF16)
        kvb = jnp.concatenate([past, kvb], axis=1)
        lf_all = jnp.concatenate([jnp.pad(lf_past, ((0, 0), (0, 0), (0, LANES - C_HEADS))), log_f], axis=1)
        q_off = p_len
    t_k = lf_all.shape[1]
    tc = 256
    t_pad = -(-t_k // tc) * tc
    f_cum, *f_pieces = _cumsum(jnp.pad(lf_all, ((0, 0), (0, t_pad - t_k), (0, 0))), tc)
    f_cum = f_cum[:, :t_k, :C_HEADS]
    nh = LANES // C_HEAD_DIM
    n_hblk = C_HEADS // nh
    if resident:
        f3 = jnp.stack([p[:, :t_k, :C_HEADS] for p in f_pieces], axis=-1)
        ones = jnp.ones_like(f3)
        pad = jnp.zeros((b, t, C_HEADS, LANES - C_HEAD_DIM - 2 * f3.shape[-1]), BF16)
        q_aug = jnp.concatenate([q.reshape(b, t, C_HEADS, C_HEAD_DIM), ones, f3, pad], axis=-1)
        k_aug = jnp.concatenate([kvb[:, :, :D_MODEL].reshape(b, t, C_HEADS, C_HEAD_DIM), -f3, ones, pad],
                                axis=-1)
        vt = _key_block_transpose(kvb[:, :, D_MODEL:], FLASH_T)
        o = _flash_t(q_aug.reshape(b, t, C_HEADS * LANES), k_aug.reshape(b, t, C_HEADS * LANES), 0, vt,
                     nh=nh, dq=LANES, dv=C_HEAD_DIM, n_hblk=n_hblk, chunk_causal=False)
        return o.reshape(b * t, D_MODEL), k32, v32, log_f[:, :, :C_HEADS]
    f_blk = f_cum.reshape(b, t_k, n_hblk, nh).transpose(0, 2, 1, 3)
    fq = jnp.pad(f_blk[:, :, t_k - t:], ((0, 0), (0, 0), (0, 0), (0, LANES - nh)))
    fk = jnp.pad(f_blk.transpose(0, 1, 3, 2), ((0, 0), (0, 0), (0, SUBLANES - nh), (0, 0)))
    o = _flash(q.reshape(b, t, D_MODEL), kvb, kvb, nh=nh, dq=C_HEAD_DIM, dv=C_HEAD_DIM, n_hblk=n_hblk,
               k_col0=0, v_col0=n_hblk, tq=512, tk=512, q_off=q_off, chunk_causal=False, fq=fq, fk=fk)
    return o.reshape(b * t, D_MODEL), k32, v32, log_f[:, :, :C_HEADS]


def _conv_ffn(xb, b, t, conv_past, w, i):
    tab = jnp.concatenate([w['f_conv_w'][i], w['f_conv_b'][i][None],
                           jnp.zeros((CONV_ROWS - CONV_W - 1, 2 * D_FF), F32)], axis=0)
    if conv_past is None:
        past = jnp.zeros((b, CONV_ROWS, 2 * D_FF), F32)
    else:
        past = jnp.pad(conv_past, ((0, 0), (CONV_ROWS - (CONV_W - 1), 0), (0, 0)))
    act, sa, sg = _ffn_up(xb.reshape(b, t, D_MODEL), w['f_w_up'][i].astype(BF16), tab, past)
    state = jnp.concatenate([sa, sg], axis=-1)[:, CONV_ROWS - (CONV_W - 1):]
    return act.reshape(b * t, D_FF), state


def _trunk(x, pos0, past, w):
    b, t, _ = x.shape
    pos = pos0 + jnp.arange(t)
    xf = x.reshape(b * t, D_MODEL)
    xb = xf.astype(BF16)
    outs = {n: [] for n in ('a_ckv', 'a_kpe', 'b_k', 'b_v', 'c_k', 'c_v', 'c_logf', 'ffn_conv')}
    ia = ib = ic = 0
    get = lambda name, j: None if past is None else past[name][j]
    for i in range(DEPTH):
        kind = i % N_MIXERS
        if kind == 0:
            o, ckv, kpe = _mla_mixer(xb, b, t, pos, get('a_ckv', ia), get('a_kpe', ia), w, ia)
            outs['a_ckv'].append(ckv)
            outs['a_kpe'].append(kpe)
            w_o = w['a_w_o'][ia]
            ia += 1
        elif kind == 1:
            o, kb, vb = _band_mixer(xb, b, t, pos0, get('b_k', ib), get('b_v', ib), w, ib)
            outs['b_k'].append(kb)
            outs['b_v'].append(vb)
            w_o = w['b_w_o'][ib]
            ib += 1
        else:
            o, kc, vc, lf = _fox_mixer(xb, b, t, get('c_k', ic), get('c_v', ic), get('c_logf', ic), w, ic)
            outs['c_k'].append(kc)
            outs['c_v'].append(vc)
            outs['c_logf'].append(lf)
            w_o = w['c_w_o'][ic]
            ic += 1
        xf, xb = _mm_res_ln(o, w_o.astype(BF16), xf, w['ln1_g'][i], w['ln1_b'][i])
        act, conv_state = _conv_ffn(xb, b, t, get('ffn_conv', i), w, i)
        outs['ffn_conv'].append(conv_state)
        xf, xb = _mm_res_ln(act, w['f_w_down'][i].astype(BF16), xf, w['ln2_g'][i], w['ln2_b'][i])
    return xf.reshape(b, t, D_MODEL), {n: jnp.stack(v) for n, v in outs.items()}


def kernel(x_prompt, x_sample, cache_a_ckv, cache_a_kpe, cache_b_k, cache_b_v, cache_c_k, cache_c_v,
           cache_c_logf, state_ffn_conv, a_w_dq, a_g_q, a_w_uq, a_w_dkv, a_g_kv, a_w_kr, a_w_uk, a_w_uv,
           a_w_o, b_w_qkv, b_rel_bias, b_w_o, c_w_qkv, c_w_f, c_b_f, c_w_o, f_w_up, f_conv_w, f_conv_b,
           f_w_down, ln1_g, ln1_b, ln2_g, ln2_b):
    w = dict(a_w_dq=a_w_dq, a_g_q=a_g_q, a_w_uq=a_w_uq, a_w_dkv=a_w_dkv, a_g_kv=a_g_kv, a_w_kr=a_w_kr,
             a_w_uk=a_w_uk, a_w_uv=a_w_uv, a_w_o=a_w_o, b_w_qkv=b_w_qkv, b_rel_bias=b_rel_bias, b_w_o=b_w_o,
             c_w_qkv=c_w_qkv, c_w_f=c_w_f, c_b_f=c_b_f, c_w_o=c_w_o, f_w_up=f_w_up, f_conv_w=f_conv_w,
             f_conv_b=f_conv_b, f_w_down=f_w_down, ln1_g=ln1_g, ln1_b=ln1_b, ln2_g=ln2_g, ln2_b=ln2_b)
    past = dict(a_ckv=cache_a_ckv, a_kpe=cache_a_kpe, b_k=cache_b_k, b_v=cache_b_v, c_k=cache_c_k,
                c_v=cache_c_v, c_logf=cache_c_logf, ffn_conv=state_ffn_conv)
    past_len = cache_a_ckv.shape[2]
    y_prompt, p = _trunk(x_prompt, 0, None, w)
    y_sample, s = _trunk(x_sample, past_len, past, w)
    names = ('a_ckv', 'a_kpe', 'b_k', 'b_v', 'c_k', 'c_v', 'c_logf', 'ffn_conv')
    return (y_prompt, y_sample) + tuple(p[n] for n in names) + tuple(s[n] for n in names)
```

```python
import functools
import math

import numpy as np
import jax
import jax.numpy as jnp
from jax import lax
from jax.experimental import pallas as pl
from jax.experimental.pallas import tpu as pltpu

F32 = jnp.float32
BF16 = jnp.bfloat16

D_MODEL = 1024
DEPTH = 4
CHUNK = 64
N_MIXERS = 3

A_HEADS = 8
A_Q_LORA = 384
A_KV_LORA = 256
A_NOPE = 128
A_ROPE = 64
A_V = 128
A_QK_PAD = 256
A_LAT_PAD = 384
ROPE_THETA = 10000.0

B_HEADS = 16
B_HEAD_DIM = D_MODEL // B_HEADS
B_LEFT_CHUNKS = 8
B_WIN = B_LEFT_CHUNKS * CHUNK
B_REL_CLIP = 128

C_HEADS = 16
C_HEAD_DIM = D_MODEL // C_HEADS

D_FF = 2816
CONV_W = 3

ALPHA = (2.0 * DEPTH) ** 0.25
LN_EPS = 1e-5
RMS_EPS = 1e-6
NEG_INF = -1e30

LANES = 128
SUBLANES = 8
VMEM_LIMIT_BYTES = 48 * 2 ** 20


def _params(*sem):
    return pltpu.CompilerParams(dimension_semantics=sem, vmem_limit_bytes=VMEM_LIMIT_BYTES)


def _row_tile(m, tm):
    while m % tm:
        tm //= 2
    assert tm % SUBLANES == 0, (m, tm)
    return tm


def _mm_kernel(x_ref, w_ref, *o_refs):
    acc = jnp.dot(x_ref[...], w_ref[...], preferred_element_type=F32)
    for o_ref in o_refs:
        o_ref[...] = acc.astype(o_ref.dtype)


def _mm(x, w, out_dtypes, tm=512, tn=1024):
    m, k = x.shape
    n = w.shape[1]
    tm, tn = _row_tile(m, tm), min(tn, n)
    return pl.pallas_call(
        _mm_kernel,
        grid=(m // tm, n // tn),
        in_specs=[pl.BlockSpec((tm, k), lambda i, j: (i, 0)),
                  pl.BlockSpec((k, tn), lambda i, j: (0, j))],
        out_specs=[pl.BlockSpec((tm, tn), lambda i, j: (i, j)) for _ in out_dtypes],
        out_shape=[jax.ShapeDtypeStruct((m, n), d) for d in out_dtypes],
        compiler_params=_params("parallel", "parallel"),
        name="mm",
    )(x, w)


def _mm_res_ln_kernel(a_ref, w_ref, x_ref, g_ref, b_ref, of_ref, ob_ref):
    y = ALPHA * x_ref[...] + jnp.dot(a_ref[...], w_ref[...], preferred_element_type=F32)
    mu = jnp.mean(y, axis=-1, keepdims=True)
    d = y - mu
    var = jnp.mean(d * d, axis=-1, keepdims=True)
    out = d * lax.rsqrt(var + LN_EPS) * g_ref[...] + b_ref[...]
    of_ref[...] = out
    ob_ref[...] = out.astype(BF16)


def _mm_res_ln(a, w, x, g, b, tm=512):
    m, k = a.shape
    n = w.shape[1]
    tm = _row_tile(m, tm)
    row = lambda i: (i, 0)
    fixed = lambda i: (0, 0)
    return pl.pallas_call(
        _mm_res_ln_kernel,
        grid=(m // tm,),
        in_specs=[pl.BlockSpec((tm, k), row), pl.BlockSpec((k, n), fixed),
                  pl.BlockSpec((tm, n), row), pl.BlockSpec((1, n), fixed),
                  pl.BlockSpec((1, n), fixed)],
        out_specs=[pl.BlockSpec((tm, n), row), pl.BlockSpec((tm, n), row)],
        out_shape=[jax.ShapeDtypeStruct((m, n), F32), jax.ShapeDtypeStruct((m, n), BF16)],
        compiler_params=_params("parallel"),
        name="mm_res_ln",
    )(a, w, x, g.reshape(1, n), b.reshape(1, n))


_W1_CQ = (0, A_Q_LORA)
_W1_CKV = (A_Q_LORA, A_Q_LORA + A_KV_LORA)
_W1_KR = (_W1_CKV[1], _W1_CKV[1] + A_ROPE)
_W1_KRS = (_W1_CKV[1] + LANES, _W1_CKV[1] + LANES + A_ROPE)
_W1_COLS = _W1_CKV[1] + 2 * LANES


def _rms(v, g):
    return v * lax.rsqrt(jnp.mean(v * v, axis=-1, keepdims=True) + RMS_EPS) * g


def _mla_proj_kernel(x_ref, w1_ref, gq_ref, gkv_ref, wq_ref, cq_ref, sq_ref, ck_ref, sk_ref,
                     q_ref, lat_ref, ckv_ref, kpe_ref):
    y = jnp.dot(x_ref[...], w1_ref[...], preferred_element_type=F32)
    cq = _rms(y[:, _W1_CQ[0]:_W1_CQ[1]], gq_ref[...]).astype(BF16)
    ckv = _rms(y[:, _W1_CKV[0]:_W1_CKV[1]], gkv_ref[...])
    kpe = y[:, _W1_KR[0]:_W1_KR[1]] * ck_ref[...] + y[:, _W1_KRS[0]:_W1_KRS[1]] * sk_ref[...]
    ckv_ref[...] = ckv
    kpe_ref[...] = kpe
    lat_ref[:, 0:A_KV_LORA] = ckv.astype(BF16)
    lat_ref[:, A_KV_LORA:A_KV_LORA + A_ROPE] = kpe.astype(BF16)
    lat_ref[:, A_KV_LORA + A_ROPE:] = jnp.zeros(
        (lat_ref.shape[0], A_LAT_PAD - A_KV_LORA - A_ROPE), BF16)
    sw0 = A_HEADS * A_QK_PAD
    for h in range(A_HEADS):
        lo, hi = h * A_QK_PAD, (h + 1) * A_QK_PAD
        qp = jnp.dot(cq, wq_ref[:, lo:hi], preferred_element_type=F32)
        qs = jnp.dot(cq, wq_ref[:, sw0 + lo:sw0 + hi], preferred_element_type=F32)
        q_ref[:, lo:hi] = (qp * cq_ref[...] + qs * sq_ref[...]).astype(BF16)


def _mla_proj(xb, w1, gq, gkv, wq, cosq, sinq, cosk, sink, tm=512):
    m = xb.shape[0]
    tm = _row_tile(m, tm)
    row = lambda i: (i, 0)
    fixed = lambda i: (0, 0)
    nq = A_HEADS * A_QK_PAD
    return pl.pallas_call(
        _mla_proj_kernel,
        grid=(m // tm,),
        in_specs=[pl.BlockSpec((tm, D_MODEL), row), pl.BlockSpec(w1.shape, fixed),
                  pl.BlockSpec((1, A_Q_LORA), fixed), pl.BlockSpec((1, A_KV_LORA), fixed),
                  pl.BlockSpec(wq.shape, fixed),
                  pl.BlockSpec((tm, A_QK_PAD), row), pl.BlockSpec((tm, A_QK_PAD), row),
                  pl.BlockSpec((tm, A_ROPE), row), pl.BlockSpec((tm, A_ROPE), row)],
        out_specs=[pl.BlockSpec((tm, nq), row), pl.BlockSpec((tm, A_LAT_PAD), row),
                   pl.BlockSpec((tm, A_KV_LORA), row), pl.BlockSpec((tm, A_ROPE), row)],
        out_shape=[jax.ShapeDtypeStruct((m, nq), BF16), jax.ShapeDtypeStruct((m, A_LAT_PAD), BF16),
                   jax.ShapeDtypeStruct((m, A_KV_LORA), F32), jax.ShapeDtypeStruct((m, A_ROPE), F32)],
        compiler_params=_params("parallel"),
        name="mla_proj",
    )(xb, w1, gq.reshape(1, -1), gkv.reshape(1, -1), wq, cosq, sinq, cosk, sink)


def _flash_kernel(*refs, nh, dq, dv, tq, tk, nk, q_off, chunk_causal, forget):
    if forget:
        q_ref, k_ref, v_ref, fq_ref, fk_ref, o_ref, m_s, l_s, acc_s = refs
    else:
        q_ref, k_ref, v_ref, o_ref, m_s, l_s, acc_s = refs
    iq = pl.program_id(2)
    ik = pl.program_id(3)
    q_lo = q_off + iq * tq
    q_hi = q_lo + tq - 1
    if chunk_causal:
        vis_lo = (q_lo // CHUNK) * CHUNK + CHUNK - 1
        vis_hi = (q_hi // CHUNK) * CHUNK + CHUNK - 1
    else:
        vis_lo, vis_hi = q_lo, q_hi
    k_lo = ik * tk
    needed = k_lo <= vis_hi
    unmasked = k_lo + tk - 1 <= vis_lo

    @pl.when(ik == 0)
    def _init():
        m_s[...] = jnp.full(m_s.shape, -jnp.inf, F32)
        l_s[...] = jnp.zeros(l_s.shape, F32)
        acc_s[...] = jnp.zeros(acc_s.shape, F32)

    def step(masked):
        if masked:
            q_pos = q_lo + lax.broadcasted_iota(jnp.int32, (tq, tk), 0)
            k_pos = k_lo + lax.broadcasted_iota(jnp.int32, (tq, tk), 1)
            if chunk_causal:
                shift = CHUNK.bit_length() - 1
                mask = jnp.right_shift(k_pos, shift) <= jnp.right_shift(q_pos, shift)
            else:
                mask = k_pos <= q_pos
        for h in range(nh):
            q = q_ref[:, h * dq:(h + 1) * dq]
            k = k_ref[:, h * dq:(h + 1) * dq]
            v = v_ref[:, h * dv:(h + 1) * dv]
            s = lax.dot_general(q, k, (((1,), (1,)), ((), ())), preferred_element_type=F32)
            if forget:
                s = s + fq_ref[:, h:h + 1] - fk_ref[h:h + 1, :]
            if masked:
                s = jnp.where(mask, s, NEG_INF)
            m_prev = m_s[h]
            m_new = jnp.maximum(m_prev, jnp.max(s, axis=-1, keepdims=True))
            alpha = jnp.exp(m_prev - m_new)
            p = jnp.exp(s - m_new)
            l_s[h] = alpha * l_s[h] + jnp.sum(p, axis=-1, keepdims=True)
            acc_s[h] = alpha * acc_s[h] + jnp.dot(p.astype(BF16), v, preferred_element_type=F32)
            m_s[h] = m_new

    @pl.when(needed & unmasked)
    def _plain():
        step(False)

    @pl.when(needed & jnp.logical_not(unmasked))
    def _masked():
        step(True)

    @pl.when(ik == nk - 1)
    def _fin():
        for h in range(nh):
            o_ref[:, h * dv:(h + 1) * dv] = (acc_s[h] / l_s[h]).astype(o_ref.dtype)


def _flash(q, k, v, *, nh, dq, dv, n_hblk, k_col0, v_col0, tq, tk, q_off, chunk_causal,
           fq=None, fk=None):
    b, t_q = q.shape[0], q.shape[1]
    t_k = k.shape[1]
    tq = tq if t_q % tq == 0 else t_q
    tk = tk if t_k % tk == 0 else t_k
    nq, nk = t_q // tq, t_k // tk
    forget = fq is not None

    def last_blk(i):
        q_hi = q_off + (i + 1) * tq - 1
        vis = (q_hi // CHUNK) * CHUNK + CHUNK - 1 if chunk_causal else q_hi
        return jnp.minimum(vis // tk, nk - 1)

    in_specs = [
        pl.BlockSpec((None, tq, nh * dq), lambda bi, h, i, j: (bi, i, h)),
        pl.BlockSpec((None, tk, nh * dq), lambda bi, h, i, j: (bi, jnp.minimum(j, last_blk(i)), k_col0 + h)),
        pl.BlockSpec((None, tk, nh * dv), lambda bi, h, i, j: (bi, jnp.minimum(j, last_blk(i)), v_col0 + h)),
    ]
    args = [q, k, v]
    if forget:
        in_specs += [
            pl.BlockSpec((None, None, tq, LANES), lambda bi, h, i, j: (bi, h, i, 0)),
            pl.BlockSpec((None, None, SUBLANES, tk),
                         lambda bi, h, i, j: (bi, h, 0, jnp.minimum(j, last_blk(i)))),
        ]
        args += [fq, fk]
    kern = functools.partial(_flash_kernel, nh=nh, dq=dq, dv=dv, tq=tq, tk=tk, nk=nk, q_off=q_off,
                             chunk_causal=chunk_causal, forget=forget)
    return pl.pallas_call(
        kern,
        grid=(b, n_hblk, nq, nk),
        in_specs=in_specs,
        out_specs=pl.BlockSpec((None, tq, nh * dv), lambda bi, h, i, j: (bi, i, h)),
        out_shape=jax.ShapeDtypeStruct((b, t_q, n_hblk * nh * dv), BF16),
        scratch_shapes=[pltpu.VMEM((nh, tq, 1), F32), pltpu.VMEM((nh, tq, 1), F32),
                        pltpu.VMEM((nh, tq, dv), F32)],
        compiler_params=_params("parallel", "parallel", "parallel", "arbitrary"),
        name="flash_fox" if forget else "flash_mla",
    )(*args)


FLASH_T = 1024
FLASH_QC = 256
FLASH_AHEAD = 3
LOG2E = math.log2(math.e)


def _flash_t_kernel(qt_ref, k_ref, vt_ref, o_ref, m_s, l_s, acc_s, *, nh, dq, dv, t, chunk_causal):
    iq = pl.program_id(2)
    m_s[...] = jnp.full(m_s.shape, -jnp.inf, F32)
    l_s[...] = jnp.zeros(l_s.shape, F32)
    acc_s[...] = jnp.zeros(acc_s.shape, F32)

    def block(j, masked):
        row0 = pl.multiple_of(j * t, t)
        qc = FLASH_QC
        chains = [(h, c) for h in range(nh) for c in range(t // qc)]

        def n_keys(c):
            return (c + 1) * qc if masked else t

        def qk(h, c):
            return jnp.dot(k_ref[pl.ds(row0, n_keys(c)), h * dq:(h + 1) * dq],
                           qt_ref[h * dq:(h + 1) * dq, c * qc:(c + 1) * qc], preferred_element_type=F32)

        def softmax_pv(h, c, s):
            cols = slice(c * qc, (c + 1) * qc)
            if masked:
                k_pos = lax.broadcasted_iota(jnp.int32, (n_keys(c), qc), 0)
                q_pos = lax.broadcasted_iota(jnp.int32, (n_keys(c), qc), 1) + c * qc
                if chunk_causal:
                    shift = CHUNK.bit_length() - 1
                    mask = jnp.right_shift(k_pos, shift) <= jnp.right_shift(q_pos, shift)
                else:
                    mask = k_pos <= q_pos
                s = jnp.where(mask, s, NEG_INF)
            m_prev = m_s[h, :, cols]
            m_new = jnp.maximum(m_prev, jnp.max(s, axis=0, keepdims=True))
            alpha = jnp.exp2(m_prev - m_new)
            p = jnp.exp2(s - m_new)
            l_s[h, :, cols] = alpha * l_s[h, :, cols] + jnp.sum(p, axis=0, keepdims=True)
            acc_s[h, :, cols] = alpha * acc_s[h, :, cols] + jnp.dot(
                vt_ref[j, h * dv:(h + 1) * dv, 0:n_keys(c)], p.astype(BF16), preferred_element_type=F32)
            m_s[h, :, cols] = m_new

        pending = [qk(*chain) for chain in chains[:FLASH_AHEAD]]
        for i, chain in enumerate(chains):
            if i + FLASH_AHEAD < len(chains):
                pending.append(qk(*chains[i + FLASH_AHEAD]))
            softmax_pv(*chain, pending.pop(0))

    def full_block(j, carry):
        block(j, False)
        return carry

    lax.fori_loop(0, iq, full_block, 0)
    block(iq, True)
    out = jnp.concatenate([acc_s[h] / l_s[h] for h in range(nh)], axis=0)
    o_ref[...] = out.T.astype(o_ref.dtype)


def _flash_t(q, k, k_col0, vt, *, nh, dq, dv, n_hblk, chunk_causal):
    b, t_all = q.shape[0], q.shape[1]
    t = FLASH_T
    nblk = t_all // t
    kern = functools.partial(_flash_t_kernel, nh=nh, dq=dq, dv=dv, t=t, chunk_causal=chunk_causal)
    qt = q.transpose(0, 2, 1)
    return pl.pallas_call(
        kern,
        grid=(b, n_hblk, nblk),
        in_specs=[pl.BlockSpec((None, nh * dq, t), lambda bi, h, i: (bi, h, i)),
                  pl.BlockSpec((None, t_all, nh * dq), lambda bi, h, i: (bi, 0, k_col0 + h)),
                  pl.BlockSpec((None, nblk, nh * dv, t), lambda bi, h, i: (bi, 0, h, 0))],
        out_specs=pl.BlockSpec((None, t, nh * dv), lambda bi, h, i: (bi, i, h)),
        out_shape=jax.ShapeDtypeStruct((b, t_all, n_hblk * nh * dv), BF16),
        scratch_shapes=[pltpu.VMEM((nh, 1, t), F32), pltpu.VMEM((nh, 1, t), F32),
                        pltpu.VMEM((nh, dv, t), F32)],
        compiler_params=_params("parallel", "parallel", "parallel"),
        name="flash_t_chunk" if chunk_causal else "flash_t_frame",
    )(qt, k, vt)


def _key_block_transpose(v, t):
    b, t_all, c = v.shape
    return v.reshape(b, t_all // t, t, c).transpose(0, 1, 3, 2)


B_HEADS_PER_STEP = LANES // B_HEAD_DIM


def _band_kernel(*refs, nkb, tkb, clamp_front):
    q_ref = refs[0]
    k_refs = refs[1:1 + nkb]
    v_refs = refs[1 + nkb:1 + 2 * nkb]
    bias_ref, o_ref = refs[1 + 2 * nkb], refs[2 + 2 * nkb]
    iq = pl.program_id(2)
    dh = B_HEAD_DIM
    for h in range(B_HEADS_PER_STEP):
        q = q_ref[:, h * dh:(h + 1) * dh]
        ss = []
        for j in range(nkb):
            s = lax.dot_general(q, k_refs[j][:, h * dh:(h + 1) * dh], (((1,), (1,)), ((), ())),
                                preferred_element_type=F32)
            s = s + bias_ref[h, :, j * tkb:(j + 1) * tkb]
            if clamp_front and j < nkb - 1:
                s = jnp.where(iq >= nkb - 1 - j, s, NEG_INF)
            ss.append(s)
        m = functools.reduce(jnp.maximum, [jnp.max(s, axis=-1, keepdims=True) for s in ss])
        ps = [jnp.exp(s - m) for s in ss]
        l = functools.reduce(lambda a, c: a + c, [jnp.sum(p, axis=-1, keepdims=True) for p in ps])
        o = functools.reduce(lambda a, c: a + c, [
            jnp.dot(ps[j].astype(BF16), v_refs[j][:, h * dh:(h + 1) * dh], preferred_element_type=F32)
            for j in range(nkb)])
        o_ref[:, h * dh:(h + 1) * dh] = (o / l).astype(o_ref.dtype)


def _band_bias(rel_bias, q_pos, k_pos):
    nq, nk = len(q_pos), len(k_pos)
    assert (np.diff(q_pos) == 1).all() and (np.diff(k_pos) == 1).all()
    m = np.arange(nq + nk - 1)
    u = rel_bias[:, np.clip(q_pos[0] - k_pos[0] + nq - 1 - m, -B_REL_CLIP, B_REL_CLIP) + B_REL_CLIP]
    period = nq + nk
    w = jnp.concatenate([u[:, nq - 1:], jnp.zeros((u.shape[0], 1), u.dtype), u[:, :nq - 1]], axis=1)
    skew = jnp.tile(w, (1, nq))[:, :nq * (period - 1)].reshape(-1, nq, period - 1)[:, :, :nk]
    qc = q_pos[:, None] // CHUNK
    kc = k_pos[None, :] // CHUNK
    mask = (kc <= qc) & (kc >= qc - B_LEFT_CHUNKS) & (k_pos[None, :] >= 0)
    return jnp.where(jnp.asarray(mask)[None], skew, NEG_INF).astype(F32)


def _band(q, k, v, k_col0, v_col0, bias, *, tq, tkb, nkb, clamp_front):
    b, t_q = q.shape[0], q.shape[1]
    nq = t_q // tq
    n_hblk = B_HEADS // B_HEADS_PER_STEP

    def kv_spec(j, col0):
        back = nkb - 1 - j
        return pl.BlockSpec((None, tkb, LANES),
                            lambda bi, h, i: (bi, jnp.maximum(i - back, 0), col0 + h))

    in_specs = ([pl.BlockSpec((None, tq, LANES), lambda bi, h, i: (bi, i, h))]
                + [kv_spec(j, k_col0) for j in range(nkb)]
                + [kv_spec(j, v_col0) for j in range(nkb)]
                + [pl.BlockSpec((B_HEADS_PER_STEP, tq, nkb * tkb), lambda bi, h, i: (h, 0, 0))])
    kern = functools.partial(_band_kernel, nkb=nkb, tkb=tkb, clamp_front=clamp_front)
    return pl.pallas_call(
        kern,
        grid=(b, n_hblk, nq),
        in_specs=in_specs,
        out_specs=pl.BlockSpec((None, tq, LANES), lambda bi, h, i: (bi, i, h)),
        out_shape=jax.ShapeDtypeStruct((b, t_q, D_MODEL), BF16),
        compiler_params=_params("parallel", "parallel", "parallel"),
        name="band_attn",
    )(q, *([k] * nkb), *([v] * nkb), bias)


def _logf_kernel(x_ref, w_ref, b_ref, o_ref):
    z = jnp.dot(x_ref[...], w_ref[...], preferred_element_type=F32) + b_ref[...]
    o_ref[...] = -(jnp.maximum(-z, 0.0) + jnp.log1p(jnp.exp(-jnp.abs(z))))


def _logf(xb, w_pad, b_pad, tm=512):
    m = xb.shape[0]
    tm = _row_tile(m, tm)
    return pl.pallas_call(
        _logf_kernel,
        grid=(m // tm,),
        in_specs=[pl.BlockSpec((tm, D_MODEL), lambda i: (i, 0)),
                  pl.BlockSpec((D_MODEL, LANES), lambda i: (0, 0)),
                  pl.BlockSpec((1, LANES), lambda i: (0, 0))],
        out_specs=pl.BlockSpec((tm, LANES), lambda i: (i, 0)),
        out_shape=jax.ShapeDtypeStruct((m, LANES), F32),
        compiler_params=_params("parallel"),
        name="logf",
    )(xb, w_pad, b_pad)


def _split3(x):
    hi = x.astype(BF16)
    r = x - hi.astype(F32)
    mid = r.astype(BF16)
    lo = (r - mid.astype(F32)).astype(BF16)
    return hi, mid, lo


def _cumsum_kernel(x_ref, o_ref, hi_ref, mid_ref, lo_ref, carry, *, tc):
    @pl.when(pl.program_id(1) == 0)
    def _():
        carry[...] = jnp.zeros(carry.shape, F32)

    tri = (lax.broadcasted_iota(jnp.int32, (tc, tc), 0)
           >= lax.broadcasted_iota(jnp.int32, (tc, tc), 1)).astype(BF16)
    c = functools.reduce(lambda a, b: a + b, [jnp.dot(tri, piece, preferred_element_type=F32)
                                              for piece in _split3(x_ref[...])])
    out = c + carry[0:1, :]
    o_ref[...] = out
    hi_ref[...], mid_ref[...], lo_ref[...] = _split3(out * LOG2E)
    carry[...] = jnp.broadcast_to(out[tc - 1:tc, :], carry.shape)


def _cumsum(x, tc=256):
    b, t, _ = x.shape
    spec = pl.BlockSpec((None, tc, LANES), lambda bi, i: (bi, i, 0))
    piece = jax.ShapeDtypeStruct(x.shape, BF16)
    return pl.pallas_call(
        functools.partial(_cumsum_kernel, tc=tc),
        grid=(b, t // tc),
        in_specs=[spec],
        out_specs=[spec] * 4,
        out_shape=[jax.ShapeDtypeStruct(x.shape, F32), piece, piece, piece],
        scratch_shapes=[pltpu.VMEM((SUBLANES, LANES), F32)],
        compiler_params=_params("parallel", "arbitrary"),
        name="cumsum",
    )(x)


FFN_TN = D_FF // 2
CONV_ROWS = SUBLANES


def _ffn_up_kernel(x_ref, wa_ref, wg_ref, ca_ref, cg_ref, pa_ref, pg_ref,
                   act_ref, sa_ref, sg_ref, hpa, hpg, *, tm):
    @pl.when(pl.program_id(2) == 0)
    def _():
        hpa[0:CONV_ROWS] = pa_ref[...]
        hpg[0:CONV_ROWS] = pg_ref[...]

    x = x_ref[...]

    def conv(w_ref, c_ref, s_ref, hp):
        hp[CONV_ROWS:CONV_ROWS + tm] = jnp.dot(x, w_ref[...], preferred_element_type=F32)
        hc = c_ref[CONV_W:CONV_W + 1]
        for j in range(CONV_W):
            off = CONV_ROWS - (CONV_W - 1) + j
            hc = hc + c_ref[j:j + 1] * hp[off:off + tm]
        tail = hp[tm:tm + CONV_ROWS]
        s_ref[...] = tail
        hp[0:CONV_ROWS] = tail
        return hc

    a = conv(wa_ref, ca_ref, sa_ref, hpa)
    g = conv(wg_ref, cg_ref, sg_ref, hpg)
    act_ref[...] = (g * jax.nn.sigmoid(g) * a).astype(BF16)


def _ffn_up(xb, w_up, conv_tab, past, tm=512):
    b, t, _ = xb.shape
    tm = _row_tile(t, tm)
    tn = FFN_TN
    nn = D_FF // tn
    a_col = lambda n, bi, ti: (0, n)
    g_col = lambda n, bi, ti: (0, n + nn)
    state = jax.ShapeDtypeStruct((b, CONV_ROWS, D_FF), F32)
    return pl.pallas_call(
        functools.partial(_ffn_up_kernel, tm=tm),
        grid=(nn, b, t // tm),
        in_specs=[pl.BlockSpec((None, tm, D_MODEL), lambda n, bi, ti: (bi, ti, 0)),
                  pl.BlockSpec((D_MODEL, tn), a_col), pl.BlockSpec((D_MODEL, tn), g_col),
                  pl.BlockSpec((CONV_ROWS, tn), a_col), pl.BlockSpec((CONV_ROWS, tn), g_col),
                  pl.BlockSpec((None, CONV_ROWS, tn), lambda n, bi, ti: (bi, 0, n)),
                  pl.BlockSpec((None, CONV_ROWS, tn), lambda n, bi, ti: (bi, 0, n + nn))],
        out_specs=[pl.BlockSpec((None, tm, tn), lambda n, bi, ti: (bi, ti, n)),
                   pl.BlockSpec((None, CONV_ROWS, tn), lambda n, bi, ti: (bi, 0, n)),
                   pl.BlockSpec((None, CONV_ROWS, tn), lambda n, bi, ti: (bi, 0, n))],
        out_shape=[jax.ShapeDtypeStruct((b, t, D_FF), BF16), state, state],
        scratch_shapes=[pltpu.VMEM((tm + CONV_ROWS, tn), F32), pltpu.VMEM((tm + CONV_ROWS, tn), F32)],
        compiler_params=_params("parallel", "parallel", "arbitrary"),
        name="ffn_up",
    )(xb, w_up, w_up, conv_tab, conv_tab, past, past)


def _rope_tables(pos, batch, q_scale):
    half = A_ROPE // 2
    inv_freq = ROPE_THETA ** (-jnp.arange(half, dtype=F32) / half)
    ang = pos.astype(F32)[:, None] * inv_freq
    cos, sin = jnp.cos(ang), jnp.sin(ang)
    cosk = jnp.concatenate([cos, cos], axis=-1)
    sink = jnp.concatenate([sin, sin], axis=-1)
    t = pos.shape[0]
    pad = jnp.zeros((t, A_QK_PAD - A_NOPE - A_ROPE), F32)
    cosq = q_scale * jnp.concatenate([jnp.ones((t, A_NOPE), F32), cosk, pad], axis=-1)
    sinq = q_scale * jnp.concatenate([jnp.zeros((t, A_NOPE), F32), sink, pad], axis=-1)
    return tuple(jnp.tile(a, (batch, 1)) for a in (cosq, sinq, cosk, sink))


def _swap_halves(w):
    half = w.shape[-1] // 2
    return jnp.concatenate([-w[..., half:], w[..., :half]], axis=-1)


def _mla_weights(w_dq, w_dkv, w_kr, w_uq, w_uk, w_uv):
    zc = jnp.zeros((D_MODEL, LANES - A_ROPE), F32)
    w1 = jnp.concatenate([w_dq, w_dkv, w_kr, zc, _swap_halves(w_kr), zc], axis=1).astype(BF16)
    wq = w_uq.reshape(A_Q_LORA, A_HEADS, A_NOPE + A_ROPE)
    nope, rope = wq[..., :A_NOPE], wq[..., A_NOPE:]
    zpad = jnp.zeros((A_Q_LORA, A_HEADS, A_QK_PAD - A_NOPE - A_ROPE), F32)
    w_cat = jnp.concatenate([nope, rope, zpad], axis=-1).reshape(A_Q_LORA, -1)
    w_sw = jnp.concatenate([jnp.zeros_like(nope), _swap_halves(rope), zpad], axis=-1).reshape(A_Q_LORA, -1)
    wq2 = jnp.concatenate([w_cat, w_sw], axis=1).astype(BF16)
    wk = jnp.zeros((A_LAT_PAD, A_HEADS, A_QK_PAD), F32)
    wk = wk.at[:A_KV_LORA, :, :A_NOPE].set(w_uk)
    eye = jnp.broadcast_to(jnp.eye(A_ROPE, dtype=F32)[:, None, :], (A_ROPE, A_HEADS, A_ROPE))
    wk = wk.at[A_KV_LORA:A_KV_LORA + A_ROPE, :, A_NOPE:A_NOPE + A_ROPE].set(eye)
    wv = jnp.zeros((A_LAT_PAD, A_HEADS * A_V), F32).at[:A_KV_LORA].set(w_uv.reshape(A_KV_LORA, -1))
    wkv = jnp.concatenate([wk.reshape(A_LAT_PAD, -1), wv], axis=1).astype(BF16)
    return w1, wq2, wkv


def _mla_mixer(xb, b, t, pos, ckv_past, kpe_past, w, i):
    w1, wq2, wkv = _mla_weights(w['a_w_dq'][i], w['a_w_dkv'][i], w['a_w_kr'][i], w['a_w_uq'][i],
                                w['a_w_uk'][i], w['a_w_uv'][i])
    resident = ckv_past is None and t % FLASH_T == 0
    q_scale = (A_NOPE + A_ROPE) ** -0.5 * (LOG2E if resident else 1.0)
    q, lat, ckv, kpe = _mla_proj(xb, w1, w['a_g_q'][i], w['a_g_kv'][i], wq2,
                                 *_rope_tables(pos, b, q_scale))
    lat = lat.reshape(b, t, A_LAT_PAD)
    q_off = 0
    if ckv_past is not None:
        p_len = ckv_past.shape[1]
        past = jnp.concatenate(
            [ckv_past, kpe_past, jnp.zeros((b, p_len, A_LAT_PAD - A_KV_LORA - A_ROPE), F32)], axis=-1)
        lat = jnp.concatenate([past.astype(BF16), lat], axis=1)
        q_off = p_len
    t_k = lat.shape[1]
    (kv,) = _mm(lat.reshape(b * t_k, A_LAT_PAD), wkv, [BF16])
    kv = kv.reshape(b, t_k, -1)
    if resident:
        vt = _key_block_transpose(kv[:, :, A_HEADS * A_QK_PAD:], FLASH_T)
        o = _flash_t(q.reshape(b, t, -1), kv, 0, vt, nh=1, dq=A_QK_PAD, dv=A_V, n_hblk=A_HEADS,
                     chunk_causal=True)
    else:
        o = _flash(q.reshape(b, t, -1), kv, kv, nh=1, dq=A_QK_PAD, dv=A_V, n_hblk=A_HEADS,
                   k_col0=0, v_col0=A_HEADS * A_QK_PAD // A_V, tq=512, tk=512, q_off=q_off,
                   chunk_causal=True)
    return o.reshape(b * t, A_HEADS * A_V), ckv.reshape(b, t, -1), kpe.reshape(b, t, -1)


def _qkv(xb, w_qkv, q_scale):
    wq = (w_qkv[:, :D_MODEL] * q_scale).astype(BF16)
    (q,) = _mm(xb, wq, [BF16])
    kv32, kvb = _mm(xb, w_qkv[:, D_MODEL:].astype(BF16), [F32, BF16])
    return q, kv32, kvb


def _band_mixer(xb, b, t, pos0, k_past, v_past, w, i):
    q, kv32, kvb = _qkv(xb, w['b_w_qkv'][i], B_HEAD_DIM ** -0.5)
    q = q.reshape(b, t, D_MODEL)
    k32 = kv32[:, :D_MODEL].reshape(b, t, B_HEADS, B_HEAD_DIM)
    v32 = kv32[:, D_MODEL:].reshape(b, t, B_HEADS, B_HEAD_DIM)
    n_cols = D_MODEL // LANES
    if k_past is None:
        tq = 4 * CHUNK
        nkb = B_WIN // tq + 1
        bias = _band_bias(w['b_rel_bias'][i], B_WIN + np.arange(tq), np.arange(B_WIN + tq))
        kvb = kvb.reshape(b, t, 2 * D_MODEL)
        o = _band(q, kvb, kvb, 0, n_cols, bias, tq=tq, tkb=tq, nkb=nkb, clamp_front=True)
        keep = min(B_WIN, t)
        k_new, v_new = k32[:, t - keep:], v32[:, t - keep:]
    else:
        p_len = k_past.shape[1]
        kvb = kvb.reshape(b, t, 2 * D_MODEL)
        past = jnp.concatenate([k_past.reshape(b, p_len, D_MODEL), v_past.reshape(b, p_len, D_MODEL)],
                               axis=-1).astype(BF16)
        kv_all = jnp.concatenate([past, kvb], axis=1)
        q_pos = pos0 + np.arange(t)
        k_pos = np.concatenate([np.arange(pos0 - p_len, pos0), q_pos])
        bias = _band_bias(w['b_rel_bias'][i], q_pos, k_pos)
        o = _band(q, kv_all, kv_all, 0, n_cols, bias, tq=t, tkb=p_len + t, nkb=1, clamp_front=False)
        k_new, v_new = k32, v32
    return o.reshape(b * t, D_MODEL), k_new, v_new


C_AUG = LANES
N_PIECES = 3


def _aug_kernel(x_ref, w_ref, hi_ref, mid_ref, lo_ref, p_ref, b_ref, o_ref):
    acc = jnp.dot(x_ref[...], w_ref[...], preferred_element_type=F32) + b_ref[...]
    for piece, f_ref in enumerate((hi_ref, mid_ref, lo_ref)):
        acc = acc + jnp.dot(f_ref[...], p_ref[piece], preferred_element_type=F32)
    o_ref[...] = acc.astype(BF16)


def _aug_tables():
    n = C_HEADS * C_AUG
    place = np.zeros((N_PIECES, LANES, 2 * n), np.float32)
    ones = np.zeros((1, 2 * n), np.float32)
    for h in range(C_HEADS):
        base = h * C_AUG + C_HEAD_DIM
        for piece in range(N_PIECES):
            ones[0, base + piece] = 1.0
            place[piece, h, base + N_PIECES + piece] = 1.0
            place[piece, h, n + base + piece] = -1.0
            ones[0, n + base + N_PIECES + piece] = 1.0
    return jnp.asarray(place, BF16), jnp.asarray(ones)


def _aug_qk(xb, w_aug, pieces, tm=512, tn=1024):
    m = xb.shape[0]
    n = w_aug.shape[1]
    tm = _row_tile(m, tm)
    place, ones = _aug_tables()
    row = lambda i, j: (i, 0)
    return pl.pallas_call(
        _aug_kernel,
        grid=(m // tm, n // tn),
        in_specs=[pl.BlockSpec((tm, D_MODEL), row), pl.BlockSpec((D_MODEL, tn), lambda i, j: (0, j)),
                  pl.BlockSpec((tm, LANES), row), pl.BlockSpec((tm, LANES), row), pl.BlockSpec((tm, LANES), row),
                  pl.BlockSpec((N_PIECES, LANES, tn), lambda i, j: (0, 0, j)),
                  pl.BlockSpec((1, tn), lambda i, j: (0, j))],
        out_specs=pl.BlockSpec((tm, tn), lambda i, j: (i, j)),
        out_shape=jax.ShapeDtypeStruct((m, n), BF16),
        compiler_params=_params("parallel", "parallel"),
        name="aug_qk",
    )(xb, w_aug, *pieces, place, ones)


def _pad_heads(w, scale):
    w = (w * scale).reshape(D_MODEL, C_HEADS, C_HEAD_DIM)
    return jnp.pad(w, ((0, 0), (0, 0), (0, C_AUG - C_HEAD_DIM))).reshape(D_MODEL, C_HEADS * C_AUG)


def _fox_mixer(xb, b, t, k_past, v_past, lf_past, w, i):
    resident = k_past is None and t % FLASH_T == 0
    w_qkv = w['c_w_qkv'][i]
    kv32, kvb = _mm(xb, w_qkv[:, D_MODEL:].astype(BF16), [F32, BF16])
    k32 = kv32[:, :D_MODEL].reshape(b, t, C_HEADS, C_HEAD_DIM)
    v32 = kv32[:, D_MODEL:].reshape(b, t, C_HEADS, C_HEAD_DIM)
    w_f = jnp.zeros((D_MODEL, LANES), F32).at[:, :C_HEADS].set(w['c_w_f'][i]).astype(BF16)
    b_f = jnp.zeros((1, LANES), F32).at[0, :C_HEADS].set(w['c_b_f'][i])
    log_f = _logf(xb, w_f, b_f).reshape(b, t, LANES)
    kvb = kvb.reshape(b, t, 2 * D_MODEL)
    lf_all = log_f
    q_off = 0
    if k_past is not None:
        p_len = k_past.shape[1]
        past = jnp.concatenate([k_past.reshape(b, p_len, D_MODEL), v_past.reshape(b, p_len, D_MODEL)],
                               axis=-1).astype(BF16)
        kvb = jnp.concatenate([past, kvb], axis=1)
        lf_all = jnp.concatenate([jnp.pad(lf_past, ((0, 0), (0, 0), (0, LANES - C_HEADS))), log_f], axis=1)
        q_off = p_len
    t_k = lf_all.shape[1]
    tc = 256
    t_pad = -(-t_k // tc) * tc
    f_cum, *f_pieces = _cumsum(jnp.pad(lf_all, ((0, 0), (0, t_pad - t_k), (0, 0))), tc)
    nh = LANES // C_HEAD_DIM
    n_hblk = C_HEADS // nh
    q_scale = C_HEAD_DIM ** -0.5
    if resident:
        w_aug = jnp.concatenate([_pad_heads(w_qkv[:, :D_MODEL], q_scale * LOG2E),
                                 _pad_heads(w_qkv[:, D_MODEL:2 * D_MODEL], 1.0)], axis=1).astype(BF16)
        qk_aug = _aug_qk(xb, w_aug, [p.reshape(b * t, LANES) for p in f_pieces]).reshape(b, t, -1)
        n_q = C_HEADS * C_AUG
        vt = _key_block_transpose(kvb[:, :, D_MODEL:], FLASH_T)
        o = _flash_t(qk_aug[:, :, :n_q], qk_aug, n_q // (nh * C_AUG), vt,
                     nh=nh, dq=C_AUG, dv=C_HEAD_DIM, n_hblk=n_hblk, chunk_causal=False)
        return o.reshape(b * t, D_MODEL), k32, v32, log_f[:, :, :C_HEADS]
    (q,) = _mm(xb, (w_qkv[:, :D_MODEL] * q_scale).astype(BF16), [BF16])
    f_blk = f_cum[:, :t_k, :C_HEADS].reshape(b, t_k, n_hblk, nh).transpose(0, 2, 1, 3)
    fq = jnp.pad(f_blk[:, :, t_k - t:], ((0, 0), (0, 0), (0, 0), (0, LANES - nh)))
    fk = jnp.pad(f_blk.transpose(0, 1, 3, 2), ((0, 0), (0, 0), (0, SUBLANES - nh), (0, 0)))
    o = _flash(q.reshape(b, t, D_MODEL), kvb, kvb, nh=nh, dq=C_HEAD_DIM, dv=C_HEAD_DIM, n_hblk=n_hblk,
               k_col0=0, v_col0=n_hblk, tq=512, tk=512, q_off=q_off, chunk_causal=False, fq=fq, fk=fk)
    return o.reshape(b * t, D_MODEL), k32, v32, log_f[:, :, :C_HEADS]


def _conv_ffn(xb, b, t, conv_past, w, i):
    tab = jnp.concatenate([w['f_conv_w'][i], w['f_conv_b'][i][None],
                           jnp.zeros((CONV_ROWS - CONV_W - 1, 2 * D_FF), F32)], axis=0)
    if conv_past is None:
        past = jnp.zeros((b, CONV_ROWS, 2 * D_FF), F32)
    else:
        past = jnp.pad(conv_past, ((0, 0), (CONV_ROWS - (CONV_W - 1), 0), (0, 0)))
    act, sa, sg = _ffn_up(xb.reshape(b, t, D_MODEL), w['f_w_up'][i].astype(BF16), tab, past)
    state = jnp.concatenate([sa, sg], axis=-1)[:, CONV_ROWS - (CONV_W - 1):]
    return act.reshape(b * t, D_FF), state


def _trunk(x, pos0, past, w):
    b, t, _ = x.shape
    pos = pos0 + jnp.arange(t)
    xf = x.reshape(b * t, D_MODEL)
    xb = xf.astype(BF16)
    outs = {n: [] for n in ('a_ckv', 'a_kpe', 'b_k', 'b_v', 'c_k', 'c_v', 'c_logf', 'ffn_conv')}
    ia = ib = ic = 0
    get = lambda name, j: None if past is None else past[name][j]
    for i in range(DEPTH):
        kind = i % N_MIXERS
        if kind == 0:
            o, ckv, kpe = _mla_mixer(xb, b, t, pos, get('a_ckv', ia), get('a_kpe', ia), w, ia)
            outs['a_ckv'].append(ckv)
            outs['a_kpe'].append(kpe)
            w_o = w['a_w_o'][ia]
            ia += 1
        elif kind == 1:
            o, kb, vb = _band_mixer(xb, b, t, pos0, get('b_k', ib), get('b_v', ib), w, ib)
            outs['b_k'].append(kb)
            outs['b_v'].append(vb)
            w_o = w['b_w_o'][ib]
            ib += 1
        else:
            o, kc, vc, lf = _fox_mixer(xb, b, t, get('c_k', ic), get('c_v', ic), get('c_logf', ic), w, ic)
            outs['c_k'].append(kc)
            outs['c_v'].append(vc)
            outs['c_logf'].append(lf)
            w_o = w['c_w_o'][ic]
            ic += 1
        xf, xb = _mm_res_ln(o, w_o.astype(BF16), xf, w['ln1_g'][i], w['ln1_b'][i])
        act, conv_state = _conv_ffn(xb, b, t, get('ffn_conv', i), w, i)
        outs['ffn_conv'].append(conv_state)
        xf, xb = _mm_res_ln(act, w['f_w_down'][i].astype(BF16), xf, w['ln2_g'][i], w['ln2_b'][i])
    return xf.reshape(b, t, D_MODEL), {n: jnp.stack(v) for n, v in outs.items()}


def kernel(x_prompt, x_sample, cache_a_ckv, cache_a_kpe, cache_b_k, cache_b_v, cache_c_k, cache_c_v,
           cache_c_logf, state_ffn_conv, a_w_dq, a_g_q, a_w_uq, a_w_dkv, a_g_kv, a_w_kr, a_w_uk, a_w_uv,
           a_w_o, b_w_qkv, b_rel_bias, b_w_o, c_w_qkv, c_w_f, c_b_f, c_w_o, f_w_up, f_conv_w, f_conv_b,
           f_w_down, ln1_g, ln1_b, ln2_g, ln2_b):
    w = dict(a_w_dq=a_w_dq, a_g_q=a_g_q, a_w_uq=a_w_uq, a_w_dkv=a_w_dkv, a_g_kv=a_g_kv, a_w_kr=a_w_kr,
             a_w_uk=a_w_uk, a_w_uv=a_w_uv, a_w_o=a_w_o, b_w_qkv=b_w_qkv, b_rel_bias=b_rel_bias, b_w_o=b_w_o,
             c_w_qkv=c_w_qkv, c_w_f=c_w_f, c_b_f=c_b_f, c_w_o=c_w_o, f_w_up=f_w_up, f_conv_w=f_conv_w,
             f_conv_b=f_conv_b, f_w_down=f_w_down, ln1_g=ln1_g, ln1_b=ln1_b, ln2_g=ln2_g, ln2_b=ln2_b)
    past = dict(a_ckv=cache_a_ckv, a_kpe=cache_a_kpe, b_k=cache_b_k, b_v=cache_b_v, c_k=cache_c_k,
                c_v=cache_c_v, c_logf=cache_c_logf, ffn_conv=state_ffn_conv)
    past_len = cache_a_ckv.shape[2]
    y_prompt, p = _trunk(x_prompt, 0, None, w)
    y_sample, s = _trunk(x_sample, past_len, past, w)
    names = ('a_ckv', 'a_kpe', 'b_k', 'b_v', 'c_k', 'c_v', 'c_logf', 'ffn_conv')
    return (y_prompt, y_sample) + tuple(p[n] for n in names) + tuple(s[n] for n in names)
```

```python
import functools
import math

import numpy as np
import jax
import jax.numpy as jnp
from jax import lax
from jax.experimental import pallas as pl
from jax.experimental.pallas import tpu as pltpu

F32 = jnp.float32
BF16 = jnp.bfloat16

D_MODEL = 1024
DEPTH = 4
CHUNK = 64
N_MIXERS = 3

A_HEADS = 8
A_Q_LORA = 384
A_KV_LORA = 256
A_NOPE = 128
A_ROPE = 64
A_V = 128
A_QK_PAD = 256
A_LAT_PAD = 384
ROPE_THETA = 10000.0

B_HEADS = 16
B_HEAD_DIM = D_MODEL // B_HEADS
B_LEFT_CHUNKS = 8
B_WIN = B_LEFT_CHUNKS * CHUNK
B_REL_CLIP = 128

C_HEADS = 16
C_HEAD_DIM = D_MODEL // C_HEADS

D_FF = 2816
CONV_W = 3

ALPHA = (2.0 * DEPTH) ** 0.25
LN_EPS = 1e-5
RMS_EPS = 1e-6
NEG_INF = -1e30

LANES = 128
SUBLANES = 8
VMEM_LIMIT_BYTES = 48 * 2 ** 20


def _params(*sem):
    return pltpu.CompilerParams(dimension_semantics=sem, vmem_limit_bytes=VMEM_LIMIT_BYTES)


def _row_tile(m, tm):
    while m % tm:
        tm //= 2
    assert tm % SUBLANES == 0, (m, tm)
    return tm


def _mm_kernel(x_ref, w_ref, *o_refs):
    acc = jnp.dot(x_ref[...], w_ref[...], preferred_element_type=F32)
    for o_ref in o_refs:
        o_ref[...] = acc.astype(o_ref.dtype)


def _mm(x, w, out_dtypes, tm=512, tn=1024):
    m, k = x.shape
    n = w.shape[1]
    tm, tn = _row_tile(m, tm), min(tn, n)
    return pl.pallas_call(
        _mm_kernel,
        grid=(m // tm, n // tn),
        in_specs=[pl.BlockSpec((tm, k), lambda i, j: (i, 0)),
                  pl.BlockSpec((k, tn), lambda i, j: (0, j))],
        out_specs=[pl.BlockSpec((tm, tn), lambda i, j: (i, j)) for _ in out_dtypes],
        out_shape=[jax.ShapeDtypeStruct((m, n), d) for d in out_dtypes],
        compiler_params=_params("parallel", "parallel"),
        name="mm",
    )(x, w)


def _mm_res_ln_kernel(a_ref, w_ref, x_ref, g_ref, b_ref, of_ref, ob_ref):
    y = ALPHA * x_ref[...] + jnp.dot(a_ref[...], w_ref[...], preferred_element_type=F32)
    mu = jnp.mean(y, axis=-1, keepdims=True)
    d = y - mu
    var = jnp.mean(d * d, axis=-1, keepdims=True)
    out = d * lax.rsqrt(var + LN_EPS) * g_ref[...] + b_ref[...]
    of_ref[...] = out
    ob_ref[...] = out.astype(BF16)


def _mm_res_ln(a, w, x, g, b, tm=512):
    m, k = a.shape
    n = w.shape[1]
    tm = _row_tile(m, tm)
    row = lambda i: (i, 0)
    fixed = lambda i: (0, 0)
    return pl.pallas_call(
        _mm_res_ln_kernel,
        grid=(m // tm,),
        in_specs=[pl.BlockSpec((tm, k), row), pl.BlockSpec((k, n), fixed),
                  pl.BlockSpec((tm, n), row), pl.BlockSpec((1, n), fixed),
                  pl.BlockSpec((1, n), fixed)],
        out_specs=[pl.BlockSpec((tm, n), row), pl.BlockSpec((tm, n), row)],
        out_shape=[jax.ShapeDtypeStruct((m, n), F32), jax.ShapeDtypeStruct((m, n), BF16)],
        compiler_params=_params("parallel"),
        name="mm_res_ln",
    )(a, w, x, g.reshape(1, n), b.reshape(1, n))


_W1_CQ = (0, A_Q_LORA)
_W1_CKV = (A_Q_LORA, A_Q_LORA + A_KV_LORA)
_W1_KR = (_W1_CKV[1], _W1_CKV[1] + A_ROPE)
_W1_KRS = (_W1_CKV[1] + LANES, _W1_CKV[1] + LANES + A_ROPE)
_W1_COLS = _W1_CKV[1] + 2 * LANES


def _rms(v, g):
    return v * lax.rsqrt(jnp.mean(v * v, axis=-1, keepdims=True) + RMS_EPS) * g


def _mla_proj_kernel(x_ref, w1_ref, gq_ref, gkv_ref, wq_ref, cq_ref, sq_ref, ck_ref, sk_ref,
                     q_ref, lat_ref, ckv_ref, kpe_ref):
    y = jnp.dot(x_ref[...], w1_ref[...], preferred_element_type=F32)
    cq = _rms(y[:, _W1_CQ[0]:_W1_CQ[1]], gq_ref[...]).astype(BF16)
    ckv = _rms(y[:, _W1_CKV[0]:_W1_CKV[1]], gkv_ref[...])
    kpe = y[:, _W1_KR[0]:_W1_KR[1]] * ck_ref[...] + y[:, _W1_KRS[0]:_W1_KRS[1]] * sk_ref[...]
    ckv_ref[...] = ckv
    kpe_ref[...] = kpe
    lat_ref[:, 0:A_KV_LORA] = ckv.astype(BF16)
    lat_ref[:, A_KV_LORA:A_KV_LORA + A_ROPE] = kpe.astype(BF16)
    lat_ref[:, A_KV_LORA + A_ROPE:] = jnp.zeros(
        (lat_ref.shape[0], A_LAT_PAD - A_KV_LORA - A_ROPE), BF16)
    sw0 = A_HEADS * A_QK_PAD
    for h in range(A_HEADS):
        lo, hi = h * A_QK_PAD, (h + 1) * A_QK_PAD
        qp = jnp.dot(cq, wq_ref[:, lo:hi], preferred_element_type=F32)
        qs = jnp.dot(cq, wq_ref[:, sw0 + lo:sw0 + hi], preferred_element_type=F32)
        q_ref[:, lo:hi] = (qp * cq_ref[...] + qs * sq_ref[...]).astype(BF16)


def _mla_proj(xb, w1, gq, gkv, wq, cosq, sinq, cosk, sink, tm=512):
    m = xb.shape[0]
    tm = _row_tile(m, tm)
    row = lambda i: (i, 0)
    fixed = lambda i: (0, 0)
    nq = A_HEADS * A_QK_PAD
    return pl.pallas_call(
        _mla_proj_kernel,
        grid=(m // tm,),
        in_specs=[pl.BlockSpec((tm, D_MODEL), row), pl.BlockSpec(w1.shape, fixed),
                  pl.BlockSpec((1, A_Q_LORA), fixed), pl.BlockSpec((1, A_KV_LORA), fixed),
                  pl.BlockSpec(wq.shape, fixed),
                  pl.BlockSpec((tm, A_QK_PAD), row), pl.BlockSpec((tm, A_QK_PAD), row),
                  pl.BlockSpec((tm, A_ROPE), row), pl.BlockSpec((tm, A_ROPE), row)],
        out_specs=[pl.BlockSpec((tm, nq), row), pl.BlockSpec((tm, A_LAT_PAD), row),
                   pl.BlockSpec((tm, A_KV_LORA), row), pl.BlockSpec((tm, A_ROPE), row)],
        out_shape=[jax.ShapeDtypeStruct((m, nq), BF16), jax.ShapeDtypeStruct((m, A_LAT_PAD), BF16),
                   jax.ShapeDtypeStruct((m, A_KV_LORA), F32), jax.ShapeDtypeStruct((m, A_ROPE), F32)],
        compiler_params=_params("parallel"),
        name="mla_proj",
    )(xb, w1, gq.reshape(1, -1), gkv.reshape(1, -1), wq, cosq, sinq, cosk, sink)


def _flash_kernel(*refs, nh, dq, dv, tq, tk, nk, q_off, chunk_causal, forget):
    if forget:
        q_ref, k_ref, v_ref, fq_ref, fk_ref, o_ref, m_s, l_s, acc_s = refs
    else:
        q_ref, k_ref, v_ref, o_ref, m_s, l_s, acc_s = refs
    iq = pl.program_id(2)
    ik = pl.program_id(3)
    q_lo = q_off + iq * tq
    q_hi = q_lo + tq - 1
    if chunk_causal:
        vis_lo = (q_lo // CHUNK) * CHUNK + CHUNK - 1
        vis_hi = (q_hi // CHUNK) * CHUNK + CHUNK - 1
    else:
        vis_lo, vis_hi = q_lo, q_hi
    k_lo = ik * tk
    needed = k_lo <= vis_hi
    unmasked = k_lo + tk - 1 <= vis_lo

    @pl.when(ik == 0)
    def _init():
        m_s[...] = jnp.full(m_s.shape, -jnp.inf, F32)
        l_s[...] = jnp.zeros(l_s.shape, F32)
        acc_s[...] = jnp.zeros(acc_s.shape, F32)

    def step(masked):
        if masked:
            q_pos = q_lo + lax.broadcasted_iota(jnp.int32, (tq, tk), 0)
            k_pos = k_lo + lax.broadcasted_iota(jnp.int32, (tq, tk), 1)
            if chunk_causal:
                shift = CHUNK.bit_length() - 1
                mask = jnp.right_shift(k_pos, shift) <= jnp.right_shift(q_pos, shift)
            else:
                mask = k_pos <= q_pos
        for h in range(nh):
            q = q_ref[:, h * dq:(h + 1) * dq]
            k = k_ref[:, h * dq:(h + 1) * dq]
            v = v_ref[:, h * dv:(h + 1) * dv]
            s = lax.dot_general(q, k, (((1,), (1,)), ((), ())), preferred_element_type=F32)
            if forget:
                s = s + fq_ref[:, h:h + 1] - fk_ref[h:h + 1, :]
            if masked:
                s = jnp.where(mask, s, NEG_INF)
            m_prev = m_s[h]
            m_new = jnp.maximum(m_prev, jnp.max(s, axis=-1, keepdims=True))
            alpha = jnp.exp(m_prev - m_new)
            p = jnp.exp(s - m_new)
            l_s[h] = alpha * l_s[h] + jnp.sum(p, axis=-1, keepdims=True)
            acc_s[h] = alpha * acc_s[h] + jnp.dot(p.astype(BF16), v, preferred_element_type=F32)
            m_s[h] = m_new

    @pl.when(needed & unmasked)
    def _plain():
        step(False)

    @pl.when(needed & jnp.logical_not(unmasked))
    def _masked():
        step(True)

    @pl.when(ik == nk - 1)
    def _fin():
        for h in range(nh):
            o_ref[:, h * dv:(h + 1) * dv] = (acc_s[h] / l_s[h]).astype(o_ref.dtype)


def _flash(q, k, v, *, nh, dq, dv, n_hblk, k_col0, v_col0, tq, tk, q_off, chunk_causal,
           fq=None, fk=None):
    b, t_q = q.shape[0], q.shape[1]
    t_k = k.shape[1]
    tq = tq if t_q % tq == 0 else t_q
    tk = tk if t_k % tk == 0 else t_k
    nq, nk = t_q // tq, t_k // tk
    forget = fq is not None

    def last_blk(i):
        q_hi = q_off + (i + 1) * tq - 1
        vis = (q_hi // CHUNK) * CHUNK + CHUNK - 1 if chunk_causal else q_hi
        return jnp.minimum(vis // tk, nk - 1)

    in_specs = [
        pl.BlockSpec((None, tq, nh * dq), lambda bi, h, i, j: (bi, i, h)),
        pl.BlockSpec((None, tk, nh * dq), lambda bi, h, i, j: (bi, jnp.minimum(j, last_blk(i)), k_col0 + h)),
        pl.BlockSpec((None, tk, nh * dv), lambda bi, h, i, j: (bi, jnp.minimum(j, last_blk(i)), v_col0 + h)),
    ]
    args = [q, k, v]
    if forget:
        in_specs += [
            pl.BlockSpec((None, None, tq, LANES), lambda bi, h, i, j: (bi, h, i, 0)),
            pl.BlockSpec((None, None, SUBLANES, tk),
                         lambda bi, h, i, j: (bi, h, 0, jnp.minimum(j, last_blk(i)))),
        ]
        args += [fq, fk]
    kern = functools.partial(_flash_kernel, nh=nh, dq=dq, dv=dv, tq=tq, tk=tk, nk=nk, q_off=q_off,
                             chunk_causal=chunk_causal, forget=forget)
    return pl.pallas_call(
        kern,
        grid=(b, n_hblk, nq, nk),
        in_specs=in_specs,
        out_specs=pl.BlockSpec((None, tq, nh * dv), lambda bi, h, i, j: (bi, i, h)),
        out_shape=jax.ShapeDtypeStruct((b, t_q, n_hblk * nh * dv), BF16),
        scratch_shapes=[pltpu.VMEM((nh, tq, 1), F32), pltpu.VMEM((nh, tq, 1), F32),
                        pltpu.VMEM((nh, tq, dv), F32)],
        compiler_params=_params("parallel", "parallel", "parallel", "arbitrary"),
        name="flash_fox" if forget else "flash_mla",
    )(*args)


FLASH_T = 1024
FLASH_QC = 256
FLASH_AHEAD = 4
A_HEADS_PER_STEP = 2
LOG2E = math.log2(math.e)


def _flash_t_kernel(q_ref, k_ref, vt_ref, o_ref, m_s, l_s, acc_s, *, nh, dq, dv, t, chunk_causal):
    iq = pl.program_id(2)
    m_s[...] = jnp.full(m_s.shape, -jnp.inf, F32)
    l_s[...] = jnp.zeros(l_s.shape, F32)
    acc_s[...] = jnp.zeros(acc_s.shape, F32)

    def block(j, masked):
        row0 = pl.multiple_of(j * t, t)
        qc = FLASH_QC
        chains = [(h, c) for h in range(nh) for c in range(t // qc)]

        def n_keys(c):
            return (c + 1) * qc if masked else t

        def qk(h, c):
            return lax.dot_general(k_ref[pl.ds(row0, n_keys(c)), h * dq:(h + 1) * dq],
                                   q_ref[c * qc:(c + 1) * qc, h * dq:(h + 1) * dq],
                                   (((1,), (1,)), ((), ())), preferred_element_type=F32)

        def softmax_pv(h, c, s):
            cols = slice(c * qc, (c + 1) * qc)
            if masked:
                k_pos = lax.broadcasted_iota(jnp.int32, (n_keys(c), qc), 0)
                q_pos = lax.broadcasted_iota(jnp.int32, (n_keys(c), qc), 1) + c * qc
                if chunk_causal:
                    shift = CHUNK.bit_length() - 1
                    mask = jnp.right_shift(k_pos, shift) <= jnp.right_shift(q_pos, shift)
                else:
                    mask = k_pos <= q_pos
                s = jnp.where(mask, s, NEG_INF)
            m_prev = m_s[h, :, cols]
            m_new = jnp.maximum(m_prev, jnp.max(s, axis=0, keepdims=True))
            alpha = jnp.exp2(m_prev - m_new)
            p = jnp.exp2(s - m_new)
            l_s[h, :, cols] = alpha * l_s[h, :, cols] + jnp.sum(p, axis=0, keepdims=True)
            acc_s[h, :, cols] = alpha * acc_s[h, :, cols] + jnp.dot(
                vt_ref[j, h * dv:(h + 1) * dv, 0:n_keys(c)], p.astype(BF16), preferred_element_type=F32)
            m_s[h, :, cols] = m_new

        pending = [qk(*chain) for chain in chains[:FLASH_AHEAD]]
        for i, chain in enumerate(chains):
            if i + FLASH_AHEAD < len(chains):
                pending.append(qk(*chains[i + FLASH_AHEAD]))
            softmax_pv(*chain, pending.pop(0))

    def full_block(j, carry):
        block(j, False)
        return carry

    lax.fori_loop(0, iq, full_block, 0)
    block(iq, True)
    out = jnp.concatenate([acc_s[h] / l_s[h] for h in range(nh)], axis=0)
    o_ref[...] = out.T.astype(o_ref.dtype)


def _flash_t(q, k, k_col0, vt, *, nh, dq, dv, n_hblk, chunk_causal):
    b, t_all = q.shape[0], q.shape[1]
    t = FLASH_T
    nblk = t_all // t
    kern = functools.partial(_flash_t_kernel, nh=nh, dq=dq, dv=dv, t=t, chunk_causal=chunk_causal)
    return pl.pallas_call(
        kern,
        grid=(b, n_hblk, nblk),
        in_specs=[pl.BlockSpec((None, t, nh * dq), lambda bi, h, i: (bi, i, h)),
                  pl.BlockSpec((None, t_all, nh * dq), lambda bi, h, i: (bi, 0, k_col0 + h),
                               pipeline_mode=pl.Buffered(1)),
                  pl.BlockSpec((None, nblk, nh * dv, t), lambda bi, h, i: (bi, 0, h, 0),
                               pipeline_mode=pl.Buffered(1))],
        out_specs=pl.BlockSpec((None, t, nh * dv), lambda bi, h, i: (bi, i, h)),
        out_shape=jax.ShapeDtypeStruct((b, t_all, n_hblk * nh * dv), BF16),
        scratch_shapes=[pltpu.VMEM((nh, 1, t), F32), pltpu.VMEM((nh, 1, t), F32),
                        pltpu.VMEM((nh, dv, t), F32)],
        compiler_params=_params("parallel", "parallel", "parallel"),
        name="flash_t_chunk" if chunk_causal else "flash_t_frame",
    )(q, k, vt)


def _key_block_transpose(v, t):
    b, t_all, c = v.shape
    return v.reshape(b, t_all // t, t, c).transpose(0, 1, 3, 2)


B_HEADS_PER_STEP = LANES // B_HEAD_DIM


def _band_kernel(*refs, nkb, tkb, clamp_front):
    q_ref = refs[0]
    k_refs = refs[1:1 + nkb]
    v_refs = refs[1 + nkb:1 + 2 * nkb]
    bias_ref, o_ref = refs[1 + 2 * nkb], refs[2 + 2 * nkb]
    iq = pl.program_id(2)
    dh = B_HEAD_DIM
    for h in range(B_HEADS_PER_STEP):
        q = q_ref[:, h * dh:(h + 1) * dh]
        ss = []
        for j in range(nkb):
            s = lax.dot_general(q, k_refs[j][:, h * dh:(h + 1) * dh], (((1,), (1,)), ((), ())),
                                preferred_element_type=F32)
            s = s + bias_ref[h, :, j * tkb:(j + 1) * tkb]
            if clamp_front and j < nkb - 1:
                s = jnp.where(iq >= nkb - 1 - j, s, NEG_INF)
            ss.append(s)
        m = functools.reduce(jnp.maximum, [jnp.max(s, axis=-1, keepdims=True) for s in ss])
        ps = [jnp.exp(s - m) for s in ss]
        l = functools.reduce(lambda a, c: a + c, [jnp.sum(p, axis=-1, keepdims=True) for p in ps])
        o = functools.reduce(lambda a, c: a + c, [
            jnp.dot(ps[j].astype(BF16), v_refs[j][:, h * dh:(h + 1) * dh], preferred_element_type=F32)
            for j in range(nkb)])
        o_ref[:, h * dh:(h + 1) * dh] = (o / l).astype(o_ref.dtype)


def _band_bias(rel_bias, q_pos, k_pos):
    nq, nk = len(q_pos), len(k_pos)
    assert (np.diff(q_pos) == 1).all() and (np.diff(k_pos) == 1).all()
    m = np.arange(nq + nk - 1)
    u = rel_bias[:, np.clip(q_pos[0] - k_pos[0] + nq - 1 - m, -B_REL_CLIP, B_REL_CLIP) + B_REL_CLIP]
    period = nq + nk
    w = jnp.concatenate([u[:, nq - 1:], jnp.zeros((u.shape[0], 1), u.dtype), u[:, :nq - 1]], axis=1)
    skew = jnp.tile(w, (1, nq))[:, :nq * (period - 1)].reshape(-1, nq, period - 1)[:, :, :nk]
    qc = q_pos[:, None] // CHUNK
    kc = k_pos[None, :] // CHUNK
    mask = (kc <= qc) & (kc >= qc - B_LEFT_CHUNKS) & (k_pos[None, :] >= 0)
    return jnp.where(jnp.asarray(mask)[None], skew, NEG_INF).astype(F32)


def _band(q, k, v, k_col0, v_col0, bias, *, tq, tkb, nkb, clamp_front):
    b, t_q = q.shape[0], q.shape[1]
    nq = t_q // tq
    n_hblk = B_HEADS // B_HEADS_PER_STEP

    def kv_spec(j, col0):
        back = nkb - 1 - j
        return pl.BlockSpec((None, tkb, LANES),
                            lambda bi, h, i: (bi, jnp.maximum(i - back, 0), col0 + h))

    in_specs = ([pl.BlockSpec((None, tq, LANES), lambda bi, h, i: (bi, i, h))]
                + [kv_spec(j, k_col0) for j in range(nkb)]
                + [kv_spec(j, v_col0) for j in range(nkb)]
                + [pl.BlockSpec((B_HEADS_PER_STEP, tq, nkb * tkb), lambda bi, h, i: (h, 0, 0))])
    kern = functools.partial(_band_kernel, nkb=nkb, tkb=tkb, clamp_front=clamp_front)
    return pl.pallas_call(
        kern,
        grid=(b, n_hblk, nq),
        in_specs=in_specs,
        out_specs=pl.BlockSpec((None, tq, LANES), lambda bi, h, i: (bi, i, h)),
        out_shape=jax.ShapeDtypeStruct((b, t_q, D_MODEL), BF16),
        compiler_params=_params("parallel", "parallel", "parallel"),
        name="band_attn",
    )(q, *([k] * nkb), *([v] * nkb), bias)


BAND_T = 4 * CHUNK


def _band_t_kernel(*refs, nkb, t):
    q_ref = refs[0]
    k_refs = refs[1:1 + nkb]
    vt_refs = refs[1 + nkb:1 + 2 * nkb]
    bias_ref, o_ref = refs[1 + 2 * nkb], refs[2 + 2 * nkb]
    iq = pl.program_id(1)
    dh = B_HEAD_DIM

    def qk(h):
        cols = slice(h * dh, (h + 1) * dh)
        parts = []
        for j in range(nkb):
            s = lax.dot_general(k_refs[j][:, cols], q_ref[:, cols], (((1,), (1,)), ((), ())),
                                preferred_element_type=F32)
            if j < nkb - 1:
                s = jnp.where(iq >= nkb - 1 - j, s, NEG_INF)
            parts.append(s)
        return jnp.concatenate(parts, axis=0) + bias_ref[h]

    def softmax_pv(h, s):
        p = jnp.exp2(s - jnp.max(s, axis=0, keepdims=True))
        l = jnp.sum(p, axis=0, keepdims=True)
        pb = p.astype(BF16)
        o = functools.reduce(lambda a, c: a + c, [
            jnp.dot(vt_refs[j][h * dh:(h + 1) * dh, :], pb[j * t:(j + 1) * t], preferred_element_type=F32)
            for j in range(nkb)])
        return o / l

    heads = list(range(B_HEADS))
    pending = [qk(h) for h in heads[:FLASH_AHEAD]]
    outs = []
    for h in heads:
        if h + FLASH_AHEAD < B_HEADS:
            pending.append(qk(h + FLASH_AHEAD))
        outs.append(softmax_pv(h, pending.pop(0)))
        if len(outs) == B_HEADS_PER_STEP:
            c0 = (h + 1 - B_HEADS_PER_STEP) * dh
            o_ref[:, c0:c0 + LANES] = jnp.concatenate(outs, axis=0).T.astype(o_ref.dtype)
            outs = []


def _band_t(q, kv, vt, bias_t, *, nkb):
    b, t_all = q.shape[0], q.shape[1]
    t = BAND_T

    def back(j):
        return nkb - 1 - j

    k_specs = [pl.BlockSpec((None, t, D_MODEL), lambda bi, i, j=j: (bi, jnp.maximum(i - back(j), 0), 0))
               for j in range(nkb)]
    vt_specs = [pl.BlockSpec((None, None, D_MODEL, t),
                             lambda bi, i, j=j: (bi, jnp.maximum(i - back(j), 0), 0, 0))
                for j in range(nkb)]
    return pl.pallas_call(
        functools.partial(_band_t_kernel, nkb=nkb, t=t),
        grid=(b, t_all // t),
        in_specs=([pl.BlockSpec((None, t, D_MODEL), lambda bi, i: (bi, i, 0))] + k_specs + vt_specs
                  + [pl.BlockSpec(bias_t.shape, lambda bi, i: (0, 0, 0), pipeline_mode=pl.Buffered(1))]),
        out_specs=pl.BlockSpec((None, t, D_MODEL), lambda bi, i: (bi, i, 0)),
        out_shape=jax.ShapeDtypeStruct((b, t_all, D_MODEL), BF16),
        compiler_params=_params("parallel", "parallel"),
        name="band_t",
    )(q, *([kv] * nkb), *([vt] * nkb), bias_t)


def _logf_kernel(x_ref, w_ref, b_ref, o_ref):
    z = jnp.dot(x_ref[...], w_ref[...], preferred_element_type=F32) + b_ref[...]
    o_ref[...] = -(jnp.maximum(-z, 0.0) + jnp.log1p(jnp.exp(-jnp.abs(z))))


def _logf(xb, w_pad, b_pad, tm=512):
    m = xb.shape[0]
    tm = _row_tile(m, tm)
    return pl.pallas_call(
        _logf_kernel,
        grid=(m // tm,),
        in_specs=[pl.BlockSpec((tm, D_MODEL), lambda i: (i, 0)),
                  pl.BlockSpec((D_MODEL, LANES), lambda i: (0, 0)),
                  pl.BlockSpec((1, LANES), lambda i: (0, 0))],
        out_specs=pl.BlockSpec((tm, LANES), lambda i: (i, 0)),
        out_shape=jax.ShapeDtypeStruct((m, LANES), F32),
        compiler_params=_params("parallel"),
        name="logf",
    )(xb, w_pad, b_pad)


def _split3(x):
    hi = x.astype(BF16)
    r = x - hi.astype(F32)
    mid = r.astype(BF16)
    lo = (r - mid.astype(F32)).astype(BF16)
    return hi, mid, lo


def _cumsum_kernel(x_ref, o_ref, hi_ref, mid_ref, lo_ref, carry, *, tc):
    @pl.when(pl.program_id(1) == 0)
    def _():
        carry[...] = jnp.zeros(carry.shape, F32)

    tri = (lax.broadcasted_iota(jnp.int32, (tc, tc), 0)
           >= lax.broadcasted_iota(jnp.int32, (tc, tc), 1)).astype(BF16)
    c = functools.reduce(lambda a, b: a + b, [jnp.dot(tri, piece, preferred_element_type=F32)
                                              for piece in _split3(x_ref[...])])
    out = c + carry[0:1, :]
    o_ref[...] = out
    hi_ref[...], mid_ref[...], lo_ref[...] = _split3(out * LOG2E)
    carry[...] = jnp.broadcast_to(out[tc - 1:tc, :], carry.shape)


def _cumsum(x, tc=256):
    b, t, _ = x.shape
    spec = pl.BlockSpec((None, tc, LANES), lambda bi, i: (bi, i, 0))
    piece = jax.ShapeDtypeStruct(x.shape, BF16)
    return pl.pallas_call(
        functools.partial(_cumsum_kernel, tc=tc),
        grid=(b, t // tc),
        in_specs=[spec],
        out_specs=[spec] * 4,
        out_shape=[jax.ShapeDtypeStruct(x.shape, F32), piece, piece, piece],
        scratch_shapes=[pltpu.VMEM((SUBLANES, LANES), F32)],
        compiler_params=_params("parallel", "arbitrary"),
        name="cumsum",
    )(x)


FFN_CHUNK = 256
CONV_ROWS = SUBLANES


def _ffn_up_kernel(x_ref, w_ref, c_ref, p_ref, act_ref, s_ref, halo, *, tm):
    @pl.when(pl.program_id(1) == 0)
    def _():
        halo[...] = p_ref[...]

    x = x_ref[...]
    n_chunks = D_FF // FFN_CHUNK
    groups = tm // CONV_ROWS
    row = lax.broadcasted_iota(jnp.int32, (groups, CONV_ROWS, FFN_CHUNK), 1)

    def up(c, half):
        col = half * D_FF + c * FFN_CHUNK
        return jnp.dot(x, w_ref[:, col:col + FFN_CHUNK], preferred_element_type=F32)

    def conv(h, c, half):
        cols = slice(half * D_FF + c * FFN_CHUNK, half * D_FF + (c + 1) * FFN_CHUNK)
        ext = jnp.concatenate([halo[:, cols], h], axis=0).reshape(groups + 1, CONV_ROWS, FFN_CHUNK)
        hc = c_ref[CONV_W:CONV_W + 1, cols] + c_ref[CONV_W - 1:CONV_W, cols] * h
        for s in range(1, CONV_W):
            rot = pltpu.roll(ext, s, axis=1)
            shifted = jnp.where(row < s, rot[:groups], rot[1:]).reshape(tm, FFN_CHUNK)
            hc = hc + c_ref[CONV_W - 1 - s:CONV_W - s, cols] * shifted
        tail = h[tm - CONV_ROWS:tm]
        halo[:, cols] = tail
        s_ref[:, cols] = tail
        return hc

    pending = [(up(0, 0), up(0, 1))]
    for c in range(n_chunks):
        if c + 1 < n_chunks:
            pending.append((up(c + 1, 0), up(c + 1, 1)))
        ha, hg = pending.pop(0)
        a = conv(ha, c, 0)
        g = conv(hg, c, 1)
        act_ref[:, c * FFN_CHUNK:(c + 1) * FFN_CHUNK] = (g * jax.nn.sigmoid(g) * a).astype(BF16)


def _ffn_up(xb, w_up, conv_tab, past, tm=512):
    b, t, _ = xb.shape
    tm = _row_tile(t, tm)
    fixed = lambda bi, ti: (0, 0)
    return pl.pallas_call(
        functools.partial(_ffn_up_kernel, tm=tm),
        grid=(b, t // tm),
        in_specs=[pl.BlockSpec((None, tm, D_MODEL), lambda bi, ti: (bi, ti, 0)),
                  pl.BlockSpec((D_MODEL, 2 * D_FF), fixed),
                  pl.BlockSpec((CONV_ROWS, 2 * D_FF), fixed),
                  pl.BlockSpec((None, CONV_ROWS, 2 * D_FF), lambda bi, ti: (bi, 0, 0))],
        out_specs=[pl.BlockSpec((None, tm, D_FF), lambda bi, ti: (bi, ti, 0)),
                   pl.BlockSpec((None, CONV_ROWS, 2 * D_FF), lambda bi, ti: (bi, 0, 0))],
        out_shape=[jax.ShapeDtypeStruct((b, t, D_FF), BF16),
                   jax.ShapeDtypeStruct((b, CONV_ROWS, 2 * D_FF), F32)],
        scratch_shapes=[pltpu.VMEM((CONV_ROWS, 2 * D_FF), F32)],
        compiler_params=_params("parallel", "arbitrary"),
        name="ffn_up",
    )(xb, w_up, conv_tab, past)


def _rope_tables(pos, batch, q_scale):
    half = A_ROPE // 2
    inv_freq = ROPE_THETA ** (-jnp.arange(half, dtype=F32) / half)
    ang = pos.astype(F32)[:, None] * inv_freq
    cos, sin = jnp.cos(ang), jnp.sin(ang)
    cosk = jnp.concatenate([cos, cos], axis=-1)
    sink = jnp.concatenate([sin, sin], axis=-1)
    t = pos.shape[0]
    pad = jnp.zeros((t, A_QK_PAD - A_NOPE - A_ROPE), F32)
    cosq = q_scale * jnp.concatenate([jnp.ones((t, A_NOPE), F32), cosk, pad], axis=-1)
    sinq = q_scale * jnp.concatenate([jnp.zeros((t, A_NOPE), F32), sink, pad], axis=-1)
    return tuple(jnp.tile(a, (batch, 1)) for a in (cosq, sinq, cosk, sink))


def _swap_halves(w):
    half = w.shape[-1] // 2
    return jnp.concatenate([-w[..., half:], w[..., :half]], axis=-1)


def _mla_weights(w_dq, w_dkv, w_kr, w_uq, w_uk, w_uv):
    zc = jnp.zeros((D_MODEL, LANES - A_ROPE), F32)
    w1 = jnp.concatenate([w_dq, w_dkv, w_kr, zc, _swap_halves(w_kr), zc], axis=1).astype(BF16)
    wq = w_uq.reshape(A_Q_LORA, A_HEADS, A_NOPE + A_ROPE)
    nope, rope = wq[..., :A_NOPE], wq[..., A_NOPE:]
    zpad = jnp.zeros((A_Q_LORA, A_HEADS, A_QK_PAD - A_NOPE - A_ROPE), F32)
    w_cat = jnp.concatenate([nope, rope, zpad], axis=-1).reshape(A_Q_LORA, -1)
    w_sw = jnp.concatenate([jnp.zeros_like(nope), _swap_halves(rope), zpad], axis=-1).reshape(A_Q_LORA, -1)
    wq2 = jnp.concatenate([w_cat, w_sw], axis=1).astype(BF16)
    wk = jnp.zeros((A_LAT_PAD, A_HEADS, A_QK_PAD), F32)
    wk = wk.at[:A_KV_LORA, :, :A_NOPE].set(w_uk)
    eye = jnp.broadcast_to(jnp.eye(A_ROPE, dtype=F32)[:, None, :], (A_ROPE, A_HEADS, A_ROPE))
    wk = wk.at[A_KV_LORA:A_KV_LORA + A_ROPE, :, A_NOPE:A_NOPE + A_ROPE].set(eye)
    wv = jnp.zeros((A_LAT_PAD, A_HEADS * A_V), F32).at[:A_KV_LORA].set(w_uv.reshape(A_KV_LORA, -1))
    wkv = jnp.concatenate([wk.reshape(A_LAT_PAD, -1), wv], axis=1).astype(BF16)
    return w1, wq2, wkv


def _mla_mixer(xb, b, t, pos, ckv_past, kpe_past, w, i):
    w1, wq2, wkv = _mla_weights(w['a_w_dq'][i], w['a_w_dkv'][i], w['a_w_kr'][i], w['a_w_uq'][i],
                                w['a_w_uk'][i], w['a_w_uv'][i])
    resident = ckv_past is None and t % FLASH_T == 0
    q_scale = (A_NOPE + A_ROPE) ** -0.5 * (LOG2E if resident else 1.0)
    q, lat, ckv, kpe = _mla_proj(xb, w1, w['a_g_q'][i], w['a_g_kv'][i], wq2,
                                 *_rope_tables(pos, b, q_scale))
    lat = lat.reshape(b, t, A_LAT_PAD)
    q_off = 0
    if ckv_past is not None:
        p_len = ckv_past.shape[1]
        past = jnp.concatenate(
            [ckv_past, kpe_past, jnp.zeros((b, p_len, A_LAT_PAD - A_KV_LORA - A_ROPE), F32)], axis=-1)
        lat = jnp.concatenate([past.astype(BF16), lat], axis=1)
        q_off = p_len
    t_k = lat.shape[1]
    (kv,) = _mm(lat.reshape(b * t_k, A_LAT_PAD), wkv, [BF16])
    kv = kv.reshape(b, t_k, -1)
    if resident:
        vt = _key_block_transpose(kv[:, :, A_HEADS * A_QK_PAD:], FLASH_T)
        o = _flash_t(q.reshape(b, t, -1), kv, 0, vt, nh=A_HEADS_PER_STEP, dq=A_QK_PAD, dv=A_V,
                     n_hblk=A_HEADS // A_HEADS_PER_STEP, chunk_causal=True)
    else:
        o = _flash(q.reshape(b, t, -1), kv, kv, nh=1, dq=A_QK_PAD, dv=A_V, n_hblk=A_HEADS,
                   k_col0=0, v_col0=A_HEADS * A_QK_PAD // A_V, tq=512, tk=512, q_off=q_off,
                   chunk_causal=True)
    return o.reshape(b * t, A_HEADS * A_V), ckv.reshape(b, t, -1), kpe.reshape(b, t, -1)


def _qkv(xb, w_qkv, q_scale):
    wq = (w_qkv[:, :D_MODEL] * q_scale).astype(BF16)
    (q,) = _mm(xb, wq, [BF16])
    kv32, kvb = _mm(xb, w_qkv[:, D_MODEL:].astype(BF16), [F32, BF16])
    return q, kv32, kvb


def _band_mixer(xb, b, t, pos0, k_past, v_past, w, i):
    prompt = k_past is None
    assert not prompt or t % BAND_T == 0
    q, kv32, kvb = _qkv(xb, w['b_w_qkv'][i], B_HEAD_DIM ** -0.5 * (LOG2E if prompt else 1.0))
    q = q.reshape(b, t, D_MODEL)
    k32 = kv32[:, :D_MODEL].reshape(b, t, B_HEADS, B_HEAD_DIM)
    v32 = kv32[:, D_MODEL:].reshape(b, t, B_HEADS, B_HEAD_DIM)
    n_cols = D_MODEL // LANES
    if prompt:
        nkb = B_WIN // BAND_T + 1
        bias = _band_bias(w['b_rel_bias'][i], B_WIN + np.arange(BAND_T), np.arange(B_WIN + BAND_T))
        kvb = kvb.reshape(b, t, 2 * D_MODEL)
        o = _band_t(q, kvb, _key_block_transpose(kvb[:, :, D_MODEL:], BAND_T),
                    LOG2E * bias.transpose(0, 2, 1), nkb=nkb)
        keep = min(B_WIN, t)
        k_new, v_new = k32[:, t - keep:], v32[:, t - keep:]
    else:
        p_len = k_past.shape[1]
        kvb = kvb.reshape(b, t, 2 * D_MODEL)
        past = jnp.concatenate([k_past.reshape(b, p_len, D_MODEL), v_past.reshape(b, p_len, D_MODEL)],
                               axis=-1).astype(BF16)
        kv_all = jnp.concatenate([past, kvb], axis=1)
        q_pos = pos0 + np.arange(t)
        k_pos = np.concatenate([np.arange(pos0 - p_len, pos0), q_pos])
        bias = _band_bias(w['b_rel_bias'][i], q_pos, k_pos)
        o = _band(q, kv_all, kv_all, 0, n_cols, bias, tq=t, tkb=p_len + t, nkb=1, clamp_front=False)
        k_new, v_new = k32, v32
    return o.reshape(b * t, D_MODEL), k_new, v_new


C_AUG = LANES
N_PIECES = 3


def _aug_kernel(x_ref, w_ref, hi_ref, mid_ref, lo_ref, p_ref, b_ref, o_ref):
    acc = jnp.dot(x_ref[...], w_ref[...], preferred_element_type=F32) + b_ref[...]
    for piece, f_ref in enumerate((hi_ref, mid_ref, lo_ref)):
        acc = acc + jnp.dot(f_ref[...], p_ref[piece], preferred_element_type=F32)
    o_ref[...] = acc.astype(BF16)


def _aug_tables():
    n = C_HEADS * C_AUG
    place = np.zeros((N_PIECES, LANES, 2 * n), np.float32)
    ones = np.zeros((1, 2 * n), np.float32)
    for h in range(C_HEADS):
        base = h * C_AUG + C_HEAD_DIM
        for piece in range(N_PIECES):
            ones[0, base + piece] = 1.0
            place[piece, h, base + N_PIECES + piece] = 1.0
            place[piece, h, n + base + piece] = -1.0
            ones[0, n + base + N_PIECES + piece] = 1.0
    return jnp.asarray(place, BF16), jnp.asarray(ones)


def _aug_qk(xb, w_aug, pieces, tm=512, tn=1024):
    m = xb.shape[0]
    n = w_aug.shape[1]
    tm = _row_tile(m, tm)
    place, ones = _aug_tables()
    row = lambda i, j: (i, 0)
    return pl.pallas_call(
        _aug_kernel,
        grid=(m // tm, n // tn),
        in_specs=[pl.BlockSpec((tm, D_MODEL), row), pl.BlockSpec((D_MODEL, tn), lambda i, j: (0, j)),
                  pl.BlockSpec((tm, LANES), row), pl.BlockSpec((tm, LANES), row), pl.BlockSpec((tm, LANES), row),
                  pl.BlockSpec((N_PIECES, LANES, tn), lambda i, j: (0, 0, j)),
                  pl.BlockSpec((1, tn), lambda i, j: (0, j))],
        out_specs=pl.BlockSpec((tm, tn), lambda i, j: (i, j)),
        out_shape=jax.ShapeDtypeStruct((m, n), BF16),
        compiler_params=_params("parallel", "parallel"),
        name="aug_qk",
    )(xb, w_aug, *pieces, place, ones)


def _pad_heads(w, scale):
    w = (w * scale).reshape(D_MODEL, C_HEADS, C_HEAD_DIM)
    return jnp.pad(w, ((0, 0), (0, 0), (0, C_AUG - C_HEAD_DIM))).reshape(D_MODEL, C_HEADS * C_AUG)


def _fox_mixer(xb, b, t, k_past, v_past, lf_past, w, i):
    resident = k_past is None and t % FLASH_T == 0
    w_qkv = w['c_w_qkv'][i]
    kv32, kvb = _mm(xb, w_qkv[:, D_MODEL:].astype(BF16), [F32, BF16])
    k32 = kv32[:, :D_MODEL].reshape(b, t, C_HEADS, C_HEAD_DIM)
    v32 = kv32[:, D_MODEL:].reshape(b, t, C_HEADS, C_HEAD_DIM)
    w_f = jnp.zeros((D_MODEL, LANES), F32).at[:, :C_HEADS].set(w['c_w_f'][i]).astype(BF16)
    b_f = jnp.zeros((1, LANES), F32).at[0, :C_HEADS].set(w['c_b_f'][i])
    log_f = _logf(xb, w_f, b_f).reshape(b, t, LANES)
    kvb = kvb.reshape(b, t, 2 * D_MODEL)
    lf_all = log_f
    q_off = 0
    if k_past is not None:
        p_len = k_past.shape[1]
        past = jnp.concatenate([k_past.reshape(b, p_len, D_MODEL), v_past.reshape(b, p_len, D_MODEL)],
                               axis=-1).astype(BF16)
        kvb = jnp.concatenate([past, kvb], axis=1)
        lf_all = jnp.concatenate([jnp.pad(lf_past, ((0, 0), (0, 0), (0, LANES - C_HEADS))), log_f], axis=1)
        q_off = p_len
    t_k = lf_all.shape[1]
    tc = 256
    t_pad = -(-t_k // tc) * tc
    f_cum, *f_pieces = _cumsum(jnp.pad(lf_all, ((0, 0), (0, t_pad - t_k), (0, 0))), tc)
    nh = LANES // C_HEAD_DIM
    n_hblk = C_HEADS // nh
    q_scale = C_HEAD_DIM ** -0.5
    if resident:
        w_aug = jnp.concatenate([_pad_heads(w_qkv[:, :D_MODEL], q_scale * LOG2E),
                                 _pad_heads(w_qkv[:, D_MODEL:2 * D_MODEL], 1.0)], axis=1).astype(BF16)
        qk_aug = _aug_qk(xb, w_aug, [p.reshape(b * t, LANES) for p in f_pieces]).reshape(b, t, -1)
        n_q = C_HEADS * C_AUG
        vt = _key_block_transpose(kvb[:, :, D_MODEL:], FLASH_T)
        o = _flash_t(qk_aug, qk_aug, n_q // (nh * C_AUG), vt,
                     nh=nh, dq=C_AUG, dv=C_HEAD_DIM, n_hblk=n_hblk, chunk_causal=False)
        return o.reshape(b * t, D_MODEL), k32, v32, log_f[:, :, :C_HEADS]
    (q,) = _mm(xb, (w_qkv[:, :D_MODEL] * q_scale).astype(BF16), [BF16])
    f_blk = f_cum[:, :t_k, :C_HEADS].reshape(b, t_k, n_hblk, nh).transpose(0, 2, 1, 3)
    fq = jnp.pad(f_blk[:, :, t_k - t:], ((0, 0), (0, 0), (0, 0), (0, LANES - nh)))
    fk = jnp.pad(f_blk.transpose(0, 1, 3, 2), ((0, 0), (0, 0), (0, SUBLANES - nh), (0, 0)))
    o = _flash(q.reshape(b, t, D_MODEL), kvb, kvb, nh=nh, dq=C_HEAD_DIM, dv=C_HEAD_DIM, n_hblk=n_hblk,
               k_col0=0, v_col0=n_hblk, tq=512, tk=512, q_off=q_off, chunk_causal=False, fq=fq, fk=fk)
    return o.reshape(b * t, D_MODEL), k32, v32, log_f[:, :, :C_HEADS]


def _conv_ffn(xb, b, t, conv_past, w, i):
    tab = jnp.concatenate([w['f_conv_w'][i], w['f_conv_b'][i][None],
                           jnp.zeros((CONV_ROWS - CONV_W - 1, 2 * D_FF), F32)], axis=0)
    if conv_past is None:
        past = jnp.zeros((b, CONV_ROWS, 2 * D_FF), F32)
    else:
        past = jnp.pad(conv_past, ((0, 0), (CONV_ROWS - (CONV_W - 1), 0), (0, 0)))
    act, tail = _ffn_up(xb.reshape(b, t, D_MODEL), w['f_w_up'][i].astype(BF16), tab, past)
    return act.reshape(b * t, D_FF), tail[:, CONV_ROWS - (CONV_W - 1):]


def _trunk(x, pos0, past, w):
    b, t, _ = x.shape
    pos = pos0 + jnp.arange(t)
    xf = x.reshape(b * t, D_MODEL)
    xb = xf.astype(BF16)
    outs = {n: [] for n in ('a_ckv', 'a_kpe', 'b_k', 'b_v', 'c_k', 'c_v', 'c_logf', 'ffn_conv')}
    ia = ib = ic = 0
    get = lambda name, j: None if past is None else past[name][j]
    for i in range(DEPTH):
        kind = i % N_MIXERS
        if kind == 0:
            o, ckv, kpe = _mla_mixer(xb, b, t, pos, get('a_ckv', ia), get('a_kpe', ia), w, ia)
            outs['a_ckv'].append(ckv)
            outs['a_kpe'].append(kpe)
            w_o = w['a_w_o'][ia]
            ia += 1
        elif kind == 1:
            o, kb, vb = _band_mixer(xb, b, t, pos0, get('b_k', ib), get('b_v', ib), w, ib)
            outs['b_k'].append(kb)
            outs['b_v'].append(vb)
            w_o = w['b_w_o'][ib]
            ib += 1
        else:
            o, kc, vc, lf = _fox_mixer(xb, b, t, get('c_k', ic), get('c_v', ic), get('c_logf', ic), w, ic)
            outs['c_k'].append(kc)
            outs['c_v'].append(vc)
            outs['c_logf'].append(lf)
            w_o = w['c_w_o'][ic]
            ic += 1
        xf, xb = _mm_res_ln(o, w_o.astype(BF16), xf, w['ln1_g'][i], w['ln1_b'][i])
        act, conv_state = _conv_ffn(xb, b, t, get('ffn_conv', i), w, i)
        outs['ffn_conv'].append(conv_state)
        xf, xb = _mm_res_ln(act, w['f_w_down'][i].astype(BF16), xf, w['ln2_g'][i], w['ln2_b'][i])
    return xf.reshape(b, t, D_MODEL), {n: jnp.stack(v) for n, v in outs.items()}


def kernel(x_prompt, x_sample, cache_a_ckv, cache_a_kpe, cache_b_k, cache_b_v, cache_c_k, cache_c_v,
           cache_c_logf, state_ffn_conv, a_w_dq, a_g_q, a_w_uq, a_w_dkv, a_g_kv, a_w_kr, a_w_uk, a_w_uv,
           a_w_o, b_w_qkv, b_rel_bias, b_w_o, c_w_qkv, c_w_f, c_b_f, c_w_o, f_w_up, f_conv_w, f_conv_b,
           f_w_down, ln1_g, ln1_b, ln2_g, ln2_b):
    w = dict(a_w_dq=a_w_dq, a_g_q=a_g_q, a_w_uq=a_w_uq, a_w_dkv=a_w_dkv, a_g_kv=a_g_kv, a_w_kr=a_w_kr,
             a_w_uk=a_w_uk, a_w_uv=a_w_uv, a_w_o=a_w_o, b_w_qkv=b_w_qkv, b_rel_bias=b_rel_bias, b_w_o=b_w_o,
             c_w_qkv=c_w_qkv, c_w_f=c_w_f, c_b_f=c_b_f, c_w_o=c_w_o, f_w_up=f_w_up, f_conv_w=f_conv_w,
             f_conv_b=f_conv_b, f_w_down=f_w_down, ln1_g=ln1_g, ln1_b=ln1_b, ln2_g=ln2_g, ln2_b=ln2_b)
    past = dict(a_ckv=cache_a_ckv, a_kpe=cache_a_kpe, b_k=cache_b_k, b_v=cache_b_v, c_k=cache_c_k,
                c_v=cache_c_v, c_logf=cache_c_logf, ffn_conv=state_ffn_conv)
    past_len = cache_a_ckv.shape[2]
    y_prompt, p = _trunk(x_prompt, 0, None, w)
    y_sample, s = _trunk(x_sample, past_len, past, w)
    names = ('a_ckv', 'a_kpe', 'b_k', 'b_v', 'c_k', 'c_v', 'c_logf', 'ffn_conv')
    return (y_prompt, y_sample) + tuple(p[n] for n in names) + tuple(s[n] for n in names)
```

```python
import functools
import math

import numpy as np
import jax
import jax.numpy as jnp
from jax import lax
from jax.experimental import pallas as pl
from jax.experimental.pallas import tpu as pltpu

F32 = jnp.float32
BF16 = jnp.bfloat16

D_MODEL = 1024
DEPTH = 4
CHUNK = 64
N_MIXERS = 3

A_HEADS = 8
A_Q_LORA = 384
A_KV_LORA = 256
A_NOPE = 128
A_ROPE = 64
A_V = 128
A_QK_PAD = 256
A_LAT_PAD = 384
ROPE_THETA = 10000.0

B_HEADS = 16
B_HEAD_DIM = D_MODEL // B_HEADS
B_LEFT_CHUNKS = 8
B_WIN = B_LEFT_CHUNKS * CHUNK
B_REL_CLIP = 128

C_HEADS = 16
C_HEAD_DIM = D_MODEL // C_HEADS

D_FF = 2816
CONV_W = 3

ALPHA = (2.0 * DEPTH) ** 0.25
LN_EPS = 1e-5
RMS_EPS = 1e-6
NEG_INF = -1e30

LANES = 128
SUBLANES = 8
VMEM_LIMIT_BYTES = 48 * 2 ** 20


def _params(*sem):
    return pltpu.CompilerParams(dimension_semantics=sem, vmem_limit_bytes=VMEM_LIMIT_BYTES)


def _row_tile(m, tm):
    while m % tm:
        tm //= 2
    assert tm % SUBLANES == 0, (m, tm)
    return tm


def _mm_kernel(x_ref, w_ref, *o_refs):
    acc = jnp.dot(x_ref[...], w_ref[...], preferred_element_type=F32)
    for o_ref in o_refs:
        o_ref[...] = acc.astype(o_ref.dtype)


def _mm(x, w, out_dtypes, tm=512, tn=1024):
    m, k = x.shape
    n = w.shape[1]
    tm, tn = _row_tile(m, tm), min(tn, n)
    return pl.pallas_call(
        _mm_kernel,
        grid=(m // tm, n // tn),
        in_specs=[pl.BlockSpec((tm, k), lambda i, j: (i, 0)),
                  pl.BlockSpec((k, tn), lambda i, j: (0, j))],
        out_specs=[pl.BlockSpec((tm, tn), lambda i, j: (i, j)) for _ in out_dtypes],
        out_shape=[jax.ShapeDtypeStruct((m, n), d) for d in out_dtypes],
        compiler_params=_params("parallel", "parallel"),
        name="mm",
    )(x, w)


def _mm_res_ln_kernel(a_ref, w_ref, x_ref, g_ref, b_ref, of_ref, ob_ref):
    y = ALPHA * x_ref[...] + jnp.dot(a_ref[...], w_ref[...], preferred_element_type=F32)
    mu = jnp.mean(y, axis=-1, keepdims=True)
    d = y - mu
    var = jnp.mean(d * d, axis=-1, keepdims=True)
    out = d * lax.rsqrt(var + LN_EPS) * g_ref[...] + b_ref[...]
    of_ref[...] = out
    ob_ref[...] = out.astype(BF16)


def _mm_res_ln(a, w, x, g, b, tm=512):
    m, k = a.shape
    n = w.shape[1]
    tm = _row_tile(m, tm)
    row = lambda i: (i, 0)
    fixed = lambda i: (0, 0)
    return pl.pallas_call(
        _mm_res_ln_kernel,
        grid=(m // tm,),
        in_specs=[pl.BlockSpec((tm, k), row), pl.BlockSpec((k, n), fixed),
                  pl.BlockSpec((tm, n), row), pl.BlockSpec((1, n), fixed),
                  pl.BlockSpec((1, n), fixed)],
        out_specs=[pl.BlockSpec((tm, n), row), pl.BlockSpec((tm, n), row)],
        out_shape=[jax.ShapeDtypeStruct((m, n), F32), jax.ShapeDtypeStruct((m, n), BF16)],
        compiler_params=_params("parallel"),
        name="mm_res_ln",
    )(a, w, x, g.reshape(1, n), b.reshape(1, n))


_W1_CQ = (0, A_Q_LORA)
_W1_CKV = (A_Q_LORA, A_Q_LORA + A_KV_LORA)
_W1_KR = (_W1_CKV[1], _W1_CKV[1] + A_ROPE)
_W1_KRS = (_W1_CKV[1] + LANES, _W1_CKV[1] + LANES + A_ROPE)
_W1_COLS = _W1_CKV[1] + 2 * LANES


def _rms(v, g):
    return v * lax.rsqrt(jnp.mean(v * v, axis=-1, keepdims=True) + RMS_EPS) * g


def _mla_proj_kernel(x_ref, w1_ref, gq_ref, gkv_ref, wq_ref, cq_ref, sq_ref, ck_ref, sk_ref,
                     q_ref, lat_ref, ckv_ref, kpe_ref):
    y = jnp.dot(x_ref[...], w1_ref[...], preferred_element_type=F32)
    cq = _rms(y[:, _W1_CQ[0]:_W1_CQ[1]], gq_ref[...]).astype(BF16)
    ckv = _rms(y[:, _W1_CKV[0]:_W1_CKV[1]], gkv_ref[...])
    kpe = y[:, _W1_KR[0]:_W1_KR[1]] * ck_ref[...] + y[:, _W1_KRS[0]:_W1_KRS[1]] * sk_ref[...]
    ckv_ref[...] = ckv
    kpe_ref[...] = kpe
    lat_ref[:, 0:A_KV_LORA] = ckv.astype(BF16)
    lat_ref[:, A_KV_LORA:A_KV_LORA + A_ROPE] = kpe.astype(BF16)
    lat_ref[:, A_KV_LORA + A_ROPE:] = jnp.zeros(
        (lat_ref.shape[0], A_LAT_PAD - A_KV_LORA - A_ROPE), BF16)
    sw0 = A_HEADS * A_QK_PAD
    for h in range(A_HEADS):
        lo, hi = h * A_QK_PAD, (h + 1) * A_QK_PAD
        qp = jnp.dot(cq, wq_ref[:, lo:hi], preferred_element_type=F32)
        qs = jnp.dot(cq, wq_ref[:, sw0 + lo:sw0 + hi], preferred_element_type=F32)
        q_ref[:, lo:hi] = (qp * cq_ref[...] + qs * sq_ref[...]).astype(BF16)


def _mla_proj(xb, w1, gq, gkv, wq, cosq, sinq, cosk, sink, tm=512):
    m = xb.shape[0]
    tm = _row_tile(m, tm)
    row = lambda i: (i, 0)
    fixed = lambda i: (0, 0)
    nq = A_HEADS * A_QK_PAD
    return pl.pallas_call(
        _mla_proj_kernel,
        grid=(m // tm,),
        in_specs=[pl.BlockSpec((tm, D_MODEL), row), pl.BlockSpec(w1.shape, fixed),
                  pl.BlockSpec((1, A_Q_LORA), fixed), pl.BlockSpec((1, A_KV_LORA), fixed),
                  pl.BlockSpec(wq.shape, fixed),
                  pl.BlockSpec((tm, A_QK_PAD), row), pl.BlockSpec((tm, A_QK_PAD), row),
                  pl.BlockSpec((tm, A_ROPE), row), pl.BlockSpec((tm, A_ROPE), row)],
        out_specs=[pl.BlockSpec((tm, nq), row), pl.BlockSpec((tm, A_LAT_PAD), row),
                   pl.BlockSpec((tm, A_KV_LORA), row), pl.BlockSpec((tm, A_ROPE), row)],
        out_shape=[jax.ShapeDtypeStruct((m, nq), BF16), jax.ShapeDtypeStruct((m, A_LAT_PAD), BF16),
                   jax.ShapeDtypeStruct((m, A_KV_LORA), F32), jax.ShapeDtypeStruct((m, A_ROPE), F32)],
        compiler_params=_params("parallel"),
        name="mla_proj",
    )(xb, w1, gq.reshape(1, -1), gkv.reshape(1, -1), wq, cosq, sinq, cosk, sink)


def _flash_kernel(*refs, nh, dq, dv, tq, tk, nk, q_off, chunk_causal, forget):
    if forget:
        q_ref, k_ref, v_ref, fq_ref, fk_ref, o_ref, m_s, l_s, acc_s = refs
    else:
        q_ref, k_ref, v_ref, o_ref, m_s, l_s, acc_s = refs
    iq = pl.program_id(2)
    ik = pl.program_id(3)
    q_lo = q_off + iq * tq
    q_hi = q_lo + tq - 1
    if chunk_causal:
        vis_lo = (q_lo // CHUNK) * CHUNK + CHUNK - 1
        vis_hi = (q_hi // CHUNK) * CHUNK + CHUNK - 1
    else:
        vis_lo, vis_hi = q_lo, q_hi
    k_lo = ik * tk
    needed = k_lo <= vis_hi
    unmasked = k_lo + tk - 1 <= vis_lo

    @pl.when(ik == 0)
    def _init():
        m_s[...] = jnp.full(m_s.shape, -jnp.inf, F32)
        l_s[...] = jnp.zeros(l_s.shape, F32)
        acc_s[...] = jnp.zeros(acc_s.shape, F32)

    def step(masked):
        if masked:
            q_pos = q_lo + lax.broadcasted_iota(jnp.int32, (tq, tk), 0)
            k_pos = k_lo + lax.broadcasted_iota(jnp.int32, (tq, tk), 1)
            if chunk_causal:
                shift = CHUNK.bit_length() - 1
                mask = jnp.right_shift(k_pos, shift) <= jnp.right_shift(q_pos, shift)
            else:
                mask = k_pos <= q_pos
        for h in range(nh):
            q = q_ref[:, h * dq:(h + 1) * dq]
            k = k_ref[:, h * dq:(h + 1) * dq]
            v = v_ref[:, h * dv:(h + 1) * dv]
            s = lax.dot_general(q, k, (((1,), (1,)), ((), ())), preferred_element_type=F32)
            if forget:
                s = s + fq_ref[:, h:h + 1] - fk_ref[h:h + 1, :]
            if masked:
                s = jnp.where(mask, s, NEG_INF)
            m_prev = m_s[h]
            m_new = jnp.maximum(m_prev, jnp.max(s, axis=-1, keepdims=True))
            alpha = jnp.exp(m_prev - m_new)
            p = jnp.exp(s - m_new)
            l_s[h] = alpha * l_s[h] + jnp.sum(p, axis=-1, keepdims=True)
            acc_s[h] = alpha * acc_s[h] + jnp.dot(p.astype(BF16), v, preferred_element_type=F32)
            m_s[h] = m_new

    @pl.when(needed & unmasked)
    def _plain():
        step(False)

    @pl.when(needed & jnp.logical_not(unmasked))
    def _masked():
        step(True)

    @pl.when(ik == nk - 1)
    def _fin():
        for h in range(nh):
            o_ref[:, h * dv:(h + 1) * dv] = (acc_s[h] / l_s[h]).astype(o_ref.dtype)


def _flash(q, k, v, *, nh, dq, dv, n_hblk, k_col0, v_col0, tq, tk, q_off, chunk_causal,
           fq=None, fk=None):
    b, t_q = q.shape[0], q.shape[1]
    t_k = k.shape[1]
    tq = tq if t_q % tq == 0 else t_q
    tk = tk if t_k % tk == 0 else t_k
    nq, nk = t_q // tq, t_k // tk
    forget = fq is not None

    def last_blk(i):
        q_hi = q_off + (i + 1) * tq - 1
        vis = (q_hi // CHUNK) * CHUNK + CHUNK - 1 if chunk_causal else q_hi
        return jnp.minimum(vis // tk, nk - 1)

    in_specs = [
        pl.BlockSpec((None, tq, nh * dq), lambda bi, h, i, j: (bi, i, h)),
        pl.BlockSpec((None, tk, nh * dq), lambda bi, h, i, j: (bi, jnp.minimum(j, last_blk(i)), k_col0 + h)),
        pl.BlockSpec((None, tk, nh * dv), lambda bi, h, i, j: (bi, jnp.minimum(j, last_blk(i)), v_col0 + h)),
    ]
    args = [q, k, v]
    if forget:
        in_specs += [
            pl.BlockSpec((None, None, tq, LANES), lambda bi, h, i, j: (bi, h, i, 0)),
            pl.BlockSpec((None, None, SUBLANES, tk),
                         lambda bi, h, i, j: (bi, h, 0, jnp.minimum(j, last_blk(i)))),
        ]
        args += [fq, fk]
    kern = functools.partial(_flash_kernel, nh=nh, dq=dq, dv=dv, tq=tq, tk=tk, nk=nk, q_off=q_off,
                             chunk_causal=chunk_causal, forget=forget)
    return pl.pallas_call(
        kern,
        grid=(b, n_hblk, nq, nk),
        in_specs=in_specs,
        out_specs=pl.BlockSpec((None, tq, nh * dv), lambda bi, h, i, j: (bi, i, h)),
        out_shape=jax.ShapeDtypeStruct((b, t_q, n_hblk * nh * dv), BF16),
        scratch_shapes=[pltpu.VMEM((nh, tq, 1), F32), pltpu.VMEM((nh, tq, 1), F32),
                        pltpu.VMEM((nh, tq, dv), F32)],
        compiler_params=_params("parallel", "parallel", "parallel", "arbitrary"),
        name="flash_fox" if forget else "flash_mla",
    )(*args)


FLASH_T = 1024
FLASH_QC = 256
FLASH_AHEAD = 4
ONES_ROWS = 16
A_HEADS_PER_STEP = 2
LOG2E = math.log2(math.e)


def _flash_t_kernel(q_ref, k_ref, vt_ref, o_ref, m_s, acc_s, *, nh, dq, dv, t, chunk_causal):
    iq = pl.program_id(2)
    dva = dv + ONES_ROWS
    m_s[...] = jnp.full(m_s.shape, -jnp.inf, F32)
    acc_s[...] = jnp.zeros(acc_s.shape, F32)

    def block(j, masked):
        row0 = pl.multiple_of(j * t, t)
        qc = FLASH_QC
        chains = [(h, c) for h in range(nh) for c in range(t // qc)]

        def n_keys(c):
            return (c + 1) * qc if masked else t

        def qk(h, c):
            return lax.dot_general(k_ref[pl.ds(row0, n_keys(c)), h * dq:(h + 1) * dq],
                                   q_ref[c * qc:(c + 1) * qc, h * dq:(h + 1) * dq],
                                   (((1,), (1,)), ((), ())), preferred_element_type=F32)

        def softmax_pv(h, c, s):
            cols = slice(c * qc, (c + 1) * qc)
            if masked:
                k_pos = lax.broadcasted_iota(jnp.int32, (n_keys(c), qc), 0)
                q_pos = lax.broadcasted_iota(jnp.int32, (n_keys(c), qc), 1) + c * qc
                if chunk_causal:
                    shift = CHUNK.bit_length() - 1
                    mask = jnp.right_shift(k_pos, shift) <= jnp.right_shift(q_pos, shift)
                else:
                    mask = k_pos <= q_pos
                s = jnp.where(mask, s, NEG_INF)
            m_prev = m_s[h, :, cols]
            m_new = jnp.maximum(m_prev, jnp.max(s, axis=0, keepdims=True))
            alpha = jnp.exp2(m_prev - m_new)
            p = jnp.exp2((s - m_new).astype(BF16))
            acc_s[h, :, cols] = alpha * acc_s[h, :, cols] + jnp.dot(
                vt_ref[j, h * dva:(h + 1) * dva, 0:n_keys(c)], p, preferred_element_type=F32)
            m_s[h, :, cols] = m_new

        pending = [qk(*chain) for chain in chains[:FLASH_AHEAD]]
        for i, chain in enumerate(chains):
            if i + FLASH_AHEAD < len(chains):
                pending.append(qk(*chains[i + FLASH_AHEAD]))
            softmax_pv(*chain, pending.pop(0))

    def full_block(j, carry):
        block(j, False)
        return carry

    lax.fori_loop(0, iq, full_block, 0)
    block(iq, True)
    out = jnp.concatenate([acc_s[h, 0:dv] / acc_s[h, dv:dv + 1] for h in range(nh)], axis=0)
    o_ref[...] = out.T.astype(o_ref.dtype)


def _flash_t(q, k, k_col0, vt, *, nh, dq, dv, n_hblk, chunk_causal):
    b, t_all = q.shape[0], q.shape[1]
    t = FLASH_T
    nblk = t_all // t
    kern = functools.partial(_flash_t_kernel, nh=nh, dq=dq, dv=dv, t=t, chunk_causal=chunk_causal)
    return pl.pallas_call(
        kern,
        grid=(b, n_hblk, nblk),
        in_specs=[pl.BlockSpec((None, t, nh * dq), lambda bi, h, i: (bi, i, h)),
                  pl.BlockSpec((None, t_all, nh * dq), lambda bi, h, i: (bi, 0, k_col0 + h),
                               pipeline_mode=pl.Buffered(1)),
                  pl.BlockSpec((None, nblk, nh * (dv + ONES_ROWS), t), lambda bi, h, i: (bi, 0, h, 0),
                               pipeline_mode=pl.Buffered(1))],
        out_specs=pl.BlockSpec((None, t, nh * dv), lambda bi, h, i: (bi, i, h)),
        out_shape=jax.ShapeDtypeStruct((b, t_all, n_hblk * nh * dv), BF16),
        scratch_shapes=[pltpu.VMEM((nh, 1, t), F32), pltpu.VMEM((nh, dv + ONES_ROWS, t), F32)],
        compiler_params=_params("parallel", "parallel", "parallel"),
        name="flash_t_chunk" if chunk_causal else "flash_t_frame",
    )(q, k, vt)


def _key_block_transpose(v, t, dv=None):
    b, t_all, c = v.shape
    if dv is not None:
        v = v.reshape(b, t_all, c // dv, dv)
        v = jnp.concatenate([v, jnp.ones((b, t_all, c // dv, ONES_ROWS), v.dtype)], axis=-1)
        c = v.shape[2] * v.shape[3]
    return v.reshape(b, t_all // t, t, c).transpose(0, 1, 3, 2)


B_HEADS_PER_STEP = LANES // B_HEAD_DIM


def _band_kernel(*refs, nkb, tkb, clamp_front):
    q_ref = refs[0]
    k_refs = refs[1:1 + nkb]
    v_refs = refs[1 + nkb:1 + 2 * nkb]
    bias_ref, o_ref = refs[1 + 2 * nkb], refs[2 + 2 * nkb]
    iq = pl.program_id(2)
    dh = B_HEAD_DIM
    for h in range(B_HEADS_PER_STEP):
        q = q_ref[:, h * dh:(h + 1) * dh]
        ss = []
        for j in range(nkb):
            s = lax.dot_general(q, k_refs[j][:, h * dh:(h + 1) * dh], (((1,), (1,)), ((), ())),
                                preferred_element_type=F32)
            s = s + bias_ref[h, :, j * tkb:(j + 1) * tkb]
            if clamp_front and j < nkb - 1:
                s = jnp.where(iq >= nkb - 1 - j, s, NEG_INF)
            ss.append(s)
        m = functools.reduce(jnp.maximum, [jnp.max(s, axis=-1, keepdims=True) for s in ss])
        ps = [jnp.exp(s - m) for s in ss]
        l = functools.reduce(lambda a, c: a + c, [jnp.sum(p, axis=-1, keepdims=True) for p in ps])
        o = functools.reduce(lambda a, c: a + c, [
            jnp.dot(ps[j].astype(BF16), v_refs[j][:, h * dh:(h + 1) * dh], preferred_element_type=F32)
            for j in range(nkb)])
        o_ref[:, h * dh:(h + 1) * dh] = (o / l).astype(o_ref.dtype)


def _band_bias(rel_bias, q_pos, k_pos):
    nq, nk = len(q_pos), len(k_pos)
    assert (np.diff(q_pos) == 1).all() and (np.diff(k_pos) == 1).all()
    m = np.arange(nq + nk - 1)
    u = rel_bias[:, np.clip(q_pos[0] - k_pos[0] + nq - 1 - m, -B_REL_CLIP, B_REL_CLIP) + B_REL_CLIP]
    period = nq + nk
    w = jnp.concatenate([u[:, nq - 1:], jnp.zeros((u.shape[0], 1), u.dtype), u[:, :nq - 1]], axis=1)
    skew = jnp.tile(w, (1, nq))[:, :nq * (period - 1)].reshape(-1, nq, period - 1)[:, :, :nk]
    qc = q_pos[:, None] // CHUNK
    kc = k_pos[None, :] // CHUNK
    mask = (kc <= qc) & (kc >= qc - B_LEFT_CHUNKS) & (k_pos[None, :] >= 0)
    return jnp.where(jnp.asarray(mask)[None], skew, NEG_INF).astype(F32)


def _band(q, k, v, k_col0, v_col0, bias, *, tq, tkb, nkb, clamp_front):
    b, t_q = q.shape[0], q.shape[1]
    nq = t_q // tq
    n_hblk = B_HEADS // B_HEADS_PER_STEP

    def kv_spec(j, col0):
        back = nkb - 1 - j
        return pl.BlockSpec((None, tkb, LANES),
                            lambda bi, h, i: (bi, jnp.maximum(i - back, 0), col0 + h))

    in_specs = ([pl.BlockSpec((None, tq, LANES), lambda bi, h, i: (bi, i, h))]
                + [kv_spec(j, k_col0) for j in range(nkb)]
                + [kv_spec(j, v_col0) for j in range(nkb)]
                + [pl.BlockSpec((B_HEADS_PER_STEP, tq, nkb * tkb), lambda bi, h, i: (h, 0, 0))])
    kern = functools.partial(_band_kernel, nkb=nkb, tkb=tkb, clamp_front=clamp_front)
    return pl.pallas_call(
        kern,
        grid=(b, n_hblk, nq),
        in_specs=in_specs,
        out_specs=pl.BlockSpec((None, tq, LANES), lambda bi, h, i: (bi, i, h)),
        out_shape=jax.ShapeDtypeStruct((b, t_q, D_MODEL), BF16),
        compiler_params=_params("parallel", "parallel", "parallel"),
        name="band_attn",
    )(q, *([k] * nkb), *([v] * nkb), bias)


BAND_T = 4 * CHUNK


def _band_t_kernel(*refs, nkb, t):
    q_ref = refs[0]
    k_refs = refs[1:1 + nkb]
    vt_refs = refs[1 + nkb:1 + 2 * nkb]
    bias_ref, o_ref = refs[1 + 2 * nkb], refs[2 + 2 * nkb]
    iq = pl.program_id(1)
    dh = B_HEAD_DIM

    def qk(h):
        cols = slice(h * dh, (h + 1) * dh)
        parts = []
        for j in range(nkb):
            s = lax.dot_general(k_refs[j][:, cols], q_ref[:, cols], (((1,), (1,)), ((), ())),
                                preferred_element_type=F32)
            if j < nkb - 1:
                s = jnp.where(iq >= nkb - 1 - j, s, NEG_INF)
            parts.append(s)
        return jnp.concatenate(parts, axis=0) + bias_ref[h]

    def softmax_pv(h, s):
        p = jnp.exp2(s - jnp.max(s, axis=0, keepdims=True))
        l = jnp.sum(p, axis=0, keepdims=True)
        pb = p.astype(BF16)
        o = functools.reduce(lambda a, c: a + c, [
            jnp.dot(vt_refs[j][h * dh:(h + 1) * dh, :], pb[j * t:(j + 1) * t], preferred_element_type=F32)
            for j in range(nkb)])
        return o / l

    heads = list(range(B_HEADS))
    pending = [qk(h) for h in heads[:FLASH_AHEAD]]
    outs = []
    for h in heads:
        if h + FLASH_AHEAD < B_HEADS:
            pending.append(qk(h + FLASH_AHEAD))
        outs.append(softmax_pv(h, pending.pop(0)))
        if len(outs) == B_HEADS_PER_STEP:
            c0 = (h + 1 - B_HEADS_PER_STEP) * dh
            o_ref[:, c0:c0 + LANES] = jnp.concatenate(outs, axis=0).T.astype(o_ref.dtype)
            outs = []


def _band_t(q, kv, vt, bias_t, *, nkb):
    b, t_all = q.shape[0], q.shape[1]
    t = BAND_T

    def back(j):
        return nkb - 1 - j

    k_specs = [pl.BlockSpec((None, t, D_MODEL), lambda bi, i, j=j: (bi, jnp.maximum(i - back(j), 0), 0))
               for j in range(nkb)]
    vt_specs = [pl.BlockSpec((None, None, D_MODEL, t),
                             lambda bi, i, j=j: (bi, jnp.maximum(i - back(j), 0), 0, 0))
                for j in range(nkb)]
    return pl.pallas_call(
        functools.partial(_band_t_kernel, nkb=nkb, t=t),
        grid=(b, t_all // t),
        in_specs=([pl.BlockSpec((None, t, D_MODEL), lambda bi, i: (bi, i, 0))] + k_specs + vt_specs
                  + [pl.BlockSpec(bias_t.shape, lambda bi, i: (0, 0, 0), pipeline_mode=pl.Buffered(1))]),
        out_specs=pl.BlockSpec((None, t, D_MODEL), lambda bi, i: (bi, i, 0)),
        out_shape=jax.ShapeDtypeStruct((b, t_all, D_MODEL), BF16),
        compiler_params=_params("parallel", "parallel"),
        name="band_t",
    )(q, *([kv] * nkb), *([vt] * nkb), bias_t)


def _logf_kernel(x_ref, w_ref, b_ref, o_ref):
    z = jnp.dot(x_ref[...], w_ref[...], preferred_element_type=F32) + b_ref[...]
    o_ref[...] = -(jnp.maximum(-z, 0.0) + jnp.log1p(jnp.exp(-jnp.abs(z))))


def _logf(xb, w_pad, b_pad, tm=512):
    m = xb.shape[0]
    tm = _row_tile(m, tm)
    return pl.pallas_call(
        _logf_kernel,
        grid=(m // tm,),
        in_specs=[pl.BlockSpec((tm, D_MODEL), lambda i: (i, 0)),
                  pl.BlockSpec((D_MODEL, LANES), lambda i: (0, 0)),
                  pl.BlockSpec((1, LANES), lambda i: (0, 0))],
        out_specs=pl.BlockSpec((tm, LANES), lambda i: (i, 0)),
        out_shape=jax.ShapeDtypeStruct((m, LANES), F32),
        compiler_params=_params("parallel"),
        name="logf",
    )(xb, w_pad, b_pad)


def _split3(x):
    hi = x.astype(BF16)
    r = x - hi.astype(F32)
    mid = r.astype(BF16)
    lo = (r - mid.astype(F32)).astype(BF16)
    return hi, mid, lo


def _cumsum_kernel(x_ref, o_ref, hi_ref, mid_ref, lo_ref, carry, *, tc):
    @pl.when(pl.program_id(1) == 0)
    def _():
        carry[...] = jnp.zeros(carry.shape, F32)

    tri = (lax.broadcasted_iota(jnp.int32, (tc, tc), 0)
           >= lax.broadcasted_iota(jnp.int32, (tc, tc), 1)).astype(BF16)
    c = functools.reduce(lambda a, b: a + b, [jnp.dot(tri, piece, preferred_element_type=F32)
                                              for piece in _split3(x_ref[...])])
    out = c + carry[0:1, :]
    o_ref[...] = out
    hi_ref[...], mid_ref[...], lo_ref[...] = _split3(out * LOG2E)
    carry[...] = jnp.broadcast_to(out[tc - 1:tc, :], carry.shape)


def _cumsum(x, tc=256):
    b, t, _ = x.shape
    spec = pl.BlockSpec((None, tc, LANES), lambda bi, i: (bi, i, 0))
    piece = jax.ShapeDtypeStruct(x.shape, BF16)
    return pl.pallas_call(
        functools.partial(_cumsum_kernel, tc=tc),
        grid=(b, t // tc),
        in_specs=[spec],
        out_specs=[spec] * 4,
        out_shape=[jax.ShapeDtypeStruct(x.shape, F32), piece, piece, piece],
        scratch_shapes=[pltpu.VMEM((SUBLANES, LANES), F32)],
        compiler_params=_params("parallel", "arbitrary"),
        name="cumsum",
    )(x)


FFN_CHUNK = 256
CONV_ROWS = SUBLANES


def _ffn_up_kernel(x_ref, w_ref, c_ref, p_ref, act_ref, s_ref, halo, *, tm):
    @pl.when(pl.program_id(1) == 0)
    def _():
        halo[...] = p_ref[...]

    x = x_ref[...]
    n_chunks = D_FF // FFN_CHUNK
    groups = tm // CONV_ROWS
    row = lax.broadcasted_iota(jnp.int32, (groups, CONV_ROWS, FFN_CHUNK), 1)

    def up(c, half):
        col = half * D_FF + c * FFN_CHUNK
        return jnp.dot(x, w_ref[:, col:col + FFN_CHUNK], preferred_element_type=F32)

    def conv(h, c, half):
        cols = slice(half * D_FF + c * FFN_CHUNK, half * D_FF + (c + 1) * FFN_CHUNK)
        ext = jnp.concatenate([halo[:, cols], h], axis=0).reshape(groups + 1, CONV_ROWS, FFN_CHUNK)
        hc = c_ref[CONV_W:CONV_W + 1, cols] + c_ref[CONV_W - 1:CONV_W, cols] * h
        for s in range(1, CONV_W):
            rot = pltpu.roll(ext, s, axis=1)
            shifted = jnp.where(row < s, rot[:groups], rot[1:]).reshape(tm, FFN_CHUNK)
            hc = hc + c_ref[CONV_W - 1 - s:CONV_W - s, cols] * shifted
        tail = h[tm - CONV_ROWS:tm]
        halo[:, cols] = tail
        s_ref[:, cols] = tail
        return hc

    pending = [(up(0, 0), up(0, 1))]
    for c in range(n_chunks):
        if c + 1 < n_chunks:
            pending.append((up(c + 1, 0), up(c + 1, 1)))
        ha, hg = pending.pop(0)
        a = conv(ha, c, 0)
        g = conv(hg, c, 1)
        act_ref[:, c * FFN_CHUNK:(c + 1) * FFN_CHUNK] = (g * jax.nn.sigmoid(g) * a).astype(BF16)


def _ffn_up(xb, w_up, conv_tab, past, tm=512):
    b, t, _ = xb.shape
    tm = _row_tile(t, tm)
    fixed = lambda bi, ti: (0, 0)
    return pl.pallas_call(
        functools.partial(_ffn_up_kernel, tm=tm),
        grid=(b, t // tm),
        in_specs=[pl.BlockSpec((None, tm, D_MODEL), lambda bi, ti: (bi, ti, 0)),
                  pl.BlockSpec((D_MODEL, 2 * D_FF), fixed),
                  pl.BlockSpec((CONV_ROWS, 2 * D_FF), fixed),
                  pl.BlockSpec((None, CONV_ROWS, 2 * D_FF), lambda bi, ti: (bi, 0, 0))],
        out_specs=[pl.BlockSpec((None, tm, D_FF), lambda bi, ti: (bi, ti, 0)),
                   pl.BlockSpec((None, CONV_ROWS, 2 * D_FF), lambda bi, ti: (bi, 0, 0))],
        out_shape=[jax.ShapeDtypeStruct((b, t, D_FF), BF16),
                   jax.ShapeDtypeStruct((b, CONV_ROWS, 2 * D_FF), F32)],
        scratch_shapes=[pltpu.VMEM((CONV_ROWS, 2 * D_FF), F32)],
        compiler_params=_params("parallel", "arbitrary"),
        name="ffn_up",
    )(xb, w_up, conv_tab, past)


def _rope_tables(pos, batch, q_scale):
    half = A_ROPE // 2
    inv_freq = ROPE_THETA ** (-jnp.arange(half, dtype=F32) / half)
    ang = pos.astype(F32)[:, None] * inv_freq
    cos, sin = jnp.cos(ang), jnp.sin(ang)
    cosk = jnp.concatenate([cos, cos], axis=-1)
    sink = jnp.concatenate([sin, sin], axis=-1)
    t = pos.shape[0]
    pad = jnp.zeros((t, A_QK_PAD - A_NOPE - A_ROPE), F32)
    cosq = q_scale * jnp.concatenate([jnp.ones((t, A_NOPE), F32), cosk, pad], axis=-1)
    sinq = q_scale * jnp.concatenate([jnp.zeros((t, A_NOPE), F32), sink, pad], axis=-1)
    return tuple(jnp.tile(a, (batch, 1)) for a in (cosq, sinq, cosk, sink))


def _swap_halves(w):
    half = w.shape[-1] // 2
    return jnp.concatenate([-w[..., half:], w[..., :half]], axis=-1)


def _mla_weights(w_dq, w_dkv, w_kr, w_uq, w_uk, w_uv):
    zc = jnp.zeros((D_MODEL, LANES - A_ROPE), F32)
    w1 = jnp.concatenate([w_dq, w_dkv, w_kr, zc, _swap_halves(w_kr), zc], axis=1).astype(BF16)
    wq = w_uq.reshape(A_Q_LORA, A_HEADS, A_NOPE + A_ROPE)
    nope, rope = wq[..., :A_NOPE], wq[..., A_NOPE:]
    zpad = jnp.zeros((A_Q_LORA, A_HEADS, A_QK_PAD - A_NOPE - A_ROPE), F32)
    w_cat = jnp.concatenate([nope, rope, zpad], axis=-1).reshape(A_Q_LORA, -1)
    w_sw = jnp.concatenate([jnp.zeros_like(nope), _swap_halves(rope), zpad], axis=-1).reshape(A_Q_LORA, -1)
    wq2 = jnp.concatenate([w_cat, w_sw], axis=1).astype(BF16)
    wk = jnp.zeros((A_LAT_PAD, A_HEADS, A_QK_PAD), F32)
    wk = wk.at[:A_KV_LORA, :, :A_NOPE].set(w_uk)
    eye = jnp.broadcast_to(jnp.eye(A_ROPE, dtype=F32)[:, None, :], (A_ROPE, A_HEADS, A_ROPE))
    wk = wk.at[A_KV_LORA:A_KV_LORA + A_ROPE, :, A_NOPE:A_NOPE + A_ROPE].set(eye)
    wv = jnp.zeros((A_LAT_PAD, A_HEADS * A_V), F32).at[:A_KV_LORA].set(w_uv.reshape(A_KV_LORA, -1))
    wkv = jnp.concatenate([wk.reshape(A_LAT_PAD, -1), wv], axis=1).astype(BF16)
    return w1, wq2, wkv


def _mla_mixer(xb, b, t, pos, ckv_past, kpe_past, w, i):
    w1, wq2, wkv = _mla_weights(w['a_w_dq'][i], w['a_w_dkv'][i], w['a_w_kr'][i], w['a_w_uq'][i],
                                w['a_w_uk'][i], w['a_w_uv'][i])
    resident = ckv_past is None and t % FLASH_T == 0
    q_scale = (A_NOPE + A_ROPE) ** -0.5 * (LOG2E if resident else 1.0)
    q, lat, ckv, kpe = _mla_proj(xb, w1, w['a_g_q'][i], w['a_g_kv'][i], wq2,
                                 *_rope_tables(pos, b, q_scale))
    lat = lat.reshape(b, t, A_LAT_PAD)
    q_off = 0
    if ckv_past is not None:
        p_len = ckv_past.shape[1]
        past = jnp.concatenate(
            [ckv_past, kpe_past, jnp.zeros((b, p_len, A_LAT_PAD - A_KV_LORA - A_ROPE), F32)], axis=-1)
        lat = jnp.concatenate([past.astype(BF16), lat], axis=1)
        q_off = p_len
    t_k = lat.shape[1]
    (kv,) = _mm(lat.reshape(b * t_k, A_LAT_PAD), wkv, [BF16])
    kv = kv.reshape(b, t_k, -1)
    if resident:
        vt = _key_block_transpose(kv[:, :, A_HEADS * A_QK_PAD:], FLASH_T, A_V)
        o = _flash_t(q.reshape(b, t, -1), kv, 0, vt, nh=A_HEADS_PER_STEP, dq=A_QK_PAD, dv=A_V,
                     n_hblk=A_HEADS // A_HEADS_PER_STEP, chunk_causal=True)
    else:
        o = _flash(q.reshape(b, t, -1), kv, kv, nh=1, dq=A_QK_PAD, dv=A_V, n_hblk=A_HEADS,
                   k_col0=0, v_col0=A_HEADS * A_QK_PAD // A_V, tq=512, tk=512, q_off=q_off,
                   chunk_causal=True)
    return o.reshape(b * t, A_HEADS * A_V), ckv.reshape(b, t, -1), kpe.reshape(b, t, -1)


def _qkv(xb, w_qkv, q_scale):
    wq = (w_qkv[:, :D_MODEL] * q_scale).astype(BF16)
    (q,) = _mm(xb, wq, [BF16])
    kv32, kvb = _mm(xb, w_qkv[:, D_MODEL:].astype(BF16), [F32, BF16])
    return q, kv32, kvb


def _band_mixer(xb, b, t, pos0, k_past, v_past, w, i):
    prompt = k_past is None
    assert not prompt or t % BAND_T == 0
    q, kv32, kvb = _qkv(xb, w['b_w_qkv'][i], B_HEAD_DIM ** -0.5 * (LOG2E if prompt else 1.0))
    q = q.reshape(b, t, D_MODEL)
    k32 = kv32[:, :D_MODEL].reshape(b, t, B_HEADS, B_HEAD_DIM)
    v32 = kv32[:, D_MODEL:].reshape(b, t, B_HEADS, B_HEAD_DIM)
    n_cols = D_MODEL // LANES
    if prompt:
        nkb = B_WIN // BAND_T + 1
        bias = _band_bias(w['b_rel_bias'][i], B_WIN + np.arange(BAND_T), np.arange(B_WIN + BAND_T))
        kvb = kvb.reshape(b, t, 2 * D_MODEL)
        o = _band_t(q, kvb, _key_block_transpose(kvb[:, :, D_MODEL:], BAND_T),
                    LOG2E * bias.transpose(0, 2, 1), nkb=nkb)
        keep = min(B_WIN, t)
        k_new, v_new = k32[:, t - keep:], v32[:, t - keep:]
    else:
        p_len = k_past.shape[1]
        kvb = kvb.reshape(b, t, 2 * D_MODEL)
        past = jnp.concatenate([k_past.reshape(b, p_len, D_MODEL), v_past.reshape(b, p_len, D_MODEL)],
                               axis=-1).astype(BF16)
        kv_all = jnp.concatenate([past, kvb], axis=1)
        q_pos = pos0 + np.arange(t)
        k_pos = np.concatenate([np.arange(pos0 - p_len, pos0), q_pos])
        bias = _band_bias(w['b_rel_bias'][i], q_pos, k_pos)
        o = _band(q, kv_all, kv_all, 0, n_cols, bias, tq=t, tkb=p_len + t, nkb=1, clamp_front=False)
        k_new, v_new = k32, v32
    return o.reshape(b * t, D_MODEL), k_new, v_new


C_AUG = LANES
N_PIECES = 3


def _aug_kernel(x_ref, w_ref, hi_ref, mid_ref, lo_ref, p_ref, b_ref, o_ref):
    acc = jnp.dot(x_ref[...], w_ref[...], preferred_element_type=F32) + b_ref[...]
    for piece, f_ref in enumerate((hi_ref, mid_ref, lo_ref)):
        acc = acc + jnp.dot(f_ref[...], p_ref[piece], preferred_element_type=F32)
    o_ref[...] = acc.astype(BF16)


def _aug_tables():
    n = C_HEADS * C_AUG
    place = np.zeros((N_PIECES, LANES, 2 * n), np.float32)
    ones = np.zeros((1, 2 * n), np.float32)
    for h in range(C_HEADS):
        base = h * C_AUG + C_HEAD_DIM
        for piece in range(N_PIECES):
            ones[0, base + piece] = 1.0
            place[piece, h, base + N_PIECES + piece] = 1.0
            place[piece, h, n + base + piece] = -1.0
            ones[0, n + base + N_PIECES + piece] = 1.0
    return jnp.asarray(place, BF16), jnp.asarray(ones)


def _aug_qk(xb, w_aug, pieces, tm=512, tn=1024):
    m = xb.shape[0]
    n = w_aug.shape[1]
    tm = _row_tile(m, tm)
    place, ones = _aug_tables()
    row = lambda i, j: (i, 0)
    return pl.pallas_call(
        _aug_kernel,
        grid=(m // tm, n // tn),
        in_specs=[pl.BlockSpec((tm, D_MODEL), row), pl.BlockSpec((D_MODEL, tn), lambda i, j: (0, j)),
                  pl.BlockSpec((tm, LANES), row), pl.BlockSpec((tm, LANES), row), pl.BlockSpec((tm, LANES), row),
                  pl.BlockSpec((N_PIECES, LANES, tn), lambda i, j: (0, 0, j)),
                  pl.BlockSpec((1, tn), lambda i, j: (0, j))],
        out_specs=pl.BlockSpec((tm, tn), lambda i, j: (i, j)),
        out_shape=jax.ShapeDtypeStruct((m, n), BF16),
        compiler_params=_params("parallel", "parallel"),
        name="aug_qk",
    )(xb, w_aug, *pieces, place, ones)


def _pad_heads(w, scale):
    w = (w * scale).reshape(D_MODEL, C_HEADS, C_HEAD_DIM)
    return jnp.pad(w, ((0, 0), (0, 0), (0, C_AUG - C_HEAD_DIM))).reshape(D_MODEL, C_HEADS * C_AUG)


def _fox_mixer(xb, b, t, k_past, v_past, lf_past, w, i):
    resident = k_past is None and t % FLASH_T == 0
    w_qkv = w['c_w_qkv'][i]
    kv32, kvb = _mm(xb, w_qkv[:, D_MODEL:].astype(BF16), [F32, BF16])
    k32 = kv32[:, :D_MODEL].reshape(b, t, C_HEADS, C_HEAD_DIM)
    v32 = kv32[:, D_MODEL:].reshape(b, t, C_HEADS, C_HEAD_DIM)
    w_f = jnp.zeros((D_MODEL, LANES), F32).at[:, :C_HEADS].set(w['c_w_f'][i]).astype(BF16)
    b_f = jnp.zeros((1, LANES), F32).at[0, :C_HEADS].set(w['c_b_f'][i])
    log_f = _logf(xb, w_f, b_f).reshape(b, t, LANES)
    kvb = kvb.reshape(b, t, 2 * D_MODEL)
    lf_all = log_f
    q_off = 0
    if k_past is not None:
        p_len = k_past.shape[1]
        past = jnp.concatenate([k_past.reshape(b, p_len, D_MODEL), v_past.reshape(b, p_len, D_MODEL)],
                               axis=-1).astype(BF16)
        kvb = jnp.concatenate([past, kvb], axis=1)
        lf_all = jnp.concatenate([jnp.pad(lf_past, ((0, 0), (0, 0), (0, LANES - C_HEADS))), log_f], axis=1)
        q_off = p_len
    t_k = lf_all.shape[1]
    tc = 256
    t_pad = -(-t_k // tc) * tc
    f_cum, *f_pieces = _cumsum(jnp.pad(lf_all, ((0, 0), (0, t_pad - t_k), (0, 0))), tc)
    nh = LANES // C_HEAD_DIM
    n_hblk = C_HEADS // nh
    q_scale = C_HEAD_DIM ** -0.5
    if resident:
        w_aug = jnp.concatenate([_pad_heads(w_qkv[:, :D_MODEL], q_scale * LOG2E),
                                 _pad_heads(w_qkv[:, D_MODEL:2 * D_MODEL], 1.0)], axis=1).astype(BF16)
        qk_aug = _aug_qk(xb, w_aug, [p.reshape(b * t, LANES) for p in f_pieces]).reshape(b, t, -1)
        n_q = C_HEADS * C_AUG
        vt = _key_block_transpose(kvb[:, :, D_MODEL:], FLASH_T, C_HEAD_DIM)
        o = _flash_t(qk_aug, qk_aug, n_q // (nh * C_AUG), vt,
                     nh=nh, dq=C_AUG, dv=C_HEAD_DIM, n_hblk=n_hblk, chunk_causal=False)
        return o.reshape(b * t, D_MODEL), k32, v32, log_f[:, :, :C_HEADS]
    (q,) = _mm(xb, (w_qkv[:, :D_MODEL] * q_scale).astype(BF16), [BF16])
    f_blk = f_cum[:, :t_k, :C_HEADS].reshape(b, t_k, n_hblk, nh).transpose(0, 2, 1, 3)
    fq = jnp.pad(f_blk[:, :, t_k - t:], ((0, 0), (0, 0), (0, 0), (0, LANES - nh)))
    fk = jnp.pad(f_blk.transpose(0, 1, 3, 2), ((0, 0), (0, 0), (0, SUBLANES - nh), (0, 0)))
    o = _flash(q.reshape(b, t, D_MODEL), kvb, kvb, nh=nh, dq=C_HEAD_DIM, dv=C_HEAD_DIM, n_hblk=n_hblk,
               k_col0=0, v_col0=n_hblk, tq=512, tk=512, q_off=q_off, chunk_causal=False, fq=fq, fk=fk)
    return o.reshape(b * t, D_MODEL), k32, v32, log_f[:, :, :C_HEADS]


def _conv_ffn(xb, b, t, conv_past, w, i):
    tab = jnp.concatenate([w['f_conv_w'][i], w['f_conv_b'][i][None],
                           jnp.zeros((CONV_ROWS - CONV_W - 1, 2 * D_FF), F32)], axis=0)
    if conv_past is None:
        past = jnp.zeros((b, CONV_ROWS, 2 * D_FF), F32)
    else:
        past = jnp.pad(conv_past, ((0, 0), (CONV_ROWS - (CONV_W - 1), 0), (0, 0)))
    act, tail = _ffn_up(xb.reshape(b, t, D_MODEL), w['f_w_up'][i].astype(BF16), tab, past)
    return act.reshape(b * t, D_FF), tail[:, CONV_ROWS - (CONV_W - 1):]


def _trunk(x, pos0, past, w):
    b, t, _ = x.shape
    pos = pos0 + jnp.arange(t)
    xf = x.reshape(b * t, D_MODEL)
    xb = xf.astype(BF16)
    outs = {n: [] for n in ('a_ckv', 'a_kpe', 'b_k', 'b_v', 'c_k', 'c_v', 'c_logf', 'ffn_conv')}
    ia = ib = ic = 0
    get = lambda name, j: None if past is None else past[name][j]
    for i in range(DEPTH):
        kind = i % N_MIXERS
        if kind == 0:
            o, ckv, kpe = _mla_mixer(xb, b, t, pos, get('a_ckv', ia), get('a_kpe', ia), w, ia)
            outs['a_ckv'].append(ckv)
            outs['a_kpe'].append(kpe)
            w_o = w['a_w_o'][ia]
            ia += 1
        elif kind == 1:
            o, kb, vb = _band_mixer(xb, b, t, pos0, get('b_k', ib), get('b_v', ib), w, ib)
            outs['b_k'].append(kb)
            outs['b_v'].append(vb)
            w_o = w['b_w_o'][ib]
            ib += 1
        else:
            o, kc, vc, lf = _fox_mixer(xb, b, t, get('c_k', ic), get('c_v', ic), get('c_logf', ic), w, ic)
            outs['c_k'].append(kc)
            outs['c_v'].append(vc)
            outs['c_logf'].append(lf)
            w_o = w['c_w_o'][ic]
            ic += 1
        xf, xb = _mm_res_ln(o, w_o.astype(BF16), xf, w['ln1_g'][i], w['ln1_b'][i])
        act, conv_state = _conv_ffn(xb, b, t, get('ffn_conv', i), w, i)
        outs['ffn_conv'].append(conv_state)
        xf, xb = _mm_res_ln(act, w['f_w_down'][i].astype(BF16), xf, w['ln2_g'][i], w['ln2_b'][i])
    return xf.reshape(b, t, D_MODEL), {n: jnp.stack(v) for n, v in outs.items()}


def kernel(x_prompt, x_sample, cache_a_ckv, cache_a_kpe, cache_b_k, cache_b_v, cache_c_k, cache_c_v,
           cache_c_logf, state_ffn_conv, a_w_dq, a_g_q, a_w_uq, a_w_dkv, a_g_kv, a_w_kr, a_w_uk, a_w_uv,
           a_w_o, b_w_qkv, b_rel_bias, b_w_o, c_w_qkv, c_w_f, c_b_f, c_w_o, f_w_up, f_conv_w, f_conv_b,
           f_w_down, ln1_g, ln1_b, ln2_g, ln2_b):
    w = dict(a_w_dq=a_w_dq, a_g_q=a_g_q, a_w_uq=a_w_uq, a_w_dkv=a_w_dkv, a_g_kv=a_g_kv, a_w_kr=a_w_kr,
             a_w_uk=a_w_uk, a_w_uv=a_w_uv, a_w_o=a_w_o, b_w_qkv=b_w_qkv, b_rel_bias=b_rel_bias, b_w_o=b_w_o,
             c_w_qkv=c_w_qkv, c_w_f=c_w_f, c_b_f=c_b_f, c_w_o=c_w_o, f_w_up=f_w_up, f_conv_w=f_conv_w,
             f_conv_b=f_conv_b, f_w_down=f_w_down, ln1_g=ln1_g, ln1_b=ln1_b, ln2_g=ln2_g, ln2_b=ln2_b)
    past = dict(a_ckv=cache_a_ckv, a_kpe=cache_a_kpe, b_k=cache_b_k, b_v=cache_b_v, c_k=cache_c_k,
                c_v=cache_c_v, c_logf=cache_c_logf, ffn_conv=state_ffn_conv)
    past_len = cache_a_ckv.shape[2]
    y_prompt, p = _trunk(x_prompt, 0, None, w)
    y_sample, s = _trunk(x_sample, past_len, past, w)
    names = ('a_ckv', 'a_kpe', 'b_k', 'b_v', 'c_k', 'c_v', 'c_logf', 'ffn_conv')
    return (y_prompt, y_sample) + tuple(p[n] for n in names) + tuple(s[n] for n in names)
```

```python
import functools
import math

import numpy as np
import jax
import jax.numpy as jnp
from jax import lax
from jax.experimental import pallas as pl
from jax.experimental.pallas import tpu as pltpu

F32 = jnp.float32
BF16 = jnp.bfloat16

D_MODEL = 1024
DEPTH = 4
CHUNK = 64
N_MIXERS = 3

A_HEADS = 8
A_Q_LORA = 384
A_KV_LORA = 256
A_NOPE = 128
A_ROPE = 64
A_V = 128
A_QK_PAD = 256
A_LAT_PAD = 384
ROPE_THETA = 10000.0

B_HEADS = 16
B_HEAD_DIM = D_MODEL // B_HEADS
B_LEFT_CHUNKS = 8
B_WIN = B_LEFT_CHUNKS * CHUNK
B_REL_CLIP = 128

C_HEADS = 16
C_HEAD_DIM = D_MODEL // C_HEADS

D_FF = 2816
CONV_W = 3

ALPHA = (2.0 * DEPTH) ** 0.25
LN_EPS = 1e-5
RMS_EPS = 1e-6
NEG_INF = -1e30

LANES = 128
SUBLANES = 8
VMEM_LIMIT_BYTES = 48 * 2 ** 20


def _params(*sem):
    return pltpu.CompilerParams(dimension_semantics=sem, vmem_limit_bytes=VMEM_LIMIT_BYTES)


def _row_tile(m, tm):
    while m % tm:
        tm //= 2
    assert tm % SUBLANES == 0, (m, tm)
    return tm


def _mm_kernel(x_ref, w_ref, *o_refs):
    acc = jnp.dot(x_ref[...], w_ref[...], preferred_element_type=F32)
    for o_ref in o_refs:
        o_ref[...] = acc.astype(o_ref.dtype)


def _mm(x, w, out_dtypes, tm=512, tn=1024):
    m, k = x.shape
    n = w.shape[1]
    tm, tn = _row_tile(m, tm), min(tn, n)
    return pl.pallas_call(
        _mm_kernel,
        grid=(m // tm, n // tn),
        in_specs=[pl.BlockSpec((tm, k), lambda i, j: (i, 0)),
                  pl.BlockSpec((k, tn), lambda i, j: (0, j))],
        out_specs=[pl.BlockSpec((tm, tn), lambda i, j: (i, j)) for _ in out_dtypes],
        out_shape=[jax.ShapeDtypeStruct((m, n), d) for d in out_dtypes],
        compiler_params=_params("parallel", "parallel"),
        name="mm",
    )(x, w)


def _proj_t_kernel(wt_ref, x_ref, o_ref):
    o_ref[...] = lax.dot_general(wt_ref[...], x_ref[...], (((1,), (1,)), ((), ())),
                                 preferred_element_type=F32).astype(o_ref.dtype)


def _proj_t(x, w, t_blk, tm=512):
    m, k = x.shape
    n = w.shape[1]
    tm = min(tm, t_blk)
    per = t_blk // tm
    return pl.pallas_call(
        _proj_t_kernel,
        grid=(m // tm,),
        in_specs=[pl.BlockSpec((n, k), lambda i: (0, 0)), pl.BlockSpec((tm, k), lambda i: (i, 0))],
        out_specs=pl.BlockSpec((None, n, tm), lambda i: (i // per, 0, i % per)),
        out_shape=jax.ShapeDtypeStruct((m // t_blk, n, t_blk), BF16),
        compiler_params=_params("parallel"),
        name="proj_t",
    )(w.T.astype(BF16), x)


def _cast_kernel(x_ref, o_ref):
    o_ref[...] = x_ref[...].astype(o_ref.dtype)


def _to_bf16(w, tr=256):
    n_l, r, c = w.shape
    tr = _row_tile(r, tr)
    spec = pl.BlockSpec((None, tr, c), lambda l, i: (l, i, 0))
    return pl.pallas_call(
        _cast_kernel,
        grid=(n_l, r // tr),
        in_specs=[spec],
        out_specs=spec,
        out_shape=jax.ShapeDtypeStruct(w.shape, BF16),
        compiler_params=_params("parallel", "parallel"),
        name="to_bf16",
    )(w)


def _mm_res_ln_kernel(a_ref, w_ref, x_ref, g_ref, b_ref, of_ref, ob_ref):
    y = ALPHA * x_ref[...] + jnp.dot(a_ref[...], w_ref[...], preferred_element_type=F32)
    mu = jnp.mean(y, axis=-1, keepdims=True)
    d = y - mu
    var = jnp.mean(d * d, axis=-1, keepdims=True)
    out = d * lax.rsqrt(var + LN_EPS) * g_ref[...] + b_ref[...]
    of_ref[...] = out
    ob_ref[...] = out.astype(BF16)


def _mm_res_ln(a, w, x, g, b, tm=512):
    m, k = a.shape
    n = w.shape[1]
    tm = _row_tile(m, tm)
    row = lambda i: (i, 0)
    fixed = lambda i: (0, 0)
    return pl.pallas_call(
        _mm_res_ln_kernel,
        grid=(m // tm,),
        in_specs=[pl.BlockSpec((tm, k), row), pl.BlockSpec((k, n), fixed),
                  pl.BlockSpec((tm, n), row), pl.BlockSpec((1, n), fixed),
                  pl.BlockSpec((1, n), fixed)],
        out_specs=[pl.BlockSpec((tm, n), row), pl.BlockSpec((tm, n), row)],
        out_shape=[jax.ShapeDtypeStruct((m, n), F32), jax.ShapeDtypeStruct((m, n), BF16)],
        compiler_params=_params("parallel"),
        name="mm_res_ln",
    )(a, w, x, g.reshape(1, n), b.reshape(1, n))


_W1_CQ = (0, A_Q_LORA)
_W1_CKV = (A_Q_LORA, A_Q_LORA + A_KV_LORA)
_W1_KR = (_W1_CKV[1], _W1_CKV[1] + A_ROPE)
_W1_KRS = (_W1_CKV[1] + LANES, _W1_CKV[1] + LANES + A_ROPE)
_W1_COLS = _W1_CKV[1] + 2 * LANES


def _rms(v, g):
    return v * lax.rsqrt(jnp.mean(v * v, axis=-1, keepdims=True) + RMS_EPS) * g


def _mla_proj_kernel(x_ref, w1_ref, gq_ref, gkv_ref, wq_ref, cq_ref, sq_ref, ck_ref, sk_ref,
                     q_ref, lat_ref, ckv_ref, kpe_ref):
    y = jnp.dot(x_ref[...], w1_ref[...], preferred_element_type=F32)
    cq = _rms(y[:, _W1_CQ[0]:_W1_CQ[1]], gq_ref[...]).astype(BF16)
    ckv = _rms(y[:, _W1_CKV[0]:_W1_CKV[1]], gkv_ref[...])
    kpe = y[:, _W1_KR[0]:_W1_KR[1]] * ck_ref[...] + y[:, _W1_KRS[0]:_W1_KRS[1]] * sk_ref[...]
    ckv_ref[...] = ckv
    kpe_ref[...] = kpe
    lat_ref[:, 0:A_KV_LORA] = ckv.astype(BF16)
    lat_ref[:, A_KV_LORA:A_KV_LORA + A_ROPE] = kpe.astype(BF16)
    lat_ref[:, A_KV_LORA + A_ROPE:] = jnp.zeros(
        (lat_ref.shape[0], A_LAT_PAD - A_KV_LORA - A_ROPE), BF16)
    sw0 = A_HEADS * A_QK_PAD
    for h in range(A_HEADS):
        lo, hi = h * A_QK_PAD, (h + 1) * A_QK_PAD
        qp = jnp.dot(cq, wq_ref[:, lo:hi], preferred_element_type=F32)
        qs = jnp.dot(cq, wq_ref[:, sw0 + lo:sw0 + hi], preferred_element_type=F32)
        q_ref[:, lo:hi] = (qp * cq_ref[...] + qs * sq_ref[...]).astype(BF16)


def _mla_proj(xb, w1, gq, gkv, wq, cosq, sinq, cosk, sink, tm=512):
    m = xb.shape[0]
    tm = _row_tile(m, tm)
    row = lambda i: (i, 0)
    fixed = lambda i: (0, 0)
    nq = A_HEADS * A_QK_PAD
    return pl.pallas_call(
        _mla_proj_kernel,
        grid=(m // tm,),
        in_specs=[pl.BlockSpec((tm, D_MODEL), row), pl.BlockSpec(w1.shape, fixed),
                  pl.BlockSpec((1, A_Q_LORA), fixed), pl.BlockSpec((1, A_KV_LORA), fixed),
                  pl.BlockSpec(wq.shape, fixed),
                  pl.BlockSpec((tm, A_QK_PAD), row), pl.BlockSpec((tm, A_QK_PAD), row),
                  pl.BlockSpec((tm, A_ROPE), row), pl.BlockSpec((tm, A_ROPE), row)],
        out_specs=[pl.BlockSpec((tm, nq), row), pl.BlockSpec((tm, A_LAT_PAD), row),
                   pl.BlockSpec((tm, A_KV_LORA), row), pl.BlockSpec((tm, A_ROPE), row)],
        out_shape=[jax.ShapeDtypeStruct((m, nq), BF16), jax.ShapeDtypeStruct((m, A_LAT_PAD), BF16),
                   jax.ShapeDtypeStruct((m, A_KV_LORA), F32), jax.ShapeDtypeStruct((m, A_ROPE), F32)],
        compiler_params=_params("parallel"),
        name="mla_proj",
    )(xb, w1, gq.reshape(1, -1), gkv.reshape(1, -1), wq, cosq, sinq, cosk, sink)


def _flash_kernel(*refs, nh, dq, dv, tq, tk, nk, q_off, chunk_causal, forget):
    if forget:
        q_ref, k_ref, v_ref, fq_ref, fk_ref, o_ref, m_s, l_s, acc_s = refs
    else:
        q_ref, k_ref, v_ref, o_ref, m_s, l_s, acc_s = refs
    iq = pl.program_id(2)
    ik = pl.program_id(3)
    q_lo = q_off + iq * tq
    q_hi = q_lo + tq - 1
    if chunk_causal:
        vis_lo = (q_lo // CHUNK) * CHUNK + CHUNK - 1
        vis_hi = (q_hi // CHUNK) * CHUNK + CHUNK - 1
    else:
        vis_lo, vis_hi = q_lo, q_hi
    k_lo = ik * tk
    needed = k_lo <= vis_hi
    unmasked = k_lo + tk - 1 <= vis_lo

    @pl.when(ik == 0)
    def _init():
        m_s[...] = jnp.full(m_s.shape, -jnp.inf, F32)
        l_s[...] = jnp.zeros(l_s.shape, F32)
        acc_s[...] = jnp.zeros(acc_s.shape, F32)

    def step(masked):
        if masked:
            q_pos = q_lo + lax.broadcasted_iota(jnp.int32, (tq, tk), 0)
            k_pos = k_lo + lax.broadcasted_iota(jnp.int32, (tq, tk), 1)
            if chunk_causal:
                shift = CHUNK.bit_length() - 1
                mask = jnp.right_shift(k_pos, shift) <= jnp.right_shift(q_pos, shift)
            else:
                mask = k_pos <= q_pos
        for h in range(nh):
            q = q_ref[:, h * dq:(h + 1) * dq]
            k = k_ref[:, h * dq:(h + 1) * dq]
            v = v_ref[:, h * dv:(h + 1) * dv]
            s = lax.dot_general(q, k, (((1,), (1,)), ((), ())), preferred_element_type=F32)
            if forget:
                s = s + fq_ref[:, h:h + 1] - fk_ref[h:h + 1, :]
            if masked:
                s = jnp.where(mask, s, NEG_INF)
            m_prev = m_s[h]
            m_new = jnp.maximum(m_prev, jnp.max(s, axis=-1, keepdims=True))
            alpha = jnp.exp(m_prev - m_new)
            p = jnp.exp(s - m_new)
            l_s[h] = alpha * l_s[h] + jnp.sum(p, axis=-1, keepdims=True)
            acc_s[h] = alpha * acc_s[h] + jnp.dot(p.astype(BF16), v, preferred_element_type=F32)
            m_s[h] = m_new

    @pl.when(needed & unmasked)
    def _plain():
        step(False)

    @pl.when(needed & jnp.logical_not(unmasked))
    def _masked():
        step(True)

    @pl.when(ik == nk - 1)
    def _fin():
        for h in range(nh):
            o_ref[:, h * dv:(h + 1) * dv] = (acc_s[h] / l_s[h]).astype(o_ref.dtype)


def _flash(q, k, v, *, nh, dq, dv, n_hblk, k_col0, v_col0, tq, tk, q_off, chunk_causal,
           fq=None, fk=None):
    b, t_q = q.shape[0], q.shape[1]
    t_k = k.shape[1]
    tq = tq if t_q % tq == 0 else t_q
    tk = tk if t_k % tk == 0 else t_k
    nq, nk = t_q // tq, t_k // tk
    forget = fq is not None

    def last_blk(i):
        q_hi = q_off + (i + 1) * tq - 1
        vis = (q_hi // CHUNK) * CHUNK + CHUNK - 1 if chunk_causal else q_hi
        return jnp.minimum(vis // tk, nk - 1)

    in_specs = [
        pl.BlockSpec((None, tq, nh * dq), lambda bi, h, i, j: (bi, i, h)),
        pl.BlockSpec((None, tk, nh * dq), lambda bi, h, i, j: (bi, jnp.minimum(j, last_blk(i)), k_col0 + h)),
        pl.BlockSpec((None, tk, nh * dv), lambda bi, h, i, j: (bi, jnp.minimum(j, last_blk(i)), v_col0 + h)),
    ]
    args = [q, k, v]
    if forget:
        in_specs += [
            pl.BlockSpec((None, None, tq, LANES), lambda bi, h, i, j: (bi, h, i, 0)),
            pl.BlockSpec((None, None, fk.shape[2], tk),
                         lambda bi, h, i, j: (bi, h, 0, jnp.minimum(j, last_blk(i)))),
        ]
        args += [fq, fk]
    kern = functools.partial(_flash_kernel, nh=nh, dq=dq, dv=dv, tq=tq, tk=tk, nk=nk, q_off=q_off,
                             chunk_causal=chunk_causal, forget=forget)
    return pl.pallas_call(
        kern,
        grid=(b, n_hblk, nq, nk),
        in_specs=in_specs,
        out_specs=pl.BlockSpec((None, tq, nh * dv), lambda bi, h, i, j: (bi, i, h)),
        out_shape=jax.ShapeDtypeStruct((b, t_q, n_hblk * nh * dv), BF16),
        scratch_shapes=[pltpu.VMEM((nh, tq, 1), F32), pltpu.VMEM((nh, tq, 1), F32),
                        pltpu.VMEM((nh, tq, dv), F32)],
        compiler_params=_params("parallel", "parallel", "parallel", "arbitrary"),
        name="flash_fox" if forget else "flash_mla",
    )(*args)


FLASH_T = 1024
FLASH_QC = 256
FLASH_AHEAD = 4
A_HEADS_PER_STEP = 2
LOG2E = math.log2(math.e)


def _flash_t_kernel(q_ref, k_ref, vt_ref, o_ref, m_s, l_s, acc_s, *, nh, dq, dv, t, chunk_causal):
    iq = pl.program_id(2)
    m_s[...] = jnp.full(m_s.shape, -jnp.inf, F32)
    l_s[...] = jnp.zeros(l_s.shape, F32)
    acc_s[...] = jnp.zeros(acc_s.shape, F32)

    def block(j, masked):
        row0 = pl.multiple_of(j * t, t)
        qc = FLASH_QC
        chains = [(h, c) for h in range(nh) for c in range(t // qc)]

        def n_keys(c):
            return (c + 1) * qc if masked else t

        def qk(h, c):
            return lax.dot_general(k_ref[pl.ds(row0, n_keys(c)), h * dq:(h + 1) * dq],
                                   q_ref[c * qc:(c + 1) * qc, h * dq:(h + 1) * dq],
                                   (((1,), (1,)), ((), ())), preferred_element_type=F32)

        def softmax_pv(h, c, s):
            cols = slice(c * qc, (c + 1) * qc)
            if masked:
                k_pos = lax.broadcasted_iota(jnp.int32, (n_keys(c), qc), 0)
                q_pos = lax.broadcasted_iota(jnp.int32, (n_keys(c), qc), 1) + c * qc
                if chunk_causal:
                    shift = CHUNK.bit_length() - 1
                    mask = jnp.right_shift(k_pos, shift) <= jnp.right_shift(q_pos, shift)
                else:
                    mask = k_pos <= q_pos
                s = jnp.where(mask, s, NEG_INF)
            m_prev = m_s[h, :, cols]
            m_new = jnp.maximum(m_prev, jnp.max(s, axis=0, keepdims=True))
            alpha = jnp.exp2(m_prev - m_new)
            p = jnp.exp2(s - m_new)
            l_s[h, :, cols] = alpha * l_s[h, :, cols] + jnp.sum(p, axis=0, keepdims=True)
            acc_s[h, :, cols] = alpha * acc_s[h, :, cols] + jnp.dot(
                vt_ref[j, h * dv:(h + 1) * dv, 0:n_keys(c)], p.astype(BF16), preferred_element_type=F32)
            m_s[h, :, cols] = m_new

        pending = [qk(*chain) for chain in chains[:FLASH_AHEAD]]
        for i, chain in enumerate(chains):
            if i + FLASH_AHEAD < len(chains):
                pending.append(qk(*chains[i + FLASH_AHEAD]))
            softmax_pv(*chain, pending.pop(0))

    def full_block(j, carry):
        block(j, False)
        return carry

    lax.fori_loop(0, iq, full_block, 0)
    block(iq, True)
    out = jnp.concatenate([acc_s[h] / l_s[h] for h in range(nh)], axis=0)
    o_ref[...] = out.T.astype(o_ref.dtype)


def _flash_t(q, k, k_col0, vt, *, nh, dq, dv, n_hblk, chunk_causal):
    b, t_all = q.shape[0], q.shape[1]
    t = FLASH_T
    nblk = t_all // t
    kern = functools.partial(_flash_t_kernel, nh=nh, dq=dq, dv=dv, t=t, chunk_causal=chunk_causal)
    return pl.pallas_call(
        kern,
        grid=(b, n_hblk, nblk),
        in_specs=[pl.BlockSpec((None, t, nh * dq), lambda bi, h, i: (bi, i, h)),
                  pl.BlockSpec((None, t_all, nh * dq), lambda bi, h, i: (bi, 0, k_col0 + h),
                               pipeline_mode=pl.Buffered(1)),
                  pl.BlockSpec((None, nblk, nh * dv, t), lambda bi, h, i: (bi, 0, h, 0),
                               pipeline_mode=pl.Buffered(1))],
        out_specs=pl.BlockSpec((None, t, nh * dv), lambda bi, h, i: (bi, i, h)),
        out_shape=jax.ShapeDtypeStruct((b, t_all, n_hblk * nh * dv), BF16),
        scratch_shapes=[pltpu.VMEM((nh, 1, t), F32), pltpu.VMEM((nh, 1, t), F32),
                        pltpu.VMEM((nh, dv, t), F32)],
        compiler_params=_params("parallel", "parallel", "parallel"),
        name="flash_t_chunk" if chunk_causal else "flash_t_frame",
    )(q, k, vt)


B_HEADS_PER_STEP = LANES // B_HEAD_DIM


def _band_step_kernel(q_ref, k_ref, v_ref, bias_ref, o_ref):
    dh = B_HEAD_DIM
    for h in range(B_HEADS):
        cols = slice(h * dh, (h + 1) * dh)
        s = lax.dot_general(q_ref[:, cols], k_ref[:, cols], (((1,), (1,)), ((), ())),
                            preferred_element_type=F32) + bias_ref[h]
        p = jnp.exp(s - jnp.max(s, axis=-1, keepdims=True))
        l = jnp.sum(p, axis=-1, keepdims=True)
        o = jnp.dot(p.astype(BF16), v_ref[:, cols], preferred_element_type=F32)
        o_ref[:, cols] = (o / l).astype(o_ref.dtype)


def _band_bias(rel_bias, q_pos, k_pos):
    nq, nk = len(q_pos), len(k_pos)
    assert (np.diff(q_pos) == 1).all() and (np.diff(k_pos) == 1).all()
    m = np.arange(nq + nk - 1)
    u = rel_bias[:, np.clip(q_pos[0] - k_pos[0] + nq - 1 - m, -B_REL_CLIP, B_REL_CLIP) + B_REL_CLIP]
    period = nq + nk
    w = jnp.concatenate([u[:, nq - 1:], jnp.zeros((u.shape[0], 1), u.dtype), u[:, :nq - 1]], axis=1)
    skew = jnp.tile(w, (1, nq))[:, :nq * (period - 1)].reshape(-1, nq, period - 1)[:, :, :nk]
    qc = q_pos[:, None] // CHUNK
    kc = k_pos[None, :] // CHUNK
    mask = (kc <= qc) & (kc >= qc - B_LEFT_CHUNKS) & (k_pos[None, :] >= 0)
    return jnp.where(jnp.asarray(mask)[None], skew, NEG_INF).astype(F32)


def _band_step(q, kv, bias):
    b, t, _ = q.shape
    t_k = kv.shape[1]
    return pl.pallas_call(
        _band_step_kernel,
        grid=(b,),
        in_specs=[pl.BlockSpec((None, t, D_MODEL), lambda bi: (bi, 0, 0)),
                  pl.BlockSpec((None, t_k, D_MODEL), lambda bi: (bi, 0, 0)),
                  pl.BlockSpec((None, t_k, D_MODEL), lambda bi: (bi, 0, 1)),
                  pl.BlockSpec(bias.shape, lambda bi: (0, 0, 0))],
        out_specs=pl.BlockSpec((None, t, D_MODEL), lambda bi: (bi, 0, 0)),
        out_shape=jax.ShapeDtypeStruct((b, t, D_MODEL), BF16),
        compiler_params=_params("parallel"),
        name="band_step",
    )(q, kv, kv, bias)


BAND_T = 4 * CHUNK


def _band_t_kernel(*refs, nkb, t):
    q_ref = refs[0]
    k_refs = refs[1:1 + nkb]
    vt_refs = refs[1 + nkb:1 + 2 * nkb]
    bias_ref, o_ref = refs[1 + 2 * nkb], refs[2 + 2 * nkb]
    iq = pl.program_id(1)
    dh = B_HEAD_DIM

    def qk(h):
        cols = slice(h * dh, (h + 1) * dh)
        parts = []
        for j in range(nkb):
            s = lax.dot_general(k_refs[j][:, cols], q_ref[:, cols], (((1,), (1,)), ((), ())),
                                preferred_element_type=F32)
            if j < nkb - 1:
                s = jnp.where(iq >= nkb - 1 - j, s, NEG_INF)
            parts.append(s)
        return jnp.concatenate(parts, axis=0) + bias_ref[h]

    def softmax_pv(h, s):
        p = jnp.exp2(s - jnp.max(s, axis=0, keepdims=True))
        l = jnp.sum(p, axis=0, keepdims=True)
        pb = p.astype(BF16)
        o = functools.reduce(lambda a, c: a + c, [
            jnp.dot(vt_refs[j][h * dh:(h + 1) * dh, :], pb[j * t:(j + 1) * t], preferred_element_type=F32)
            for j in range(nkb)])
        return o / l

    heads = list(range(B_HEADS))
    pending = [qk(h) for h in heads[:FLASH_AHEAD]]
    outs = []
    for h in heads:
        if h + FLASH_AHEAD < B_HEADS:
            pending.append(qk(h + FLASH_AHEAD))
        outs.append(softmax_pv(h, pending.pop(0)))
        if len(outs) == B_HEADS_PER_STEP:
            c0 = (h + 1 - B_HEADS_PER_STEP) * dh
            o_ref[:, c0:c0 + LANES] = jnp.concatenate(outs, axis=0).T.astype(o_ref.dtype)
            outs = []


def _band_t(q, kv, vt, bias_t, *, nkb):
    b, t_all = q.shape[0], q.shape[1]
    t = BAND_T

    def back(j):
        return nkb - 1 - j

    k_specs = [pl.BlockSpec((None, t, D_MODEL), lambda bi, i, j=j: (bi, jnp.maximum(i - back(j), 0), 0))
               for j in range(nkb)]
    vt_specs = [pl.BlockSpec((None, None, D_MODEL, t),
                             lambda bi, i, j=j: (bi, jnp.maximum(i - back(j), 0), 0, 0))
                for j in range(nkb)]
    return pl.pallas_call(
        functools.partial(_band_t_kernel, nkb=nkb, t=t),
        grid=(b, t_all // t),
        in_specs=([pl.BlockSpec((None, t, D_MODEL), lambda bi, i: (bi, i, 0))] + k_specs + vt_specs
                  + [pl.BlockSpec(bias_t.shape, lambda bi, i: (0, 0, 0), pipeline_mode=pl.Buffered(1))]),
        out_specs=pl.BlockSpec((None, t, D_MODEL), lambda bi, i: (bi, i, 0)),
        out_shape=jax.ShapeDtypeStruct((b, t_all, D_MODEL), BF16),
        compiler_params=_params("parallel", "parallel"),
        name="band_t",
    )(q, *([kv] * nkb), *([vt] * nkb), bias_t)


def _logf_kernel(x_ref, w_ref, b_ref, o_ref):
    z = jnp.dot(x_ref[...], w_ref[...], preferred_element_type=F32) + b_ref[...]
    o_ref[...] = -(jnp.maximum(-z, 0.0) + jnp.log1p(jnp.exp(-jnp.abs(z))))


def _logf(xb, w_pad, b_pad, tm=512):
    m = xb.shape[0]
    tm = _row_tile(m, tm)
    return pl.pallas_call(
        _logf_kernel,
        grid=(m // tm,),
        in_specs=[pl.BlockSpec((tm, D_MODEL), lambda i: (i, 0)),
                  pl.BlockSpec((D_MODEL, LANES), lambda i: (0, 0)),
                  pl.BlockSpec((1, LANES), lambda i: (0, 0))],
        out_specs=pl.BlockSpec((tm, LANES), lambda i: (i, 0)),
        out_shape=jax.ShapeDtypeStruct((m, LANES), F32),
        compiler_params=_params("parallel"),
        name="logf",
    )(xb, w_pad, b_pad)


def _split3(x):
    hi = x.astype(BF16)
    r = x - hi.astype(F32)
    mid = r.astype(BF16)
    lo = (r - mid.astype(F32)).astype(BF16)
    return hi, mid, lo


def _cumsum_kernel(x_ref, o_ref, hi_ref, mid_ref, lo_ref, carry, *, tc):
    @pl.when(pl.program_id(1) == 0)
    def _():
        carry[...] = jnp.zeros(carry.shape, F32)

    tri = (lax.broadcasted_iota(jnp.int32, (tc, tc), 0)
           >= lax.broadcasted_iota(jnp.int32, (tc, tc), 1)).astype(BF16)
    c = functools.reduce(lambda a, b: a + b, [jnp.dot(tri, piece, preferred_element_type=F32)
                                              for piece in _split3(x_ref[...])])
    out = c + carry[0:1, :]
    o_ref[...] = out
    hi_ref[...], mid_ref[...], lo_ref[...] = _split3(out * LOG2E)
    carry[...] = jnp.broadcast_to(out[tc - 1:tc, :], carry.shape)


def _cumsum(x, tc=256):
    b, t, _ = x.shape
    spec = pl.BlockSpec((None, tc, LANES), lambda bi, i: (bi, i, 0))
    piece = jax.ShapeDtypeStruct(x.shape, BF16)
    return pl.pallas_call(
        functools.partial(_cumsum_kernel, tc=tc),
        grid=(b, t // tc),
        in_specs=[spec],
        out_specs=[spec] * 4,
        out_shape=[jax.ShapeDtypeStruct(x.shape, F32), piece, piece, piece],
        scratch_shapes=[pltpu.VMEM((SUBLANES, LANES), F32)],
        compiler_params=_params("parallel", "arbitrary"),
        name="cumsum",
    )(x)


FFN_CHUNK = 256
CONV_ROWS = SUBLANES


def _ffn_up_kernel(x_ref, w_ref, c_ref, p_ref, act_ref, s_ref, halo, *, tm):
    @pl.when(pl.program_id(1) == 0)
    def _():
        halo[...] = p_ref[...]

    x = x_ref[...]
    n_chunks = D_FF // FFN_CHUNK
    groups = tm // CONV_ROWS
    row = lax.broadcasted_iota(jnp.int32, (groups, CONV_ROWS, FFN_CHUNK), 1)

    def up(c, half):
        col = half * D_FF + c * FFN_CHUNK
        return jnp.dot(x, w_ref[:, col:col + FFN_CHUNK], preferred_element_type=F32)

    def conv(h, c, half):
        cols = slice(half * D_FF + c * FFN_CHUNK, half * D_FF + (c + 1) * FFN_CHUNK)
        ext = jnp.concatenate([halo[:, cols], h], axis=0).reshape(groups + 1, CONV_ROWS, FFN_CHUNK)
        hc = c_ref[CONV_W:CONV_W + 1, cols] + c_ref[CONV_W - 1:CONV_W, cols] * h
        for s in range(1, CONV_W):
            rot = pltpu.roll(ext, s, axis=1)
            shifted = jnp.where(row < s, rot[:groups], rot[1:]).reshape(tm, FFN_CHUNK)
            hc = hc + c_ref[CONV_W - 1 - s:CONV_W - s, cols] * shifted
        tail = h[tm - CONV_ROWS:tm]
        halo[:, cols] = tail
        s_ref[:, cols] = tail
        return hc

    pending = [(up(0, 0), up(0, 1))]
    for c in range(n_chunks):
        if c + 1 < n_chunks:
            pending.append((up(c + 1, 0), up(c + 1, 1)))
        ha, hg = pending.pop(0)
        a = conv(ha, c, 0)
        g = conv(hg, c, 1)
        act_ref[:, c * FFN_CHUNK:(c + 1) * FFN_CHUNK] = (g * jax.nn.sigmoid(g) * a).astype(BF16)


def _ffn_up(xb, w_up, conv_tab, past, tm=256):
    b, t, _ = xb.shape
    tm = _row_tile(t, tm)
    fixed = lambda bi, ti: (0, 0)
    return pl.pallas_call(
        functools.partial(_ffn_up_kernel, tm=tm),
        grid=(b, t // tm),
        in_specs=[pl.BlockSpec((None, tm, D_MODEL), lambda bi, ti: (bi, ti, 0)),
                  pl.BlockSpec((D_MODEL, 2 * D_FF), fixed),
                  pl.BlockSpec((CONV_ROWS, 2 * D_FF), fixed),
                  pl.BlockSpec((None, CONV_ROWS, 2 * D_FF), lambda bi, ti: (bi, 0, 0))],
        out_specs=[pl.BlockSpec((None, tm, D_FF), lambda bi, ti: (bi, ti, 0)),
                   pl.BlockSpec((None, CONV_ROWS, 2 * D_FF), lambda bi, ti: (bi, 0, 0))],
        out_shape=[jax.ShapeDtypeStruct((b, t, D_FF), BF16),
                   jax.ShapeDtypeStruct((b, CONV_ROWS, 2 * D_FF), F32)],
        scratch_shapes=[pltpu.VMEM((CONV_ROWS, 2 * D_FF), F32)],
        compiler_params=_params("parallel", "arbitrary"),
        name="ffn_up",
    )(xb, w_up, conv_tab, past)


def _rope_tables(pos, batch, q_scale):
    half = A_ROPE // 2
    inv_freq = ROPE_THETA ** (-jnp.arange(half, dtype=F32) / half)
    ang = pos.astype(F32)[:, None] * inv_freq
    cos, sin = jnp.cos(ang), jnp.sin(ang)
    cosk = jnp.concatenate([cos, cos], axis=-1)
    sink = jnp.concatenate([sin, sin], axis=-1)
    t = pos.shape[0]
    pad = jnp.zeros((t, A_QK_PAD - A_NOPE - A_ROPE), F32)
    cosq = q_scale * jnp.concatenate([jnp.ones((t, A_NOPE), F32), cosk, pad], axis=-1)
    sinq = q_scale * jnp.concatenate([jnp.zeros((t, A_NOPE), F32), sink, pad], axis=-1)
    return tuple(jnp.tile(a, (batch, 1)) for a in (cosq, sinq, cosk, sink))


def _swap_halves(w):
    half = w.shape[-1] // 2
    return jnp.concatenate([-w[..., half:], w[..., :half]], axis=-1)


def _mla_weights(w_dq, w_dkv, w_kr, w_uq, w_uk, w_uv):
    zc = jnp.zeros((D_MODEL, LANES - A_ROPE), F32)
    w1 = jnp.concatenate([w_dq, w_dkv, w_kr, zc, _swap_halves(w_kr), zc], axis=1).astype(BF16)
    wq = w_uq.reshape(A_Q_LORA, A_HEADS, A_NOPE + A_ROPE)
    nope, rope = wq[..., :A_NOPE], wq[..., A_NOPE:]
    zpad = jnp.zeros((A_Q_LORA, A_HEADS, A_QK_PAD - A_NOPE - A_ROPE), F32)
    w_cat = jnp.concatenate([nope, rope, zpad], axis=-1).reshape(A_Q_LORA, -1)
    w_sw = jnp.concatenate([jnp.zeros_like(nope), _swap_halves(rope), zpad], axis=-1).reshape(A_Q_LORA, -1)
    wq2 = jnp.concatenate([w_cat, w_sw], axis=1).astype(BF16)
    wk = jnp.zeros((A_LAT_PAD, A_HEADS, A_QK_PAD), F32)
    wk = wk.at[:A_KV_LORA, :, :A_NOPE].set(w_uk)
    eye = jnp.broadcast_to(jnp.eye(A_ROPE, dtype=F32)[:, None, :], (A_ROPE, A_HEADS, A_ROPE))
    wk = wk.at[A_KV_LORA:A_KV_LORA + A_ROPE, :, A_NOPE:A_NOPE + A_ROPE].set(eye)
    wv = jnp.zeros((A_LAT_PAD, A_HEADS * A_V), F32).at[:A_KV_LORA].set(w_uv.reshape(A_KV_LORA, -1))
    wkv = jnp.concatenate([wk.reshape(A_LAT_PAD, -1), wv], axis=1).astype(BF16)
    return w1, wq2, wkv


def _mla_mixer(xb, b, t, pos, ckv_past, kpe_past, w, i):
    w1, wq2, wkv = _mla_weights(w['a_w_dq'][i], w['a_w_dkv'][i], w['a_w_kr'][i], w['a_w_uq'][i],
                                w['a_w_uk'][i], w['a_w_uv'][i])
    resident = ckv_past is None and t % FLASH_T == 0
    q_scale = (A_NOPE + A_ROPE) ** -0.5 * (LOG2E if resident else 1.0)
    q, lat, ckv, kpe = _mla_proj(xb, w1, w['a_g_q'][i], w['a_g_kv'][i], wq2,
                                 *_rope_tables(pos, b, q_scale))
    lat = lat.reshape(b, t, A_LAT_PAD)
    q_off = 0
    if ckv_past is not None:
        p_len = ckv_past.shape[1]
        past = jnp.concatenate(
            [ckv_past, kpe_past, jnp.zeros((b, p_len, A_LAT_PAD - A_KV_LORA - A_ROPE), F32)], axis=-1)
        lat = jnp.concatenate([past.astype(BF16), lat], axis=1)
        q_off = p_len
    t_k = lat.shape[1]
    lat2 = lat.reshape(b * t_k, A_LAT_PAD)
    n_k = A_HEADS * A_QK_PAD
    if resident:
        (k_cat,) = _mm(lat2, wkv[:, :n_k], [BF16])
        vt = _proj_t(lat2, wkv[:, n_k:], FLASH_T).reshape(b, t_k // FLASH_T, A_HEADS * A_V, FLASH_T)
        o = _flash_t(q.reshape(b, t, -1), k_cat.reshape(b, t_k, n_k), 0, vt, nh=A_HEADS_PER_STEP,
                     dq=A_QK_PAD, dv=A_V, n_hblk=A_HEADS // A_HEADS_PER_STEP, chunk_causal=True)
    else:
        (kv,) = _mm(lat2, wkv, [BF16])
        kv = kv.reshape(b, t_k, -1)
        o = _flash(q.reshape(b, t, -1), kv, kv, nh=A_HEADS, dq=A_QK_PAD, dv=A_V, n_hblk=1,
                   k_col0=0, v_col0=A_QK_PAD // A_V, tq=512, tk=512, q_off=q_off, chunk_causal=True)
    return o.reshape(b * t, A_HEADS * A_V), ckv.reshape(b, t, -1), kpe.reshape(b, t, -1)


def _qkv(xb, w_qkv, q_scale):
    wq = (w_qkv[:, :D_MODEL] * q_scale).astype(BF16)
    (q,) = _mm(xb, wq, [BF16])
    kv32, kvb = _mm(xb, w_qkv[:, D_MODEL:].astype(BF16), [F32, BF16])
    return q, kv32, kvb


def _band_mixer(xb, b, t, pos0, k_past, v_past, w, i):
    prompt = k_past is None
    assert not prompt or t % BAND_T == 0
    q, kv32, kvb = _qkv(xb, w['b_w_qkv'][i], B_HEAD_DIM ** -0.5 * (LOG2E if prompt else 1.0))
    q = q.reshape(b, t, D_MODEL)
    k32 = kv32[:, :D_MODEL].reshape(b, t, B_HEADS, B_HEAD_DIM)
    v32 = kv32[:, D_MODEL:].reshape(b, t, B_HEADS, B_HEAD_DIM)
    if prompt:
        nkb = B_WIN // BAND_T + 1
        bias = _band_bias(w['b_rel_bias'][i], B_WIN + np.arange(BAND_T), np.arange(B_WIN + BAND_T))
        kvb = kvb.reshape(b, t, 2 * D_MODEL)
        vt = _proj_t(xb, w['b_w_qkv'][i][:, 2 * D_MODEL:], BAND_T).reshape(b, t // BAND_T, D_MODEL, BAND_T)
        o = _band_t(q, kvb, vt, LOG2E * bias.transpose(0, 2, 1), nkb=nkb)
        keep = min(B_WIN, t)
        k_new, v_new = k32[:, t - keep:], v32[:, t - keep:]
    else:
        p_len = k_past.shape[1]
        kvb = kvb.reshape(b, t, 2 * D_MODEL)
        past = jnp.concatenate([k_past.reshape(b, p_len, D_MODEL), v_past.reshape(b, p_len, D_MODEL)],
                               axis=-1).astype(BF16)
        kv_all = jnp.concatenate([past, kvb], axis=1)
        q_pos = pos0 + np.arange(t)
        k_pos = np.concatenate([np.arange(pos0 - p_len, pos0), q_pos])
        bias = _band_bias(w['b_rel_bias'][i], q_pos, k_pos)
        o = _band_step(q, kv_all, bias)
        k_new, v_new = k32, v32
    return o.reshape(b * t, D_MODEL), k_new, v_new


C_AUG = LANES
N_PIECES = 3


def _aug_kernel(x_ref, w_ref, hi_ref, mid_ref, lo_ref, p_ref, b_ref, o_ref):
    acc = jnp.dot(x_ref[...], w_ref[...], preferred_element_type=F32) + b_ref[...]
    for piece, f_ref in enumerate((hi_ref, mid_ref, lo_ref)):
        acc = acc + jnp.dot(f_ref[...], p_ref[piece], preferred_element_type=F32)
    o_ref[...] = acc.astype(BF16)


def _aug_tables():
    n = C_HEADS * C_AUG
    place = np.zeros((N_PIECES, LANES, 2 * n), np.float32)
    ones = np.zeros((1, 2 * n), np.float32)
    for h in range(C_HEADS):
        base = h * C_AUG + C_HEAD_DIM
        for piece in range(N_PIECES):
            ones[0, base + piece] = 1.0
            place[piece, h, base + N_PIECES + piece] = 1.0
            place[piece, h, n + base + piece] = -1.0
            ones[0, n + base + N_PIECES + piece] = 1.0
    return jnp.asarray(place, BF16), jnp.asarray(ones)


def _aug_qk(xb, w_aug, pieces, tm=512, tn=1024):
    m = xb.shape[0]
    n = w_aug.shape[1]
    tm = _row_tile(m, tm)
    place, ones = _aug_tables()
    row = lambda i, j: (i, 0)
    return pl.pallas_call(
        _aug_kernel,
        grid=(m // tm, n // tn),
        in_specs=[pl.BlockSpec((tm, D_MODEL), row), pl.BlockSpec((D_MODEL, tn), lambda i, j: (0, j)),
                  pl.BlockSpec((tm, LANES), row), pl.BlockSpec((tm, LANES), row), pl.BlockSpec((tm, LANES), row),
                  pl.BlockSpec((N_PIECES, LANES, tn), lambda i, j: (0, 0, j)),
                  pl.BlockSpec((1, tn), lambda i, j: (0, j))],
        out_specs=pl.BlockSpec((tm, tn), lambda i, j: (i, j)),
        out_shape=jax.ShapeDtypeStruct((m, n), BF16),
        compiler_params=_params("parallel", "parallel"),
        name="aug_qk",
    )(xb, w_aug, *pieces, place, ones)


def _pad_heads(w, scale):
    w = (w * scale).reshape(D_MODEL, C_HEADS, C_HEAD_DIM)
    return jnp.pad(w, ((0, 0), (0, 0), (0, C_AUG - C_HEAD_DIM))).reshape(D_MODEL, C_HEADS * C_AUG)


def _fox_mixer(xb, b, t, k_past, v_past, lf_past, w, i):
    resident = k_past is None and t % FLASH_T == 0
    w_qkv = w['c_w_qkv'][i]
    k32, *kb = _mm(xb, w_qkv[:, D_MODEL:2 * D_MODEL].astype(BF16), [F32] if resident else [F32, BF16])
    v32, *vb = _mm(xb, w_qkv[:, 2 * D_MODEL:].astype(BF16), [F32] if resident else [F32, BF16])
    k32 = k32.reshape(b, t, C_HEADS, C_HEAD_DIM)
    v32 = v32.reshape(b, t, C_HEADS, C_HEAD_DIM)
    w_f = jnp.zeros((D_MODEL, LANES), F32).at[:, :C_HEADS].set(w['c_w_f'][i]).astype(BF16)
    b_f = jnp.zeros((1, LANES), F32).at[0, :C_HEADS].set(w['c_b_f'][i])
    log_f = _logf(xb, w_f, b_f).reshape(b, t, LANES)
    if not resident:
        kvb = jnp.concatenate([kb[0], vb[0]], axis=-1).reshape(b, t, 2 * D_MODEL)
    lf_all = log_f
    q_off = 0
    if k_past is not None:
        p_len = k_past.shape[1]
        past = jnp.concatenate([k_past.reshape(b, p_len, D_MODEL), v_past.reshape(b, p_len, D_MODEL)],
                               axis=-1).astype(BF16)
        kvb = jnp.concatenate([past, kvb], axis=1)
        lf_all = jnp.concatenate([jnp.pad(lf_past, ((0, 0), (0, 0), (0, LANES - C_HEADS))), log_f], axis=1)
        q_off = p_len
    t_k = lf_all.shape[1]
    tc = 256
    t_pad = -(-t_k // tc) * tc
    f_cum, *f_pieces = _cumsum(jnp.pad(lf_all, ((0, 0), (0, t_pad - t_k), (0, 0))), tc)
    nh = LANES // C_HEAD_DIM
    n_hblk = C_HEADS // nh
    q_scale = C_HEAD_DIM ** -0.5
    if resident:
        w_aug = jnp.concatenate([_pad_heads(w_qkv[:, :D_MODEL], q_scale * LOG2E),
                                 _pad_heads(w_qkv[:, D_MODEL:2 * D_MODEL], 1.0)], axis=1).astype(BF16)
        qk_aug = _aug_qk(xb, w_aug, [p.reshape(b * t, LANES) for p in f_pieces]).reshape(b, t, -1)
        n_q = C_HEADS * C_AUG
        vt = _proj_t(xb, w_qkv[:, 2 * D_MODEL:], FLASH_T).reshape(b, t // FLASH_T, D_MODEL, FLASH_T)
        o = _flash_t(qk_aug, qk_aug, n_q // (nh * C_AUG), vt,
                     nh=nh, dq=C_AUG, dv=C_HEAD_DIM, n_hblk=n_hblk, chunk_causal=False)
        return o.reshape(b * t, D_MODEL), k32, v32, log_f[:, :, :C_HEADS]
    (q,) = _mm(xb, (w_qkv[:, :D_MODEL] * q_scale).astype(BF16), [BF16])
    f_cum = f_cum[:, None, :t_k, :C_HEADS]
    fq = jnp.pad(f_cum[:, :, t_k - t:], ((0, 0), (0, 0), (0, 0), (0, LANES - C_HEADS)))
    fk = f_cum.transpose(0, 1, 3, 2)
    o = _flash(q.reshape(b, t, D_MODEL), kvb, kvb, nh=C_HEADS, dq=C_HEAD_DIM, dv=C_HEAD_DIM, n_hblk=1,
               k_col0=0, v_col0=1, tq=512, tk=512, q_off=q_off, chunk_causal=False, fq=fq, fk=fk)
    return o.reshape(b * t, D_MODEL), k32, v32, log_f[:, :, :C_HEADS]


def _conv_ffn(xb, b, t, conv_past, w, i):
    tab = jnp.concatenate([w['f_conv_w'][i], w['f_conv_b'][i][None],
                           jnp.zeros((CONV_ROWS - CONV_W - 1, 2 * D_FF), F32)], axis=0)
    if conv_past is None:
        past = jnp.zeros((b, CONV_ROWS, 2 * D_FF), F32)
    else:
        past = jnp.pad(conv_past, ((0, 0), (CONV_ROWS - (CONV_W - 1), 0), (0, 0)))
    act, tail = _ffn_up(xb.reshape(b, t, D_MODEL), w['f_w_up_bf16'][i], tab, past)
    return act.reshape(b * t, D_FF), tail[:, CONV_ROWS - (CONV_W - 1):]


def _trunk(x, pos0, past, w):
    b, t, _ = x.shape
    pos = pos0 + jnp.arange(t)
    xf = x.reshape(b * t, D_MODEL)
    xb = xf.astype(BF16)
    outs = {n: [] for n in ('a_ckv', 'a_kpe', 'b_k', 'b_v', 'c_k', 'c_v', 'c_logf', 'ffn_conv')}
    ia = ib = ic = 0
    get = lambda name, j: None if past is None else past[name][j]
    for i in range(DEPTH):
        kind = i % N_MIXERS
        if kind == 0:
            o, ckv, kpe = _mla_mixer(xb, b, t, pos, get('a_ckv', ia), get('a_kpe', ia), w, ia)
            outs['a_ckv'].append(ckv)
            outs['a_kpe'].append(kpe)
            w_o = w['a_w_o_bf16'][ia]
            ia += 1
        elif kind == 1:
            o, kb, vb = _band_mixer(xb, b, t, pos0, get('b_k', ib), get('b_v', ib), w, ib)
            outs['b_k'].append(kb)
            outs['b_v'].append(vb)
            w_o = w['b_w_o_bf16'][ib]
            ib += 1
        else:
            o, kc, vc, lf = _fox_mixer(xb, b, t, get('c_k', ic), get('c_v', ic), get('c_logf', ic), w, ic)
            outs['c_k'].append(kc)
            outs['c_v'].append(vc)
            outs['c_logf'].append(lf)
            w_o = w['c_w_o_bf16'][ic]
            ic += 1
        xf, xb = _mm_res_ln(o, w_o, xf, w['ln1_g'][i], w['ln1_b'][i])
        act, conv_state = _conv_ffn(xb, b, t, get('ffn_conv', i), w, i)
        outs['ffn_conv'].append(conv_state)
        xf, xb = _mm_res_ln(act, w['f_w_down_bf16'][i], xf, w['ln2_g'][i], w['ln2_b'][i])
    return xf.reshape(b, t, D_MODEL), {n: jnp.stack(v) for n, v in outs.items()}


def kernel(x_prompt, x_sample, cache_a_ckv, cache_a_kpe, cache_b_k, cache_b_v, cache_c_k, cache_c_v,
           cache_c_logf, state_ffn_conv, a_w_dq, a_g_q, a_w_uq, a_w_dkv, a_g_kv, a_w_kr, a_w_uk, a_w_uv,
           a_w_o, b_w_qkv, b_rel_bias, b_w_o, c_w_qkv, c_w_f, c_b_f, c_w_o, f_w_up, f_conv_w, f_conv_b,
           f_w_down, ln1_g, ln1_b, ln2_g, ln2_b):
    w = dict(a_w_dq=a_w_dq, a_g_q=a_g_q, a_w_uq=a_w_uq, a_w_dkv=a_w_dkv, a_g_kv=a_g_kv, a_w_kr=a_w_kr,
             a_w_uk=a_w_uk, a_w_uv=a_w_uv, a_w_o=a_w_o, b_w_qkv=b_w_qkv, b_rel_bias=b_rel_bias, b_w_o=b_w_o,
             c_w_qkv=c_w_qkv, c_w_f=c_w_f, c_b_f=c_b_f, c_w_o=c_w_o, f_w_up=f_w_up, f_conv_w=f_conv_w,
             f_conv_b=f_conv_b, f_w_down=f_w_down, ln1_g=ln1_g, ln1_b=ln1_b, ln2_g=ln2_g, ln2_b=ln2_b)
    past = dict(a_ckv=cache_a_ckv, a_kpe=cache_a_kpe, b_k=cache_b_k, b_v=cache_b_v, c_k=cache_c_k,
                c_v=cache_c_v, c_logf=cache_c_logf, ffn_conv=state_ffn_conv)
    past_len = cache_a_ckv.shape[2]
    for name in ('f_w_up', 'f_w_down', 'a_w_o', 'b_w_o', 'c_w_o'):
        w[name + '_bf16'] = _to_bf16(w[name])
    y_prompt, p = _trunk(x_prompt, 0, None, w)
    y_sample, s = _trunk(x_sample, past_len, past, w)
    names = ('a_ckv', 'a_kpe', 'b_k', 'b_v', 'c_k', 'c_v', 'c_logf', 'ffn_conv')
    return (y_prompt, y_sample) + tuple(p[n] for n in names) + tuple(s[n] for n in names)
```

```python
import functools
import math

import numpy as np
import jax
import jax.numpy as jnp
from jax import lax
from jax.experimental import pallas as pl
from jax.experimental.pallas import tpu as pltpu

F32 = jnp.float32
BF16 = jnp.bfloat16

D_MODEL = 1024
DEPTH = 4
CHUNK = 64
N_MIXERS = 3

A_HEADS = 8
A_Q_LORA = 384
A_KV_LORA = 256
A_NOPE = 128
A_ROPE = 64
A_V = 128
A_QK_PAD = 256
A_LAT_PAD = 384
ROPE_THETA = 10000.0

B_HEADS = 16
B_HEAD_DIM = D_MODEL // B_HEADS
B_LEFT_CHUNKS = 8
B_WIN = B_LEFT_CHUNKS * CHUNK
B_REL_CLIP = 128

C_HEADS = 16
C_HEAD_DIM = D_MODEL // C_HEADS

D_FF = 2816
CONV_W = 3

ALPHA = (2.0 * DEPTH) ** 0.25
LN_EPS = 1e-5
RMS_EPS = 1e-6
NEG_INF = -1e30

LANES = 128
SUBLANES = 8
VMEM_LIMIT_BYTES = 48 * 2 ** 20


def _params(*sem):
    return pltpu.CompilerParams(dimension_semantics=sem, vmem_limit_bytes=VMEM_LIMIT_BYTES)


def _row_tile(m, tm):
    while m % tm:
        tm //= 2
    assert tm % SUBLANES == 0, (m, tm)
    return tm


def _mm_kernel(x_ref, w_ref, *o_refs):
    acc = jnp.dot(x_ref[...], w_ref[...], preferred_element_type=F32)
    for o_ref in o_refs:
        o_ref[...] = acc.astype(o_ref.dtype)


def _mm(x, w, out_dtypes, tm=512, tn=1024):
    m, k = x.shape
    n = w.shape[1]
    tm, tn = _row_tile(m, tm), min(tn, n)
    return pl.pallas_call(
        _mm_kernel,
        grid=(m // tm, n // tn),
        in_specs=[pl.BlockSpec((tm, k), lambda i, j: (i, 0)),
                  pl.BlockSpec((k, tn), lambda i, j: (0, j))],
        out_specs=[pl.BlockSpec((tm, tn), lambda i, j: (i, j)) for _ in out_dtypes],
        out_shape=[jax.ShapeDtypeStruct((m, n), d) for d in out_dtypes],
        compiler_params=_params("parallel", "parallel"),
        name="mm",
    )(x, w)


def _proj_t_kernel(wt_ref, x_ref, o_ref):
    o_ref[...] = lax.dot_general(wt_ref[...], x_ref[...], (((1,), (1,)), ((), ())),
                                 preferred_element_type=F32).astype(o_ref.dtype)


def _proj_t(x, w, t_blk, tm=512):
    m, k = x.shape
    n = w.shape[1]
    tm = min(tm, t_blk)
    per = t_blk // tm
    return pl.pallas_call(
        _proj_t_kernel,
        grid=(m // tm,),
        in_specs=[pl.BlockSpec((n, k), lambda i: (0, 0)), pl.BlockSpec((tm, k), lambda i: (i, 0))],
        out_specs=pl.BlockSpec((None, n, tm), lambda i: (i // per, 0, i % per)),
        out_shape=jax.ShapeDtypeStruct((m // t_blk, n, t_blk), BF16),
        compiler_params=_params("parallel"),
        name="proj_t",
    )(w.T.astype(BF16), x)


def _cast_kernel(x_ref, o_ref):
    o_ref[...] = x_ref[...].astype(o_ref.dtype)


def _to_bf16(w, tr=256):
    n_l, r, c = w.shape
    tr = _row_tile(r, tr)
    spec = pl.BlockSpec((None, tr, c), lambda l, i: (l, i, 0))
    return pl.pallas_call(
        _cast_kernel,
        grid=(n_l, r // tr),
        in_specs=[spec],
        out_specs=spec,
        out_shape=jax.ShapeDtypeStruct(w.shape, BF16),
        compiler_params=_params("parallel", "parallel"),
        name="to_bf16",
    )(w)


def _mm_res_ln_kernel(a_ref, w_ref, x_ref, g_ref, b_ref, of_ref, ob_ref):
    y = ALPHA * x_ref[...] + jnp.dot(a_ref[...], w_ref[...], preferred_element_type=F32)
    mu = jnp.mean(y, axis=-1, keepdims=True)
    d = y - mu
    var = jnp.mean(d * d, axis=-1, keepdims=True)
    out = d * lax.rsqrt(var + LN_EPS) * g_ref[...] + b_ref[...]
    of_ref[...] = out
    ob_ref[...] = out.astype(BF16)


def _mm_res_ln(a, w, x, g, b, tm=512):
    m, k = a.shape
    n = w.shape[1]
    tm = _row_tile(m, tm)
    row = lambda i: (i, 0)
    fixed = lambda i: (0, 0)
    return pl.pallas_call(
        _mm_res_ln_kernel,
        grid=(m // tm,),
        in_specs=[pl.BlockSpec((tm, k), row), pl.BlockSpec((k, n), fixed),
                  pl.BlockSpec((tm, n), row), pl.BlockSpec((1, n), fixed),
                  pl.BlockSpec((1, n), fixed)],
        out_specs=[pl.BlockSpec((tm, n), row), pl.BlockSpec((tm, n), row)],
        out_shape=[jax.ShapeDtypeStruct((m, n), F32), jax.ShapeDtypeStruct((m, n), BF16)],
        compiler_params=_params("parallel"),
        name="mm_res_ln",
    )(a, w, x, g.reshape(1, n), b.reshape(1, n))


_W1_CQ = (0, A_Q_LORA)
_W1_CKV = (A_Q_LORA, A_Q_LORA + A_KV_LORA)
_W1_KR = (_W1_CKV[1], _W1_CKV[1] + A_ROPE)
_W1_KRS = (_W1_CKV[1] + LANES, _W1_CKV[1] + LANES + A_ROPE)
_W1_COLS = _W1_CKV[1] + 2 * LANES


def _rms(v, g):
    return v * lax.rsqrt(jnp.mean(v * v, axis=-1, keepdims=True) + RMS_EPS) * g


def _mla_proj_kernel(x_ref, w1_ref, gq_ref, gkv_ref, wq_ref, cq_ref, sq_ref, ck_ref, sk_ref,
                     q_ref, lat_ref, ckv_ref, kpe_ref):
    y = jnp.dot(x_ref[...], w1_ref[...], preferred_element_type=F32)
    cq = _rms(y[:, _W1_CQ[0]:_W1_CQ[1]], gq_ref[...]).astype(BF16)
    ckv = _rms(y[:, _W1_CKV[0]:_W1_CKV[1]], gkv_ref[...])
    kpe = y[:, _W1_KR[0]:_W1_KR[1]] * ck_ref[...] + y[:, _W1_KRS[0]:_W1_KRS[1]] * sk_ref[...]
    ckv_ref[...] = ckv
    kpe_ref[...] = kpe
    lat_ref[:, 0:A_KV_LORA] = ckv.astype(BF16)
    lat_ref[:, A_KV_LORA:A_KV_LORA + A_ROPE] = kpe.astype(BF16)
    lat_ref[:, A_KV_LORA + A_ROPE:] = jnp.zeros(
        (lat_ref.shape[0], A_LAT_PAD - A_KV_LORA - A_ROPE), BF16)
    sw0 = A_HEADS * A_QK_PAD
    for h in range(A_HEADS):
        lo, hi = h * A_QK_PAD, (h + 1) * A_QK_PAD
        qp = jnp.dot(cq, wq_ref[:, lo:hi], preferred_element_type=F32)
        qs = jnp.dot(cq, wq_ref[:, sw0 + lo:sw0 + hi], preferred_element_type=F32)
        q_ref[:, lo:hi] = (qp * cq_ref[...] + qs * sq_ref[...]).astype(BF16)


def _mla_proj(xb, w1, gq, gkv, wq, cosq, sinq, cosk, sink, tm=512):
    m = xb.shape[0]
    tm = _row_tile(m, tm)
    row = lambda i: (i, 0)
    fixed = lambda i: (0, 0)
    nq = A_HEADS * A_QK_PAD
    return pl.pallas_call(
        _mla_proj_kernel,
        grid=(m // tm,),
        in_specs=[pl.BlockSpec((tm, D_MODEL), row), pl.BlockSpec(w1.shape, fixed),
                  pl.BlockSpec((1, A_Q_LORA), fixed), pl.BlockSpec((1, A_KV_LORA), fixed),
                  pl.BlockSpec(wq.shape, fixed),
                  pl.BlockSpec((tm, A_QK_PAD), row), pl.BlockSpec((tm, A_QK_PAD), row),
                  pl.BlockSpec((tm, A_ROPE), row), pl.BlockSpec((tm, A_ROPE), row)],
        out_specs=[pl.BlockSpec((tm, nq), row), pl.BlockSpec((tm, A_LAT_PAD), row),
                   pl.BlockSpec((tm, A_KV_LORA), row), pl.BlockSpec((tm, A_ROPE), row)],
        out_shape=[jax.ShapeDtypeStruct((m, nq), BF16), jax.ShapeDtypeStruct((m, A_LAT_PAD), BF16),
                   jax.ShapeDtypeStruct((m, A_KV_LORA), F32), jax.ShapeDtypeStruct((m, A_ROPE), F32)],
        compiler_params=_params("parallel"),
        name="mla_proj",
    )(xb, w1, gq.reshape(1, -1), gkv.reshape(1, -1), wq, cosq, sinq, cosk, sink)


def _flash_kernel(*refs, nh, dq, dv, tq, tk, nk, q_off, chunk_causal, forget):
    if forget:
        q_ref, k_ref, v_ref, fq_ref, fk_ref, o_ref, m_s, l_s, acc_s = refs
    else:
        q_ref, k_ref, v_ref, o_ref, m_s, l_s, acc_s = refs
    iq = pl.program_id(2)
    ik = pl.program_id(3)
    q_lo = q_off + iq * tq
    q_hi = q_lo + tq - 1
    if chunk_causal:
        vis_lo = (q_lo // CHUNK) * CHUNK + CHUNK - 1
        vis_hi = (q_hi // CHUNK) * CHUNK + CHUNK - 1
    else:
        vis_lo, vis_hi = q_lo, q_hi
    k_lo = ik * tk
    needed = k_lo <= vis_hi
    unmasked = k_lo + tk - 1 <= vis_lo

    @pl.when(ik == 0)
    def _init():
        m_s[...] = jnp.full(m_s.shape, -jnp.inf, F32)
        l_s[...] = jnp.zeros(l_s.shape, F32)
        acc_s[...] = jnp.zeros(acc_s.shape, F32)

    def step(masked):
        if masked:
            q_pos = q_lo + lax.broadcasted_iota(jnp.int32, (tq, tk), 0)
            k_pos = k_lo + lax.broadcasted_iota(jnp.int32, (tq, tk), 1)
            if chunk_causal:
                shift = CHUNK.bit_length() - 1
                mask = jnp.right_shift(k_pos, shift) <= jnp.right_shift(q_pos, shift)
            else:
                mask = k_pos <= q_pos
        for h in range(nh):
            q = q_ref[:, h * dq:(h + 1) * dq]
            k = k_ref[:, h * dq:(h + 1) * dq]
            v = v_ref[:, h * dv:(h + 1) * dv]
            s = lax.dot_general(q, k, (((1,), (1,)), ((), ())), preferred_element_type=F32)
            if forget:
                s = s + fq_ref[:, h:h + 1] - fk_ref[h:h + 1, :]
            if masked:
                s = jnp.where(mask, s, NEG_INF)
            m_prev = m_s[h]
            m_new = jnp.maximum(m_prev, jnp.max(s, axis=-1, keepdims=True))
            alpha = jnp.exp(m_prev - m_new)
            p = jnp.exp(s - m_new)
            l_s[h] = alpha * l_s[h] + jnp.sum(p, axis=-1, keepdims=True)
            acc_s[h] = alpha * acc_s[h] + jnp.dot(p.astype(BF16), v, preferred_element_type=F32)
            m_s[h] = m_new

    @pl.when(needed & unmasked)
    def _plain():
        step(False)

    @pl.when(needed & jnp.logical_not(unmasked))
    def _masked():
        step(True)

    @pl.when(ik == nk - 1)
    def _fin():
        for h in range(nh):
            o_ref[:, h * dv:(h + 1) * dv] = (acc_s[h] / l_s[h]).astype(o_ref.dtype)


def _flash(q, k, v, *, nh, dq, dv, n_hblk, k_col0, v_col0, tq, tk, q_off, chunk_causal,
           fq=None, fk=None):
    b, t_q = q.shape[0], q.shape[1]
    t_k = k.shape[1]
    tq = tq if t_q % tq == 0 else t_q
    tk = tk if t_k % tk == 0 else t_k
    nq, nk = t_q // tq, t_k // tk
    forget = fq is not None

    def last_blk(i):
        q_hi = q_off + (i + 1) * tq - 1
        vis = (q_hi // CHUNK) * CHUNK + CHUNK - 1 if chunk_causal else q_hi
        return jnp.minimum(vis // tk, nk - 1)

    in_specs = [
        pl.BlockSpec((None, tq, nh * dq), lambda bi, h, i, j: (bi, i, h)),
        pl.BlockSpec((None, tk, nh * dq), lambda bi, h, i, j: (bi, jnp.minimum(j, last_blk(i)), k_col0 + h)),
        pl.BlockSpec((None, tk, nh * dv), lambda bi, h, i, j: (bi, jnp.minimum(j, last_blk(i)), v_col0 + h)),
    ]
    args = [q, k, v]
    if forget:
        in_specs += [
            pl.BlockSpec((None, None, tq, LANES), lambda bi, h, i, j: (bi, h, i, 0)),
            pl.BlockSpec((None, None, fk.shape[2], tk),
                         lambda bi, h, i, j: (bi, h, 0, jnp.minimum(j, last_blk(i)))),
        ]
        args += [fq, fk]
    kern = functools.partial(_flash_kernel, nh=nh, dq=dq, dv=dv, tq=tq, tk=tk, nk=nk, q_off=q_off,
                             chunk_causal=chunk_causal, forget=forget)
    return pl.pallas_call(
        kern,
        grid=(b, n_hblk, nq, nk),
        in_specs=in_specs,
        out_specs=pl.BlockSpec((None, tq, nh * dv), lambda bi, h, i, j: (bi, i, h)),
        out_shape=jax.ShapeDtypeStruct((b, t_q, n_hblk * nh * dv), BF16),
        scratch_shapes=[pltpu.VMEM((nh, tq, 1), F32), pltpu.VMEM((nh, tq, 1), F32),
                        pltpu.VMEM((nh, tq, dv), F32)],
        compiler_params=_params("parallel", "parallel", "parallel", "arbitrary"),
        name="flash_fox" if forget else "flash_mla",
    )(*args)


FLASH_T = 1024
FLASH_QC = 256
FLASH_KC = 256
FLASH_AHEAD = 8
A_HEADS_PER_STEP = 2
LOG2E = math.log2(math.e)


def _flash_t_kernel(q_ref, k_ref, vt_ref, o_ref, m_s, l_s, acc_s, *, nh, dq, dv, t, chunk_causal):
    iq = pl.program_id(2)
    m_s[...] = jnp.full(m_s.shape, -jnp.inf, F32)
    l_s[...] = jnp.zeros(l_s.shape, F32)
    acc_s[...] = jnp.zeros(acc_s.shape, F32)

    def block(j, masked):
        row0 = pl.multiple_of(j * t, t)
        qc, kc = FLASH_QC, FLASH_KC

        def n_keys(g, c):
            return min(kc, (c + 1) * qc - g * kc) if masked else kc

        chains = [(g, h, c) for g in range(t // kc) for h in range(nh) for c in range(t // qc)
                  if n_keys(g, c) > 0]

        def qk(g, h, c):
            return lax.dot_general(k_ref[pl.ds(row0 + g * kc, n_keys(g, c)), h * dq:(h + 1) * dq],
                                   q_ref[c * qc:(c + 1) * qc, h * dq:(h + 1) * dq],
                                   (((1,), (1,)), ((), ())), preferred_element_type=F32)

        def softmax_pv(g, h, c, s):
            cols = slice(c * qc, (c + 1) * qc)
            nk = n_keys(g, c)
            if masked and g * kc + nk > c * qc:
                k_pos = lax.broadcasted_iota(jnp.int32, (nk, qc), 0) + g * kc
                q_pos = lax.broadcasted_iota(jnp.int32, (nk, qc), 1) + c * qc
                if chunk_causal:
                    shift = CHUNK.bit_length() - 1
                    mask = jnp.right_shift(k_pos, shift) <= jnp.right_shift(q_pos, shift)
                else:
                    mask = k_pos <= q_pos
                s = jnp.where(mask, s, NEG_INF)
            m_prev = m_s[h, :, cols]
            m_new = jnp.maximum(m_prev, jnp.max(s, axis=0, keepdims=True))
            alpha = jnp.exp2(m_prev - m_new)
            p = jnp.exp2(s - m_new)
            l_s[h, :, cols] = alpha * l_s[h, :, cols] + jnp.sum(p, axis=0, keepdims=True)
            acc_s[h, :, cols] = alpha * acc_s[h, :, cols] + jnp.dot(
                vt_ref[j, h * dv:(h + 1) * dv, g * kc:g * kc + nk], p.astype(BF16),
                preferred_element_type=F32)
            m_s[h, :, cols] = m_new

        pending = [qk(*chain) for chain in chains[:FLASH_AHEAD]]
        for i, chain in enumerate(chains):
            if i + FLASH_AHEAD < len(chains):
                pending.append(qk(*chains[i + FLASH_AHEAD]))
            softmax_pv(*chain, pending.pop(0))

    def full_block(j, carry):
        block(j, False)
        return carry

    lax.fori_loop(0, iq, full_block, 0)
    block(iq, True)
    out = jnp.concatenate([acc_s[h] / l_s[h] for h in range(nh)], axis=0)
    o_ref[...] = out.T.astype(o_ref.dtype)


def _flash_t(q, k, k_col0, vt, *, nh, dq, dv, n_hblk, chunk_causal):
    b, t_all = q.shape[0], q.shape[1]
    t = FLASH_T
    nblk = t_all // t
    kern = functools.partial(_flash_t_kernel, nh=nh, dq=dq, dv=dv, t=t, chunk_causal=chunk_causal)
    return pl.pallas_call(
        kern,
        grid=(b, n_hblk, nblk),
        in_specs=[pl.BlockSpec((None, t, nh * dq), lambda bi, h, i: (bi, i, h)),
                  pl.BlockSpec((None, t_all, nh * dq), lambda bi, h, i: (bi, 0, k_col0 + h),
                               pipeline_mode=pl.Buffered(1)),
                  pl.BlockSpec((None, nblk, nh * dv, t), lambda bi, h, i: (bi, 0, h, 0),
                               pipeline_mode=pl.Buffered(1))],
        out_specs=pl.BlockSpec((None, t, nh * dv), lambda bi, h, i: (bi, i, h)),
        out_shape=jax.ShapeDtypeStruct((b, t_all, n_hblk * nh * dv), BF16),
        scratch_shapes=[pltpu.VMEM((nh, 1, t), F32), pltpu.VMEM((nh, 1, t), F32),
                        pltpu.VMEM((nh, dv, t), F32)],
        compiler_params=_params("parallel", "parallel", "parallel"),
        name="flash_t_chunk" if chunk_causal else "flash_t_frame",
    )(q, k, vt)


B_HEADS_PER_STEP = LANES // B_HEAD_DIM


def _band_step_kernel(q_ref, k_ref, v_ref, bias_ref, o_ref):
    dh = B_HEAD_DIM
    for h in range(B_HEADS):
        cols = slice(h * dh, (h + 1) * dh)
        s = lax.dot_general(q_ref[:, cols], k_ref[:, cols], (((1,), (1,)), ((), ())),
                            preferred_element_type=F32) + bias_ref[h]
        p = jnp.exp(s - jnp.max(s, axis=-1, keepdims=True))
        l = jnp.sum(p, axis=-1, keepdims=True)
        o = jnp.dot(p.astype(BF16), v_ref[:, cols], preferred_element_type=F32)
        o_ref[:, cols] = (o / l).astype(o_ref.dtype)


def _band_bias(rel_bias, q_pos, k_pos):
    nq, nk = len(q_pos), len(k_pos)
    assert (np.diff(q_pos) == 1).all() and (np.diff(k_pos) == 1).all()
    m = np.arange(nq + nk - 1)
    u = rel_bias[:, np.clip(q_pos[0] - k_pos[0] + nq - 1 - m, -B_REL_CLIP, B_REL_CLIP) + B_REL_CLIP]
    period = nq + nk
    w = jnp.concatenate([u[:, nq - 1:], jnp.zeros((u.shape[0], 1), u.dtype), u[:, :nq - 1]], axis=1)
    skew = jnp.tile(w, (1, nq))[:, :nq * (period - 1)].reshape(-1, nq, period - 1)[:, :, :nk]
    qc = q_pos[:, None] // CHUNK
    kc = k_pos[None, :] // CHUNK
    mask = (kc <= qc) & (kc >= qc - B_LEFT_CHUNKS) & (k_pos[None, :] >= 0)
    return jnp.where(jnp.asarray(mask)[None], skew, NEG_INF).astype(F32)


def _band_step(q, kv, bias):
    b, t, _ = q.shape
    t_k = kv.shape[1]
    return pl.pallas_call(
        _band_step_kernel,
        grid=(b,),
        in_specs=[pl.BlockSpec((None, t, D_MODEL), lambda bi: (bi, 0, 0)),
                  pl.BlockSpec((None, t_k, D_MODEL), lambda bi: (bi, 0, 0)),
                  pl.BlockSpec((None, t_k, D_MODEL), lambda bi: (bi, 0, 1)),
                  pl.BlockSpec(bias.shape, lambda bi: (0, 0, 0))],
        out_specs=pl.BlockSpec((None, t, D_MODEL), lambda bi: (bi, 0, 0)),
        out_shape=jax.ShapeDtypeStruct((b, t, D_MODEL), BF16),
        compiler_params=_params("parallel"),
        name="band_step",
    )(q, kv, kv, bias)


BAND_T = 4 * CHUNK


def _band_t_kernel(*refs, nkb, t):
    q_ref = refs[0]
    k_refs = refs[1:1 + nkb]
    vt_refs = refs[1 + nkb:1 + 2 * nkb]
    bias_ref, o_ref = refs[1 + 2 * nkb], refs[2 + 2 * nkb]
    iq = pl.program_id(1)
    dh = B_HEAD_DIM

    def qk(h):
        cols = slice(h * dh, (h + 1) * dh)
        parts = []
        for j in range(nkb):
            s = lax.dot_general(k_refs[j][:, cols], q_ref[:, cols], (((1,), (1,)), ((), ())),
                                preferred_element_type=F32)
            if j < nkb - 1:
                s = jnp.where(iq >= nkb - 1 - j, s, NEG_INF)
            parts.append(s)
        return jnp.concatenate(parts, axis=0) + bias_ref[h]

    def softmax_pv(h, s):
        p = jnp.exp2(s - jnp.max(s, axis=0, keepdims=True))
        l = jnp.sum(p, axis=0, keepdims=True)
        pb = p.astype(BF16)
        o = functools.reduce(lambda a, c: a + c, [
            jnp.dot(vt_refs[j][h * dh:(h + 1) * dh, :], pb[j * t:(j + 1) * t], preferred_element_type=F32)
            for j in range(nkb)])
        return o / l

    heads = list(range(B_HEADS))
    pending = [qk(h) for h in heads[:FLASH_AHEAD]]
    outs = []
    for h in heads:
        if h + FLASH_AHEAD < B_HEADS:
            pending.append(qk(h + FLASH_AHEAD))
        outs.append(softmax_pv(h, pending.pop(0)))
        if len(outs) == B_HEADS_PER_STEP:
            c0 = (h + 1 - B_HEADS_PER_STEP) * dh
            o_ref[:, c0:c0 + LANES] = jnp.concatenate(outs, axis=0).T.astype(o_ref.dtype)
            outs = []


def _band_t(q, kv, vt, bias_t, *, nkb):
    b, t_all = q.shape[0], q.shape[1]
    t = BAND_T

    def back(j):
        return nkb - 1 - j

    k_specs = [pl.BlockSpec((None, t, D_MODEL), lambda bi, i, j=j: (bi, jnp.maximum(i - back(j), 0), 0))
               for j in range(nkb)]
    vt_specs = [pl.BlockSpec((None, None, D_MODEL, t),
                             lambda bi, i, j=j: (bi, jnp.maximum(i - back(j), 0), 0, 0))
                for j in range(nkb)]
    return pl.pallas_call(
        functools.partial(_band_t_kernel, nkb=nkb, t=t),
        grid=(b, t_all // t),
        in_specs=([pl.BlockSpec((None, t, D_MODEL), lambda bi, i: (bi, i, 0))] + k_specs + vt_specs
                  + [pl.BlockSpec(bias_t.shape, lambda bi, i: (0, 0, 0), pipeline_mode=pl.Buffered(1))]),
        out_specs=pl.BlockSpec((None, t, D_MODEL), lambda bi, i: (bi, i, 0)),
        out_shape=jax.ShapeDtypeStruct((b, t_all, D_MODEL), BF16),
        compiler_params=_params("parallel", "parallel"),
        name="band_t",
    )(q, *([kv] * nkb), *([vt] * nkb), bias_t)


def _logf_kernel(x_ref, w_ref, b_ref, o_ref):
    z = jnp.dot(x_ref[...], w_ref[...], preferred_element_type=F32) + b_ref[...]
    o_ref[...] = -(jnp.maximum(-z, 0.0) + jnp.log1p(jnp.exp(-jnp.abs(z))))


def _logf(xb, w_pad, b_pad, tm=512):
    m = xb.shape[0]
    tm = _row_tile(m, tm)
    return pl.pallas_call(
        _logf_kernel,
        grid=(m // tm,),
        in_specs=[pl.BlockSpec((tm, D_MODEL), lambda i: (i, 0)),
                  pl.BlockSpec((D_MODEL, LANES), lambda i: (0, 0)),
                  pl.BlockSpec((1, LANES), lambda i: (0, 0))],
        out_specs=pl.BlockSpec((tm, LANES), lambda i: (i, 0)),
        out_shape=jax.ShapeDtypeStruct((m, LANES), F32),
        compiler_params=_params("parallel"),
        name="logf",
    )(xb, w_pad, b_pad)


def _split3(x):
    hi = x.astype(BF16)
    r = x - hi.astype(F32)
    mid = r.astype(BF16)
    lo = (r - mid.astype(F32)).astype(BF16)
    return hi, mid, lo


def _cumsum_kernel(x_ref, o_ref, hi_ref, mid_ref, lo_ref, carry, *, tc):
    @pl.when(pl.program_id(1) == 0)
    def _():
        carry[...] = jnp.zeros(carry.shape, F32)

    tri = (lax.broadcasted_iota(jnp.int32, (tc, tc), 0)
           >= lax.broadcasted_iota(jnp.int32, (tc, tc), 1)).astype(BF16)
    c = functools.reduce(lambda a, b: a + b, [jnp.dot(tri, piece, preferred_element_type=F32)
                                              for piece in _split3(x_ref[...])])
    out = c + carry[0:1, :]
    o_ref[...] = out
    hi_ref[...], mid_ref[...], lo_ref[...] = _split3(out * LOG2E)
    carry[...] = jnp.broadcast_to(out[tc - 1:tc, :], carry.shape)


def _cumsum(x, tc=256):
    b, t, _ = x.shape
    spec = pl.BlockSpec((None, tc, LANES), lambda bi, i: (bi, i, 0))
    piece = jax.ShapeDtypeStruct(x.shape, BF16)
    return pl.pallas_call(
        functools.partial(_cumsum_kernel, tc=tc),
        grid=(b, t // tc),
        in_specs=[spec],
        out_specs=[spec] * 4,
        out_shape=[jax.ShapeDtypeStruct(x.shape, F32), piece, piece, piece],
        scratch_shapes=[pltpu.VMEM((SUBLANES, LANES), F32)],
        compiler_params=_params("parallel", "arbitrary"),
        name="cumsum",
    )(x)


FFN_CHUNK = 256
CONV_ROWS = SUBLANES


def _ffn_up_kernel(x_ref, w_ref, c_ref, p_ref, act_ref, s_ref, halo, *, tm):
    @pl.when(pl.program_id(1) == 0)
    def _():
        halo[...] = p_ref[...]

    x = x_ref[...]
    n_chunks = D_FF // FFN_CHUNK
    groups = tm // CONV_ROWS
    row = lax.broadcasted_iota(jnp.int32, (groups, CONV_ROWS, FFN_CHUNK), 1)

    def up(c, half):
        col = half * D_FF + c * FFN_CHUNK
        return jnp.dot(x, w_ref[:, col:col + FFN_CHUNK], preferred_element_type=F32)

    def conv(h, c, half):
        cols = slice(half * D_FF + c * FFN_CHUNK, half * D_FF + (c + 1) * FFN_CHUNK)
        ext = jnp.concatenate([halo[:, cols], h], axis=0).reshape(groups + 1, CONV_ROWS, FFN_CHUNK)
        hc = c_ref[CONV_W:CONV_W + 1, cols] + c_ref[CONV_W - 1:CONV_W, cols] * h
        for s in range(1, CONV_W):
            rot = pltpu.roll(ext, s, axis=1)
            shifted = jnp.where(row < s, rot[:groups], rot[1:]).reshape(tm, FFN_CHUNK)
            hc = hc + c_ref[CONV_W - 1 - s:CONV_W - s, cols] * shifted
        tail = h[tm - CONV_ROWS:tm]
        halo[:, cols] = tail
        s_ref[:, cols] = tail
        return hc

    pending = [(up(0, 0), up(0, 1))]
    for c in range(n_chunks):
        if c + 1 < n_chunks:
            pending.append((up(c + 1, 0), up(c + 1, 1)))
        ha, hg = pending.pop(0)
        a = conv(ha, c, 0)
        g = conv(hg, c, 1)
        act_ref[:, c * FFN_CHUNK:(c + 1) * FFN_CHUNK] = (g * jax.nn.sigmoid(g) * a).astype(BF16)


def _ffn_up(xb, w_up, conv_tab, past, tm=256):
    b, t, _ = xb.shape
    tm = _row_tile(t, tm)
    fixed = lambda bi, ti: (0, 0)
    return pl.pallas_call(
        functools.partial(_ffn_up_kernel, tm=tm),
        grid=(b, t // tm),
        in_specs=[pl.BlockSpec((None, tm, D_MODEL), lambda bi, ti: (bi, ti, 0)),
                  pl.BlockSpec((D_MODEL, 2 * D_FF), fixed),
                  pl.BlockSpec((CONV_ROWS, 2 * D_FF), fixed),
                  pl.BlockSpec((None, CONV_ROWS, 2 * D_FF), lambda bi, ti: (bi, 0, 0))],
        out_specs=[pl.BlockSpec((None, tm, D_FF), lambda bi, ti: (bi, ti, 0)),
                   pl.BlockSpec((None, CONV_ROWS, 2 * D_FF), lambda bi, ti: (bi, 0, 0))],
        out_shape=[jax.ShapeDtypeStruct((b, t, D_FF), BF16),
                   jax.ShapeDtypeStruct((b, CONV_ROWS, 2 * D_FF), F32)],
        scratch_shapes=[pltpu.VMEM((CONV_ROWS, 2 * D_FF), F32)],
        compiler_params=_params("parallel", "arbitrary"),
        name="ffn_up",
    )(xb, w_up, conv_tab, past)


def _rope_tables(pos, batch, q_scale):
    half = A_ROPE // 2
    inv_freq = ROPE_THETA ** (-jnp.arange(half, dtype=F32) / half)
    ang = pos.astype(F32)[:, None] * inv_freq
    cos, sin = jnp.cos(ang), jnp.sin(ang)
    cosk = jnp.concatenate([cos, cos], axis=-1)
    sink = jnp.concatenate([sin, sin], axis=-1)
    t = pos.shape[0]
    pad = jnp.zeros((t, A_QK_PAD - A_NOPE - A_ROPE), F32)
    cosq = q_scale * jnp.concatenate([jnp.ones((t, A_NOPE), F32), cosk, pad], axis=-1)
    sinq = q_scale * jnp.concatenate([jnp.zeros((t, A_NOPE), F32), sink, pad], axis=-1)
    return tuple(jnp.tile(a, (batch, 1)) for a in (cosq, sinq, cosk, sink))


def _swap_halves(w):
    half = w.shape[-1] // 2
    return jnp.concatenate([-w[..., half:], w[..., :half]], axis=-1)


def _mla_weights(w_dq, w_dkv, w_kr, w_uq, w_uk, w_uv):
    zc = jnp.zeros((D_MODEL, LANES - A_ROPE), F32)
    w1 = jnp.concatenate([w_dq, w_dkv, w_kr, zc, _swap_halves(w_kr), zc], axis=1).astype(BF16)
    wq = w_uq.reshape(A_Q_LORA, A_HEADS, A_NOPE + A_ROPE)
    nope, rope = wq[..., :A_NOPE], wq[..., A_NOPE:]
    zpad = jnp.zeros((A_Q_LORA, A_HEADS, A_QK_PAD - A_NOPE - A_ROPE), F32)
    w_cat = jnp.concatenate([nope, rope, zpad], axis=-1).reshape(A_Q_LORA, -1)
    w_sw = jnp.concatenate([jnp.zeros_like(nope), _swap_halves(rope), zpad], axis=-1).reshape(A_Q_LORA, -1)
    wq2 = jnp.concatenate([w_cat, w_sw], axis=1).astype(BF16)
    wk = jnp.zeros((A_LAT_PAD, A_HEADS, A_QK_PAD), F32)
    wk = wk.at[:A_KV_LORA, :, :A_NOPE].set(w_uk)
    eye = jnp.broadcast_to(jnp.eye(A_ROPE, dtype=F32)[:, None, :], (A_ROPE, A_HEADS, A_ROPE))
    wk = wk.at[A_KV_LORA:A_KV_LORA + A_ROPE, :, A_NOPE:A_NOPE + A_ROPE].set(eye)
    wv = jnp.zeros((A_LAT_PAD, A_HEADS * A_V), F32).at[:A_KV_LORA].set(w_uv.reshape(A_KV_LORA, -1))
    wkv = jnp.concatenate([wk.reshape(A_LAT_PAD, -1), wv], axis=1).astype(BF16)
    return w1, wq2, wkv


def _mla_mixer(xb, b, t, pos, ckv_past, kpe_past, w, i):
    w1, wq2, wkv = _mla_weights(w['a_w_dq'][i], w['a_w_dkv'][i], w['a_w_kr'][i], w['a_w_uq'][i],
                                w['a_w_uk'][i], w['a_w_uv'][i])
    resident = ckv_past is None and t % FLASH_T == 0
    q_scale = (A_NOPE + A_ROPE) ** -0.5 * (LOG2E if resident else 1.0)
    q, lat, ckv, kpe = _mla_proj(xb, w1, w['a_g_q'][i], w['a_g_kv'][i], wq2,
                                 *_rope_tables(pos, b, q_scale))
    lat = lat.reshape(b, t, A_LAT_PAD)
    q_off = 0
    if ckv_past is not None:
        p_len = ckv_past.shape[1]
        past = jnp.concatenate(
            [ckv_past, kpe_past, jnp.zeros((b, p_len, A_LAT_PAD - A_KV_LORA - A_ROPE), F32)], axis=-1)
        lat = jnp.concatenate([past.astype(BF16), lat], axis=1)
        q_off = p_len
    t_k = lat.shape[1]
    lat2 = lat.reshape(b * t_k, A_LAT_PAD)
    n_k = A_HEADS * A_QK_PAD
    if resident:
        (k_cat,) = _mm(lat2, wkv[:, :n_k], [BF16])
        vt = _proj_t(lat2, wkv[:, n_k:], FLASH_T).reshape(b, t_k // FLASH_T, A_HEADS * A_V, FLASH_T)
        o = _flash_t(q.reshape(b, t, -1), k_cat.reshape(b, t_k, n_k), 0, vt, nh=A_HEADS_PER_STEP,
                     dq=A_QK_PAD, dv=A_V, n_hblk=A_HEADS // A_HEADS_PER_STEP, chunk_causal=True)
    else:
        (kv,) = _mm(lat2, wkv, [BF16])
        kv = kv.reshape(b, t_k, -1)
        o = _flash(q.reshape(b, t, -1), kv, kv, nh=A_HEADS, dq=A_QK_PAD, dv=A_V, n_hblk=1,
                   k_col0=0, v_col0=A_QK_PAD // A_V, tq=512, tk=512, q_off=q_off, chunk_causal=True)
    return o.reshape(b * t, A_HEADS * A_V), ckv.reshape(b, t, -1), kpe.reshape(b, t, -1)


def _qkv(xb, w_qkv, q_scale):
    wq = (w_qkv[:, :D_MODEL] * q_scale).astype(BF16)
    (q,) = _mm(xb, wq, [BF16])
    kv32, kvb = _mm(xb, w_qkv[:, D_MODEL:].astype(BF16), [F32, BF16])
    return q, kv32, kvb


def _band_mixer(xb, b, t, pos0, k_past, v_past, w, i):
    prompt = k_past is None
    assert not prompt or t % BAND_T == 0
    q, kv32, kvb = _qkv(xb, w['b_w_qkv'][i], B_HEAD_DIM ** -0.5 * (LOG2E if prompt else 1.0))
    q = q.reshape(b, t, D_MODEL)
    k32 = kv32[:, :D_MODEL].reshape(b, t, B_HEADS, B_HEAD_DIM)
    v32 = kv32[:, D_MODEL:].reshape(b, t, B_HEADS, B_HEAD_DIM)
    if prompt:
        nkb = B_WIN // BAND_T + 1
        bias = _band_bias(w['b_rel_bias'][i], B_WIN + np.arange(BAND_T), np.arange(B_WIN + BAND_T))
        kvb = kvb.reshape(b, t, 2 * D_MODEL)
        vt = _proj_t(xb, w['b_w_qkv'][i][:, 2 * D_MODEL:], BAND_T).reshape(b, t // BAND_T, D_MODEL, BAND_T)
        o = _band_t(q, kvb, vt, LOG2E * bias.transpose(0, 2, 1), nkb=nkb)
        keep = min(B_WIN, t)
        k_new, v_new = k32[:, t - keep:], v32[:, t - keep:]
    else:
        p_len = k_past.shape[1]
        kvb = kvb.reshape(b, t, 2 * D_MODEL)
        past = jnp.concatenate([k_past.reshape(b, p_len, D_MODEL), v_past.reshape(b, p_len, D_MODEL)],
                               axis=-1).astype(BF16)
        kv_all = jnp.concatenate([past, kvb], axis=1)
        q_pos = pos0 + np.arange(t)
        k_pos = np.concatenate([np.arange(pos0 - p_len, pos0), q_pos])
        bias = _band_bias(w['b_rel_bias'][i], q_pos, k_pos)
        o = _band_step(q, kv_all, bias)
        k_new, v_new = k32, v32
    return o.reshape(b * t, D_MODEL), k_new, v_new


C_AUG = LANES
N_PIECES = 3


def _aug_kernel(x_ref, w_ref, hi_ref, mid_ref, lo_ref, p_ref, b_ref, o_ref):
    acc = jnp.dot(x_ref[...], w_ref[...], preferred_element_type=F32) + b_ref[...]
    for piece, f_ref in enumerate((hi_ref, mid_ref, lo_ref)):
        acc = acc + jnp.dot(f_ref[...], p_ref[piece], preferred_element_type=F32)
    o_ref[...] = acc.astype(BF16)


def _aug_tables():
    n = C_HEADS * C_AUG
    place = np.zeros((N_PIECES, LANES, 2 * n), np.float32)
    ones = np.zeros((1, 2 * n), np.float32)
    for h in range(C_HEADS):
        base = h * C_AUG + C_HEAD_DIM
        for piece in range(N_PIECES):
            ones[0, base + piece] = 1.0
            place[piece, h, base + N_PIECES + piece] = 1.0
            place[piece, h, n + base + piece] = -1.0
            ones[0, n + base + N_PIECES + piece] = 1.0
    return jnp.asarray(place, BF16), jnp.asarray(ones)


def _aug_qk(xb, w_aug, pieces, tm=512, tn=1024):
    m = xb.shape[0]
    n = w_aug.shape[1]
    tm = _row_tile(m, tm)
    place, ones = _aug_tables()
    row = lambda i, j: (i, 0)
    return pl.pallas_call(
        _aug_kernel,
        grid=(m // tm, n // tn),
        in_specs=[pl.BlockSpec((tm, D_MODEL), row), pl.BlockSpec((D_MODEL, tn), lambda i, j: (0, j)),
                  pl.BlockSpec((tm, LANES), row), pl.BlockSpec((tm, LANES), row), pl.BlockSpec((tm, LANES), row),
                  pl.BlockSpec((N_PIECES, LANES, tn), lambda i, j: (0, 0, j)),
                  pl.BlockSpec((1, tn), lambda i, j: (0, j))],
        out_specs=pl.BlockSpec((tm, tn), lambda i, j: (i, j)),
        out_shape=jax.ShapeDtypeStruct((m, n), BF16),
        compiler_params=_params("parallel", "parallel"),
        name="aug_qk",
    )(xb, w_aug, *pieces, place, ones)


def _pad_heads(w, scale):
    w = (w * scale).reshape(D_MODEL, C_HEADS, C_HEAD_DIM)
    return jnp.pad(w, ((0, 0), (0, 0), (0, C_AUG - C_HEAD_DIM))).reshape(D_MODEL, C_HEADS * C_AUG)


def _fox_mixer(xb, b, t, k_past, v_past, lf_past, w, i):
    resident = k_past is None and t % FLASH_T == 0
    w_qkv = w['c_w_qkv'][i]
    k32, *kb = _mm(xb, w_qkv[:, D_MODEL:2 * D_MODEL].astype(BF16), [F32] if resident else [F32, BF16])
    v32, *vb = _mm(xb, w_qkv[:, 2 * D_MODEL:].astype(BF16), [F32] if resident else [F32, BF16])
    k32 = k32.reshape(b, t, C_HEADS, C_HEAD_DIM)
    v32 = v32.reshape(b, t, C_HEADS, C_HEAD_DIM)
    w_f = jnp.zeros((D_MODEL, LANES), F32).at[:, :C_HEADS].set(w['c_w_f'][i]).astype(BF16)
    b_f = jnp.zeros((1, LANES), F32).at[0, :C_HEADS].set(w['c_b_f'][i])
    log_f = _logf(xb, w_f, b_f).reshape(b, t, LANES)
    if not resident:
        kvb = jnp.concatenate([kb[0], vb[0]], axis=-1).reshape(b, t, 2 * D_MODEL)
    lf_all = log_f
    q_off = 0
    if k_past is not None:
        p_len = k_past.shape[1]
        past = jnp.concatenate([k_past.reshape(b, p_len, D_MODEL), v_past.reshape(b, p_len, D_MODEL)],
                               axis=-1).astype(BF16)
        kvb = jnp.concatenate([past, kvb], axis=1)
        lf_all = jnp.concatenate([jnp.pad(lf_past, ((0, 0), (0, 0), (0, LANES - C_HEADS))), log_f], axis=1)
        q_off = p_len
    t_k = lf_all.shape[1]
    tc = 256
    t_pad = -(-t_k // tc) * tc
    f_cum, *f_pieces = _cumsum(jnp.pad(lf_all, ((0, 0), (0, t_pad - t_k), (0, 0))), tc)
    nh = LANES // C_HEAD_DIM
    n_hblk = C_HEADS // nh
    q_scale = C_HEAD_DIM ** -0.5
    if resident:
        w_aug = jnp.concatenate([_pad_heads(w_qkv[:, :D_MODEL], q_scale * LOG2E),
                                 _pad_heads(w_qkv[:, D_MODEL:2 * D_MODEL], 1.0)], axis=1).astype(BF16)
        qk_aug = _aug_qk(xb, w_aug, [p.reshape(b * t, LANES) for p in f_pieces]).reshape(b, t, -1)
        n_q = C_HEADS * C_AUG
        vt = _proj_t(xb, w_qkv[:, 2 * D_MODEL:], FLASH_T).reshape(b, t // FLASH_T, D_MODEL, FLASH_T)
        o = _flash_t(qk_aug, qk_aug, n_q // (nh * C_AUG), vt,
                     nh=nh, dq=C_AUG, dv=C_HEAD_DIM, n_hblk=n_hblk, chunk_causal=False)
        return o.reshape(b * t, D_MODEL), k32, v32, log_f[:, :, :C_HEADS]
    (q,) = _mm(xb, (w_qkv[:, :D_MODEL] * q_scale).astype(BF16), [BF16])
    f_cum = f_cum[:, None, :t_k, :C_HEADS]
    fq = jnp.pad(f_cum[:, :, t_k - t:], ((0, 0), (0, 0), (0, 0), (0, LANES - C_HEADS)))
    fk = f_cum.transpose(0, 1, 3, 2)
    o = _flash(q.reshape(b, t, D_MODEL), kvb, kvb, nh=C_HEADS, dq=C_HEAD_DIM, dv=C_HEAD_DIM, n_hblk=1,
               k_col0=0, v_col0=1, tq=512, tk=512, q_off=q_off, chunk_causal=False, fq=fq, fk=fk)
    return o.reshape(b * t, D_MODEL), k32, v32, log_f[:, :, :C_HEADS]


def _conv_ffn(xb, b, t, conv_past, w, i):
    tab = jnp.concatenate([w['f_conv_w'][i], w['f_conv_b'][i][None],
                           jnp.zeros((CONV_ROWS - CONV_W - 1, 2 * D_FF), F32)], axis=0)
    if conv_past is None:
        past = jnp.zeros((b, CONV_ROWS, 2 * D_FF), F32)
    else:
        past = jnp.pad(conv_past, ((0, 0), (CONV_ROWS - (CONV_W - 1), 0), (0, 0)))
    act, tail = _ffn_up(xb.reshape(b, t, D_MODEL), w['f_w_up_bf16'][i], tab, past)
    return act.reshape(b * t, D_FF), tail[:, CONV_ROWS - (CONV_W - 1):]


def _trunk(x, pos0, past, w):
    b, t, _ = x.shape
    pos = pos0 + jnp.arange(t)
    xf = x.reshape(b * t, D_MODEL)
    xb = xf.astype(BF16)
    outs = {n: [] for n in ('a_ckv', 'a_kpe', 'b_k', 'b_v', 'c_k', 'c_v', 'c_logf', 'ffn_conv')}
    ia = ib = ic = 0
    get = lambda name, j: None if past is None else past[name][j]
    for i in range(DEPTH):
        kind = i % N_MIXERS
        if kind == 0:
            o, ckv, kpe = _mla_mixer(xb, b, t, pos, get('a_ckv', ia), get('a_kpe', ia), w, ia)
            outs['a_ckv'].append(ckv)
            outs['a_kpe'].append(kpe)
            w_o = w['a_w_o_bf16'][ia]
            ia += 1
        elif kind == 1:
            o, kb, vb = _band_mixer(xb, b, t, pos0, get('b_k', ib), get('b_v', ib), w, ib)
            outs['b_k'].append(kb)
            outs['b_v'].append(vb)
            w_o = w['b_w_o_bf16'][ib]
            ib += 1
        else:
            o, kc, vc, lf = _fox_mixer(xb, b, t, get('c_k', ic), get('c_v', ic), get('c_logf', ic), w, ic)
            outs['c_k'].append(kc)
            outs['c_v'].append(vc)
            outs['c_logf'].append(lf)
            w_o = w['c_w_o_bf16'][ic]
            ic += 1
        xf, xb = _mm_res_ln(o, w_o, xf, w['ln1_g'][i], w['ln1_b'][i])
        act, conv_state = _conv_ffn(xb, b, t, get('ffn_conv', i), w, i)
        outs['ffn_conv'].append(conv_state)
        xf, xb = _mm_res_ln(act, w['f_w_down_bf16'][i], xf, w['ln2_g'][i], w['ln2_b'][i])
    return xf.reshape(b, t, D_MODEL), {n: jnp.stack(v) for n, v in outs.items()}


def kernel(x_prompt, x_sample, cache_a_ckv, cache_a_kpe, cache_b_k, cache_b_v, cache_c_k, cache_c_v,
           cache_c_logf, state_ffn_conv, a_w_dq, a_g_q, a_w_uq, a_w_dkv, a_g_kv, a_w_kr, a_w_uk, a_w_uv,
           a_w_o, b_w_qkv, b_rel_bias, b_w_o, c_w_qkv, c_w_f, c_b_f, c_w_o, f_w_up, f_conv_w, f_conv_b,
           f_w_down, ln1_g, ln1_b, ln2_g, ln2_b):
    w = dict(a_w_dq=a_w_dq, a_g_q=a_g_q, a_w_uq=a_w_uq, a_w_dkv=a_w_dkv, a_g_kv=a_g_kv, a_w_kr=a_w_kr,
             a_w_uk=a_w_uk, a_w_uv=a_w_uv, a_w_o=a_w_o, b_w_qkv=b_w_qkv, b_rel_bias=b_rel_bias, b_w_o=b_w_o,
             c_w_qkv=c_w_qkv, c_w_f=c_w_f, c_b_f=c_b_f, c_w_o=c_w_o, f_w_up=f_w_up, f_conv_w=f_conv_w,
             f_conv_b=f_conv_b, f_w_down=f_w_down, ln1_g=ln1_g, ln1_b=ln1_b, ln2_g=ln2_g, ln2_b=ln2_b)
    past = dict(a_ckv=cache_a_ckv, a_kpe=cache_a_kpe, b_k=cache_b_k, b_v=cache_b_v, c_k=cache_c_k,
                c_v=cache_c_v, c_logf=cache_c_logf, ffn_conv=state_ffn_conv)
    past_len = cache_a_ckv.shape[2]
    for name in ('f_w_up', 'f_w_down', 'a_w_o', 'b_w_o', 'c_w_o'):
        w[name + '_bf16'] = _to_bf16(w[name])
    y_prompt, p = _trunk(x_prompt, 0, None, w)
    y_sample, s = _trunk(x_sample, past_len, past, w)
    names = ('a_ckv', 'a_kpe', 'b_k', 'b_v', 'c_k', 'c_v', 'c_logf', 'ffn_conv')
    return (y_prompt, y_sample) + tuple(p[n] for n in names) + tuple(s[n] for n in names)
```

```python
import functools
import math

import numpy as np
import jax
import jax.numpy as jnp
from jax import lax
from jax.experimental import pallas as pl
from jax.experimental.pallas import tpu as pltpu

F32 = jnp.float32
BF16 = jnp.bfloat16

D_MODEL = 1024
DEPTH = 4
CHUNK = 64
N_MIXERS = 3

A_HEADS = 8
A_Q_LORA = 384
A_KV_LORA = 256
A_NOPE = 128
A_ROPE = 64
A_V = 128
A_QK_PAD = 256
A_LAT_PAD = 384
ROPE_THETA = 10000.0

B_HEADS = 16
B_HEAD_DIM = D_MODEL // B_HEADS
B_LEFT_CHUNKS = 8
B_WIN = B_LEFT_CHUNKS * CHUNK
B_REL_CLIP = 128

C_HEADS = 16
C_HEAD_DIM = D_MODEL // C_HEADS

D_FF = 2816
CONV_W = 3

ALPHA = (2.0 * DEPTH) ** 0.25
LN_EPS = 1e-5
RMS_EPS = 1e-6
NEG_INF = -1e30

LANES = 128
SUBLANES = 8
VMEM_LIMIT_BYTES = 48 * 2 ** 20


def _params(*sem):
    return pltpu.CompilerParams(dimension_semantics=sem, vmem_limit_bytes=VMEM_LIMIT_BYTES)


def _row_tile(m, tm):
    while m % tm:
        tm //= 2
    assert tm % SUBLANES == 0, (m, tm)
    return tm


def _mm_kernel(x_ref, w_ref, *o_refs):
    acc = jnp.dot(x_ref[...], w_ref[...], preferred_element_type=F32)
    for o_ref in o_refs:
        o_ref[...] = acc.astype(o_ref.dtype)


def _mm(x, w, out_dtypes, tm=512, tn=1024):
    m, k = x.shape
    n = w.shape[1]
    tm, tn = _row_tile(m, tm), min(tn, n)
    return pl.pallas_call(
        _mm_kernel,
        grid=(m // tm, n // tn),
        in_specs=[pl.BlockSpec((tm, k), lambda i, j: (i, 0)),
                  pl.BlockSpec((k, tn), lambda i, j: (0, j))],
        out_specs=[pl.BlockSpec((tm, tn), lambda i, j: (i, j)) for _ in out_dtypes],
        out_shape=[jax.ShapeDtypeStruct((m, n), d) for d in out_dtypes],
        compiler_params=_params("parallel", "parallel"),
        name="mm",
    )(x, w)


ONES_ROWS = 16


def _proj_t_kernel(wt_ref, b_ref, x_ref, o_ref):
    o_ref[...] = (lax.dot_general(wt_ref[...], x_ref[...], (((1,), (1,)), ((), ())),
                                  preferred_element_type=F32) + b_ref[...]).astype(o_ref.dtype)


def _proj_t(x, w, t_blk, head_dim=None, tm=512):
    m, k = x.shape
    n = w.shape[1]
    bias = jnp.zeros((n, 1), F32)
    if head_dim is not None:
        heads = n // head_dim
        w = jnp.pad(w.reshape(k, heads, head_dim), ((0, 0), (0, 0), (0, ONES_ROWS)))
        bias = jnp.pad(jnp.zeros((heads, head_dim, 1), F32), ((0, 0), (0, ONES_ROWS), (0, 0)),
                       constant_values=1.0)
        n = heads * (head_dim + ONES_ROWS)
        w, bias = w.reshape(k, n), bias.reshape(n, 1)
    tm = min(tm, t_blk)
    per = t_blk // tm
    return pl.pallas_call(
        _proj_t_kernel,
        grid=(m // tm,),
        in_specs=[pl.BlockSpec((n, k), lambda i: (0, 0)), pl.BlockSpec((n, 1), lambda i: (0, 0)),
                  pl.BlockSpec((tm, k), lambda i: (i, 0))],
        out_specs=pl.BlockSpec((None, n, tm), lambda i: (i // per, 0, i % per)),
        out_shape=jax.ShapeDtypeStruct((m // t_blk, n, t_blk), BF16),
        compiler_params=_params("parallel"),
        name="proj_t",
    )(w.T.astype(BF16), bias, x)


def _cast_kernel(x_ref, o_ref):
    o_ref[...] = x_ref[...].astype(o_ref.dtype)


def _to_bf16(w, tr=256):
    n_l, r, c = w.shape
    tr = _row_tile(r, tr)
    spec = pl.BlockSpec((None, tr, c), lambda l, i: (l, i, 0))
    return pl.pallas_call(
        _cast_kernel,
        grid=(n_l, r // tr),
        in_specs=[spec],
        out_specs=spec,
        out_shape=jax.ShapeDtypeStruct(w.shape, BF16),
        compiler_params=_params("parallel", "parallel"),
        name="to_bf16",
    )(w)


LN_ROWS = 128


def _mm_res_ln_kernel(a_ref, w_ref, x_ref, g_ref, b_ref, of_ref, ob_ref):
    tm = a_ref.shape[0]
    pieces = [slice(r, min(r + LN_ROWS, tm)) for r in range(0, tm, LN_ROWS)]

    def mm(rows):
        return jnp.dot(a_ref[rows, :], w_ref[...], preferred_element_type=F32)

    def ln(rows, acc):
        y = ALPHA * x_ref[rows, :] + acc
        mu = jnp.mean(y, axis=-1, keepdims=True)
        d = y - mu
        var = jnp.mean(d * d, axis=-1, keepdims=True)
        out = d * lax.rsqrt(var + LN_EPS) * g_ref[...] + b_ref[...]
        of_ref[rows, :] = out
        ob_ref[rows, :] = out.astype(BF16)

    acc = mm(pieces[0])
    for i, rows in enumerate(pieces):
        nxt = mm(pieces[i + 1]) if i + 1 < len(pieces) else None
        ln(rows, acc)
        acc = nxt


def _mm_res_ln(a, w, x, g, b, tm=512):
    m, k = a.shape
    n = w.shape[1]
    tm = _row_tile(m, tm)
    row = lambda i: (i, 0)
    fixed = lambda i: (0, 0)
    return pl.pallas_call(
        _mm_res_ln_kernel,
        grid=(m // tm,),
        in_specs=[pl.BlockSpec((tm, k), row), pl.BlockSpec((k, n), fixed),
                  pl.BlockSpec((tm, n), row), pl.BlockSpec((1, n), fixed),
                  pl.BlockSpec((1, n), fixed)],
        out_specs=[pl.BlockSpec((tm, n), row), pl.BlockSpec((tm, n), row)],
        out_shape=[jax.ShapeDtypeStruct((m, n), F32), jax.ShapeDtypeStruct((m, n), BF16)],
        compiler_params=_params("parallel"),
        name="mm_res_ln",
    )(a, w, x, g.reshape(1, n), b.reshape(1, n))


_W1_CQ = (0, A_Q_LORA)
_W1_CKV = (A_Q_LORA, A_Q_LORA + A_KV_LORA)
_W1_KR = (_W1_CKV[1], _W1_CKV[1] + A_ROPE)
_W1_KRS = (_W1_CKV[1] + LANES, _W1_CKV[1] + LANES + A_ROPE)
_W1_COLS = _W1_CKV[1] + 2 * LANES


def _rms(v, g):
    return v * lax.rsqrt(jnp.mean(v * v, axis=-1, keepdims=True) + RMS_EPS) * g


def _mla_proj_kernel(x_ref, w1_ref, gq_ref, gkv_ref, wq_ref, cq_ref, sq_ref, ck_ref, sk_ref,
                     q_ref, lat_ref, ckv_ref, kpe_ref):
    y = jnp.dot(x_ref[...], w1_ref[...], preferred_element_type=F32)
    cq = _rms(y[:, _W1_CQ[0]:_W1_CQ[1]], gq_ref[...]).astype(BF16)
    ckv = _rms(y[:, _W1_CKV[0]:_W1_CKV[1]], gkv_ref[...])
    kpe = y[:, _W1_KR[0]:_W1_KR[1]] * ck_ref[...] + y[:, _W1_KRS[0]:_W1_KRS[1]] * sk_ref[...]
    ckv_ref[...] = ckv
    kpe_ref[...] = kpe
    lat_ref[:, 0:A_KV_LORA] = ckv.astype(BF16)
    lat_ref[:, A_KV_LORA:A_KV_LORA + A_ROPE] = kpe.astype(BF16)
    lat_ref[:, A_KV_LORA + A_ROPE:] = jnp.zeros(
        (lat_ref.shape[0], A_LAT_PAD - A_KV_LORA - A_ROPE), BF16)
    sw0 = A_HEADS * A_QK_PAD
    for h in range(A_HEADS):
        lo, hi = h * A_QK_PAD, (h + 1) * A_QK_PAD
        qp = jnp.dot(cq, wq_ref[:, lo:hi], preferred_element_type=F32)
        qs = jnp.dot(cq, wq_ref[:, sw0 + lo:sw0 + hi], preferred_element_type=F32)
        q_ref[:, lo:hi] = (qp * cq_ref[...] + qs * sq_ref[...]).astype(BF16)


def _mla_proj(xb, w1, gq, gkv, wq, cosq, sinq, cosk, sink, tm=512):
    m = xb.shape[0]
    tm = _row_tile(m, tm)
    row = lambda i: (i, 0)
    fixed = lambda i: (0, 0)
    nq = A_HEADS * A_QK_PAD
    return pl.pallas_call(
        _mla_proj_kernel,
        grid=(m // tm,),
        in_specs=[pl.BlockSpec((tm, D_MODEL), row), pl.BlockSpec(w1.shape, fixed),
                  pl.BlockSpec((1, A_Q_LORA), fixed), pl.BlockSpec((1, A_KV_LORA), fixed),
                  pl.BlockSpec(wq.shape, fixed),
                  pl.BlockSpec((tm, A_QK_PAD), row), pl.BlockSpec((tm, A_QK_PAD), row),
                  pl.BlockSpec((tm, A_ROPE), row), pl.BlockSpec((tm, A_ROPE), row)],
        out_specs=[pl.BlockSpec((tm, nq), row), pl.BlockSpec((tm, A_LAT_PAD), row),
                   pl.BlockSpec((tm, A_KV_LORA), row), pl.BlockSpec((tm, A_ROPE), row)],
        out_shape=[jax.ShapeDtypeStruct((m, nq), BF16), jax.ShapeDtypeStruct((m, A_LAT_PAD), BF16),
                   jax.ShapeDtypeStruct((m, A_KV_LORA), F32), jax.ShapeDtypeStruct((m, A_ROPE), F32)],
        compiler_params=_params("parallel"),
        name="mla_proj",
    )(xb, w1, gq.reshape(1, -1), gkv.reshape(1, -1), wq, cosq, sinq, cosk, sink)


def _flash_kernel(*refs, nh, dq, dv, tq, tk, nk, q_off, chunk_causal, forget):
    if forget:
        q_ref, k_ref, v_ref, fq_ref, fk_ref, o_ref, m_s, l_s, acc_s = refs
    else:
        q_ref, k_ref, v_ref, o_ref, m_s, l_s, acc_s = refs
    iq = pl.program_id(2)
    ik = pl.program_id(3)
    q_lo = q_off + iq * tq
    q_hi = q_lo + tq - 1
    if chunk_causal:
        vis_lo = (q_lo // CHUNK) * CHUNK + CHUNK - 1
        vis_hi = (q_hi // CHUNK) * CHUNK + CHUNK - 1
    else:
        vis_lo, vis_hi = q_lo, q_hi
    k_lo = ik * tk
    needed = k_lo <= vis_hi
    unmasked = k_lo + tk - 1 <= vis_lo

    @pl.when(ik == 0)
    def _init():
        m_s[...] = jnp.full(m_s.shape, -jnp.inf, F32)
        l_s[...] = jnp.zeros(l_s.shape, F32)
        acc_s[...] = jnp.zeros(acc_s.shape, F32)

    def step(masked):
        if masked:
            q_pos = q_lo + lax.broadcasted_iota(jnp.int32, (tq, tk), 0)
            k_pos = k_lo + lax.broadcasted_iota(jnp.int32, (tq, tk), 1)
            if chunk_causal:
                shift = CHUNK.bit_length() - 1
                mask = jnp.right_shift(k_pos, shift) <= jnp.right_shift(q_pos, shift)
            else:
                mask = k_pos <= q_pos
        for h in range(nh):
            q = q_ref[:, h * dq:(h + 1) * dq]
            k = k_ref[:, h * dq:(h + 1) * dq]
            v = v_ref[:, h * dv:(h + 1) * dv]
            s = lax.dot_general(q, k, (((1,), (1,)), ((), ())), preferred_element_type=F32)
            if forget:
                s = s + fq_ref[:, h:h + 1] - fk_ref[h:h + 1, :]
            if masked:
                s = jnp.where(mask, s, NEG_INF)
            m_prev = m_s[h]
            m_new = jnp.maximum(m_prev, jnp.max(s, axis=-1, keepdims=True))
            alpha = jnp.exp(m_prev - m_new)
            p = jnp.exp(s - m_new)
            l_s[h] = alpha * l_s[h] + jnp.sum(p, axis=-1, keepdims=True)
            acc_s[h] = alpha * acc_s[h] + jnp.dot(p.astype(BF16), v, preferred_element_type=F32)
            m_s[h] = m_new

    @pl.when(needed & unmasked)
    def _plain():
        step(False)

    @pl.when(needed & jnp.logical_not(unmasked))
    def _masked():
        step(True)

    @pl.when(ik == nk - 1)
    def _fin():
        for h in range(nh):
            o_ref[:, h * dv:(h + 1) * dv] = (acc_s[h] / l_s[h]).astype(o_ref.dtype)


def _flash(q, k, v, *, nh, dq, dv, n_hblk, k_col0, v_col0, tq, tk, q_off, chunk_causal,
           fq=None, fk=None):
    b, t_q = q.shape[0], q.shape[1]
    t_k = k.shape[1]
    tq = tq if t_q % tq == 0 else t_q
    tk = tk if t_k % tk == 0 else t_k
    nq, nk = t_q // tq, t_k // tk
    forget = fq is not None

    def last_blk(i):
        q_hi = q_off + (i + 1) * tq - 1
        vis = (q_hi // CHUNK) * CHUNK + CHUNK - 1 if chunk_causal else q_hi
        return jnp.minimum(vis // tk, nk - 1)

    in_specs = [
        pl.BlockSpec((None, tq, nh * dq), lambda bi, h, i, j: (bi, i, h)),
        pl.BlockSpec((None, tk, nh * dq), lambda bi, h, i, j: (bi, jnp.minimum(j, last_blk(i)), k_col0 + h)),
        pl.BlockSpec((None, tk, nh * dv), lambda bi, h, i, j: (bi, jnp.minimum(j, last_blk(i)), v_col0 + h)),
    ]
    args = [q, k, v]
    if forget:
        in_specs += [
            pl.BlockSpec((None, None, tq, LANES), lambda bi, h, i, j: (bi, h, i, 0)),
            pl.BlockSpec((None, None, fk.shape[2], tk),
                         lambda bi, h, i, j: (bi, h, 0, jnp.minimum(j, last_blk(i)))),
        ]
        args += [fq, fk]
    kern = functools.partial(_flash_kernel, nh=nh, dq=dq, dv=dv, tq=tq, tk=tk, nk=nk, q_off=q_off,
                             chunk_causal=chunk_causal, forget=forget)
    return pl.pallas_call(
        kern,
        grid=(b, n_hblk, nq, nk),
        in_specs=in_specs,
        out_specs=pl.BlockSpec((None, tq, nh * dv), lambda bi, h, i, j: (bi, i, h)),
        out_shape=jax.ShapeDtypeStruct((b, t_q, n_hblk * nh * dv), BF16),
        scratch_shapes=[pltpu.VMEM((nh, tq, 1), F32), pltpu.VMEM((nh, tq, 1), F32),
                        pltpu.VMEM((nh, tq, dv), F32)],
        compiler_params=_params("parallel", "parallel", "parallel", "arbitrary"),
        name="flash_fox" if forget else "flash_mla",
    )(*args)


FLASH_T = 1024
FLASH_QC = 256
FLASH_KC = 256
FLASH_AHEAD = 8
A_HEADS_PER_STEP = 2
C_HEADS_PER_STEP = 4
LOG2E = math.log2(math.e)


def _flash_t_kernel(q_ref, k_ref, vt_ref, o_ref, m_s, acc_s, *, nh, dq, dv, t, chunk_causal):
    iq = pl.program_id(2)
    dva = dv + ONES_ROWS
    m_s[...] = jnp.full(m_s.shape, -jnp.inf, F32)
    acc_s[...] = jnp.zeros(acc_s.shape, F32)

    def block(j, masked):
        row0 = pl.multiple_of(j * t, t)
        qc, kc = FLASH_QC, FLASH_KC

        def n_keys(g, c):
            return min(kc, (c + 1) * qc - g * kc) if masked else kc

        chains = [(g, h, c) for g in range(t // kc) for h in range(nh) for c in range(t // qc)
                  if n_keys(g, c) > 0]

        def qk(g, h, c):
            return lax.dot_general(k_ref[pl.ds(row0 + g * kc, n_keys(g, c)), h * dq:(h + 1) * dq],
                                   q_ref[c * qc:(c + 1) * qc, h * dq:(h + 1) * dq],
                                   (((1,), (1,)), ((), ())), preferred_element_type=F32)

        def softmax_pv(g, h, c, s):
            cols = slice(c * qc, (c + 1) * qc)
            nk = n_keys(g, c)
            if masked and g * kc + nk > c * qc:
                k_pos = lax.broadcasted_iota(jnp.int32, (nk, qc), 0) + g * kc
                q_pos = lax.broadcasted_iota(jnp.int32, (nk, qc), 1) + c * qc
                if chunk_causal:
                    shift = CHUNK.bit_length() - 1
                    mask = jnp.right_shift(k_pos, shift) <= jnp.right_shift(q_pos, shift)
                else:
                    mask = k_pos <= q_pos
                s = jnp.where(mask, s, NEG_INF)
            m_prev = m_s[h, :, cols]
            m_new = jnp.maximum(m_prev, jnp.max(s, axis=0, keepdims=True))
            alpha = jnp.exp2(m_prev - m_new)
            p = jnp.exp2(s - m_new)
            acc_s[h, :, cols] = alpha * acc_s[h, :, cols] + jnp.dot(
                vt_ref[j, h * dva:(h + 1) * dva, g * kc:g * kc + nk], p.astype(BF16),
                preferred_element_type=F32)
            m_s[h, :, cols] = m_new

        pending = [qk(*chain) for chain in chains[:FLASH_AHEAD]]
        for i, chain in enumerate(chains):
            if i + FLASH_AHEAD < len(chains):
                pending.append(qk(*chains[i + FLASH_AHEAD]))
            softmax_pv(*chain, pending.pop(0))

    def full_block(j, carry):
        block(j, False)
        return carry

    lax.fori_loop(0, iq, full_block, 0)
    block(iq, True)
    out = jnp.concatenate([acc_s[h, 0:dv] / acc_s[h, dv:dv + 1] for h in range(nh)], axis=0)
    o_ref[...] = out.T.astype(o_ref.dtype)


def _flash_t(q, k, k_col0, vt, *, nh, dq, dv, n_hblk, chunk_causal):
    b, t_all = q.shape[0], q.shape[1]
    t = FLASH_T
    nblk = t_all // t
    kern = functools.partial(_flash_t_kernel, nh=nh, dq=dq, dv=dv, t=t, chunk_causal=chunk_causal)
    return pl.pallas_call(
        kern,
        grid=(b, n_hblk, nblk),
        in_specs=[pl.BlockSpec((None, t, nh * dq), lambda bi, h, i: (bi, i, h)),
                  pl.BlockSpec((None, t_all, nh * dq), lambda bi, h, i: (bi, 0, k_col0 + h),
                               pipeline_mode=pl.Buffered(1)),
                  pl.BlockSpec((None, nblk, nh * (dv + ONES_ROWS), t), lambda bi, h, i: (bi, 0, h, 0),
                               pipeline_mode=pl.Buffered(1))],
        out_specs=pl.BlockSpec((None, t, nh * dv), lambda bi, h, i: (bi, i, h)),
        out_shape=jax.ShapeDtypeStruct((b, t_all, n_hblk * nh * dv), BF16),
        scratch_shapes=[pltpu.VMEM((nh, 1, t), F32), pltpu.VMEM((nh, dv + ONES_ROWS, t), F32)],
        compiler_params=_params("parallel", "parallel", "parallel"),
        name="flash_t_chunk" if chunk_causal else "flash_t_frame",
    )(q, k, vt)


B_HEADS_PER_STEP = LANES // B_HEAD_DIM


def _band_step_kernel(q_ref, k_ref, v_ref, bias_ref, o_ref):
    dh = B_HEAD_DIM
    for h in range(B_HEADS):
        cols = slice(h * dh, (h + 1) * dh)
        s = lax.dot_general(q_ref[:, cols], k_ref[:, cols], (((1,), (1,)), ((), ())),
                            preferred_element_type=F32) + bias_ref[h]
        p = jnp.exp(s - jnp.max(s, axis=-1, keepdims=True))
        l = jnp.sum(p, axis=-1, keepdims=True)
        o = jnp.dot(p.astype(BF16), v_ref[:, cols], preferred_element_type=F32)
        o_ref[:, cols] = (o / l).astype(o_ref.dtype)


def _band_bias(rel_bias, q_pos, k_pos):
    nq, nk = len(q_pos), len(k_pos)
    assert (np.diff(q_pos) == 1).all() and (np.diff(k_pos) == 1).all()
    m = np.arange(nq + nk - 1)
    u = rel_bias[:, np.clip(q_pos[0] - k_pos[0] + nq - 1 - m, -B_REL_CLIP, B_REL_CLIP) + B_REL_CLIP]
    period = nq + nk
    w = jnp.concatenate([u[:, nq - 1:], jnp.zeros((u.shape[0], 1), u.dtype), u[:, :nq - 1]], axis=1)
    skew = jnp.tile(w, (1, nq))[:, :nq * (period - 1)].reshape(-1, nq, period - 1)[:, :, :nk]
    qc = q_pos[:, None] // CHUNK
    kc = k_pos[None, :] // CHUNK
    mask = (kc <= qc) & (kc >= qc - B_LEFT_CHUNKS) & (k_pos[None, :] >= 0)
    return jnp.where(jnp.asarray(mask)[None], skew, NEG_INF).astype(F32)


def _band_step(q, kv, bias):
    b, t, _ = q.shape
    t_k = kv.shape[1]
    return pl.pallas_call(
        _band_step_kernel,
        grid=(b,),
        in_specs=[pl.BlockSpec((None, t, D_MODEL), lambda bi: (bi, 0, 0)),
                  pl.BlockSpec((None, t_k, D_MODEL), lambda bi: (bi, 0, 0)),
                  pl.BlockSpec((None, t_k, D_MODEL), lambda bi: (bi, 0, 1)),
                  pl.BlockSpec(bias.shape, lambda bi: (0, 0, 0))],
        out_specs=pl.BlockSpec((None, t, D_MODEL), lambda bi: (bi, 0, 0)),
        out_shape=jax.ShapeDtypeStruct((b, t, D_MODEL), BF16),
        compiler_params=_params("parallel"),
        name="band_step",
    )(q, kv, kv, bias)


BAND_T = 4 * CHUNK


def _band_t_kernel(*refs, nkb, t):
    q_ref = refs[0]
    k_refs = refs[1:1 + nkb]
    vt_refs = refs[1 + nkb:1 + 2 * nkb]
    bias_ref, o_ref = refs[1 + 2 * nkb], refs[2 + 2 * nkb]
    iq = pl.program_id(1)
    dh = B_HEAD_DIM

    def qk(h):
        cols = slice(h * dh, (h + 1) * dh)
        parts = []
        for j in range(nkb):
            s = lax.dot_general(k_refs[j][:, cols], q_ref[:, cols], (((1,), (1,)), ((), ())),
                                preferred_element_type=F32)
            if j < nkb - 1:
                s = jnp.where(iq >= nkb - 1 - j, s, NEG_INF)
            parts.append(s)
        return jnp.concatenate(parts, axis=0) + bias_ref[h]

    def softmax_pv(h, s):
        p = jnp.exp2(s - jnp.max(s, axis=0, keepdims=True))
        l = jnp.sum(p, axis=0, keepdims=True)
        pb = p.astype(BF16)
        o = functools.reduce(lambda a, c: a + c, [
            jnp.dot(vt_refs[j][h * dh:(h + 1) * dh, :], pb[j * t:(j + 1) * t], preferred_element_type=F32)
            for j in range(nkb)])
        return o / l

    heads = list(range(B_HEADS))
    pending = [qk(h) for h in heads[:FLASH_AHEAD]]
    outs = []
    for h in heads:
        if h + FLASH_AHEAD < B_HEADS:
            pending.append(qk(h + FLASH_AHEAD))
        outs.append(softmax_pv(h, pending.pop(0)))
        if len(outs) == B_HEADS_PER_STEP:
            c0 = (h + 1 - B_HEADS_PER_STEP) * dh
            o_ref[:, c0:c0 + LANES] = jnp.concatenate(outs, axis=0).T.astype(o_ref.dtype)
            outs = []


def _band_t(q, kv, vt, bias_t, *, nkb):
    b, t_all = q.shape[0], q.shape[1]
    t = BAND_T

    def back(j):
        return nkb - 1 - j

    k_specs = [pl.BlockSpec((None, t, D_MODEL), lambda bi, i, j=j: (bi, jnp.maximum(i - back(j), 0), 0))
               for j in range(nkb)]
    vt_specs = [pl.BlockSpec((None, None, D_MODEL, t),
                             lambda bi, i, j=j: (bi, jnp.maximum(i - back(j), 0), 0, 0))
                for j in range(nkb)]
    return pl.pallas_call(
        functools.partial(_band_t_kernel, nkb=nkb, t=t),
        grid=(b, t_all // t),
        in_specs=([pl.BlockSpec((None, t, D_MODEL), lambda bi, i: (bi, i, 0))] + k_specs + vt_specs
                  + [pl.BlockSpec(bias_t.shape, lambda bi, i: (0, 0, 0), pipeline_mode=pl.Buffered(1))]),
        out_specs=pl.BlockSpec((None, t, D_MODEL), lambda bi, i: (bi, i, 0)),
        out_shape=jax.ShapeDtypeStruct((b, t_all, D_MODEL), BF16),
        compiler_params=_params("parallel", "parallel"),
        name="band_t",
    )(q, *([kv] * nkb), *([vt] * nkb), bias_t)


def _logf_kernel(x_ref, w_ref, b_ref, o_ref):
    z = jnp.dot(x_ref[...], w_ref[...], preferred_element_type=F32) + b_ref[...]
    o_ref[...] = -(jnp.maximum(-z, 0.0) + jnp.log1p(jnp.exp(-jnp.abs(z))))


def _logf(xb, w_pad, b_pad, tm=512):
    m = xb.shape[0]
    tm = _row_tile(m, tm)
    return pl.pallas_call(
        _logf_kernel,
        grid=(m // tm,),
        in_specs=[pl.BlockSpec((tm, D_MODEL), lambda i: (i, 0)),
                  pl.BlockSpec((D_MODEL, LANES), lambda i: (0, 0)),
                  pl.BlockSpec((1, LANES), lambda i: (0, 0))],
        out_specs=pl.BlockSpec((tm, LANES), lambda i: (i, 0)),
        out_shape=jax.ShapeDtypeStruct((m, LANES), F32),
        compiler_params=_params("parallel"),
        name="logf",
    )(xb, w_pad, b_pad)


def _split3(x):
    hi = x.astype(BF16)
    r = x - hi.astype(F32)
    mid = r.astype(BF16)
    lo = (r - mid.astype(F32)).astype(BF16)
    return hi, mid, lo


def _cumsum_kernel(x_ref, o_ref, hi_ref, mid_ref, lo_ref, carry, *, tc):
    @pl.when(pl.program_id(1) == 0)
    def _():
        carry[...] = jnp.zeros(carry.shape, F32)

    tri = (lax.broadcasted_iota(jnp.int32, (tc, tc), 0)
           >= lax.broadcasted_iota(jnp.int32, (tc, tc), 1)).astype(BF16)
    c = functools.reduce(lambda a, b: a + b, [jnp.dot(tri, piece, preferred_element_type=F32)
                                              for piece in _split3(x_ref[...])])
    out = c + carry[0:1, :]
    o_ref[...] = out
    hi_ref[...], mid_ref[...], lo_ref[...] = _split3(out * LOG2E)
    carry[...] = jnp.broadcast_to(out[tc - 1:tc, :], carry.shape)


def _cumsum(x, tc=256):
    b, t, _ = x.shape
    spec = pl.BlockSpec((None, tc, LANES), lambda bi, i: (bi, i, 0))
    piece = jax.ShapeDtypeStruct(x.shape, BF16)
    return pl.pallas_call(
        functools.partial(_cumsum_kernel, tc=tc),
        grid=(b, t // tc),
        in_specs=[spec],
        out_specs=[spec] * 4,
        out_shape=[jax.ShapeDtypeStruct(x.shape, F32), piece, piece, piece],
        scratch_shapes=[pltpu.VMEM((SUBLANES, LANES), F32)],
        compiler_params=_params("parallel", "arbitrary"),
        name="cumsum",
    )(x)


FFN_CHUNK = 256
CONV_ROWS = SUBLANES


def _ffn_up_kernel(x_ref, w_ref, c_ref, p_ref, act_ref, s_ref, halo, *, tm):
    @pl.when(pl.program_id(1) == 0)
    def _():
        halo[...] = p_ref[...]

    x = x_ref[...]
    n_chunks = D_FF // FFN_CHUNK
    groups = tm // CONV_ROWS
    row = lax.broadcasted_iota(jnp.int32, (groups, CONV_ROWS, FFN_CHUNK), 1)

    def up(c, half):
        col = half * D_FF + c * FFN_CHUNK
        return jnp.dot(x, w_ref[:, col:col + FFN_CHUNK], preferred_element_type=F32)

    def conv(h, c, half):
        cols = slice(half * D_FF + c * FFN_CHUNK, half * D_FF + (c + 1) * FFN_CHUNK)
        ext = jnp.concatenate([halo[:, cols], h], axis=0).reshape(groups + 1, CONV_ROWS, FFN_CHUNK)
        hc = c_ref[CONV_W:CONV_W + 1, cols] + c_ref[CONV_W - 1:CONV_W, cols] * h
        for s in range(1, CONV_W):
            rot = pltpu.roll(ext, s, axis=1)
            shifted = jnp.where(row < s, rot[:groups], rot[1:]).reshape(tm, FFN_CHUNK)
            hc = hc + c_ref[CONV_W - 1 - s:CONV_W - s, cols] * shifted
        tail = h[tm - CONV_ROWS:tm]
        halo[:, cols] = tail
        s_ref[:, cols] = tail
        return hc

    pending = [(up(0, 0), up(0, 1))]
    for c in range(n_chunks):
        if c + 1 < n_chunks:
            pending.append((up(c + 1, 0), up(c + 1, 1)))
        ha, hg = pending.pop(0)
        a = conv(ha, c, 0)
        g = conv(hg, c, 1)
        act_ref[:, c * FFN_CHUNK:(c + 1) * FFN_CHUNK] = (g * jax.nn.sigmoid(g) * a).astype(BF16)


def _ffn_up(xb, w_up, conv_tab, past, tm=256):
    b, t, _ = xb.shape
    tm = _row_tile(t, tm)
    fixed = lambda bi, ti: (0, 0)
    return pl.pallas_call(
        functools.partial(_ffn_up_kernel, tm=tm),
        grid=(b, t // tm),
        in_specs=[pl.BlockSpec((None, tm, D_MODEL), lambda bi, ti: (bi, ti, 0)),
                  pl.BlockSpec((D_MODEL, 2 * D_FF), fixed),
                  pl.BlockSpec((CONV_ROWS, 2 * D_FF), fixed),
                  pl.BlockSpec((None, CONV_ROWS, 2 * D_FF), lambda bi, ti: (bi, 0, 0))],
        out_specs=[pl.BlockSpec((None, tm, D_FF), lambda bi, ti: (bi, ti, 0)),
                   pl.BlockSpec((None, CONV_ROWS, 2 * D_FF), lambda bi, ti: (bi, 0, 0))],
        out_shape=[jax.ShapeDtypeStruct((b, t, D_FF), BF16),
                   jax.ShapeDtypeStruct((b, CONV_ROWS, 2 * D_FF), F32)],
        scratch_shapes=[pltpu.VMEM((CONV_ROWS, 2 * D_FF), F32)],
        compiler_params=_params("parallel", "arbitrary"),
        name="ffn_up",
    )(xb, w_up, conv_tab, past)


def _rope_tables(pos, batch, q_scale):
    half = A_ROPE // 2
    inv_freq = ROPE_THETA ** (-jnp.arange(half, dtype=F32) / half)
    ang = pos.astype(F32)[:, None] * inv_freq
    cos, sin = jnp.cos(ang), jnp.sin(ang)
    cosk = jnp.concatenate([cos, cos], axis=-1)
    sink = jnp.concatenate([sin, sin], axis=-1)
    t = pos.shape[0]
    pad = jnp.zeros((t, A_QK_PAD - A_NOPE - A_ROPE), F32)
    cosq = q_scale * jnp.concatenate([jnp.ones((t, A_NOPE), F32), cosk, pad], axis=-1)
    sinq = q_scale * jnp.concatenate([jnp.zeros((t, A_NOPE), F32), sink, pad], axis=-1)
    return tuple(jnp.tile(a, (batch, 1)) for a in (cosq, sinq, cosk, sink))


def _swap_halves(w):
    half = w.shape[-1] // 2
    return jnp.concatenate([-w[..., half:], w[..., :half]], axis=-1)


def _mla_weights(w_dq, w_dkv, w_kr, w_uq, w_uk, w_uv):
    zc = jnp.zeros((D_MODEL, LANES - A_ROPE), F32)
    w1 = jnp.concatenate([w_dq, w_dkv, w_kr, zc, _swap_halves(w_kr), zc], axis=1).astype(BF16)
    wq = w_uq.reshape(A_Q_LORA, A_HEADS, A_NOPE + A_ROPE)
    nope, rope = wq[..., :A_NOPE], wq[..., A_NOPE:]
    zpad = jnp.zeros((A_Q_LORA, A_HEADS, A_QK_PAD - A_NOPE - A_ROPE), F32)
    w_cat = jnp.concatenate([nope, rope, zpad], axis=-1).reshape(A_Q_LORA, -1)
    w_sw = jnp.concatenate([jnp.zeros_like(nope), _swap_halves(rope), zpad], axis=-1).reshape(A_Q_LORA, -1)
    wq2 = jnp.concatenate([w_cat, w_sw], axis=1).astype(BF16)
    wk = jnp.zeros((A_LAT_PAD, A_HEADS, A_QK_PAD), F32)
    wk = wk.at[:A_KV_LORA, :, :A_NOPE].set(w_uk)
    eye = jnp.broadcast_to(jnp.eye(A_ROPE, dtype=F32)[:, None, :], (A_ROPE, A_HEADS, A_ROPE))
    wk = wk.at[A_KV_LORA:A_KV_LORA + A_ROPE, :, A_NOPE:A_NOPE + A_ROPE].set(eye)
    wv = jnp.zeros((A_LAT_PAD, A_HEADS * A_V), F32).at[:A_KV_LORA].set(w_uv.reshape(A_KV_LORA, -1))
    wkv = jnp.concatenate([wk.reshape(A_LAT_PAD, -1), wv], axis=1).astype(BF16)
    return w1, wq2, wkv


def _mla_mixer(xb, b, t, pos, ckv_past, kpe_past, w, i):
    w1, wq2, wkv = _mla_weights(w['a_w_dq'][i], w['a_w_dkv'][i], w['a_w_kr'][i], w['a_w_uq'][i],
                                w['a_w_uk'][i], w['a_w_uv'][i])
    resident = ckv_past is None and t % FLASH_T == 0
    q_scale = (A_NOPE + A_ROPE) ** -0.5 * (LOG2E if resident else 1.0)
    q, lat, ckv, kpe = _mla_proj(xb, w1, w['a_g_q'][i], w['a_g_kv'][i], wq2,
                                 *_rope_tables(pos, b, q_scale))
    lat = lat.reshape(b, t, A_LAT_PAD)
    q_off = 0
    if ckv_past is not None:
        p_len = ckv_past.shape[1]
        past = jnp.concatenate(
            [ckv_past, kpe_past, jnp.zeros((b, p_len, A_LAT_PAD - A_KV_LORA - A_ROPE), F32)], axis=-1)
        lat = jnp.concatenate([past.astype(BF16), lat], axis=1)
        q_off = p_len
    t_k = lat.shape[1]
    lat2 = lat.reshape(b * t_k, A_LAT_PAD)
    n_k = A_HEADS * A_QK_PAD
    if resident:
        (k_cat,) = _mm(lat2, wkv[:, :n_k], [BF16])
        vt = _proj_t(lat2, wkv[:, n_k:], FLASH_T, A_V).reshape(b, t_k // FLASH_T, -1, FLASH_T)
        o = _flash_t(q.reshape(b, t, -1), k_cat.reshape(b, t_k, n_k), 0, vt, nh=A_HEADS_PER_STEP,
                     dq=A_QK_PAD, dv=A_V, n_hblk=A_HEADS // A_HEADS_PER_STEP, chunk_causal=True)
    else:
        (kv,) = _mm(lat2, wkv, [BF16])
        kv = kv.reshape(b, t_k, -1)
        o = _flash(q.reshape(b, t, -1), kv, kv, nh=A_HEADS, dq=A_QK_PAD, dv=A_V, n_hblk=1,
                   k_col0=0, v_col0=A_QK_PAD // A_V, tq=512, tk=512, q_off=q_off, chunk_causal=True)
    return o.reshape(b * t, A_HEADS * A_V), ckv.reshape(b, t, -1), kpe.reshape(b, t, -1)


def _qkv(xb, w_qkv, q_scale):
    wq = (w_qkv[:, :D_MODEL] * q_scale).astype(BF16)
    (q,) = _mm(xb, wq, [BF16])
    kv32, kvb = _mm(xb, w_qkv[:, D_MODEL:].astype(BF16), [F32, BF16])
    return q, kv32, kvb


def _band_mixer(xb, b, t, pos0, k_past, v_past, w, i):
    prompt = k_past is None
    assert not prompt or t % BAND_T == 0
    q, kv32, kvb = _qkv(xb, w['b_w_qkv'][i], B_HEAD_DIM ** -0.5 * (LOG2E if prompt else 1.0))
    q = q.reshape(b, t, D_MODEL)
    k32 = kv32[:, :D_MODEL].reshape(b, t, B_HEADS, B_HEAD_DIM)
    v32 = kv32[:, D_MODEL:].reshape(b, t, B_HEADS, B_HEAD_DIM)
    if prompt:
        nkb = B_WIN // BAND_T + 1
        bias = _band_bias(w['b_rel_bias'][i], B_WIN + np.arange(BAND_T), np.arange(B_WIN + BAND_T))
        kvb = kvb.reshape(b, t, 2 * D_MODEL)
        vt = _proj_t(xb, w['b_w_qkv'][i][:, 2 * D_MODEL:], BAND_T).reshape(b, t // BAND_T, D_MODEL, BAND_T)
        o = _band_t(q, kvb, vt, LOG2E * bias.transpose(0, 2, 1), nkb=nkb)
        keep = min(B_WIN, t)
        k_new, v_new = k32[:, t - keep:], v32[:, t - keep:]
    else:
        p_len = k_past.shape[1]
        kvb = kvb.reshape(b, t, 2 * D_MODEL)
        past = jnp.concatenate([k_past.reshape(b, p_len, D_MODEL), v_past.reshape(b, p_len, D_MODEL)],
                               axis=-1).astype(BF16)
        kv_all = jnp.concatenate([past, kvb], axis=1)
        q_pos = pos0 + np.arange(t)
        k_pos = np.concatenate([np.arange(pos0 - p_len, pos0), q_pos])
        bias = _band_bias(w['b_rel_bias'][i], q_pos, k_pos)
        o = _band_step(q, kv_all, bias)
        k_new, v_new = k32, v32
    return o.reshape(b * t, D_MODEL), k_new, v_new


C_AUG = LANES
N_PIECES = 3


def _aug_kernel(x_ref, w_ref, hi_ref, mid_ref, lo_ref, p_ref, b_ref, o_ref):
    acc = jnp.dot(x_ref[...], w_ref[...], preferred_element_type=F32) + b_ref[...]
    for piece, f_ref in enumerate((hi_ref, mid_ref, lo_ref)):
        acc = acc + jnp.dot(f_ref[...], p_ref[piece], preferred_element_type=F32)
    o_ref[...] = acc.astype(BF16)


def _aug_tables():
    n = C_HEADS * C_AUG
    place = np.zeros((N_PIECES, LANES, 2 * n), np.float32)
    ones = np.zeros((1, 2 * n), np.float32)
    for h in range(C_HEADS):
        base = h * C_AUG + C_HEAD_DIM
        for piece in range(N_PIECES):
            ones[0, base + piece] = 1.0
            place[piece, h, base + N_PIECES + piece] = 1.0
            place[piece, h, n + base + piece] = -1.0
            ones[0, n + base + N_PIECES + piece] = 1.0
    return jnp.asarray(place, BF16), jnp.asarray(ones)


def _aug_qk(xb, w_aug, pieces, tm=512, tn=1024):
    m = xb.shape[0]
    n = w_aug.shape[1]
    tm = _row_tile(m, tm)
    place, ones = _aug_tables()
    row = lambda i, j: (i, 0)
    return pl.pallas_call(
        _aug_kernel,
        grid=(m // tm, n // tn),
        in_specs=[pl.BlockSpec((tm, D_MODEL), row), pl.BlockSpec((D_MODEL, tn), lambda i, j: (0, j)),
                  pl.BlockSpec((tm, LANES), row), pl.BlockSpec((tm, LANES), row), pl.BlockSpec((tm, LANES), row),
                  pl.BlockSpec((N_PIECES, LANES, tn), lambda i, j: (0, 0, j)),
                  pl.BlockSpec((1, tn), lambda i, j: (0, j))],
        out_specs=pl.BlockSpec((tm, tn), lambda i, j: (i, j)),
        out_shape=jax.ShapeDtypeStruct((m, n), BF16),
        compiler_params=_params("parallel", "parallel"),
        name="aug_qk",
    )(xb, w_aug, *pieces, place, ones)


def _pad_heads(w, scale):
    w = (w * scale).reshape(D_MODEL, C_HEADS, C_HEAD_DIM)
    return jnp.pad(w, ((0, 0), (0, 0), (0, C_AUG - C_HEAD_DIM))).reshape(D_MODEL, C_HEADS * C_AUG)


def _fox_mixer(xb, b, t, k_past, v_past, lf_past, w, i):
    resident = k_past is None and t % FLASH_T == 0
    w_qkv = w['c_w_qkv'][i]
    k32, *kb = _mm(xb, w_qkv[:, D_MODEL:2 * D_MODEL].astype(BF16), [F32] if resident else [F32, BF16])
    v32, *vb = _mm(xb, w_qkv[:, 2 * D_MODEL:].astype(BF16), [F32] if resident else [F32, BF16])
    k32 = k32.reshape(b, t, C_HEADS, C_HEAD_DIM)
    v32 = v32.reshape(b, t, C_HEADS, C_HEAD_DIM)
    w_f = jnp.zeros((D_MODEL, LANES), F32).at[:, :C_HEADS].set(w['c_w_f'][i]).astype(BF16)
    b_f = jnp.zeros((1, LANES), F32).at[0, :C_HEADS].set(w['c_b_f'][i])
    log_f = _logf(xb, w_f, b_f).reshape(b, t, LANES)
    if not resident:
        kvb = jnp.concatenate([kb[0], vb[0]], axis=-1).reshape(b, t, 2 * D_MODEL)
    lf_all = log_f
    q_off = 0
    if k_past is not None:
        p_len = k_past.shape[1]
        past = jnp.concatenate([k_past.reshape(b, p_len, D_MODEL), v_past.reshape(b, p_len, D_MODEL)],
                               axis=-1).astype(BF16)
        kvb = jnp.concatenate([past, kvb], axis=1)
        lf_all = jnp.concatenate([jnp.pad(lf_past, ((0, 0), (0, 0), (0, LANES - C_HEADS))), log_f], axis=1)
        q_off = p_len
    t_k = lf_all.shape[1]
    tc = 256
    t_pad = -(-t_k // tc) * tc
    f_cum, *f_pieces = _cumsum(jnp.pad(lf_all, ((0, 0), (0, t_pad - t_k), (0, 0))), tc)
    nh = C_HEADS_PER_STEP
    n_hblk = C_HEADS // nh
    q_scale = C_HEAD_DIM ** -0.5
    if resident:
        w_aug = jnp.concatenate([_pad_heads(w_qkv[:, :D_MODEL], q_scale * LOG2E),
                                 _pad_heads(w_qkv[:, D_MODEL:2 * D_MODEL], 1.0)], axis=1).astype(BF16)
        qk_aug = _aug_qk(xb, w_aug, [p.reshape(b * t, LANES) for p in f_pieces]).reshape(b, t, -1)
        n_q = C_HEADS * C_AUG
        vt = _proj_t(xb, w_qkv[:, 2 * D_MODEL:], FLASH_T, C_HEAD_DIM).reshape(b, t // FLASH_T, -1, FLASH_T)
        o = _flash_t(qk_aug, qk_aug, n_q // (nh * C_AUG), vt,
                     nh=nh, dq=C_AUG, dv=C_HEAD_DIM, n_hblk=n_hblk, chunk_causal=False)
        return o.reshape(b * t, D_MODEL), k32, v32, log_f[:, :, :C_HEADS]
    (q,) = _mm(xb, (w_qkv[:, :D_MODEL] * q_scale).astype(BF16), [BF16])
    f_cum = f_cum[:, None, :t_k, :C_HEADS]
    fq = jnp.pad(f_cum[:, :, t_k - t:], ((0, 0), (0, 0), (0, 0), (0, LANES - C_HEADS)))
    fk = f_cum.transpose(0, 1, 3, 2)
    o = _flash(q.reshape(b, t, D_MODEL), kvb, kvb, nh=C_HEADS, dq=C_HEAD_DIM, dv=C_HEAD_DIM, n_hblk=1,
               k_col0=0, v_col0=1, tq=512, tk=512, q_off=q_off, chunk_causal=False, fq=fq, fk=fk)
    return o.reshape(b * t, D_MODEL), k32, v32, log_f[:, :, :C_HEADS]


def _conv_ffn(xb, b, t, conv_past, w, i):
    tab = jnp.concatenate([w['f_conv_w'][i], w['f_conv_b'][i][None],
                           jnp.zeros((CONV_ROWS - CONV_W - 1, 2 * D_FF), F32)], axis=0)
    if conv_past is None:
        past = jnp.zeros((b, CONV_ROWS, 2 * D_FF), F32)
    else:
        past = jnp.pad(conv_past, ((0, 0), (CONV_ROWS - (CONV_W - 1), 0), (0, 0)))
    act, tail = _ffn_up(xb.reshape(b, t, D_MODEL), w['f_w_up_bf16'][i], tab, past)
    return act.reshape(b * t, D_FF), tail[:, CONV_ROWS - (CONV_W - 1):]


def _trunk(x, pos0, past, w):
    b, t, _ = x.shape
    pos = pos0 + jnp.arange(t)
    xf = x.reshape(b * t, D_MODEL)
    xb = xf.astype(BF16)
    outs = {n: [] for n in ('a_ckv', 'a_kpe', 'b_k', 'b_v', 'c_k', 'c_v', 'c_logf', 'ffn_conv')}
    ia = ib = ic = 0
    get = lambda name, j: None if past is None else past[name][j]
    for i in range(DEPTH):
        kind = i % N_MIXERS
        if kind == 0:
            o, ckv, kpe = _mla_mixer(xb, b, t, pos, get('a_ckv', ia), get('a_kpe', ia), w, ia)
            outs['a_ckv'].append(ckv)
            outs['a_kpe'].append(kpe)
            w_o = w['a_w_o_bf16'][ia]
            ia += 1
        elif kind == 1:
            o, kb, vb = _band_mixer(xb, b, t, pos0, get('b_k', ib), get('b_v', ib), w, ib)
            outs['b_k'].append(kb)
            outs['b_v'].append(vb)
            w_o = w['b_w_o_bf16'][ib]
            ib += 1
        else:
            o, kc, vc, lf = _fox_mixer(xb, b, t, get('c_k', ic), get('c_v', ic), get('c_logf', ic), w, ic)
            outs['c_k'].append(kc)
            outs['c_v'].append(vc)
            outs['c_logf'].append(lf)
            w_o = w['c_w_o_bf16'][ic]
            ic += 1
        xf, xb = _mm_res_ln(o, w_o, xf, w['ln1_g'][i], w['ln1_b'][i])
        act, conv_state = _conv_ffn(xb, b, t, get('ffn_conv', i), w, i)
        outs['ffn_conv'].append(conv_state)
        xf, xb = _mm_res_ln(act, w['f_w_down_bf16'][i], xf, w['ln2_g'][i], w['ln2_b'][i])
    return xf.reshape(b, t, D_MODEL), {n: jnp.stack(v) for n, v in outs.items()}


def kernel(x_prompt, x_sample, cache_a_ckv, cache_a_kpe, cache_b_k, cache_b_v, cache_c_k, cache_c_v,
           cache_c_logf, state_ffn_conv, a_w_dq, a_g_q, a_w_uq, a_w_dkv, a_g_kv, a_w_kr, a_w_uk, a_w_uv,
           a_w_o, b_w_qkv, b_rel_bias, b_w_o, c_w_qkv, c_w_f, c_b_f, c_w_o, f_w_up, f_conv_w, f_conv_b,
           f_w_down, ln1_g, ln1_b, ln2_g, ln2_b):
    w = dict(a_w_dq=a_w_dq, a_g_q=a_g_q, a_w_uq=a_w_uq, a_w_dkv=a_w_dkv, a_g_kv=a_g_kv, a_w_kr=a_w_kr,
             a_w_uk=a_w_uk, a_w_uv=a_w_uv, a_w_o=a_w_o, b_w_qkv=b_w_qkv, b_rel_bias=b_rel_bias, b_w_o=b_w_o,
             c_w_qkv=c_w_qkv, c_w_f=c_w_f, c_b_f=c_b_f, c_w_o=c_w_o, f_w_up=f_w_up, f_conv_w=f_conv_w,
             f_conv_b=f_conv_b, f_w_down=f_w_down, ln1_g=ln1_g, ln1_b=ln1_b, ln2_g=ln2_g, ln2_b=ln2_b)
    past = dict(a_ckv=cache_a_ckv, a_kpe=cache_a_kpe, b_k=cache_b_k, b_v=cache_b_v, c_k=cache_c_k,
                c_v=cache_c_v, c_logf=cache_c_logf, ffn_conv=state_ffn_conv)
    past_len = cache_a_ckv.shape[2]
    for name in ('f_w_up', 'f_w_down', 'a_w_o', 'b_w_o', 'c_w_o'):
        w[name + '_bf16'] = _to_bf16(w[name])
    y_prompt, p = _trunk(x_prompt, 0, None, w)
    y_sample, s = _trunk(x_sample, past_len, past, w)
    names = ('a_ckv', 'a_kpe', 'b_k', 'b_v', 'c_k', 'c_v', 'c_logf', 'ffn_conv')
    return (y_prompt, y_sample) + tuple(p[n] for n in names) + tuple(s[n] for n in names)
```

```python
import functools
import math

import numpy as np
import jax
import jax.numpy as jnp
from jax import lax
from jax.experimental import pallas as pl
from jax.experimental.pallas import tpu as pltpu

F32 = jnp.float32
BF16 = jnp.bfloat16

D_MODEL = 1024
DEPTH = 4
CHUNK = 64
N_MIXERS = 3

A_HEADS = 8
A_Q_LORA = 384
A_KV_LORA = 256
A_NOPE = 128
A_ROPE = 64
A_V = 128
A_QK_PAD = 256
A_LAT_PAD = 384
ROPE_THETA = 10000.0

B_HEADS = 16
B_HEAD_DIM = D_MODEL // B_HEADS
B_LEFT_CHUNKS = 8
B_WIN = B_LEFT_CHUNKS * CHUNK
B_REL_CLIP = 128

C_HEADS = 16
C_HEAD_DIM = D_MODEL // C_HEADS

D_FF = 2816
CONV_W = 3

ALPHA = (2.0 * DEPTH) ** 0.25
LN_EPS = 1e-5
RMS_EPS = 1e-6
NEG_INF = -1e30

LANES = 128
SUBLANES = 8
VMEM_LIMIT_BYTES = 48 * 2 ** 20


def _params(*sem):
    return pltpu.CompilerParams(dimension_semantics=sem, vmem_limit_bytes=VMEM_LIMIT_BYTES)


def _row_tile(m, tm):
    while m % tm:
        tm //= 2
    assert tm % SUBLANES == 0, (m, tm)
    return tm


def _mm_kernel(x_ref, w_ref, *o_refs):
    acc = jnp.dot(x_ref[...], w_ref[...], preferred_element_type=F32)
    for o_ref in o_refs:
        o_ref[...] = acc.astype(o_ref.dtype)


def _mm(x, w, out_dtypes, tm=512, tn=1024):
    m, k = x.shape
    n = w.shape[1]
    tm, tn = _row_tile(m, tm), min(tn, n)
    return pl.pallas_call(
        _mm_kernel,
        grid=(m // tm, n // tn),
        in_specs=[pl.BlockSpec((tm, k), lambda i, j: (i, 0)),
                  pl.BlockSpec((k, tn), lambda i, j: (0, j))],
        out_specs=[pl.BlockSpec((tm, tn), lambda i, j: (i, j)) for _ in out_dtypes],
        out_shape=[jax.ShapeDtypeStruct((m, n), d) for d in out_dtypes],
        compiler_params=_params("parallel", "parallel"),
        name="mm",
    )(x, w)


ONES_ROWS = 16


def _proj_t_kernel(wt_ref, b_ref, x_ref, o_ref):
    o_ref[...] = (lax.dot_general(wt_ref[...], x_ref[...], (((1,), (1,)), ((), ())),
                                  preferred_element_type=F32) + b_ref[...]).astype(o_ref.dtype)


def _proj_t(x, w, t_blk, head_dim=None, tm=512):
    m, k = x.shape
    n = w.shape[1]
    bias = jnp.zeros((n, 1), F32)
    if head_dim is not None:
        heads = n // head_dim
        w = jnp.pad(w.reshape(k, heads, head_dim), ((0, 0), (0, 0), (0, ONES_ROWS)))
        bias = jnp.pad(jnp.zeros((heads, head_dim, 1), F32), ((0, 0), (0, ONES_ROWS), (0, 0)),
                       constant_values=1.0)
        n = heads * (head_dim + ONES_ROWS)
        w, bias = w.reshape(k, n), bias.reshape(n, 1)
    tm = min(tm, t_blk)
    per = t_blk // tm
    return pl.pallas_call(
        _proj_t_kernel,
        grid=(m // tm,),
        in_specs=[pl.BlockSpec((n, k), lambda i: (0, 0)), pl.BlockSpec((n, 1), lambda i: (0, 0)),
                  pl.BlockSpec((tm, k), lambda i: (i, 0))],
        out_specs=pl.BlockSpec((None, n, tm), lambda i: (i // per, 0, i % per)),
        out_shape=jax.ShapeDtypeStruct((m // t_blk, n, t_blk), BF16),
        compiler_params=_params("parallel"),
        name="proj_t",
    )(w.T.astype(BF16), bias, x)


def _cast_kernel(x_ref, o_ref):
    o_ref[...] = x_ref[...].astype(o_ref.dtype)


def _to_bf16(w, tr=256):
    n_l, r, c = w.shape
    tr = _row_tile(r, tr)
    spec = pl.BlockSpec((None, tr, c), lambda l, i: (l, i, 0))
    return pl.pallas_call(
        _cast_kernel,
        grid=(n_l, r // tr),
        in_specs=[spec],
        out_specs=spec,
        out_shape=jax.ShapeDtypeStruct(w.shape, BF16),
        compiler_params=_params("parallel", "parallel"),
        name="to_bf16",
    )(w)


LN_ROWS = 128


def _mm_res_ln_kernel(a_ref, w_ref, x_ref, g_ref, b_ref, of_ref, ob_ref):
    tm = a_ref.shape[0]
    pieces = [slice(r, min(r + LN_ROWS, tm)) for r in range(0, tm, LN_ROWS)]

    def mm(rows):
        return jnp.dot(a_ref[rows, :], w_ref[...], preferred_element_type=F32)

    def ln(rows, acc):
        y = ALPHA * x_ref[rows, :] + acc
        mu = jnp.mean(y, axis=-1, keepdims=True)
        d = y - mu
        var = jnp.mean(d * d, axis=-1, keepdims=True)
        out = d * lax.rsqrt(var + LN_EPS) * g_ref[...] + b_ref[...]
        of_ref[rows, :] = out
        ob_ref[rows, :] = out.astype(BF16)

    acc = mm(pieces[0])
    for i, rows in enumerate(pieces):
        nxt = mm(pieces[i + 1]) if i + 1 < len(pieces) else None
        ln(rows, acc)
        acc = nxt


def _mm_res_ln(a, w, x, g, b, tm=512):
    m, k = a.shape
    n = w.shape[1]
    tm = _row_tile(m, tm)
    row = lambda i: (i, 0)
    fixed = lambda i: (0, 0)
    return pl.pallas_call(
        _mm_res_ln_kernel,
        grid=(m // tm,),
        in_specs=[pl.BlockSpec((tm, k), row), pl.BlockSpec((k, n), fixed),
                  pl.BlockSpec((tm, n), row), pl.BlockSpec((1, n), fixed),
                  pl.BlockSpec((1, n), fixed)],
        out_specs=[pl.BlockSpec((tm, n), row), pl.BlockSpec((tm, n), row)],
        out_shape=[jax.ShapeDtypeStruct((m, n), F32), jax.ShapeDtypeStruct((m, n), BF16)],
        compiler_params=_params("parallel"),
        name="mm_res_ln",
    )(a, w, x, g.reshape(1, n), b.reshape(1, n))


_W1_CQ = (0, A_Q_LORA)
_W1_CKV = (A_Q_LORA, A_Q_LORA + A_KV_LORA)
_W1_KR = (_W1_CKV[1], _W1_CKV[1] + A_ROPE)
_W1_KRS = (_W1_CKV[1] + LANES, _W1_CKV[1] + LANES + A_ROPE)
_W1_COLS = _W1_CKV[1] + 2 * LANES


def _rms(v, g):
    return v * lax.rsqrt(jnp.mean(v * v, axis=-1, keepdims=True) + RMS_EPS) * g


def _mla_proj_kernel(x_ref, w1_ref, gq_ref, gkv_ref, wq_ref, cq_ref, sq_ref, ck_ref, sk_ref,
                     q_ref, lat_ref, ckv_ref, kpe_ref):
    y = jnp.dot(x_ref[...], w1_ref[...], preferred_element_type=F32)
    cq = _rms(y[:, _W1_CQ[0]:_W1_CQ[1]], gq_ref[...]).astype(BF16)
    ckv = _rms(y[:, _W1_CKV[0]:_W1_CKV[1]], gkv_ref[...])
    kpe = y[:, _W1_KR[0]:_W1_KR[1]] * ck_ref[...] + y[:, _W1_KRS[0]:_W1_KRS[1]] * sk_ref[...]
    ckv_ref[...] = ckv
    kpe_ref[...] = kpe
    lat_ref[:, 0:A_KV_LORA] = ckv.astype(BF16)
    lat_ref[:, A_KV_LORA:A_KV_LORA + A_ROPE] = kpe.astype(BF16)
    lat_ref[:, A_KV_LORA + A_ROPE:] = jnp.zeros(
        (lat_ref.shape[0], A_LAT_PAD - A_KV_LORA - A_ROPE), BF16)
    sw0 = A_HEADS * A_QK_PAD
    for h in range(A_HEADS):
        lo, hi = h * A_QK_PAD, (h + 1) * A_QK_PAD
        qp = jnp.dot(cq, wq_ref[:, lo:hi], preferred_element_type=F32)
        qs = jnp.dot(cq, wq_ref[:, sw0 + lo:sw0 + hi], preferred_element_type=F32)
        q_ref[:, lo:hi] = (qp * cq_ref[...] + qs * sq_ref[...]).astype(BF16)


def _mla_proj(xb, w1, gq, gkv, wq, cosq, sinq, cosk, sink, tm=512):
    m = xb.shape[0]
    tm = _row_tile(m, tm)
    row = lambda i: (i, 0)
    fixed = lambda i: (0, 0)
    nq = A_HEADS * A_QK_PAD
    return pl.pallas_call(
        _mla_proj_kernel,
        grid=(m // tm,),
        in_specs=[pl.BlockSpec((tm, D_MODEL), row), pl.BlockSpec(w1.shape, fixed),
                  pl.BlockSpec((1, A_Q_LORA), fixed), pl.BlockSpec((1, A_KV_LORA), fixed),
                  pl.BlockSpec(wq.shape, fixed),
                  pl.BlockSpec((tm, A_QK_PAD), row), pl.BlockSpec((tm, A_QK_PAD), row),
                  pl.BlockSpec((tm, A_ROPE), row), pl.BlockSpec((tm, A_ROPE), row)],
        out_specs=[pl.BlockSpec((tm, nq), row), pl.BlockSpec((tm, A_LAT_PAD), row),
                   pl.BlockSpec((tm, A_KV_LORA), row), pl.BlockSpec((tm, A_ROPE), row)],
        out_shape=[jax.ShapeDtypeStruct((m, nq), BF16), jax.ShapeDtypeStruct((m, A_LAT_PAD), BF16),
                   jax.ShapeDtypeStruct((m, A_KV_LORA), F32), jax.ShapeDtypeStruct((m, A_ROPE), F32)],
        compiler_params=_params("parallel"),
        name="mla_proj",
    )(xb, w1, gq.reshape(1, -1), gkv.reshape(1, -1), wq, cosq, sinq, cosk, sink)


def _flash_kernel(q_ref, k_ref, v_ref, o_ref, m_s, l_s, acc_s, *, nh, dq, dv, tq, tk, nk, q_off,
                  chunk_causal):
    iq = pl.program_id(2)
    ik = pl.program_id(3)
    q_lo = q_off + iq * tq
    q_hi = q_lo + tq - 1
    if chunk_causal:
        vis_lo = (q_lo // CHUNK) * CHUNK + CHUNK - 1
        vis_hi = (q_hi // CHUNK) * CHUNK + CHUNK - 1
    else:
        vis_lo, vis_hi = q_lo, q_hi
    k_lo = ik * tk
    needed = k_lo <= vis_hi
    unmasked = k_lo + tk - 1 <= vis_lo

    @pl.when(ik == 0)
    def _init():
        m_s[...] = jnp.full(m_s.shape, -jnp.inf, F32)
        l_s[...] = jnp.zeros(l_s.shape, F32)
        acc_s[...] = jnp.zeros(acc_s.shape, F32)

    def step(masked):
        if masked:
            q_pos = q_lo + lax.broadcasted_iota(jnp.int32, (tq, tk), 0)
            k_pos = k_lo + lax.broadcasted_iota(jnp.int32, (tq, tk), 1)
            if chunk_causal:
                shift = CHUNK.bit_length() - 1
                mask = jnp.right_shift(k_pos, shift) <= jnp.right_shift(q_pos, shift)
            else:
                mask = k_pos <= q_pos
        for h in range(nh):
            q = q_ref[:, h * dq:(h + 1) * dq]
            k = k_ref[:, h * dq:(h + 1) * dq]
            v = v_ref[:, h * dv:(h + 1) * dv]
            s = lax.dot_general(q, k, (((1,), (1,)), ((), ())), preferred_element_type=F32)
            if masked:
                s = jnp.where(mask, s, NEG_INF)
            m_prev = m_s[h]
            m_new = jnp.maximum(m_prev, jnp.max(s, axis=-1, keepdims=True))
            alpha = jnp.exp(m_prev - m_new)
            p = jnp.exp(s - m_new)
            l_s[h] = alpha * l_s[h] + jnp.sum(p, axis=-1, keepdims=True)
            acc_s[h] = alpha * acc_s[h] + jnp.dot(p.astype(BF16), v, preferred_element_type=F32)
            m_s[h] = m_new

    @pl.when(needed & unmasked)
    def _plain():
        step(False)

    @pl.when(needed & jnp.logical_not(unmasked))
    def _masked():
        step(True)

    @pl.when(ik == nk - 1)
    def _fin():
        for h in range(nh):
            o_ref[:, h * dv:(h + 1) * dv] = (acc_s[h] / l_s[h]).astype(o_ref.dtype)


def _flash(q, k, v, *, nh, dq, dv, n_hblk, k_col0, v_col0, tq, tk, q_off, chunk_causal):
    b, t_q = q.shape[0], q.shape[1]
    t_k = k.shape[1]
    tq = tq if t_q % tq == 0 else t_q
    tk = tk if t_k % tk == 0 else t_k
    nq, nk = t_q // tq, t_k // tk

    def last_blk(i):
        q_hi = q_off + (i + 1) * tq - 1
        vis = (q_hi // CHUNK) * CHUNK + CHUNK - 1 if chunk_causal else q_hi
        return jnp.minimum(vis // tk, nk - 1)

    in_specs = [
        pl.BlockSpec((None, tq, nh * dq), lambda bi, h, i, j: (bi, i, h)),
        pl.BlockSpec((None, tk, nh * dq), lambda bi, h, i, j: (bi, jnp.minimum(j, last_blk(i)), k_col0 + h)),
        pl.BlockSpec((None, tk, nh * dv), lambda bi, h, i, j: (bi, jnp.minimum(j, last_blk(i)), v_col0 + h)),
    ]
    kern = functools.partial(_flash_kernel, nh=nh, dq=dq, dv=dv, tq=tq, tk=tk, nk=nk, q_off=q_off,
                             chunk_causal=chunk_causal)
    return pl.pallas_call(
        kern,
        grid=(b, n_hblk, nq, nk),
        in_specs=in_specs,
        out_specs=pl.BlockSpec((None, tq, nh * dv), lambda bi, h, i, j: (bi, i, h)),
        out_shape=jax.ShapeDtypeStruct((b, t_q, n_hblk * nh * dv), BF16),
        scratch_shapes=[pltpu.VMEM((nh, tq, 1), F32), pltpu.VMEM((nh, tq, 1), F32),
                        pltpu.VMEM((nh, tq, dv), F32)],
        compiler_params=_params("parallel", "parallel", "parallel", "arbitrary"),
        name="flash_step",
    )(q, k, v)


FLASH_T = 1024
FLASH_QC = 256
FLASH_KC = 256
FLASH_AHEAD = 8
A_HEADS_PER_STEP = 2
C_HEADS_PER_STEP = 4
LOG2E = math.log2(math.e)


def _flash_t_kernel(q_ref, k_ref, vt_ref, o_ref, m_s, acc_s, *, nh, dq, dv, t, chunk_causal):
    iq = pl.program_id(2)
    dva = dv + ONES_ROWS
    m_s[...] = jnp.full(m_s.shape, -jnp.inf, F32)
    acc_s[...] = jnp.zeros(acc_s.shape, F32)

    def block(j, masked):
        row0 = pl.multiple_of(j * t, t)
        qc, kc = FLASH_QC, FLASH_KC

        def n_keys(g, c):
            return min(kc, (c + 1) * qc - g * kc) if masked else kc

        chains = [(g, h, c) for g in range(t // kc) for h in range(nh) for c in range(t // qc)
                  if n_keys(g, c) > 0]

        def qk(g, h, c):
            return lax.dot_general(k_ref[pl.ds(row0 + g * kc, n_keys(g, c)), h * dq:(h + 1) * dq],
                                   q_ref[c * qc:(c + 1) * qc, h * dq:(h + 1) * dq],
                                   (((1,), (1,)), ((), ())), preferred_element_type=F32)

        def softmax_pv(g, h, c, s):
            cols = slice(c * qc, (c + 1) * qc)
            nk = n_keys(g, c)
            if masked and g * kc + nk > c * qc:
                k_pos = lax.broadcasted_iota(jnp.int32, (nk, qc), 0) + g * kc
                q_pos = lax.broadcasted_iota(jnp.int32, (nk, qc), 1) + c * qc
                if chunk_causal:
                    shift = CHUNK.bit_length() - 1
                    mask = jnp.right_shift(k_pos, shift) <= jnp.right_shift(q_pos, shift)
                else:
                    mask = k_pos <= q_pos
                s = jnp.where(mask, s, NEG_INF)
            m_prev = m_s[h, :, cols]
            m_new = jnp.maximum(m_prev, jnp.max(s, axis=0, keepdims=True))
            alpha = jnp.exp2(m_prev - m_new)
            p = jnp.exp2(s - m_new)
            acc_s[h, :, cols] = alpha * acc_s[h, :, cols] + jnp.dot(
                vt_ref[j, h * dva:(h + 1) * dva, g * kc:g * kc + nk], p.astype(BF16),
                preferred_element_type=F32)
            m_s[h, :, cols] = m_new

        pending = [qk(*chain) for chain in chains[:FLASH_AHEAD]]
        for i, chain in enumerate(chains):
            if i + FLASH_AHEAD < len(chains):
                pending.append(qk(*chains[i + FLASH_AHEAD]))
            softmax_pv(*chain, pending.pop(0))

    def full_block(j, carry):
        block(j, False)
        return carry

    lax.fori_loop(0, iq, full_block, 0)
    block(iq, True)
    out = jnp.concatenate([acc_s[h, 0:dv] / acc_s[h, dv:dv + 1] for h in range(nh)], axis=0)
    o_ref[...] = out.T.astype(o_ref.dtype)


def _flash_t(q, k, k_col0, vt, *, nh, dq, dv, n_hblk, chunk_causal):
    b, t_all = q.shape[0], q.shape[1]
    t = FLASH_T
    nblk = t_all // t
    kern = functools.partial(_flash_t_kernel, nh=nh, dq=dq, dv=dv, t=t, chunk_causal=chunk_causal)
    return pl.pallas_call(
        kern,
        grid=(b, n_hblk, nblk),
        in_specs=[pl.BlockSpec((None, t, nh * dq), lambda bi, h, i: (bi, i, h)),
                  pl.BlockSpec((None, t_all, nh * dq), lambda bi, h, i: (bi, 0, k_col0 + h),
                               pipeline_mode=pl.Buffered(1)),
                  pl.BlockSpec((None, nblk, nh * (dv + ONES_ROWS), t), lambda bi, h, i: (bi, 0, h, 0),
                               pipeline_mode=pl.Buffered(1))],
        out_specs=pl.BlockSpec((None, t, nh * dv), lambda bi, h, i: (bi, i, h)),
        out_shape=jax.ShapeDtypeStruct((b, t_all, n_hblk * nh * dv), BF16),
        scratch_shapes=[pltpu.VMEM((nh, 1, t), F32), pltpu.VMEM((nh, dv + ONES_ROWS, t), F32)],
        compiler_params=_params("parallel", "parallel", "parallel"),
        name="flash_t_chunk" if chunk_causal else "flash_t_frame",
    )(q, k, vt)


B_HEADS_PER_STEP = LANES // B_HEAD_DIM


def _step_attn_kernel(*refs, heads, dh, forget):
    if forget:
        q_ref, kp_ref, vp_ref, kn_ref, vn_ref, bp_ref, bn_ref, fq_ref, fkp_ref, fkn_ref, o_ref = refs
    else:
        q_ref, kp_ref, vp_ref, kn_ref, vn_ref, bp_ref, bn_ref, o_ref = refs
    nt = (((1,), (1,)), ((), ()))
    n_bias = bp_ref.shape[0]
    for h in range(heads):
        cols = slice(h * dh, (h + 1) * dh)
        q = q_ref[:, cols]
        sp = lax.dot_general(q, kp_ref[:, cols].astype(BF16), nt, preferred_element_type=F32)
        sn = lax.dot_general(q, kn_ref[:, cols], nt, preferred_element_type=F32)
        sp = sp + bp_ref[h % n_bias]
        sn = sn + bn_ref[h % n_bias]
        if forget:
            fq = fq_ref[:, h:h + 1]
            sp = sp + fq - fkp_ref[h:h + 1, :]
            sn = sn + fq - fkn_ref[h:h + 1, :]
        m = jnp.maximum(jnp.max(sp, axis=-1, keepdims=True), jnp.max(sn, axis=-1, keepdims=True))
        pp = jnp.exp(sp - m)
        pn = jnp.exp(sn - m)
        l = jnp.sum(pp, axis=-1, keepdims=True) + jnp.sum(pn, axis=-1, keepdims=True)
        o = (jnp.dot(pp.astype(BF16), vp_ref[:, cols].astype(BF16), preferred_element_type=F32)
             + jnp.dot(pn.astype(BF16), vn_ref[:, cols], preferred_element_type=F32))
        o_ref[:, cols] = (o / l).astype(o_ref.dtype)


def _band_bias(rel_bias, q_pos, k_pos):
    nq, nk = len(q_pos), len(k_pos)
    assert (np.diff(q_pos) == 1).all() and (np.diff(k_pos) == 1).all()
    m = np.arange(nq + nk - 1)
    u = rel_bias[:, np.clip(q_pos[0] - k_pos[0] + nq - 1 - m, -B_REL_CLIP, B_REL_CLIP) + B_REL_CLIP]
    period = nq + nk
    w = jnp.concatenate([u[:, nq - 1:], jnp.zeros((u.shape[0], 1), u.dtype), u[:, :nq - 1]], axis=1)
    skew = jnp.tile(w, (1, nq))[:, :nq * (period - 1)].reshape(-1, nq, period - 1)[:, :, :nk]
    qc = q_pos[:, None] // CHUNK
    kc = k_pos[None, :] // CHUNK
    mask = (kc <= qc) & (kc >= qc - B_LEFT_CHUNKS) & (k_pos[None, :] >= 0)
    return jnp.where(jnp.asarray(mask)[None], skew, NEG_INF).astype(F32)


def _step_attn(q, k_past, v_past, k_new, v_new, bias, heads, f_cum=None):
    b, t, d = q.shape
    p = k_past.shape[1]
    stream = lambda bi: (bi, 0, 0)
    fixed = lambda bi: (0, 0, 0)
    args = [q, k_past, v_past, k_new, v_new, bias[:, :, :p], bias[:, :, p:]]
    in_specs = [pl.BlockSpec((None, t, d), stream), pl.BlockSpec((None, p, d), stream),
                pl.BlockSpec((None, p, d), stream), pl.BlockSpec((None, t, d), stream),
                pl.BlockSpec((None, t, d), stream),
                pl.BlockSpec((bias.shape[0], t, p), fixed), pl.BlockSpec((bias.shape[0], t, t), fixed)]
    if f_cum is not None:
        fq = jnp.pad(f_cum[:, p:], ((0, 0), (0, 0), (0, LANES - heads)))
        fk = f_cum.transpose(0, 2, 1)
        args += [fq, fk[:, :, :p], fk[:, :, p:]]
        in_specs += [pl.BlockSpec((None, t, LANES), stream), pl.BlockSpec((None, heads, p), stream),
                     pl.BlockSpec((None, heads, t), stream)]
    return pl.pallas_call(
        functools.partial(_step_attn_kernel, heads=heads, dh=d // heads, forget=f_cum is not None),
        grid=(b,),
        in_specs=in_specs,
        out_specs=pl.BlockSpec((None, t, d), stream),
        out_shape=jax.ShapeDtypeStruct((b, t, d), BF16),
        compiler_params=_params("parallel"),
        name="step_attn",
    )(*args)


BAND_T = 4 * CHUNK


def _band_t_kernel(*refs, nkb, t):
    q_ref = refs[0]
    k_refs = refs[1:1 + nkb]
    vt_refs = refs[1 + nkb:1 + 2 * nkb]
    bias_ref, o_ref = refs[1 + 2 * nkb], refs[2 + 2 * nkb]
    iq = pl.program_id(1)
    dh = B_HEAD_DIM

    def qk(h):
        cols = slice(h * dh, (h + 1) * dh)
        parts = []
        for j in range(nkb):
            s = lax.dot_general(k_refs[j][:, cols], q_ref[:, cols], (((1,), (1,)), ((), ())),
                                preferred_element_type=F32)
            if j < nkb - 1:
                s = jnp.where(iq >= nkb - 1 - j, s, NEG_INF)
            parts.append(s)
        return jnp.concatenate(parts, axis=0) + bias_ref[h]

    def softmax_pv(h, s):
        p = jnp.exp2(s - jnp.max(s, axis=0, keepdims=True))
        l = jnp.sum(p, axis=0, keepdims=True)
        pb = p.astype(BF16)
        o = functools.reduce(lambda a, c: a + c, [
            jnp.dot(vt_refs[j][h * dh:(h + 1) * dh, :], pb[j * t:(j + 1) * t], preferred_element_type=F32)
            for j in range(nkb)])
        return o / l

    heads = list(range(B_HEADS))
    pending = [qk(h) for h in heads[:FLASH_AHEAD]]
    outs = []
    for h in heads:
        if h + FLASH_AHEAD < B_HEADS:
            pending.append(qk(h + FLASH_AHEAD))
        outs.append(softmax_pv(h, pending.pop(0)))
        if len(outs) == B_HEADS_PER_STEP:
            c0 = (h + 1 - B_HEADS_PER_STEP) * dh
            o_ref[:, c0:c0 + LANES] = jnp.concatenate(outs, axis=0).T.astype(o_ref.dtype)
            outs = []


def _band_t(q, kv, vt, bias_t, *, nkb):
    b, t_all = q.shape[0], q.shape[1]
    t = BAND_T

    def back(j):
        return nkb - 1 - j

    k_specs = [pl.BlockSpec((None, t, D_MODEL), lambda bi, i, j=j: (bi, jnp.maximum(i - back(j), 0), 0))
               for j in range(nkb)]
    vt_specs = [pl.BlockSpec((None, None, D_MODEL, t),
                             lambda bi, i, j=j: (bi, jnp.maximum(i - back(j), 0), 0, 0))
                for j in range(nkb)]
    return pl.pallas_call(
        functools.partial(_band_t_kernel, nkb=nkb, t=t),
        grid=(b, t_all // t),
        in_specs=([pl.BlockSpec((None, t, D_MODEL), lambda bi, i: (bi, i, 0))] + k_specs + vt_specs
                  + [pl.BlockSpec(bias_t.shape, lambda bi, i: (0, 0, 0), pipeline_mode=pl.Buffered(1))]),
        out_specs=pl.BlockSpec((None, t, D_MODEL), lambda bi, i: (bi, i, 0)),
        out_shape=jax.ShapeDtypeStruct((b, t_all, D_MODEL), BF16),
        compiler_params=_params("parallel", "parallel"),
        name="band_t",
    )(q, *([kv] * nkb), *([vt] * nkb), bias_t)


def _logf_kernel(x_ref, w_ref, b_ref, o_ref):
    z = jnp.dot(x_ref[...], w_ref[...], preferred_element_type=F32) + b_ref[...]
    o_ref[...] = -(jnp.maximum(-z, 0.0) + jnp.log1p(jnp.exp(-jnp.abs(z))))


def _logf(xb, w_pad, b_pad, tm=512):
    m = xb.shape[0]
    tm = _row_tile(m, tm)
    return pl.pallas_call(
        _logf_kernel,
        grid=(m // tm,),
        in_specs=[pl.BlockSpec((tm, D_MODEL), lambda i: (i, 0)),
                  pl.BlockSpec((D_MODEL, LANES), lambda i: (0, 0)),
                  pl.BlockSpec((1, LANES), lambda i: (0, 0))],
        out_specs=pl.BlockSpec((tm, LANES), lambda i: (i, 0)),
        out_shape=jax.ShapeDtypeStruct((m, LANES), F32),
        compiler_params=_params("parallel"),
        name="logf",
    )(xb, w_pad, b_pad)


def _split3(x):
    hi = x.astype(BF16)
    r = x - hi.astype(F32)
    mid = r.astype(BF16)
    lo = (r - mid.astype(F32)).astype(BF16)
    return hi, mid, lo


def _cumsum_kernel(x_ref, o_ref, hi_ref, mid_ref, lo_ref, carry, *, tc):
    @pl.when(pl.program_id(1) == 0)
    def _():
        carry[...] = jnp.zeros(carry.shape, F32)

    tri = (lax.broadcasted_iota(jnp.int32, (tc, tc), 0)
           >= lax.broadcasted_iota(jnp.int32, (tc, tc), 1)).astype(BF16)
    c = functools.reduce(lambda a, b: a + b, [jnp.dot(tri, piece, preferred_element_type=F32)
                                              for piece in _split3(x_ref[...])])
    out = c + carry[0:1, :]
    o_ref[...] = out
    hi_ref[...], mid_ref[...], lo_ref[...] = _split3(out * LOG2E)
    carry[...] = jnp.broadcast_to(out[tc - 1:tc, :], carry.shape)


def _cumsum(x, tc=256):
    b, t, _ = x.shape
    spec = pl.BlockSpec((None, tc, LANES), lambda bi, i: (bi, i, 0))
    piece = jax.ShapeDtypeStruct(x.shape, BF16)
    return pl.pallas_call(
        functools.partial(_cumsum_kernel, tc=tc),
        grid=(b, t // tc),
        in_specs=[spec],
        out_specs=[spec] * 4,
        out_shape=[jax.ShapeDtypeStruct(x.shape, F32), piece, piece, piece],
        scratch_shapes=[pltpu.VMEM((SUBLANES, LANES), F32)],
        compiler_params=_params("parallel", "arbitrary"),
        name="cumsum",
    )(x)


FFN_CHUNK = 256
CONV_ROWS = SUBLANES


def _ffn_up_kernel(x_ref, w_ref, c_ref, p_ref, act_ref, s_ref, halo, *, tm):
    @pl.when(pl.program_id(1) == 0)
    def _():
        halo[...] = p_ref[...]

    x = x_ref[...]
    n_chunks = D_FF // FFN_CHUNK
    groups = tm // CONV_ROWS
    row = lax.broadcasted_iota(jnp.int32, (groups, CONV_ROWS, FFN_CHUNK), 1)

    def up(c, half):
        col = half * D_FF + c * FFN_CHUNK
        return jnp.dot(x, w_ref[:, col:col + FFN_CHUNK], preferred_element_type=F32)

    def conv(h, c, half):
        cols = slice(half * D_FF + c * FFN_CHUNK, half * D_FF + (c + 1) * FFN_CHUNK)
        ext = jnp.concatenate([halo[:, cols], h], axis=0).reshape(groups + 1, CONV_ROWS, FFN_CHUNK)
        hc = c_ref[CONV_W:CONV_W + 1, cols] + c_ref[CONV_W - 1:CONV_W, cols] * h
        for s in range(1, CONV_W):
            rot = pltpu.roll(ext, s, axis=1)
            shifted = jnp.where(row < s, rot[:groups], rot[1:]).reshape(tm, FFN_CHUNK)
            hc = hc + c_ref[CONV_W - 1 - s:CONV_W - s, cols] * shifted
        tail = h[tm - CONV_ROWS:tm]
        halo[:, cols] = tail
        s_ref[:, cols] = tail
        return hc

    pending = [(up(0, 0), up(0, 1))]
    for c in range(n_chunks):
        if c + 1 < n_chunks:
            pending.append((up(c + 1, 0), up(c + 1, 1)))
        ha, hg = pending.pop(0)
        a = conv(ha, c, 0)
        g = conv(hg, c, 1)
        act_ref[:, c * FFN_CHUNK:(c + 1) * FFN_CHUNK] = (g * jax.nn.sigmoid(g) * a).astype(BF16)


def _ffn_up(xb, w_up, conv_tab, past, tm=256):
    b, t, _ = xb.shape
    tm = _row_tile(t, tm)
    fixed = lambda bi, ti: (0, 0)
    return pl.pallas_call(
        functools.partial(_ffn_up_kernel, tm=tm),
        grid=(b, t // tm),
        in_specs=[pl.BlockSpec((None, tm, D_MODEL), lambda bi, ti: (bi, ti, 0)),
                  pl.BlockSpec((D_MODEL, 2 * D_FF), fixed),
                  pl.BlockSpec((CONV_ROWS, 2 * D_FF), fixed),
                  pl.BlockSpec((None, CONV_ROWS, 2 * D_FF), lambda bi, ti: (bi, 0, 0))],
        out_specs=[pl.BlockSpec((None, tm, D_FF), lambda bi, ti: (bi, ti, 0)),
                   pl.BlockSpec((None, CONV_ROWS, 2 * D_FF), lambda bi, ti: (bi, 0, 0))],
        out_shape=[jax.ShapeDtypeStruct((b, t, D_FF), BF16),
                   jax.ShapeDtypeStruct((b, CONV_ROWS, 2 * D_FF), F32)],
        scratch_shapes=[pltpu.VMEM((CONV_ROWS, 2 * D_FF), F32)],
        compiler_params=_params("parallel", "arbitrary"),
        name="ffn_up",
    )(xb, w_up, conv_tab, past)


def _rope_tables(pos, batch, q_scale):
    half = A_ROPE // 2
    inv_freq = ROPE_THETA ** (-jnp.arange(half, dtype=F32) / half)
    ang = pos.astype(F32)[:, None] * inv_freq
    cos, sin = jnp.cos(ang), jnp.sin(ang)
    cosk = jnp.concatenate([cos, cos], axis=-1)
    sink = jnp.concatenate([sin, sin], axis=-1)
    t = pos.shape[0]
    pad = jnp.zeros((t, A_QK_PAD - A_NOPE - A_ROPE), F32)
    cosq = q_scale * jnp.concatenate([jnp.ones((t, A_NOPE), F32), cosk, pad], axis=-1)
    sinq = q_scale * jnp.concatenate([jnp.zeros((t, A_NOPE), F32), sink, pad], axis=-1)
    return tuple(jnp.tile(a, (batch, 1)) for a in (cosq, sinq, cosk, sink))


def _swap_halves(w):
    half = w.shape[-1] // 2
    return jnp.concatenate([-w[..., half:], w[..., :half]], axis=-1)


def _mla_weights(w_dq, w_dkv, w_kr, w_uq, w_uk, w_uv):
    zc = jnp.zeros((D_MODEL, LANES - A_ROPE), F32)
    w1 = jnp.concatenate([w_dq, w_dkv, w_kr, zc, _swap_halves(w_kr), zc], axis=1).astype(BF16)
    wq = w_uq.reshape(A_Q_LORA, A_HEADS, A_NOPE + A_ROPE)
    nope, rope = wq[..., :A_NOPE], wq[..., A_NOPE:]
    zpad = jnp.zeros((A_Q_LORA, A_HEADS, A_QK_PAD - A_NOPE - A_ROPE), F32)
    w_cat = jnp.concatenate([nope, rope, zpad], axis=-1).reshape(A_Q_LORA, -1)
    w_sw = jnp.concatenate([jnp.zeros_like(nope), _swap_halves(rope), zpad], axis=-1).reshape(A_Q_LORA, -1)
    wq2 = jnp.concatenate([w_cat, w_sw], axis=1).astype(BF16)
    wk = jnp.zeros((A_LAT_PAD, A_HEADS, A_QK_PAD), F32)
    wk = wk.at[:A_KV_LORA, :, :A_NOPE].set(w_uk)
    eye = jnp.broadcast_to(jnp.eye(A_ROPE, dtype=F32)[:, None, :], (A_ROPE, A_HEADS, A_ROPE))
    wk = wk.at[A_KV_LORA:A_KV_LORA + A_ROPE, :, A_NOPE:A_NOPE + A_ROPE].set(eye)
    wv = jnp.zeros((A_LAT_PAD, A_HEADS * A_V), F32).at[:A_KV_LORA].set(w_uv.reshape(A_KV_LORA, -1))
    wkv = jnp.concatenate([wk.reshape(A_LAT_PAD, -1), wv], axis=1).astype(BF16)
    return w1, wq2, wkv


def _mla_mixer(xb, b, t, pos, ckv_past, kpe_past, w, i):
    w1, wq2, wkv = _mla_weights(w['a_w_dq'][i], w['a_w_dkv'][i], w['a_w_kr'][i], w['a_w_uq'][i],
                                w['a_w_uk'][i], w['a_w_uv'][i])
    resident = ckv_past is None and t % FLASH_T == 0
    q_scale = (A_NOPE + A_ROPE) ** -0.5 * (LOG2E if resident else 1.0)
    q, lat, ckv, kpe = _mla_proj(xb, w1, w['a_g_q'][i], w['a_g_kv'][i], wq2,
                                 *_rope_tables(pos, b, q_scale))
    lat = lat.reshape(b, t, A_LAT_PAD)
    q_off = 0
    if ckv_past is not None:
        p_len = ckv_past.shape[1]
        past = jnp.concatenate(
            [ckv_past, kpe_past, jnp.zeros((b, p_len, A_LAT_PAD - A_KV_LORA - A_ROPE), F32)], axis=-1)
        lat = jnp.concatenate([past.astype(BF16), lat], axis=1)
        q_off = p_len
    t_k = lat.shape[1]
    lat2 = lat.reshape(b * t_k, A_LAT_PAD)
    n_k = A_HEADS * A_QK_PAD
    if resident:
        (k_cat,) = _mm(lat2, wkv[:, :n_k], [BF16])
        vt = _proj_t(lat2, wkv[:, n_k:], FLASH_T, A_V).reshape(b, t_k // FLASH_T, -1, FLASH_T)
        o = _flash_t(q.reshape(b, t, -1), k_cat.reshape(b, t_k, n_k), 0, vt, nh=A_HEADS_PER_STEP,
                     dq=A_QK_PAD, dv=A_V, n_hblk=A_HEADS // A_HEADS_PER_STEP, chunk_causal=True)
    else:
        (kv,) = _mm(lat2, wkv, [BF16])
        kv = kv.reshape(b, t_k, -1)
        o = _flash(q.reshape(b, t, -1), kv, kv, nh=A_HEADS, dq=A_QK_PAD, dv=A_V, n_hblk=1,
                   k_col0=0, v_col0=A_QK_PAD // A_V, tq=512, tk=512, q_off=q_off, chunk_causal=True)
    return o.reshape(b * t, A_HEADS * A_V), ckv.reshape(b, t, -1), kpe.reshape(b, t, -1)


def _qkv(xb, w_qkv, q_scale):
    wq = (w_qkv[:, :D_MODEL] * q_scale).astype(BF16)
    (q,) = _mm(xb, wq, [BF16])
    kv32, kvb = _mm(xb, w_qkv[:, D_MODEL:].astype(BF16), [F32, BF16])
    return q, kv32, kvb


def _band_mixer(xb, b, t, pos0, k_past, v_past, w, i):
    prompt = k_past is None
    assert not prompt or t % BAND_T == 0
    q, kv32, kvb = _qkv(xb, w['b_w_qkv'][i], B_HEAD_DIM ** -0.5 * (LOG2E if prompt else 1.0))
    q = q.reshape(b, t, D_MODEL)
    k32 = kv32[:, :D_MODEL].reshape(b, t, B_HEADS, B_HEAD_DIM)
    v32 = kv32[:, D_MODEL:].reshape(b, t, B_HEADS, B_HEAD_DIM)
    if prompt:
        nkb = B_WIN // BAND_T + 1
        bias = _band_bias(w['b_rel_bias'][i], B_WIN + np.arange(BAND_T), np.arange(B_WIN + BAND_T))
        kvb = kvb.reshape(b, t, 2 * D_MODEL)
        vt = _proj_t(xb, w['b_w_qkv'][i][:, 2 * D_MODEL:], BAND_T).reshape(b, t // BAND_T, D_MODEL, BAND_T)
        o = _band_t(q, kvb, vt, LOG2E * bias.transpose(0, 2, 1), nkb=nkb)
        keep = min(B_WIN, t)
        k_new, v_new = k32[:, t - keep:], v32[:, t - keep:]
    else:
        p_len = k_past.shape[1]
        kvb = kvb.reshape(b, t, 2 * D_MODEL)
        q_pos = pos0 + np.arange(t)
        k_pos = np.concatenate([np.arange(pos0 - p_len, pos0), q_pos])
        bias = _band_bias(w['b_rel_bias'][i], q_pos, k_pos)
        o = _step_attn(q, k_past.reshape(b, p_len, D_MODEL), v_past.reshape(b, p_len, D_MODEL),
                       kvb[:, :, :D_MODEL], kvb[:, :, D_MODEL:], bias, B_HEADS)
        k_new, v_new = k32, v32
    return o.reshape(b * t, D_MODEL), k_new, v_new


C_AUG = LANES
N_PIECES = 3


def _aug_kernel(x_ref, w_ref, hi_ref, mid_ref, lo_ref, p_ref, b_ref, o_ref):
    acc = jnp.dot(x_ref[...], w_ref[...], preferred_element_type=F32) + b_ref[...]
    for piece, f_ref in enumerate((hi_ref, mid_ref, lo_ref)):
        acc = acc + jnp.dot(f_ref[...], p_ref[piece], preferred_element_type=F32)
    o_ref[...] = acc.astype(BF16)


def _aug_tables():
    n = C_HEADS * C_AUG
    place = np.zeros((N_PIECES, LANES, 2 * n), np.float32)
    ones = np.zeros((1, 2 * n), np.float32)
    for h in range(C_HEADS):
        base = h * C_AUG + C_HEAD_DIM
        for piece in range(N_PIECES):
            ones[0, base + piece] = 1.0
            place[piece, h, base + N_PIECES + piece] = 1.0
            place[piece, h, n + base + piece] = -1.0
            ones[0, n + base + N_PIECES + piece] = 1.0
    return jnp.asarray(place, BF16), jnp.asarray(ones)


def _aug_qk(xb, w_aug, pieces, tm=512, tn=1024):
    m = xb.shape[0]
    n = w_aug.shape[1]
    tm = _row_tile(m, tm)
    place, ones = _aug_tables()
    row = lambda i, j: (i, 0)
    return pl.pallas_call(
        _aug_kernel,
        grid=(m // tm, n // tn),
        in_specs=[pl.BlockSpec((tm, D_MODEL), row), pl.BlockSpec((D_MODEL, tn), lambda i, j: (0, j)),
                  pl.BlockSpec((tm, LANES), row), pl.BlockSpec((tm, LANES), row), pl.BlockSpec((tm, LANES), row),
                  pl.BlockSpec((N_PIECES, LANES, tn), lambda i, j: (0, 0, j)),
                  pl.BlockSpec((1, tn), lambda i, j: (0, j))],
        out_specs=pl.BlockSpec((tm, tn), lambda i, j: (i, j)),
        out_shape=jax.ShapeDtypeStruct((m, n), BF16),
        compiler_params=_params("parallel", "parallel"),
        name="aug_qk",
    )(xb, w_aug, *pieces, place, ones)


def _pad_heads(w, scale):
    w = (w * scale).reshape(D_MODEL, C_HEADS, C_HEAD_DIM)
    return jnp.pad(w, ((0, 0), (0, 0), (0, C_AUG - C_HEAD_DIM))).reshape(D_MODEL, C_HEADS * C_AUG)


def _fox_mixer(xb, b, t, k_past, v_past, lf_past, w, i):
    resident = k_past is None
    assert not resident or t % FLASH_T == 0
    w_qkv = w['c_w_qkv'][i]
    k32, *kb = _mm(xb, w_qkv[:, D_MODEL:2 * D_MODEL].astype(BF16), [F32] if resident else [F32, BF16])
    v32, *vb = _mm(xb, w_qkv[:, 2 * D_MODEL:].astype(BF16), [F32] if resident else [F32, BF16])
    k32 = k32.reshape(b, t, C_HEADS, C_HEAD_DIM)
    v32 = v32.reshape(b, t, C_HEADS, C_HEAD_DIM)
    w_f = jnp.zeros((D_MODEL, LANES), F32).at[:, :C_HEADS].set(w['c_w_f'][i]).astype(BF16)
    b_f = jnp.zeros((1, LANES), F32).at[0, :C_HEADS].set(w['c_b_f'][i])
    log_f = _logf(xb, w_f, b_f).reshape(b, t, LANES)
    lf_all = log_f
    if not resident:
        lf_all = jnp.concatenate([jnp.pad(lf_past, ((0, 0), (0, 0), (0, LANES - C_HEADS))), log_f], axis=1)
    t_k = lf_all.shape[1]
    tc = 256
    t_pad = -(-t_k // tc) * tc
    f_cum, *f_pieces = _cumsum(jnp.pad(lf_all, ((0, 0), (0, t_pad - t_k), (0, 0))), tc)
    nh = C_HEADS_PER_STEP
    n_hblk = C_HEADS // nh
    q_scale = C_HEAD_DIM ** -0.5
    if resident:
        w_aug = jnp.concatenate([_pad_heads(w_qkv[:, :D_MODEL], q_scale * LOG2E),
                                 _pad_heads(w_qkv[:, D_MODEL:2 * D_MODEL], 1.0)], axis=1).astype(BF16)
        qk_aug = _aug_qk(xb, w_aug, [p.reshape(b * t, LANES) for p in f_pieces]).reshape(b, t, -1)
        n_q = C_HEADS * C_AUG
        vt = _proj_t(xb, w_qkv[:, 2 * D_MODEL:], FLASH_T, C_HEAD_DIM).reshape(b, t // FLASH_T, -1, FLASH_T)
        o = _flash_t(qk_aug, qk_aug, n_q // (nh * C_AUG), vt,
                     nh=nh, dq=C_AUG, dv=C_HEAD_DIM, n_hblk=n_hblk, chunk_causal=False)
        return o.reshape(b * t, D_MODEL), k32, v32, log_f[:, :, :C_HEADS]
    (q,) = _mm(xb, (w_qkv[:, :D_MODEL] * q_scale).astype(BF16), [BF16])
    p_len = k_past.shape[1]
    causal = np.where(np.tril(np.ones((t, t), bool)), 0.0, NEG_INF).astype(np.float32)
    bias = jnp.asarray(np.concatenate([np.zeros((t, p_len), np.float32), causal], axis=1)[None])
    o = _step_attn(q.reshape(b, t, D_MODEL), k_past.reshape(b, p_len, D_MODEL),
                   v_past.reshape(b, p_len, D_MODEL), kb[0].reshape(b, t, D_MODEL),
                   vb[0].reshape(b, t, D_MODEL), bias, C_HEADS, f_cum=f_cum[:, :t_k, :C_HEADS])
    return o.reshape(b * t, D_MODEL), k32, v32, log_f[:, :, :C_HEADS]


def _conv_ffn(xb, b, t, conv_past, w, i):
    tab = jnp.concatenate([w['f_conv_w'][i], w['f_conv_b'][i][None],
                           jnp.zeros((CONV_ROWS - CONV_W - 1, 2 * D_FF), F32)], axis=0)
    if conv_past is None:
        past = jnp.zeros((b, CONV_ROWS, 2 * D_FF), F32)
    else:
        past = jnp.pad(conv_past, ((0, 0), (CONV_ROWS - (CONV_W - 1), 0), (0, 0)))
    act, tail = _ffn_up(xb.reshape(b, t, D_MODEL), w['f_w_up_bf16'][i], tab, past)
    return act.reshape(b * t, D_FF), tail[:, CONV_ROWS - (CONV_W - 1):]


def _trunk(x, pos0, past, w):
    b, t, _ = x.shape
    pos = pos0 + jnp.arange(t)
    xf = x.reshape(b * t, D_MODEL)
    xb = xf.astype(BF16)
    outs = {n: [] for n in ('a_ckv', 'a_kpe', 'b_k', 'b_v', 'c_k', 'c_v', 'c_logf', 'ffn_conv')}
    ia = ib = ic = 0
    get = lambda name, j: None if past is None else past[name][j]
    for i in range(DEPTH):
        kind = i % N_MIXERS
        if kind == 0:
            o, ckv, kpe = _mla_mixer(xb, b, t, pos, get('a_ckv', ia), get('a_kpe', ia), w, ia)
            outs['a_ckv'].append(ckv)
            outs['a_kpe'].append(kpe)
            w_o = w['a_w_o_bf16'][ia]
            ia += 1
        elif kind == 1:
            o, kb, vb = _band_mixer(xb, b, t, pos0, get('b_k', ib), get('b_v', ib), w, ib)
            outs['b_k'].append(kb)
            outs['b_v'].append(vb)
            w_o = w['b_w_o_bf16'][ib]
            ib += 1
        else:
            o, kc, vc, lf = _fox_mixer(xb, b, t, get('c_k', ic), get('c_v', ic), get('c_logf', ic), w, ic)
            outs['c_k'].append(kc)
            outs['c_v'].append(vc)
            outs['c_logf'].append(lf)
            w_o = w['c_w_o_bf16'][ic]
            ic += 1
        xf, xb = _mm_res_ln(o, w_o, xf, w['ln1_g'][i], w['ln1_b'][i])
        act, conv_state = _conv_ffn(xb, b, t, get('ffn_conv', i), w, i)
        outs['ffn_conv'].append(conv_state)
        xf, xb = _mm_res_ln(act, w['f_w_down_bf16'][i], xf, w['ln2_g'][i], w['ln2_b'][i])
    return xf.reshape(b, t, D_MODEL), {n: jnp.stack(v) for n, v in outs.items()}


def kernel(x_prompt, x_sample, cache_a_ckv, cache_a_kpe, cache_b_k, cache_b_v, cache_c_k, cache_c_v,
           cache_c_logf, state_ffn_conv, a_w_dq, a_g_q, a_w_uq, a_w_dkv, a_g_kv, a_w_kr, a_w_uk, a_w_uv,
           a_w_o, b_w_qkv, b_rel_bias, b_w_o, c_w_qkv, c_w_f, c_b_f, c_w_o, f_w_up, f_conv_w, f_conv_b,
           f_w_down, ln1_g, ln1_b, ln2_g, ln2_b):
    w = dict(a_w_dq=a_w_dq, a_g_q=a_g_q, a_w_uq=a_w_uq, a_w_dkv=a_w_dkv, a_g_kv=a_g_kv, a_w_kr=a_w_kr,
             a_w_uk=a_w_uk, a_w_uv=a_w_uv, a_w_o=a_w_o, b_w_qkv=b_w_qkv, b_rel_bias=b_rel_bias, b_w_o=b_w_o,
             c_w_qkv=c_w_qkv, c_w_f=c_w_f, c_b_f=c_b_f, c_w_o=c_w_o, f_w_up=f_w_up, f_conv_w=f_conv_w,
             f_conv_b=f_conv_b, f_w_down=f_w_down, ln1_g=ln1_g, ln1_b=ln1_b, ln2_g=ln2_g, ln2_b=ln2_b)
    past = dict(a_ckv=cache_a_ckv, a_kpe=cache_a_kpe, b_k=cache_b_k, b_v=cache_b_v, c_k=cache_c_k,
                c_v=cache_c_v, c_logf=cache_c_logf, ffn_conv=state_ffn_conv)
    past_len = cache_a_ckv.shape[2]
    for name in ('f_w_up', 'f_w_down', 'a_w_o', 'b_w_o', 'c_w_o'):
        w[name + '_bf16'] = _to_bf16(w[name])
    y_prompt, p = _trunk(x_prompt, 0, None, w)
    y_sample, s = _trunk(x_sample, past_len, past, w)
    names = ('a_ckv', 'a_kpe', 'b_k', 'b_v', 'c_k', 'c_v', 'c_logf', 'ffn_conv')
    return (y_prompt, y_sample) + tuple(p[n] for n in names) + tuple(s[n] for n in names)
```

```python
import functools
import math

import numpy as np
import jax
import jax.numpy as jnp
from jax import lax
from jax.experimental import pallas as pl
from jax.experimental.pallas import tpu as pltpu

F32 = jnp.float32
BF16 = jnp.bfloat16

D_MODEL = 1024
DEPTH = 4
CHUNK = 64
N_MIXERS = 3

A_HEADS = 8
A_Q_LORA = 384
A_KV_LORA = 256
A_NOPE = 128
A_ROPE = 64
A_V = 128
A_QK_PAD = 256
A_LAT_PAD = 384
ROPE_THETA = 10000.0

B_HEADS = 16
B_HEAD_DIM = D_MODEL // B_HEADS
B_LEFT_CHUNKS = 8
B_WIN = B_LEFT_CHUNKS * CHUNK
B_REL_CLIP = 128

C_HEADS = 16
C_HEAD_DIM = D_MODEL // C_HEADS

D_FF = 2816
CONV_W = 3

ALPHA = (2.0 * DEPTH) ** 0.25
LN_EPS = 1e-5
RMS_EPS = 1e-6
NEG_INF = -1e30

LANES = 128
SUBLANES = 8
VMEM_LIMIT_BYTES = 48 * 2 ** 20


def _params(*sem):
    return pltpu.CompilerParams(dimension_semantics=sem, vmem_limit_bytes=VMEM_LIMIT_BYTES)


def _row_tile(m, tm):
    while m % tm:
        tm //= 2
    assert tm % SUBLANES == 0, (m, tm)
    return tm


def _mm_kernel(x_ref, w_ref, *o_refs):
    acc = jnp.dot(x_ref[...], w_ref[...], preferred_element_type=F32)
    for o_ref in o_refs:
        o_ref[...] = acc.astype(o_ref.dtype)


def _mm(x, w, out_dtypes, tm=512, tn=1024):
    m, k = x.shape
    n = w.shape[1]
    tm, tn = _row_tile(m, tm), min(tn, n)
    return pl.pallas_call(
        _mm_kernel,
        grid=(m // tm, n // tn),
        in_specs=[pl.BlockSpec((tm, k), lambda i, j: (i, 0)),
                  pl.BlockSpec((k, tn), lambda i, j: (0, j))],
        out_specs=[pl.BlockSpec((tm, tn), lambda i, j: (i, j)) for _ in out_dtypes],
        out_shape=[jax.ShapeDtypeStruct((m, n), d) for d in out_dtypes],
        compiler_params=_params("parallel", "parallel"),
        name="mm",
    )(x, w)


ONES_ROWS = 16


def _proj_t_kernel(wt_ref, b_ref, x_ref, o_ref):
    o_ref[...] = (lax.dot_general(wt_ref[...], x_ref[...], (((1,), (1,)), ((), ())),
                                  preferred_element_type=F32) + b_ref[...]).astype(o_ref.dtype)


def _proj_t(x, w, t_blk, head_dim=None, tm=512):
    m, k = x.shape
    n = w.shape[1]
    bias = jnp.zeros((n, 1), F32)
    if head_dim is not None:
        heads = n // head_dim
        w = jnp.pad(w.reshape(k, heads, head_dim), ((0, 0), (0, 0), (0, ONES_ROWS)))
        bias = jnp.pad(jnp.zeros((heads, head_dim, 1), F32), ((0, 0), (0, ONES_ROWS), (0, 0)),
                       constant_values=1.0)
        n = heads * (head_dim + ONES_ROWS)
        w, bias = w.reshape(k, n), bias.reshape(n, 1)
    tm = min(tm, t_blk)
    per = t_blk // tm
    return pl.pallas_call(
        _proj_t_kernel,
        grid=(m // tm,),
        in_specs=[pl.BlockSpec((n, k), lambda i: (0, 0)), pl.BlockSpec((n, 1), lambda i: (0, 0)),
                  pl.BlockSpec((tm, k), lambda i: (i, 0))],
        out_specs=pl.BlockSpec((None, n, tm), lambda i: (i // per, 0, i % per)),
        out_shape=jax.ShapeDtypeStruct((m // t_blk, n, t_blk), BF16),
        compiler_params=_params("parallel"),
        name="proj_t",
    )(w.T.astype(BF16), bias, x)


def _cast_kernel(x_ref, o_ref):
    o_ref[...] = x_ref[...].astype(o_ref.dtype)


def _to_bf16(w, tr=256):
    n_l, r, c = w.shape
    tr = _row_tile(r, tr)
    spec = pl.BlockSpec((None, tr, c), lambda l, i: (l, i, 0))
    return pl.pallas_call(
        _cast_kernel,
        grid=(n_l, r // tr),
        in_specs=[spec],
        out_specs=spec,
        out_shape=jax.ShapeDtypeStruct(w.shape, BF16),
        compiler_params=_params("parallel", "parallel"),
        name="to_bf16",
    )(w)


LN_ROWS = 128


def _mm_res_ln_kernel(a_ref, w_ref, x_ref, g_ref, b_ref, of_ref, ob_ref):
    tm = a_ref.shape[0]
    pieces = [slice(r, min(r + LN_ROWS, tm)) for r in range(0, tm, LN_ROWS)]

    def mm(rows):
        return jnp.dot(a_ref[rows, :], w_ref[...], preferred_element_type=F32)

    def ln(rows, acc):
        y = ALPHA * x_ref[rows, :] + acc
        mu = jnp.mean(y, axis=-1, keepdims=True)
        d = y - mu
        var = jnp.mean(d * d, axis=-1, keepdims=True)
        out = d * lax.rsqrt(var + LN_EPS) * g_ref[...] + b_ref[...]
        of_ref[rows, :] = out
        ob_ref[rows, :] = out.astype(BF16)

    acc = mm(pieces[0])
    for i, rows in enumerate(pieces):
        nxt = mm(pieces[i + 1]) if i + 1 < len(pieces) else None
        ln(rows, acc)
        acc = nxt


def _mm_res_ln(a, w, x, g, b, tm=512):
    m, k = a.shape
    n = w.shape[1]
    tm = _row_tile(m, tm)
    row = lambda i: (i, 0)
    fixed = lambda i: (0, 0)
    return pl.pallas_call(
        _mm_res_ln_kernel,
        grid=(m // tm,),
        in_specs=[pl.BlockSpec((tm, k), row), pl.BlockSpec((k, n), fixed),
                  pl.BlockSpec((tm, n), row), pl.BlockSpec((1, n), fixed),
                  pl.BlockSpec((1, n), fixed)],
        out_specs=[pl.BlockSpec((tm, n), row), pl.BlockSpec((tm, n), row)],
        out_shape=[jax.ShapeDtypeStruct((m, n), F32), jax.ShapeDtypeStruct((m, n), BF16)],
        compiler_params=_params("parallel"),
        name="mm_res_ln",
    )(a, w, x, g.reshape(1, n), b.reshape(1, n))


_W1_CQ = (0, A_Q_LORA)
_W1_CKV = (A_Q_LORA, A_Q_LORA + A_KV_LORA)
_W1_KR = (_W1_CKV[1], _W1_CKV[1] + A_ROPE)
_W1_KRS = (_W1_CKV[1] + LANES, _W1_CKV[1] + LANES + A_ROPE)
_W1_COLS = _W1_CKV[1] + 2 * LANES


def _rms(v, g):
    return v * lax.rsqrt(jnp.mean(v * v, axis=-1, keepdims=True) + RMS_EPS) * g


def _mla_proj_kernel(x_ref, w1_ref, gq_ref, gkv_ref, wq_ref, cq_ref, sq_ref, ck_ref, sk_ref,
                     q_ref, lat_ref, ckv_ref, kpe_ref):
    y = jnp.dot(x_ref[...], w1_ref[...], preferred_element_type=F32)
    cq = _rms(y[:, _W1_CQ[0]:_W1_CQ[1]], gq_ref[...]).astype(BF16)
    ckv = _rms(y[:, _W1_CKV[0]:_W1_CKV[1]], gkv_ref[...])
    kpe = y[:, _W1_KR[0]:_W1_KR[1]] * ck_ref[...] + y[:, _W1_KRS[0]:_W1_KRS[1]] * sk_ref[...]
    ckv_ref[...] = ckv
    kpe_ref[...] = kpe
    lat_ref[:, 0:A_KV_LORA] = ckv.astype(BF16)
    lat_ref[:, A_KV_LORA:A_KV_LORA + A_ROPE] = kpe.astype(BF16)
    lat_ref[:, A_KV_LORA + A_ROPE:] = jnp.zeros(
        (lat_ref.shape[0], A_LAT_PAD - A_KV_LORA - A_ROPE), BF16)
    sw0 = A_HEADS * A_QK_PAD
    for h in range(A_HEADS):
        lo, hi = h * A_QK_PAD, (h + 1) * A_QK_PAD
        qp = jnp.dot(cq, wq_ref[:, lo:hi], preferred_element_type=F32)
        qs = jnp.dot(cq, wq_ref[:, sw0 + lo:sw0 + hi], preferred_element_type=F32)
        q_ref[:, lo:hi] = (qp * cq_ref[...] + qs * sq_ref[...]).astype(BF16)


def _mla_proj(xb, w1, gq, gkv, wq, cosq, sinq, cosk, sink, tm=512):
    m = xb.shape[0]
    tm = _row_tile(m, tm)
    row = lambda i: (i, 0)
    fixed = lambda i: (0, 0)
    nq = A_HEADS * A_QK_PAD
    return pl.pallas_call(
        _mla_proj_kernel,
        grid=(m // tm,),
        in_specs=[pl.BlockSpec((tm, D_MODEL), row), pl.BlockSpec(w1.shape, fixed),
                  pl.BlockSpec((1, A_Q_LORA), fixed), pl.BlockSpec((1, A_KV_LORA), fixed),
                  pl.BlockSpec(wq.shape, fixed),
                  pl.BlockSpec((tm, A_QK_PAD), row), pl.BlockSpec((tm, A_QK_PAD), row),
                  pl.BlockSpec((tm, A_ROPE), row), pl.BlockSpec((tm, A_ROPE), row)],
        out_specs=[pl.BlockSpec((tm, nq), row), pl.BlockSpec((tm, A_LAT_PAD), row),
                   pl.BlockSpec((tm, A_KV_LORA), row), pl.BlockSpec((tm, A_ROPE), row)],
        out_shape=[jax.ShapeDtypeStruct((m, nq), BF16), jax.ShapeDtypeStruct((m, A_LAT_PAD), BF16),
                   jax.ShapeDtypeStruct((m, A_KV_LORA), F32), jax.ShapeDtypeStruct((m, A_ROPE), F32)],
        compiler_params=_params("parallel"),
        name="mla_proj",
    )(xb, w1, gq.reshape(1, -1), gkv.reshape(1, -1), wq, cosq, sinq, cosk, sink)


def _flash_kernel(q_ref, k_ref, v_ref, o_ref, m_s, l_s, acc_s, *, nh, dq, dv, tq, tk, nk, q_off,
                  chunk_causal):
    iq = pl.program_id(2)
    ik = pl.program_id(3)
    q_lo = q_off + iq * tq
    q_hi = q_lo + tq - 1
    if chunk_causal:
        vis_lo = (q_lo // CHUNK) * CHUNK + CHUNK - 1
        vis_hi = (q_hi // CHUNK) * CHUNK + CHUNK - 1
    else:
        vis_lo, vis_hi = q_lo, q_hi
    k_lo = ik * tk
    needed = k_lo <= vis_hi
    unmasked = k_lo + tk - 1 <= vis_lo

    @pl.when(ik == 0)
    def _init():
        m_s[...] = jnp.full(m_s.shape, -jnp.inf, F32)
        l_s[...] = jnp.zeros(l_s.shape, F32)
        acc_s[...] = jnp.zeros(acc_s.shape, F32)

    def step(masked):
        if masked:
            q_pos = q_lo + lax.broadcasted_iota(jnp.int32, (tq, tk), 0)
            k_pos = k_lo + lax.broadcasted_iota(jnp.int32, (tq, tk), 1)
            if chunk_causal:
                shift = CHUNK.bit_length() - 1
                mask = jnp.right_shift(k_pos, shift) <= jnp.right_shift(q_pos, shift)
            else:
                mask = k_pos <= q_pos
        for h in range(nh):
            q = q_ref[:, h * dq:(h + 1) * dq]
            k = k_ref[:, h * dq:(h + 1) * dq]
            v = v_ref[:, h * dv:(h + 1) * dv]
            s = lax.dot_general(q, k, (((1,), (1,)), ((), ())), preferred_element_type=F32)
            if masked:
                s = jnp.where(mask, s, NEG_INF)
            m_prev = m_s[h]
            m_new = jnp.maximum(m_prev, jnp.max(s, axis=-1, keepdims=True))
            alpha = jnp.exp(m_prev - m_new)
            p = jnp.exp(s - m_new)
            l_s[h] = alpha * l_s[h] + jnp.sum(p, axis=-1, keepdims=True)
            acc_s[h] = alpha * acc_s[h] + jnp.dot(p.astype(BF16), v, preferred_element_type=F32)
            m_s[h] = m_new

    @pl.when(needed & unmasked)
    def _plain():
        step(False)

    @pl.when(needed & jnp.logical_not(unmasked))
    def _masked():
        step(True)

    @pl.when(ik == nk - 1)
    def _fin():
        for h in range(nh):
            o_ref[:, h * dv:(h + 1) * dv] = (acc_s[h] / l_s[h]).astype(o_ref.dtype)


def _flash(q, k, v, *, nh, dq, dv, n_hblk, k_col0, v_col0, tq, tk, q_off, chunk_causal):
    b, t_q = q.shape[0], q.shape[1]
    t_k = k.shape[1]
    tq = tq if t_q % tq == 0 else t_q
    tk = tk if t_k % tk == 0 else t_k
    nq, nk = t_q // tq, t_k // tk

    def last_blk(i):
        q_hi = q_off + (i + 1) * tq - 1
        vis = (q_hi // CHUNK) * CHUNK + CHUNK - 1 if chunk_causal else q_hi
        return jnp.minimum(vis // tk, nk - 1)

    in_specs = [
        pl.BlockSpec((None, tq, nh * dq), lambda bi, h, i, j: (bi, i, h)),
        pl.BlockSpec((None, tk, nh * dq), lambda bi, h, i, j: (bi, jnp.minimum(j, last_blk(i)), k_col0 + h)),
        pl.BlockSpec((None, tk, nh * dv), lambda bi, h, i, j: (bi, jnp.minimum(j, last_blk(i)), v_col0 + h)),
    ]
    kern = functools.partial(_flash_kernel, nh=nh, dq=dq, dv=dv, tq=tq, tk=tk, nk=nk, q_off=q_off,
                             chunk_causal=chunk_causal)
    return pl.pallas_call(
        kern,
        grid=(b, n_hblk, nq, nk),
        in_specs=in_specs,
        out_specs=pl.BlockSpec((None, tq, nh * dv), lambda bi, h, i, j: (bi, i, h)),
        out_shape=jax.ShapeDtypeStruct((b, t_q, n_hblk * nh * dv), BF16),
        scratch_shapes=[pltpu.VMEM((nh, tq, 1), F32), pltpu.VMEM((nh, tq, 1), F32),
                        pltpu.VMEM((nh, tq, dv), F32)],
        compiler_params=_params("parallel", "parallel", "parallel", "arbitrary"),
        name="flash_step",
    )(q, k, v)


FLASH_T = 1024
FLASH_QC = 256
FLASH_KC = 256
FLASH_AHEAD = 8
A_HEADS_PER_STEP = 2
C_HEADS_PER_STEP = 4
LOG2E = math.log2(math.e)


def _flash_t_kernel(q_ref, k_ref, vt_ref, o_ref, m_s, acc_s, *, nh, dq, dv, t, chunk_causal):
    iq = pl.program_id(2)
    dva = dv + ONES_ROWS
    m_s[...] = jnp.full(m_s.shape, -jnp.inf, F32)
    acc_s[...] = jnp.zeros(acc_s.shape, F32)

    def block(j, masked):
        row0 = pl.multiple_of(j * t, t)
        qc, kc = FLASH_QC, FLASH_KC

        def n_keys(g, c):
            return min(kc, (c + 1) * qc - g * kc) if masked else kc

        chains = [(g, h, c) for g in range(t // kc) for h in range(nh) for c in range(t // qc)
                  if n_keys(g, c) > 0]

        def qk(g, h, c):
            return lax.dot_general(k_ref[pl.ds(row0 + g * kc, n_keys(g, c)), h * dq:(h + 1) * dq],
                                   q_ref[c * qc:(c + 1) * qc, h * dq:(h + 1) * dq],
                                   (((1,), (1,)), ((), ())), preferred_element_type=F32)

        def softmax_pv(g, h, c, s):
            cols = slice(c * qc, (c + 1) * qc)
            nk = n_keys(g, c)
            if masked and g * kc + nk > c * qc:
                k_pos = lax.broadcasted_iota(jnp.int32, (nk, qc), 0) + g * kc
                q_pos = lax.broadcasted_iota(jnp.int32, (nk, qc), 1) + c * qc
                if chunk_causal:
                    shift = CHUNK.bit_length() - 1
                    mask = jnp.right_shift(k_pos, shift) <= jnp.right_shift(q_pos, shift)
                else:
                    mask = k_pos <= q_pos
                s = jnp.where(mask, s, NEG_INF)
            m_prev = m_s[h, :, cols]
            m_new = jnp.maximum(m_prev, jnp.max(s, axis=0, keepdims=True))
            alpha = jnp.exp2(m_prev - m_new)
            p = jnp.exp2(s - m_new)
            acc_s[h, :, cols] = alpha * acc_s[h, :, cols] + jnp.dot(
                vt_ref[j, h * dva:(h + 1) * dva, g * kc:g * kc + nk], p.astype(BF16),
                preferred_element_type=F32)
            m_s[h, :, cols] = m_new

        pending = [qk(*chain) for chain in chains[:FLASH_AHEAD]]
        for i, chain in enumerate(chains):
            if i + FLASH_AHEAD < len(chains):
                pending.append(qk(*chains[i + FLASH_AHEAD]))
            softmax_pv(*chain, pending.pop(0))

    def full_block(j, carry):
        block(j, False)
        return carry

    lax.fori_loop(0, iq, full_block, 0)
    block(iq, True)
    out = jnp.concatenate([acc_s[h, 0:dv] / acc_s[h, dv:dv + 1] for h in range(nh)], axis=0)
    o_ref[...] = out.T.astype(o_ref.dtype)


def _flash_t(q, k, k_col0, vt, *, nh, dq, dv, n_hblk, chunk_causal):
    b, t_all = q.shape[0], q.shape[1]
    t = FLASH_T
    nblk = t_all // t
    kern = functools.partial(_flash_t_kernel, nh=nh, dq=dq, dv=dv, t=t, chunk_causal=chunk_causal)
    return pl.pallas_call(
        kern,
        grid=(b, n_hblk, nblk),
        in_specs=[pl.BlockSpec((None, t, nh * dq), lambda bi, h, i: (bi, i, h)),
                  pl.BlockSpec((None, t_all, nh * dq), lambda bi, h, i: (bi, 0, k_col0 + h),
                               pipeline_mode=pl.Buffered(1)),
                  pl.BlockSpec((None, nblk, nh * (dv + ONES_ROWS), t), lambda bi, h, i: (bi, 0, h, 0),
                               pipeline_mode=pl.Buffered(1))],
        out_specs=pl.BlockSpec((None, t, nh * dv), lambda bi, h, i: (bi, i, h)),
        out_shape=jax.ShapeDtypeStruct((b, t_all, n_hblk * nh * dv), BF16),
        scratch_shapes=[pltpu.VMEM((nh, 1, t), F32), pltpu.VMEM((nh, dv + ONES_ROWS, t), F32)],
        compiler_params=_params("parallel", "parallel", "parallel"),
        name="flash_t_chunk" if chunk_causal else "flash_t_frame",
    )(q, k, vt)


B_HEADS_PER_STEP = LANES // B_HEAD_DIM


def _step_attn_kernel(*refs, heads, dh, forget):
    if forget:
        q_ref, kp_ref, vp_ref, kn_ref, vn_ref, bp_ref, bn_ref, fq_ref, fkp_ref, fkn_ref, o_ref = refs
    else:
        q_ref, kp_ref, vp_ref, kn_ref, vn_ref, bp_ref, bn_ref, o_ref = refs
    nt = (((1,), (1,)), ((), ()))
    n_bias = bp_ref.shape[0]
    for h in range(heads):
        cols = slice(h * dh, (h + 1) * dh)
        q = q_ref[:, cols]
        sp = lax.dot_general(q, kp_ref[:, cols].astype(BF16), nt, preferred_element_type=F32)
        sn = lax.dot_general(q, kn_ref[:, cols], nt, preferred_element_type=F32)
        sp = sp + bp_ref[h % n_bias]
        sn = sn + bn_ref[h % n_bias]
        if forget:
            fq = fq_ref[:, h:h + 1]
            sp = sp + fq - fkp_ref[h:h + 1, :]
            sn = sn + fq - fkn_ref[h:h + 1, :]
        m = jnp.maximum(jnp.max(sp, axis=-1, keepdims=True), jnp.max(sn, axis=-1, keepdims=True))
        pp = jnp.exp(sp - m)
        pn = jnp.exp(sn - m)
        l = jnp.sum(pp, axis=-1, keepdims=True) + jnp.sum(pn, axis=-1, keepdims=True)
        o = (jnp.dot(pp.astype(BF16), vp_ref[:, cols].astype(BF16), preferred_element_type=F32)
             + jnp.dot(pn.astype(BF16), vn_ref[:, cols], preferred_element_type=F32))
        o_ref[:, cols] = (o / l).astype(o_ref.dtype)


def _band_bias(rel_bias, q_pos, k_pos):
    nq, nk = len(q_pos), len(k_pos)
    assert (np.diff(q_pos) == 1).all() and (np.diff(k_pos) == 1).all()
    m = np.arange(nq + nk - 1)
    u = rel_bias[:, np.clip(q_pos[0] - k_pos[0] + nq - 1 - m, -B_REL_CLIP, B_REL_CLIP) + B_REL_CLIP]
    period = nq + nk
    w = jnp.concatenate([u[:, nq - 1:], jnp.zeros((u.shape[0], 1), u.dtype), u[:, :nq - 1]], axis=1)
    skew = jnp.tile(w, (1, nq))[:, :nq * (period - 1)].reshape(-1, nq, period - 1)[:, :, :nk]
    qc = q_pos[:, None] // CHUNK
    kc = k_pos[None, :] // CHUNK
    mask = (kc <= qc) & (kc >= qc - B_LEFT_CHUNKS) & (k_pos[None, :] >= 0)
    return jnp.where(jnp.asarray(mask)[None], skew, NEG_INF).astype(F32)


def _step_attn(q, k_past, v_past, k_new, v_new, bias, heads, f_cum=None):
    b, t, d = q.shape
    p = k_past.shape[1]
    stream = lambda bi: (bi, 0, 0)
    fixed = lambda bi: (0, 0, 0)
    args = [q, k_past, v_past, k_new, v_new, bias[:, :, :p], bias[:, :, p:]]
    in_specs = [pl.BlockSpec((None, t, d), stream), pl.BlockSpec((None, p, d), stream),
                pl.BlockSpec((None, p, d), stream), pl.BlockSpec((None, t, d), stream),
                pl.BlockSpec((None, t, d), stream),
                pl.BlockSpec((bias.shape[0], t, p), fixed), pl.BlockSpec((bias.shape[0], t, t), fixed)]
    if f_cum is not None:
        fq = jnp.pad(f_cum[:, p:], ((0, 0), (0, 0), (0, LANES - heads)))
        fk = f_cum.transpose(0, 2, 1)
        args += [fq, fk[:, :, :p], fk[:, :, p:]]
        in_specs += [pl.BlockSpec((None, t, LANES), stream), pl.BlockSpec((None, heads, p), stream),
                     pl.BlockSpec((None, heads, t), stream)]
    return pl.pallas_call(
        functools.partial(_step_attn_kernel, heads=heads, dh=d // heads, forget=f_cum is not None),
        grid=(b,),
        in_specs=in_specs,
        out_specs=pl.BlockSpec((None, t, d), stream),
        out_shape=jax.ShapeDtypeStruct((b, t, d), BF16),
        compiler_params=_params("parallel"),
        name="step_attn",
    )(*args)


BAND_T = 4 * CHUNK


def _band_t_kernel(*refs, nkb, t):
    q_ref = refs[0]
    k_refs = refs[1:1 + nkb]
    vt_refs = refs[1 + nkb:1 + 2 * nkb]
    bias_ref, o_ref = refs[1 + 2 * nkb], refs[2 + 2 * nkb]
    iq = pl.program_id(1)
    dh = B_HEAD_DIM

    def qk(h):
        cols = slice(h * dh, (h + 1) * dh)
        parts = []
        for j in range(nkb):
            s = lax.dot_general(k_refs[j][:, cols], q_ref[:, cols], (((1,), (1,)), ((), ())),
                                preferred_element_type=F32)
            if j < nkb - 1:
                s = jnp.where(iq >= nkb - 1 - j, s, NEG_INF)
            parts.append(s)
        return jnp.concatenate(parts, axis=0) + bias_ref[h]

    def softmax_pv(h, s):
        p = jnp.exp2(s - jnp.max(s, axis=0, keepdims=True))
        l = jnp.sum(p, axis=0, keepdims=True)
        pb = p.astype(BF16)
        o = functools.reduce(lambda a, c: a + c, [
            jnp.dot(vt_refs[j][h * dh:(h + 1) * dh, :], pb[j * t:(j + 1) * t], preferred_element_type=F32)
            for j in range(nkb)])
        return o / l

    heads = list(range(B_HEADS))
    pending = [qk(h) for h in heads[:FLASH_AHEAD]]
    outs = []
    for h in heads:
        if h + FLASH_AHEAD < B_HEADS:
            pending.append(qk(h + FLASH_AHEAD))
        outs.append(softmax_pv(h, pending.pop(0)))
        if len(outs) == B_HEADS_PER_STEP:
            c0 = (h + 1 - B_HEADS_PER_STEP) * dh
            o_ref[:, c0:c0 + LANES] = jnp.concatenate(outs, axis=0).T.astype(o_ref.dtype)
            outs = []


def _band_t(q, kv, vt, bias_t, *, nkb):
    b, t_all = q.shape[0], q.shape[1]
    t = BAND_T

    def back(j):
        return nkb - 1 - j

    k_specs = [pl.BlockSpec((None, t, D_MODEL), lambda bi, i, j=j: (bi, jnp.maximum(i - back(j), 0), 0))
               for j in range(nkb)]
    vt_specs = [pl.BlockSpec((None, None, D_MODEL, t),
                             lambda bi, i, j=j: (bi, jnp.maximum(i - back(j), 0), 0, 0))
                for j in range(nkb)]
    return pl.pallas_call(
        functools.partial(_band_t_kernel, nkb=nkb, t=t),
        grid=(b, t_all // t),
        in_specs=([pl.BlockSpec((None, t, D_MODEL), lambda bi, i: (bi, i, 0))] + k_specs + vt_specs
                  + [pl.BlockSpec(bias_t.shape, lambda bi, i: (0, 0, 0), pipeline_mode=pl.Buffered(1))]),
        out_specs=pl.BlockSpec((None, t, D_MODEL), lambda bi, i: (bi, i, 0)),
        out_shape=jax.ShapeDtypeStruct((b, t_all, D_MODEL), BF16),
        compiler_params=_params("parallel", "parallel"),
        name="band_t",
    )(q, *([kv] * nkb), *([vt] * nkb), bias_t)


def _logf_kernel(x_ref, w_ref, b_ref, o_ref):
    z = jnp.dot(x_ref[...], w_ref[...], preferred_element_type=F32) + b_ref[...]
    o_ref[...] = -(jnp.maximum(-z, 0.0) + jnp.log1p(jnp.exp(-jnp.abs(z))))


def _logf(xb, w_pad, b_pad, tm=512):
    m = xb.shape[0]
    tm = _row_tile(m, tm)
    return pl.pallas_call(
        _logf_kernel,
        grid=(m // tm,),
        in_specs=[pl.BlockSpec((tm, D_MODEL), lambda i: (i, 0)),
                  pl.BlockSpec((D_MODEL, LANES), lambda i: (0, 0)),
                  pl.BlockSpec((1, LANES), lambda i: (0, 0))],
        out_specs=pl.BlockSpec((tm, LANES), lambda i: (i, 0)),
        out_shape=jax.ShapeDtypeStruct((m, LANES), F32),
        compiler_params=_params("parallel"),
        name="logf",
    )(xb, w_pad, b_pad)


def _split3(x):
    hi = x.astype(BF16)
    r = x - hi.astype(F32)
    mid = r.astype(BF16)
    lo = (r - mid.astype(F32)).astype(BF16)
    return hi, mid, lo


N_PIECES = 3
PIECE_LANES = 16


def _cumsum_kernel(x_ref, o_ref, pk_ref, carry, *, tc):
    @pl.when(pl.program_id(1) == 0)
    def _():
        carry[...] = jnp.zeros(carry.shape, F32)

    tri = (lax.broadcasted_iota(jnp.int32, (tc, tc), 0)
           >= lax.broadcasted_iota(jnp.int32, (tc, tc), 1)).astype(BF16)
    c = functools.reduce(lambda a, b: a + b, [jnp.dot(tri, piece, preferred_element_type=F32)
                                              for piece in _split3(x_ref[...])])
    out = c + carry[0:1, :]
    o_ref[...] = out
    src = lax.broadcasted_iota(jnp.int32, (LANES, LANES), 0)
    dst = lax.broadcasted_iota(jnp.int32, (LANES, LANES), 1)
    packed = functools.reduce(lambda a, b: a + b, [
        jnp.dot(piece, ((dst == src + k * PIECE_LANES) & (src < PIECE_LANES)).astype(BF16),
                preferred_element_type=F32)
        for k, piece in enumerate(_split3(out * LOG2E))])
    pk_ref[...] = packed.astype(BF16)
    carry[...] = jnp.broadcast_to(out[tc - 1:tc, :], carry.shape)


def _cumsum(x, tc=256):
    b, t, _ = x.shape
    spec = pl.BlockSpec((None, tc, LANES), lambda bi, i: (bi, i, 0))
    return pl.pallas_call(
        functools.partial(_cumsum_kernel, tc=tc),
        grid=(b, t // tc),
        in_specs=[spec],
        out_specs=[spec] * 2,
        out_shape=[jax.ShapeDtypeStruct(x.shape, F32), jax.ShapeDtypeStruct(x.shape, BF16)],
        scratch_shapes=[pltpu.VMEM((SUBLANES, LANES), F32)],
        compiler_params=_params("parallel", "arbitrary"),
        name="cumsum",
    )(x)


FFN_CHUNK = 256
CONV_ROWS = SUBLANES


def _ffn_up_kernel(x_ref, w_ref, c_ref, p_ref, act_ref, s_ref, halo, *, tm):
    @pl.when(pl.program_id(1) == 0)
    def _():
        halo[...] = p_ref[...]

    x = x_ref[...]
    n_chunks = D_FF // FFN_CHUNK
    groups = tm // CONV_ROWS
    row = lax.broadcasted_iota(jnp.int32, (groups, CONV_ROWS, FFN_CHUNK), 1)

    def up(c, half):
        col = half * D_FF + c * FFN_CHUNK
        return jnp.dot(x, w_ref[:, col:col + FFN_CHUNK], preferred_element_type=F32)

    def conv(h, c, half):
        cols = slice(half * D_FF + c * FFN_CHUNK, half * D_FF + (c + 1) * FFN_CHUNK)
        ext = jnp.concatenate([halo[:, cols], h], axis=0).reshape(groups + 1, CONV_ROWS, FFN_CHUNK)
        hc = c_ref[CONV_W:CONV_W + 1, cols] + c_ref[CONV_W - 1:CONV_W, cols] * h
        for s in range(1, CONV_W):
            rot = pltpu.roll(ext, s, axis=1)
            shifted = jnp.where(row < s, rot[:groups], rot[1:]).reshape(tm, FFN_CHUNK)
            hc = hc + c_ref[CONV_W - 1 - s:CONV_W - s, cols] * shifted
        tail = h[tm - CONV_ROWS:tm]
        halo[:, cols] = tail
        s_ref[:, cols] = tail
        return hc

    pending = [(up(0, 0), up(0, 1))]
    for c in range(n_chunks):
        if c + 1 < n_chunks:
            pending.append((up(c + 1, 0), up(c + 1, 1)))
        ha, hg = pending.pop(0)
        a = conv(ha, c, 0)
        g = conv(hg, c, 1)
        act_ref[:, c * FFN_CHUNK:(c + 1) * FFN_CHUNK] = (g * jax.nn.sigmoid(g) * a).astype(BF16)


def _ffn_up(xb, w_up, conv_tab, past, tm=256):
    b, t, _ = xb.shape
    tm = _row_tile(t, tm)
    fixed = lambda bi, ti: (0, 0)
    return pl.pallas_call(
        functools.partial(_ffn_up_kernel, tm=tm),
        grid=(b, t // tm),
        in_specs=[pl.BlockSpec((None, tm, D_MODEL), lambda bi, ti: (bi, ti, 0)),
                  pl.BlockSpec((D_MODEL, 2 * D_FF), fixed),
                  pl.BlockSpec((CONV_ROWS, 2 * D_FF), fixed),
                  pl.BlockSpec((None, CONV_ROWS, 2 * D_FF), lambda bi, ti: (bi, 0, 0))],
        out_specs=[pl.BlockSpec((None, tm, D_FF), lambda bi, ti: (bi, ti, 0)),
                   pl.BlockSpec((None, CONV_ROWS, 2 * D_FF), lambda bi, ti: (bi, 0, 0))],
        out_shape=[jax.ShapeDtypeStruct((b, t, D_FF), BF16),
                   jax.ShapeDtypeStruct((b, CONV_ROWS, 2 * D_FF), F32)],
        scratch_shapes=[pltpu.VMEM((CONV_ROWS, 2 * D_FF), F32)],
        compiler_params=_params("parallel", "arbitrary"),
        name="ffn_up",
    )(xb, w_up, conv_tab, past)


def _rope_tables(pos, batch, q_scale):
    half = A_ROPE // 2
    inv_freq = ROPE_THETA ** (-jnp.arange(half, dtype=F32) / half)
    ang = pos.astype(F32)[:, None] * inv_freq
    cos, sin = jnp.cos(ang), jnp.sin(ang)
    cosk = jnp.concatenate([cos, cos], axis=-1)
    sink = jnp.concatenate([sin, sin], axis=-1)
    t = pos.shape[0]
    pad = jnp.zeros((t, A_QK_PAD - A_NOPE - A_ROPE), F32)
    cosq = q_scale * jnp.concatenate([jnp.ones((t, A_NOPE), F32), cosk, pad], axis=-1)
    sinq = q_scale * jnp.concatenate([jnp.zeros((t, A_NOPE), F32), sink, pad], axis=-1)
    return tuple(jnp.tile(a, (batch, 1)) for a in (cosq, sinq, cosk, sink))


def _swap_halves(w):
    half = w.shape[-1] // 2
    return jnp.concatenate([-w[..., half:], w[..., :half]], axis=-1)


def _mla_weights(w_dq, w_dkv, w_kr, w_uq, w_uk, w_uv):
    zc = jnp.zeros((D_MODEL, LANES - A_ROPE), F32)
    w1 = jnp.concatenate([w_dq, w_dkv, w_kr, zc, _swap_halves(w_kr), zc], axis=1).astype(BF16)
    wq = w_uq.reshape(A_Q_LORA, A_HEADS, A_NOPE + A_ROPE)
    nope, rope = wq[..., :A_NOPE], wq[..., A_NOPE:]
    zpad = jnp.zeros((A_Q_LORA, A_HEADS, A_QK_PAD - A_NOPE - A_ROPE), F32)
    w_cat = jnp.concatenate([nope, rope, zpad], axis=-1).reshape(A_Q_LORA, -1)
    w_sw = jnp.concatenate([jnp.zeros_like(nope), _swap_halves(rope), zpad], axis=-1).reshape(A_Q_LORA, -1)
    wq2 = jnp.concatenate([w_cat, w_sw], axis=1).astype(BF16)
    wk = jnp.zeros((A_LAT_PAD, A_HEADS, A_QK_PAD), F32)
    wk = wk.at[:A_KV_LORA, :, :A_NOPE].set(w_uk)
    eye = jnp.broadcast_to(jnp.eye(A_ROPE, dtype=F32)[:, None, :], (A_ROPE, A_HEADS, A_ROPE))
    wk = wk.at[A_KV_LORA:A_KV_LORA + A_ROPE, :, A_NOPE:A_NOPE + A_ROPE].set(eye)
    wv = jnp.zeros((A_LAT_PAD, A_HEADS * A_V), F32).at[:A_KV_LORA].set(w_uv.reshape(A_KV_LORA, -1))
    wkv = jnp.concatenate([wk.reshape(A_LAT_PAD, -1), wv], axis=1).astype(BF16)
    return w1, wq2, wkv


def _mla_mixer(xb, b, t, pos, ckv_past, kpe_past, w, i):
    w1, wq2, wkv = _mla_weights(w['a_w_dq'][i], w['a_w_dkv'][i], w['a_w_kr'][i], w['a_w_uq'][i],
                                w['a_w_uk'][i], w['a_w_uv'][i])
    resident = ckv_past is None and t % FLASH_T == 0
    q_scale = (A_NOPE + A_ROPE) ** -0.5 * (LOG2E if resident else 1.0)
    q, lat, ckv, kpe = _mla_proj(xb, w1, w['a_g_q'][i], w['a_g_kv'][i], wq2,
                                 *_rope_tables(pos, b, q_scale))
    lat = lat.reshape(b, t, A_LAT_PAD)
    q_off = 0
    if ckv_past is not None:
        p_len = ckv_past.shape[1]
        past = jnp.concatenate(
            [ckv_past, kpe_past, jnp.zeros((b, p_len, A_LAT_PAD - A_KV_LORA - A_ROPE), F32)], axis=-1)
        lat = jnp.concatenate([past.astype(BF16), lat], axis=1)
        q_off = p_len
    t_k = lat.shape[1]
    lat2 = lat.reshape(b * t_k, A_LAT_PAD)
    n_k = A_HEADS * A_QK_PAD
    if resident:
        (k_cat,) = _mm(lat2, wkv[:, :n_k], [BF16])
        vt = _proj_t(lat2, wkv[:, n_k:], FLASH_T, A_V).reshape(b, t_k // FLASH_T, -1, FLASH_T)
        o = _flash_t(q.reshape(b, t, -1), k_cat.reshape(b, t_k, n_k), 0, vt, nh=A_HEADS_PER_STEP,
                     dq=A_QK_PAD, dv=A_V, n_hblk=A_HEADS // A_HEADS_PER_STEP, chunk_causal=True)
    else:
        (kv,) = _mm(lat2, wkv, [BF16])
        kv = kv.reshape(b, t_k, -1)
        o = _flash(q.reshape(b, t, -1), kv, kv, nh=A_HEADS, dq=A_QK_PAD, dv=A_V, n_hblk=1,
                   k_col0=0, v_col0=A_QK_PAD // A_V, tq=512, tk=512, q_off=q_off, chunk_causal=True)
    return o.reshape(b * t, A_HEADS * A_V), ckv.reshape(b, t, -1), kpe.reshape(b, t, -1)


def _band_mixer(xb, b, t, pos0, k_past, v_past, w, i):
    prompt = k_past is None
    assert not prompt or t % BAND_T == 0
    w_qkv = w['b_w_qkv'][i]
    (q,) = _mm(xb, (w_qkv[:, :D_MODEL] * (B_HEAD_DIM ** -0.5 * (LOG2E if prompt else 1.0))).astype(BF16),
               [BF16])
    q = q.reshape(b, t, D_MODEL)
    w_kv = w_qkv[:, D_MODEL:].astype(BF16)
    heads = lambda a, rows: a.reshape(b, rows, B_HEADS, B_HEAD_DIM)
    if prompt:
        nkb = B_WIN // BAND_T + 1
        bias = _band_bias(w['b_rel_bias'][i], B_WIN + np.arange(BAND_T), np.arange(B_WIN + BAND_T))
        (kvb,) = _mm(xb, w_kv, [BF16])
        kvb = kvb.reshape(b, t, 2 * D_MODEL)
        vt = _proj_t(xb, w_qkv[:, 2 * D_MODEL:], BAND_T).reshape(b, t // BAND_T, D_MODEL, BAND_T)
        o = _band_t(q, kvb, vt, LOG2E * bias.transpose(0, 2, 1), nkb=nkb)
        keep = min(B_WIN, t)
        x_tail = xb.reshape(b, t, D_MODEL)[:, t - keep:].reshape(b * keep, D_MODEL)
        (kv_tail,) = _mm(x_tail, w_kv, [F32])
        k_new, v_new = heads(kv_tail[:, :D_MODEL], keep), heads(kv_tail[:, D_MODEL:], keep)
    else:
        kv32, kvb = _mm(xb, w_kv, [F32, BF16])
        k32, v32 = heads(kv32[:, :D_MODEL], t), heads(kv32[:, D_MODEL:], t)
        p_len = k_past.shape[1]
        kvb = kvb.reshape(b, t, 2 * D_MODEL)
        q_pos = pos0 + np.arange(t)
        k_pos = np.concatenate([np.arange(pos0 - p_len, pos0), q_pos])
        bias = _band_bias(w['b_rel_bias'][i], q_pos, k_pos)
        o = _step_attn(q, k_past.reshape(b, p_len, D_MODEL), v_past.reshape(b, p_len, D_MODEL),
                       kvb[:, :, :D_MODEL], kvb[:, :, D_MODEL:], bias, B_HEADS)
        k_new, v_new = k32, v32
    return o.reshape(b * t, D_MODEL), k_new, v_new


C_AUG = LANES


def _aug_kernel(x_ref, w_ref, f_ref, p_ref, b_ref, o_ref):
    acc = (jnp.dot(x_ref[...], w_ref[...], preferred_element_type=F32) + b_ref[...]
           + jnp.dot(f_ref[...], p_ref[...], preferred_element_type=F32))
    o_ref[...] = acc.astype(BF16)


def _aug_tables():
    n = C_HEADS * C_AUG
    place = np.zeros((LANES, 2 * n), np.float32)
    ones = np.zeros((1, 2 * n), np.float32)
    for h in range(C_HEADS):
        base = h * C_AUG + C_HEAD_DIM
        for piece in range(N_PIECES):
            ones[0, base + piece] = 1.0
            place[piece * PIECE_LANES + h, base + N_PIECES + piece] = 1.0
            place[piece * PIECE_LANES + h, n + base + piece] = -1.0
            ones[0, n + base + N_PIECES + piece] = 1.0
    return jnp.asarray(place, BF16), jnp.asarray(ones)


def _aug_qk(xb, w_aug, pieces, tm=512, tn=1024):
    m = xb.shape[0]
    n = w_aug.shape[1]
    tm = _row_tile(m, tm)
    place, ones = _aug_tables()
    row = lambda i, j: (i, 0)
    return pl.pallas_call(
        _aug_kernel,
        grid=(m // tm, n // tn),
        in_specs=[pl.BlockSpec((tm, D_MODEL), row), pl.BlockSpec((D_MODEL, tn), lambda i, j: (0, j)),
                  pl.BlockSpec((tm, LANES), row), pl.BlockSpec((LANES, tn), lambda i, j: (0, j)),
                  pl.BlockSpec((1, tn), lambda i, j: (0, j))],
        out_specs=pl.BlockSpec((tm, tn), lambda i, j: (i, j)),
        out_shape=jax.ShapeDtypeStruct((m, n), BF16),
        compiler_params=_params("parallel", "parallel"),
        name="aug_qk",
    )(xb, w_aug, pieces, place, ones)


def _pad_heads(w, scale):
    w = (w * scale).reshape(D_MODEL, C_HEADS, C_HEAD_DIM)
    return jnp.pad(w, ((0, 0), (0, 0), (0, C_AUG - C_HEAD_DIM))).reshape(D_MODEL, C_HEADS * C_AUG)


def _fox_mixer(xb, b, t, k_past, v_past, lf_past, w, i):
    resident = k_past is None
    assert not resident or t % FLASH_T == 0
    w_qkv = w['c_w_qkv'][i]
    k32, *kb = _mm(xb, w_qkv[:, D_MODEL:2 * D_MODEL].astype(BF16), [F32] if resident else [F32, BF16])
    v32, *vb = _mm(xb, w_qkv[:, 2 * D_MODEL:].astype(BF16), [F32] if resident else [F32, BF16])
    k32 = k32.reshape(b, t, C_HEADS, C_HEAD_DIM)
    v32 = v32.reshape(b, t, C_HEADS, C_HEAD_DIM)
    w_f = jnp.zeros((D_MODEL, LANES), F32).at[:, :C_HEADS].set(w['c_w_f'][i]).astype(BF16)
    b_f = jnp.zeros((1, LANES), F32).at[0, :C_HEADS].set(w['c_b_f'][i])
    log_f = _logf(xb, w_f, b_f).reshape(b, t, LANES)
    lf_all = log_f
    if not resident:
        lf_all = jnp.concatenate([jnp.pad(lf_past, ((0, 0), (0, 0), (0, LANES - C_HEADS))), log_f], axis=1)
    t_k = lf_all.shape[1]
    tc = 256
    t_pad = -(-t_k // tc) * tc
    f_cum, f_packed = _cumsum(jnp.pad(lf_all, ((0, 0), (0, t_pad - t_k), (0, 0))), tc)
    nh = C_HEADS_PER_STEP
    n_hblk = C_HEADS // nh
    q_scale = C_HEAD_DIM ** -0.5
    if resident:
        w_aug = jnp.concatenate([_pad_heads(w_qkv[:, :D_MODEL], q_scale * LOG2E),
                                 _pad_heads(w_qkv[:, D_MODEL:2 * D_MODEL], 1.0)], axis=1).astype(BF16)
        qk_aug = _aug_qk(xb, w_aug, f_packed.reshape(b * t, LANES)).reshape(b, t, -1)
        n_q = C_HEADS * C_AUG
        vt = _proj_t(xb, w_qkv[:, 2 * D_MODEL:], FLASH_T, C_HEAD_DIM).reshape(b, t // FLASH_T, -1, FLASH_T)
        o = _flash_t(qk_aug, qk_aug, n_q // (nh * C_AUG), vt,
                     nh=nh, dq=C_AUG, dv=C_HEAD_DIM, n_hblk=n_hblk, chunk_causal=False)
        return o.reshape(b * t, D_MODEL), k32, v32, log_f[:, :, :C_HEADS]
    (q,) = _mm(xb, (w_qkv[:, :D_MODEL] * q_scale).astype(BF16), [BF16])
    p_len = k_past.shape[1]
    causal = np.where(np.tril(np.ones((t, t), bool)), 0.0, NEG_INF).astype(np.float32)
    bias = jnp.asarray(np.concatenate([np.zeros((t, p_len), np.float32), causal], axis=1)[None])
    o = _step_attn(q.reshape(b, t, D_MODEL), k_past.reshape(b, p_len, D_MODEL),
                   v_past.reshape(b, p_len, D_MODEL), kb[0].reshape(b, t, D_MODEL),
                   vb[0].reshape(b, t, D_MODEL), bias, C_HEADS, f_cum=f_cum[:, :t_k, :C_HEADS])
    return o.reshape(b * t, D_MODEL), k32, v32, log_f[:, :, :C_HEADS]


def _conv_ffn(xb, b, t, conv_past, w, i):
    tab = jnp.concatenate([w['f_conv_w'][i], w['f_conv_b'][i][None],
                           jnp.zeros((CONV_ROWS - CONV_W - 1, 2 * D_FF), F32)], axis=0)
    if conv_past is None:
        past = jnp.zeros((b, CONV_ROWS, 2 * D_FF), F32)
    else:
        past = jnp.pad(conv_past, ((0, 0), (CONV_ROWS - (CONV_W - 1), 0), (0, 0)))
    act, tail = _ffn_up(xb.reshape(b, t, D_MODEL), w['f_w_up_bf16'][i], tab, past)
    return act.reshape(b * t, D_FF), tail[:, CONV_ROWS - (CONV_W - 1):]


def _trunk(x, pos0, past, w):
    b, t, _ = x.shape
    pos = pos0 + jnp.arange(t)
    xf = x.reshape(b * t, D_MODEL)
    xb = xf.astype(BF16)
    outs = {n: [] for n in ('a_ckv', 'a_kpe', 'b_k', 'b_v', 'c_k', 'c_v', 'c_logf', 'ffn_conv')}
    ia = ib = ic = 0
    get = lambda name, j: None if past is None else past[name][j]
    for i in range(DEPTH):
        kind = i % N_MIXERS
        if kind == 0:
            o, ckv, kpe = _mla_mixer(xb, b, t, pos, get('a_ckv', ia), get('a_kpe', ia), w, ia)
            outs['a_ckv'].append(ckv)
            outs['a_kpe'].append(kpe)
            w_o = w['a_w_o_bf16'][ia]
            ia += 1
        elif kind == 1:
            o, kb, vb = _band_mixer(xb, b, t, pos0, get('b_k', ib), get('b_v', ib), w, ib)
            outs['b_k'].append(kb)
            outs['b_v'].append(vb)
            w_o = w['b_w_o_bf16'][ib]
            ib += 1
        else:
            o, kc, vc, lf = _fox_mixer(xb, b, t, get('c_k', ic), get('c_v', ic), get('c_logf', ic), w, ic)
            outs['c_k'].append(kc)
            outs['c_v'].append(vc)
            outs['c_logf'].append(lf)
            w_o = w['c_w_o_bf16'][ic]
            ic += 1
        xf, xb = _mm_res_ln(o, w_o, xf, w['ln1_g'][i], w['ln1_b'][i])
        act, conv_state = _conv_ffn(xb, b, t, get('ffn_conv', i), w, i)
        outs['ffn_conv'].append(conv_state)
        xf, xb = _mm_res_ln(act, w['f_w_down_bf16'][i], xf, w['ln2_g'][i], w['ln2_b'][i])
    return xf.reshape(b, t, D_MODEL), {n: jnp.stack(v) for n, v in outs.items()}


def kernel(x_prompt, x_sample, cache_a_ckv, cache_a_kpe, cache_b_k, cache_b_v, cache_c_k, cache_c_v,
           cache_c_logf, state_ffn_conv, a_w_dq, a_g_q, a_w_uq, a_w_dkv, a_g_kv, a_w_kr, a_w_uk, a_w_uv,
           a_w_o, b_w_qkv, b_rel_bias, b_w_o, c_w_qkv, c_w_f, c_b_f, c_w_o, f_w_up, f_conv_w, f_conv_b,
           f_w_down, ln1_g, ln1_b, ln2_g, ln2_b):
    w = dict(a_w_dq=a_w_dq, a_g_q=a_g_q, a_w_uq=a_w_uq, a_w_dkv=a_w_dkv, a_g_kv=a_g_kv, a_w_kr=a_w_kr,
             a_w_uk=a_w_uk, a_w_uv=a_w_uv, a_w_o=a_w_o, b_w_qkv=b_w_qkv, b_rel_bias=b_rel_bias, b_w_o=b_w_o,
             c_w_qkv=c_w_qkv, c_w_f=c_w_f, c_b_f=c_b_f, c_w_o=c_w_o, f_w_up=f_w_up, f_conv_w=f_conv_w,
             f_conv_b=f_conv_b, f_w_down=f_w_down, ln1_g=ln1_g, ln1_b=ln1_b, ln2_g=ln2_g, ln2_b=ln2_b)
    past = dict(a_ckv=cache_a_ckv, a_kpe=cache_a_kpe, b_k=cache_b_k, b_v=cache_b_v, c_k=cache_c_k,
                c_v=cache_c_v, c_logf=cache_c_logf, ffn_conv=state_ffn_conv)
    past_len = cache_a_ckv.shape[2]
    for name in ('f_w_up', 'f_w_down', 'a_w_o', 'b_w_o', 'c_w_o'):
        w[name + '_bf16'] = _to_bf16(w[name])
    y_prompt, p = _trunk(x_prompt, 0, None, w)
    y_sample, s = _trunk(x_sample, past_len, past, w)
    names = ('a_ckv', 'a_kpe', 'b_k', 'b_v', 'c_k', 'c_v', 'c_logf', 'ffn_conv')
    return (y_prompt, y_sample) + tuple(p[n] for n in names) + tuple(s[n] for n in names)
```

```python
import functools
import math

import numpy as np
import jax
import jax.numpy as jnp
from jax import lax
from jax.experimental import pallas as pl
from jax.experimental.pallas import tpu as pltpu

F32 = jnp.float32
BF16 = jnp.bfloat16

D_MODEL = 1024
DEPTH = 4
CHUNK = 64
N_MIXERS = 3

A_HEADS = 8
A_Q_LORA = 384
A_KV_LORA = 256
A_NOPE = 128
A_ROPE = 64
A_V = 128
A_QK_PAD = 256
A_LAT_PAD = 384
ROPE_THETA = 10000.0

B_HEADS = 16
B_HEAD_DIM = D_MODEL // B_HEADS
B_LEFT_CHUNKS = 8
B_WIN = B_LEFT_CHUNKS * CHUNK
B_REL_CLIP = 128

C_HEADS = 16
C_HEAD_DIM = D_MODEL // C_HEADS

D_FF = 2816
CONV_W = 3

ALPHA = (2.0 * DEPTH) ** 0.25
LN_EPS = 1e-5
RMS_EPS = 1e-6
NEG_INF = -1e30

LANES = 128
SUBLANES = 8
VMEM_LIMIT_BYTES = 48 * 2 ** 20


def _params(*sem):
    return pltpu.CompilerParams(dimension_semantics=sem, vmem_limit_bytes=VMEM_LIMIT_BYTES)


def _row_tile(m, tm):
    while m % tm:
        tm //= 2
    assert tm % SUBLANES == 0, (m, tm)
    return tm


def _mm_kernel(x_ref, w_ref, *o_refs):
    acc = jnp.dot(x_ref[...], w_ref[...], preferred_element_type=F32)
    for o_ref in o_refs:
        o_ref[...] = acc.astype(o_ref.dtype)


def _mm(x, w, out_dtypes, tm=512, tn=1024):
    m, k = x.shape
    n = w.shape[1]
    tm, tn = _row_tile(m, tm), min(tn, n)
    return pl.pallas_call(
        _mm_kernel,
        grid=(m // tm, n // tn),
        in_specs=[pl.BlockSpec((tm, k), lambda i, j: (i, 0)),
                  pl.BlockSpec((k, tn), lambda i, j: (0, j))],
        out_specs=[pl.BlockSpec((tm, tn), lambda i, j: (i, j)) for _ in out_dtypes],
        out_shape=[jax.ShapeDtypeStruct((m, n), d) for d in out_dtypes],
        compiler_params=_params("parallel", "parallel"),
        name="mm",
    )(x, w)


ONES_ROWS = 16


def _proj_t_kernel(wt_ref, b_ref, x_ref, o_ref):
    o_ref[...] = (lax.dot_general(wt_ref[...], x_ref[...], (((1,), (1,)), ((), ())),
                                  preferred_element_type=F32) + b_ref[...]).astype(o_ref.dtype)


def _proj_t(x, w, t_blk, head_dim=None, tm=512):
    m, k = x.shape
    n = w.shape[1]
    bias = jnp.zeros((n, 1), F32)
    if head_dim is not None:
        heads = n // head_dim
        w = jnp.pad(w.reshape(k, heads, head_dim), ((0, 0), (0, 0), (0, ONES_ROWS)))
        bias = jnp.pad(jnp.zeros((heads, head_dim, 1), F32), ((0, 0), (0, ONES_ROWS), (0, 0)),
                       constant_values=1.0)
        n = heads * (head_dim + ONES_ROWS)
        w, bias = w.reshape(k, n), bias.reshape(n, 1)
    tm = min(tm, t_blk)
    per = t_blk // tm
    return pl.pallas_call(
        _proj_t_kernel,
        grid=(m // tm,),
        in_specs=[pl.BlockSpec((n, k), lambda i: (0, 0)), pl.BlockSpec((n, 1), lambda i: (0, 0)),
                  pl.BlockSpec((tm, k), lambda i: (i, 0))],
        out_specs=pl.BlockSpec((None, n, tm), lambda i: (i // per, 0, i % per)),
        out_shape=jax.ShapeDtypeStruct((m // t_blk, n, t_blk), BF16),
        compiler_params=_params("parallel"),
        name="proj_t",
    )(w.T.astype(BF16), bias, x)


def _cast_kernel(x_ref, o_ref):
    o_ref[...] = x_ref[...].astype(o_ref.dtype)


def _to_bf16(w, tr=256):
    n_l, r, c = w.shape
    tr = _row_tile(r, tr)
    spec = pl.BlockSpec((None, tr, c), lambda l, i: (l, i, 0))
    return pl.pallas_call(
        _cast_kernel,
        grid=(n_l, r // tr),
        in_specs=[spec],
        out_specs=spec,
        out_shape=jax.ShapeDtypeStruct(w.shape, BF16),
        compiler_params=_params("parallel", "parallel"),
        name="to_bf16",
    )(w)


LN_ROWS = 128


def _mm_res_ln_kernel(a_ref, w_ref, x_ref, g_ref, b_ref, of_ref, ob_ref):
    tm = a_ref.shape[0]
    pieces = [slice(r, min(r + LN_ROWS, tm)) for r in range(0, tm, LN_ROWS)]

    def mm(rows):
        return jnp.dot(a_ref[rows, :], w_ref[...], preferred_element_type=F32)

    def ln(rows, acc):
        y = ALPHA * x_ref[rows, :] + acc
        mu = jnp.mean(y, axis=-1, keepdims=True)
        d = y - mu
        var = jnp.mean(d * d, axis=-1, keepdims=True)
        out = d * lax.rsqrt(var + LN_EPS) * g_ref[...] + b_ref[...]
        of_ref[rows, :] = out
        ob_ref[rows, :] = out.astype(BF16)

    acc = mm(pieces[0])
    for i, rows in enumerate(pieces):
        nxt = mm(pieces[i + 1]) if i + 1 < len(pieces) else None
        ln(rows, acc)
        acc = nxt


def _mm_res_ln(a, w, x, g, b, tm=512):
    m, k = a.shape
    n = w.shape[1]
    tm = _row_tile(m, tm)
    row = lambda i: (i, 0)
    fixed = lambda i: (0, 0)
    return pl.pallas_call(
        _mm_res_ln_kernel,
        grid=(m // tm,),
        in_specs=[pl.BlockSpec((tm, k), row), pl.BlockSpec((k, n), fixed),
                  pl.BlockSpec((tm, n), row), pl.BlockSpec((1, n), fixed),
                  pl.BlockSpec((1, n), fixed)],
        out_specs=[pl.BlockSpec((tm, n), row), pl.BlockSpec((tm, n), row)],
        out_shape=[jax.ShapeDtypeStruct((m, n), F32), jax.ShapeDtypeStruct((m, n), BF16)],
        compiler_params=_params("parallel"),
        name="mm_res_ln",
    )(a, w, x, g.reshape(1, n), b.reshape(1, n))


_W1_CQ = (0, A_Q_LORA)
_W1_CKV = (A_Q_LORA, A_Q_LORA + A_KV_LORA)
_W1_KR = (_W1_CKV[1], _W1_CKV[1] + A_ROPE)
_W1_KRS = (_W1_CKV[1] + LANES, _W1_CKV[1] + LANES + A_ROPE)
_W1_COLS = _W1_CKV[1] + 2 * LANES


def _rms(v, g):
    return v * lax.rsqrt(jnp.mean(v * v, axis=-1, keepdims=True) + RMS_EPS) * g


def _mla_proj_kernel(x_ref, w1_ref, gq_ref, gkv_ref, wq_ref, cq_ref, sq_ref, ck_ref, sk_ref,
                     q_ref, lat_ref, ckv_ref, kpe_ref):
    y = jnp.dot(x_ref[...], w1_ref[...], preferred_element_type=F32)
    cq = _rms(y[:, _W1_CQ[0]:_W1_CQ[1]], gq_ref[...]).astype(BF16)
    ckv = _rms(y[:, _W1_CKV[0]:_W1_CKV[1]], gkv_ref[...])
    kpe = y[:, _W1_KR[0]:_W1_KR[1]] * ck_ref[...] + y[:, _W1_KRS[0]:_W1_KRS[1]] * sk_ref[...]
    ckv_ref[...] = ckv
    kpe_ref[...] = kpe
    lat_ref[:, 0:A_KV_LORA] = ckv.astype(BF16)
    lat_ref[:, A_KV_LORA:A_KV_LORA + A_ROPE] = kpe.astype(BF16)
    lat_ref[:, A_KV_LORA + A_ROPE:] = jnp.zeros(
        (lat_ref.shape[0], A_LAT_PAD - A_KV_LORA - A_ROPE), BF16)
    sw0 = A_HEADS * A_QK_PAD
    for h in range(A_HEADS):
        lo, hi = h * A_QK_PAD, (h + 1) * A_QK_PAD
        qp = jnp.dot(cq, wq_ref[:, lo:hi], preferred_element_type=F32)
        qs = jnp.dot(cq, wq_ref[:, sw0 + lo:sw0 + hi], preferred_element_type=F32)
        q_ref[:, lo:hi] = (qp * cq_ref[...] + qs * sq_ref[...]).astype(BF16)


def _mla_proj(xb, w1, gq, gkv, wq, cosq, sinq, cosk, sink, tm=512):
    m = xb.shape[0]
    tm = _row_tile(m, tm)
    row = lambda i: (i, 0)
    fixed = lambda i: (0, 0)
    nq = A_HEADS * A_QK_PAD
    return pl.pallas_call(
        _mla_proj_kernel,
        grid=(m // tm,),
        in_specs=[pl.BlockSpec((tm, D_MODEL), row), pl.BlockSpec(w1.shape, fixed),
                  pl.BlockSpec((1, A_Q_LORA), fixed), pl.BlockSpec((1, A_KV_LORA), fixed),
                  pl.BlockSpec(wq.shape, fixed),
                  pl.BlockSpec((tm, A_QK_PAD), row), pl.BlockSpec((tm, A_QK_PAD), row),
                  pl.BlockSpec((tm, A_ROPE), row), pl.BlockSpec((tm, A_ROPE), row)],
        out_specs=[pl.BlockSpec((tm, nq), row), pl.BlockSpec((tm, A_LAT_PAD), row),
                   pl.BlockSpec((tm, A_KV_LORA), row), pl.BlockSpec((tm, A_ROPE), row)],
        out_shape=[jax.ShapeDtypeStruct((m, nq), BF16), jax.ShapeDtypeStruct((m, A_LAT_PAD), BF16),
                   jax.ShapeDtypeStruct((m, A_KV_LORA), F32), jax.ShapeDtypeStruct((m, A_ROPE), F32)],
        compiler_params=_params("parallel"),
        name="mla_proj",
    )(xb, w1, gq.reshape(1, -1), gkv.reshape(1, -1), wq, cosq, sinq, cosk, sink)


def _flash_kernel(q_ref, k_ref, v_ref, o_ref, m_s, l_s, acc_s, *, nh, dq, dv, tq, tk, nk, q_off,
                  chunk_causal):
    iq = pl.program_id(2)
    ik = pl.program_id(3)
    q_lo = q_off + iq * tq
    q_hi = q_lo + tq - 1
    if chunk_causal:
        vis_lo = (q_lo // CHUNK) * CHUNK + CHUNK - 1
        vis_hi = (q_hi // CHUNK) * CHUNK + CHUNK - 1
    else:
        vis_lo, vis_hi = q_lo, q_hi
    k_lo = ik * tk
    needed = k_lo <= vis_hi
    unmasked = k_lo + tk - 1 <= vis_lo

    @pl.when(ik == 0)
    def _init():
        m_s[...] = jnp.full(m_s.shape, -jnp.inf, F32)
        l_s[...] = jnp.zeros(l_s.shape, F32)
        acc_s[...] = jnp.zeros(acc_s.shape, F32)

    def step(masked):
        if masked:
            q_pos = q_lo + lax.broadcasted_iota(jnp.int32, (tq, tk), 0)
            k_pos = k_lo + lax.broadcasted_iota(jnp.int32, (tq, tk), 1)
            if chunk_causal:
                shift = CHUNK.bit_length() - 1
                mask = jnp.right_shift(k_pos, shift) <= jnp.right_shift(q_pos, shift)
            else:
                mask = k_pos <= q_pos
        for h in range(nh):
            q = q_ref[:, h * dq:(h + 1) * dq]
            k = k_ref[:, h * dq:(h + 1) * dq]
            v = v_ref[:, h * dv:(h + 1) * dv]
            s = lax.dot_general(q, k, (((1,), (1,)), ((), ())), preferred_element_type=F32)
            if masked:
                s = jnp.where(mask, s, NEG_INF)
            m_prev = m_s[h]
            m_new = jnp.maximum(m_prev, jnp.max(s, axis=-1, keepdims=True))
            alpha = jnp.exp(m_prev - m_new)
            p = jnp.exp(s - m_new)
            l_s[h] = alpha * l_s[h] + jnp.sum(p, axis=-1, keepdims=True)
            acc_s[h] = alpha * acc_s[h] + jnp.dot(p.astype(BF16), v, preferred_element_type=F32)
            m_s[h] = m_new

    @pl.when(needed & unmasked)
    def _plain():
        step(False)

    @pl.when(needed & jnp.logical_not(unmasked))
    def _masked():
        step(True)

    @pl.when(ik == nk - 1)
    def _fin():
        for h in range(nh):
            o_ref[:, h * dv:(h + 1) * dv] = (acc_s[h] / l_s[h]).astype(o_ref.dtype)


def _flash(q, k, v, *, nh, dq, dv, n_hblk, k_col0, v_col0, tq, tk, q_off, chunk_causal):
    b, t_q = q.shape[0], q.shape[1]
    t_k = k.shape[1]
    tq = tq if t_q % tq == 0 else t_q
    tk = tk if t_k % tk == 0 else t_k
    nq, nk = t_q // tq, t_k // tk

    def last_blk(i):
        q_hi = q_off + (i + 1) * tq - 1
        vis = (q_hi // CHUNK) * CHUNK + CHUNK - 1 if chunk_causal else q_hi
        return jnp.minimum(vis // tk, nk - 1)

    in_specs = [
        pl.BlockSpec((None, tq, nh * dq), lambda bi, h, i, j: (bi, i, h)),
        pl.BlockSpec((None, tk, nh * dq), lambda bi, h, i, j: (bi, jnp.minimum(j, last_blk(i)), k_col0 + h)),
        pl.BlockSpec((None, tk, nh * dv), lambda bi, h, i, j: (bi, jnp.minimum(j, last_blk(i)), v_col0 + h)),
    ]
    kern = functools.partial(_flash_kernel, nh=nh, dq=dq, dv=dv, tq=tq, tk=tk, nk=nk, q_off=q_off,
                             chunk_causal=chunk_causal)
    return pl.pallas_call(
        kern,
        grid=(b, n_hblk, nq, nk),
        in_specs=in_specs,
        out_specs=pl.BlockSpec((None, tq, nh * dv), lambda bi, h, i, j: (bi, i, h)),
        out_shape=jax.ShapeDtypeStruct((b, t_q, n_hblk * nh * dv), BF16),
        scratch_shapes=[pltpu.VMEM((nh, tq, 1), F32), pltpu.VMEM((nh, tq, 1), F32),
                        pltpu.VMEM((nh, tq, dv), F32)],
        compiler_params=_params("parallel", "parallel", "parallel", "arbitrary"),
        name="flash_step",
    )(q, k, v)


FLASH_T = 1024
FLASH_QC = 256
FLASH_KC = 256
FLASH_AHEAD = 8
A_HEADS_PER_STEP = 2
C_HEADS_PER_STEP = 4
LOG2E = math.log2(math.e)


def _flash_t_kernel(q_ref, k_ref, vt_ref, o_ref, m_s, acc_s, *, nh, dq, dv, t, chunk_causal):
    iq = pl.program_id(2)
    dva = dv + ONES_ROWS
    m_s[...] = jnp.full(m_s.shape, -jnp.inf, F32)
    acc_s[...] = jnp.zeros(acc_s.shape, F32)

    qc, kc = FLASH_QC, FLASH_KC

    def run(blocks):
        def n_keys(masked, g, c):
            return min(kc, (c + 1) * qc - g * kc) if masked else kc

        chains = [(j, masked, g, h, c) for j, masked in blocks
                  for g in range(t // kc) for h in range(nh) for c in range(t // qc)
                  if n_keys(masked, g, c) > 0]

        def qk(j, masked, g, h, c):
            row0 = pl.multiple_of(j * t, t) + g * kc
            return lax.dot_general(k_ref[pl.ds(row0, n_keys(masked, g, c)), h * dq:(h + 1) * dq],
                                   q_ref[c * qc:(c + 1) * qc, h * dq:(h + 1) * dq],
                                   (((1,), (1,)), ((), ())), preferred_element_type=F32)

        def softmax_pv(j, masked, g, h, c, s):
            cols = slice(c * qc, (c + 1) * qc)
            nk = n_keys(masked, g, c)
            if masked and g * kc + nk > c * qc:
                k_pos = lax.broadcasted_iota(jnp.int32, (nk, qc), 0) + g * kc
                q_pos = lax.broadcasted_iota(jnp.int32, (nk, qc), 1) + c * qc
                if chunk_causal:
                    shift = CHUNK.bit_length() - 1
                    mask = jnp.right_shift(k_pos, shift) <= jnp.right_shift(q_pos, shift)
                else:
                    mask = k_pos <= q_pos
                s = jnp.where(mask, s, NEG_INF)
            m_prev = m_s[h, :, cols]
            m_new = jnp.maximum(m_prev, jnp.max(s, axis=0, keepdims=True))
            alpha = jnp.exp2(m_prev - m_new)
            p = jnp.exp2(s - m_new)
            acc_s[h, :, cols] = alpha * acc_s[h, :, cols] + jnp.dot(
                vt_ref[j, h * dva:(h + 1) * dva, g * kc:g * kc + nk], p.astype(BF16),
                preferred_element_type=F32)
            m_s[h, :, cols] = m_new

        pending = [qk(*chain) for chain in chains[:FLASH_AHEAD]]
        for i, chain in enumerate(chains):
            if i + FLASH_AHEAD < len(chains):
                pending.append(qk(*chains[i + FLASH_AHEAD]))
            softmax_pv(*chain, pending.pop(0))

    def block_pair(jj, carry):
        run([(2 * jj, False), (2 * jj + 1, False)])
        return carry

    lax.fori_loop(0, iq // 2, block_pair, 0)

    @pl.when(iq % 2 == 1)
    def _odd():
        run([(iq - 1, False)])

    run([(iq, True)])
    out = jnp.concatenate([acc_s[h, 0:dv] / acc_s[h, dv:dv + 1] for h in range(nh)], axis=0)
    o_ref[...] = out.T.astype(o_ref.dtype)


def _flash_t(q, k, k_col0, vt, *, nh, dq, dv, n_hblk, chunk_causal):
    b, t_all = q.shape[0], q.shape[1]
    t = FLASH_T
    nblk = t_all // t
    kern = functools.partial(_flash_t_kernel, nh=nh, dq=dq, dv=dv, t=t, chunk_causal=chunk_causal)
    return pl.pallas_call(
        kern,
        grid=(b, n_hblk, nblk),
        in_specs=[pl.BlockSpec((None, t, nh * dq), lambda bi, h, i: (bi, i, h)),
                  pl.BlockSpec((None, t_all, nh * dq), lambda bi, h, i: (bi, 0, k_col0 + h),
                               pipeline_mode=pl.Buffered(1)),
                  pl.BlockSpec((None, nblk, nh * (dv + ONES_ROWS), t), lambda bi, h, i: (bi, 0, h, 0),
                               pipeline_mode=pl.Buffered(1))],
        out_specs=pl.BlockSpec((None, t, nh * dv), lambda bi, h, i: (bi, i, h)),
        out_shape=jax.ShapeDtypeStruct((b, t_all, n_hblk * nh * dv), BF16),
        scratch_shapes=[pltpu.VMEM((nh, 1, t), F32), pltpu.VMEM((nh, dv + ONES_ROWS, t), F32)],
        compiler_params=_params("parallel", "parallel", "parallel"),
        name="flash_t_chunk" if chunk_causal else "flash_t_frame",
    )(q, k, vt)


B_HEADS_PER_STEP = LANES // B_HEAD_DIM


def _step_attn_kernel(*refs, heads, dh, forget):
    if forget:
        q_ref, kp_ref, vp_ref, kn_ref, vn_ref, bp_ref, bn_ref, fq_ref, fkp_ref, fkn_ref, o_ref = refs
    else:
        q_ref, kp_ref, vp_ref, kn_ref, vn_ref, bp_ref, bn_ref, o_ref = refs
    nt = (((1,), (1,)), ((), ()))
    n_bias = bp_ref.shape[0]
    for h in range(heads):
        cols = slice(h * dh, (h + 1) * dh)
        q = q_ref[:, cols]
        sp = lax.dot_general(q, kp_ref[:, cols].astype(BF16), nt, preferred_element_type=F32)
        sn = lax.dot_general(q, kn_ref[:, cols], nt, preferred_element_type=F32)
        sp = sp + bp_ref[h % n_bias]
        sn = sn + bn_ref[h % n_bias]
        if forget:
            fq = fq_ref[:, h:h + 1]
            sp = sp + fq - fkp_ref[h:h + 1, :]
            sn = sn + fq - fkn_ref[h:h + 1, :]
        m = jnp.maximum(jnp.max(sp, axis=-1, keepdims=True), jnp.max(sn, axis=-1, keepdims=True))
        pp = jnp.exp(sp - m)
        pn = jnp.exp(sn - m)
        l = jnp.sum(pp, axis=-1, keepdims=True) + jnp.sum(pn, axis=-1, keepdims=True)
        o = (jnp.dot(pp.astype(BF16), vp_ref[:, cols].astype(BF16), preferred_element_type=F32)
             + jnp.dot(pn.astype(BF16), vn_ref[:, cols], preferred_element_type=F32))
        o_ref[:, cols] = (o / l).astype(o_ref.dtype)


def _band_bias(rel_bias, q_pos, k_pos):
    nq, nk = len(q_pos), len(k_pos)
    assert (np.diff(q_pos) == 1).all() and (np.diff(k_pos) == 1).all()
    m = np.arange(nq + nk - 1)
    u = rel_bias[:, np.clip(q_pos[0] - k_pos[0] + nq - 1 - m, -B_REL_CLIP, B_REL_CLIP) + B_REL_CLIP]
    period = nq + nk
    w = jnp.concatenate([u[:, nq - 1:], jnp.zeros((u.shape[0], 1), u.dtype), u[:, :nq - 1]], axis=1)
    skew = jnp.tile(w, (1, nq))[:, :nq * (period - 1)].reshape(-1, nq, period - 1)[:, :, :nk]
    qc = q_pos[:, None] // CHUNK
    kc = k_pos[None, :] // CHUNK
    mask = (kc <= qc) & (kc >= qc - B_LEFT_CHUNKS) & (k_pos[None, :] >= 0)
    return jnp.where(jnp.asarray(mask)[None], skew, NEG_INF).astype(F32)


def _step_attn(q, k_past, v_past, k_new, v_new, bias, heads, f_cum=None):
    b, t, d = q.shape
    p = k_past.shape[1]
    stream = lambda bi: (bi, 0, 0)
    fixed = lambda bi: (0, 0, 0)
    args = [q, k_past, v_past, k_new, v_new, bias[:, :, :p], bias[:, :, p:]]
    in_specs = [pl.BlockSpec((None, t, d), stream), pl.BlockSpec((None, p, d), stream),
                pl.BlockSpec((None, p, d), stream), pl.BlockSpec((None, t, d), stream),
                pl.BlockSpec((None, t, d), stream),
                pl.BlockSpec((bias.shape[0], t, p), fixed), pl.BlockSpec((bias.shape[0], t, t), fixed)]
    if f_cum is not None:
        fq = jnp.pad(f_cum[:, p:], ((0, 0), (0, 0), (0, LANES - heads)))
        fk = f_cum.transpose(0, 2, 1)
        args += [fq, fk[:, :, :p], fk[:, :, p:]]
        in_specs += [pl.BlockSpec((None, t, LANES), stream), pl.BlockSpec((None, heads, p), stream),
                     pl.BlockSpec((None, heads, t), stream)]
    return pl.pallas_call(
        functools.partial(_step_attn_kernel, heads=heads, dh=d // heads, forget=f_cum is not None),
        grid=(b,),
        in_specs=in_specs,
        out_specs=pl.BlockSpec((None, t, d), stream),
        out_shape=jax.ShapeDtypeStruct((b, t, d), BF16),
        compiler_params=_params("parallel"),
        name="step_attn",
    )(*args)


BAND_T = 4 * CHUNK


def _band_t_kernel(*refs, nkb, t):
    q_ref = refs[0]
    k_refs = refs[1:1 + nkb]
    vt_refs = refs[1 + nkb:1 + 2 * nkb]
    bias_ref, o_ref = refs[1 + 2 * nkb], refs[2 + 2 * nkb]
    iq = pl.program_id(1)
    dh = B_HEAD_DIM

    def qk(h):
        cols = slice(h * dh, (h + 1) * dh)
        parts = []
        for j in range(nkb):
            s = lax.dot_general(k_refs[j][:, cols], q_ref[:, cols], (((1,), (1,)), ((), ())),
                                preferred_element_type=F32)
            if j < nkb - 1:
                s = jnp.where(iq >= nkb - 1 - j, s, NEG_INF)
            parts.append(s)
        return jnp.concatenate(parts, axis=0) + bias_ref[h]

    def softmax_pv(h, s):
        p = jnp.exp2(s - jnp.max(s, axis=0, keepdims=True))
        l = jnp.sum(p, axis=0, keepdims=True)
        pb = p.astype(BF16)
        o = functools.reduce(lambda a, c: a + c, [
            jnp.dot(vt_refs[j][h * dh:(h + 1) * dh, :], pb[j * t:(j + 1) * t], preferred_element_type=F32)
            for j in range(nkb)])
        return o / l

    heads = list(range(B_HEADS))
    pending = [qk(h) for h in heads[:FLASH_AHEAD]]
    outs = []
    for h in heads:
        if h + FLASH_AHEAD < B_HEADS:
            pending.append(qk(h + FLASH_AHEAD))
        outs.append(softmax_pv(h, pending.pop(0)))
        if len(outs) == B_HEADS_PER_STEP:
            c0 = (h + 1 - B_HEADS_PER_STEP) * dh
            o_ref[:, c0:c0 + LANES] = jnp.concatenate(outs, axis=0).T.astype(o_ref.dtype)
            outs = []


def _band_t(q, kv, vt, bias_t, *, nkb):
    b, t_all = q.shape[0], q.shape[1]
    t = BAND_T

    def back(j):
        return nkb - 1 - j

    k_specs = [pl.BlockSpec((None, t, D_MODEL), lambda bi, i, j=j: (bi, jnp.maximum(i - back(j), 0), 0))
               for j in range(nkb)]
    vt_specs = [pl.BlockSpec((None, None, D_MODEL, t),
                             lambda bi, i, j=j: (bi, jnp.maximum(i - back(j), 0), 0, 0))
                for j in range(nkb)]
    return pl.pallas_call(
        functools.partial(_band_t_kernel, nkb=nkb, t=t),
        grid=(b, t_all // t),
        in_specs=([pl.BlockSpec((None, t, D_MODEL), lambda bi, i: (bi, i, 0))] + k_specs + vt_specs
                  + [pl.BlockSpec(bias_t.shape, lambda bi, i: (0, 0, 0), pipeline_mode=pl.Buffered(1))]),
        out_specs=pl.BlockSpec((None, t, D_MODEL), lambda bi, i: (bi, i, 0)),
        out_shape=jax.ShapeDtypeStruct((b, t_all, D_MODEL), BF16),
        compiler_params=_params("parallel", "parallel"),
        name="band_t",
    )(q, *([kv] * nkb), *([vt] * nkb), bias_t)


def _logf_kernel(x_ref, w_ref, b_ref, o_ref):
    z = jnp.dot(x_ref[...], w_ref[...], preferred_element_type=F32) + b_ref[...]
    o_ref[...] = -(jnp.maximum(-z, 0.0) + jnp.log1p(jnp.exp(-jnp.abs(z))))


def _logf(xb, w_pad, b_pad, tm=512):
    m = xb.shape[0]
    tm = _row_tile(m, tm)
    return pl.pallas_call(
        _logf_kernel,
        grid=(m // tm,),
        in_specs=[pl.BlockSpec((tm, D_MODEL), lambda i: (i, 0)),
                  pl.BlockSpec((D_MODEL, LANES), lambda i: (0, 0)),
                  pl.BlockSpec((1, LANES), lambda i: (0, 0))],
        out_specs=pl.BlockSpec((tm, LANES), lambda i: (i, 0)),
        out_shape=jax.ShapeDtypeStruct((m, LANES), F32),
        compiler_params=_params("parallel"),
        name="logf",
    )(xb, w_pad, b_pad)


def _split3(x):
    hi = x.astype(BF16)
    r = x - hi.astype(F32)
    mid = r.astype(BF16)
    lo = (r - mid.astype(F32)).astype(BF16)
    return hi, mid, lo


N_PIECES = 3
PIECE_LANES = 16


def _cumsum_kernel(x_ref, o_ref, pk_ref, carry, *, tc):
    @pl.when(pl.program_id(1) == 0)
    def _():
        carry[...] = jnp.zeros(carry.shape, F32)

    tri = (lax.broadcasted_iota(jnp.int32, (tc, tc), 0)
           >= lax.broadcasted_iota(jnp.int32, (tc, tc), 1)).astype(BF16)
    c = functools.reduce(lambda a, b: a + b, [jnp.dot(tri, piece, preferred_element_type=F32)
                                              for piece in _split3(x_ref[...])])
    out = c + carry[0:1, :]
    o_ref[...] = out
    src = lax.broadcasted_iota(jnp.int32, (LANES, LANES), 0)
    dst = lax.broadcasted_iota(jnp.int32, (LANES, LANES), 1)
    packed = functools.reduce(lambda a, b: a + b, [
        jnp.dot(piece, ((dst == src + k * PIECE_LANES) & (src < PIECE_LANES)).astype(BF16),
                preferred_element_type=F32)
        for k, piece in enumerate(_split3(out * LOG2E))])
    pk_ref[...] = packed.astype(BF16)
    carry[...] = jnp.broadcast_to(out[tc - 1:tc, :], carry.shape)


def _cumsum(x, tc=256):
    b, t, _ = x.shape
    spec = pl.BlockSpec((None, tc, LANES), lambda bi, i: (bi, i, 0))
    return pl.pallas_call(
        functools.partial(_cumsum_kernel, tc=tc),
        grid=(b, t // tc),
        in_specs=[spec],
        out_specs=[spec] * 2,
        out_shape=[jax.ShapeDtypeStruct(x.shape, F32), jax.ShapeDtypeStruct(x.shape, BF16)],
        scratch_shapes=[pltpu.VMEM((SUBLANES, LANES), F32)],
        compiler_params=_params("parallel", "arbitrary"),
        name="cumsum",
    )(x)


FFN_CHUNK = 256
CONV_ROWS = SUBLANES


def _ffn_up_kernel(x_ref, w_ref, c_ref, p_ref, act_ref, s_ref, halo, *, tm):
    @pl.when(pl.program_id(1) == 0)
    def _():
        halo[...] = p_ref[...]

    x = x_ref[...]
    n_chunks = D_FF // FFN_CHUNK
    groups = tm // CONV_ROWS
    row = lax.broadcasted_iota(jnp.int32, (groups, CONV_ROWS, FFN_CHUNK), 1)

    def up(c, half):
        col = half * D_FF + c * FFN_CHUNK
        return jnp.dot(x, w_ref[:, col:col + FFN_CHUNK], preferred_element_type=F32)

    def conv(h, c, half):
        cols = slice(half * D_FF + c * FFN_CHUNK, half * D_FF + (c + 1) * FFN_CHUNK)
        ext = jnp.concatenate([halo[:, cols], h], axis=0).reshape(groups + 1, CONV_ROWS, FFN_CHUNK)
        hc = c_ref[CONV_W:CONV_W + 1, cols] + c_ref[CONV_W - 1:CONV_W, cols] * h
        for s in range(1, CONV_W):
            rot = pltpu.roll(ext, s, axis=1)
            shifted = jnp.where(row < s, rot[:groups], rot[1:]).reshape(tm, FFN_CHUNK)
            hc = hc + c_ref[CONV_W - 1 - s:CONV_W - s, cols] * shifted
        tail = h[tm - CONV_ROWS:tm]
        halo[:, cols] = tail
        s_ref[:, cols] = tail
        return hc

    pending = [(up(0, 0), up(0, 1))]
    for c in range(n_chunks):
        if c + 1 < n_chunks:
            pending.append((up(c + 1, 0), up(c + 1, 1)))
        ha, hg = pending.pop(0)
        a = conv(ha, c, 0)
        g = conv(hg, c, 1)
        act_ref[:, c * FFN_CHUNK:(c + 1) * FFN_CHUNK] = (g * jax.nn.sigmoid(g) * a).astype(BF16)


def _ffn_up(xb, w_up, conv_tab, past, tm=256):
    b, t, _ = xb.shape
    tm = _row_tile(t, tm)
    fixed = lambda bi, ti: (0, 0)
    return pl.pallas_call(
        functools.partial(_ffn_up_kernel, tm=tm),
        grid=(b, t // tm),
        in_specs=[pl.BlockSpec((None, tm, D_MODEL), lambda bi, ti: (bi, ti, 0)),
                  pl.BlockSpec((D_MODEL, 2 * D_FF), fixed),
                  pl.BlockSpec((CONV_ROWS, 2 * D_FF), fixed),
                  pl.BlockSpec((None, CONV_ROWS, 2 * D_FF), lambda bi, ti: (bi, 0, 0))],
        out_specs=[pl.BlockSpec((None, tm, D_FF), lambda bi, ti: (bi, ti, 0)),
                   pl.BlockSpec((None, CONV_ROWS, 2 * D_FF), lambda bi, ti: (bi, 0, 0))],
        out_shape=[jax.ShapeDtypeStruct((b, t, D_FF), BF16),
                   jax.ShapeDtypeStruct((b, CONV_ROWS, 2 * D_FF), F32)],
        scratch_shapes=[pltpu.VMEM((CONV_ROWS, 2 * D_FF), F32)],
        compiler_params=_params("parallel", "arbitrary"),
        name="ffn_up",
    )(xb, w_up, conv_tab, past)


def _rope_tables(pos, batch, q_scale):
    half = A_ROPE // 2
    inv_freq = ROPE_THETA ** (-jnp.arange(half, dtype=F32) / half)
    ang = pos.astype(F32)[:, None] * inv_freq
    cos, sin = jnp.cos(ang), jnp.sin(ang)
    cosk = jnp.concatenate([cos, cos], axis=-1)
    sink = jnp.concatenate([sin, sin], axis=-1)
    t = pos.shape[0]
    pad = jnp.zeros((t, A_QK_PAD - A_NOPE - A_ROPE), F32)
    cosq = q_scale * jnp.concatenate([jnp.ones((t, A_NOPE), F32), cosk, pad], axis=-1)
    sinq = q_scale * jnp.concatenate([jnp.zeros((t, A_NOPE), F32), sink, pad], axis=-1)
    return tuple(jnp.tile(a, (batch, 1)) for a in (cosq, sinq, cosk, sink))


def _swap_halves(w):
    half = w.shape[-1] // 2
    return jnp.concatenate([-w[..., half:], w[..., :half]], axis=-1)


def _mla_weights(w_dq, w_dkv, w_kr, w_uq, w_uk, w_uv):
    zc = jnp.zeros((D_MODEL, LANES - A_ROPE), F32)
    w1 = jnp.concatenate([w_dq, w_dkv, w_kr, zc, _swap_halves(w_kr), zc], axis=1).astype(BF16)
    wq = w_uq.reshape(A_Q_LORA, A_HEADS, A_NOPE + A_ROPE)
    nope, rope = wq[..., :A_NOPE], wq[..., A_NOPE:]
    zpad = jnp.zeros((A_Q_LORA, A_HEADS, A_QK_PAD - A_NOPE - A_ROPE), F32)
    w_cat = jnp.concatenate([nope, rope, zpad], axis=-1).reshape(A_Q_LORA, -1)
    w_sw = jnp.concatenate([jnp.zeros_like(nope), _swap_halves(rope), zpad], axis=-1).reshape(A_Q_LORA, -1)
    wq2 = jnp.concatenate([w_cat, w_sw], axis=1).astype(BF16)
    wk = jnp.zeros((A_LAT_PAD, A_HEADS, A_QK_PAD), F32)
    wk = wk.at[:A_KV_LORA, :, :A_NOPE].set(w_uk)
    eye = jnp.broadcast_to(jnp.eye(A_ROPE, dtype=F32)[:, None, :], (A_ROPE, A_HEADS, A_ROPE))
    wk = wk.at[A_KV_LORA:A_KV_LORA + A_ROPE, :, A_NOPE:A_NOPE + A_ROPE].set(eye)
    wv = jnp.zeros((A_LAT_PAD, A_HEADS * A_V), F32).at[:A_KV_LORA].set(w_uv.reshape(A_KV_LORA, -1))
    wkv = jnp.concatenate([wk.reshape(A_LAT_PAD, -1), wv], axis=1).astype(BF16)
    return w1, wq2, wkv


def _mla_mixer(xb, b, t, pos, ckv_past, kpe_past, w, i):
    w1, wq2, wkv = _mla_weights(w['a_w_dq'][i], w['a_w_dkv'][i], w['a_w_kr'][i], w['a_w_uq'][i],
                                w['a_w_uk'][i], w['a_w_uv'][i])
    resident = ckv_past is None and t % FLASH_T == 0
    q_scale = (A_NOPE + A_ROPE) ** -0.5 * (LOG2E if resident else 1.0)
    q, lat, ckv, kpe = _mla_proj(xb, w1, w['a_g_q'][i], w['a_g_kv'][i], wq2,
                                 *_rope_tables(pos, b, q_scale))
    lat = lat.reshape(b, t, A_LAT_PAD)
    q_off = 0
    if ckv_past is not None:
        p_len = ckv_past.shape[1]
        past = jnp.concatenate(
            [ckv_past, kpe_past, jnp.zeros((b, p_len, A_LAT_PAD - A_KV_LORA - A_ROPE), F32)], axis=-1)
        lat = jnp.concatenate([past.astype(BF16), lat], axis=1)
        q_off = p_len
    t_k = lat.shape[1]
    lat2 = lat.reshape(b * t_k, A_LAT_PAD)
    n_k = A_HEADS * A_QK_PAD
    if resident:
        (k_cat,) = _mm(lat2, wkv[:, :n_k], [BF16])
        vt = _proj_t(lat2, wkv[:, n_k:], FLASH_T, A_V).reshape(b, t_k // FLASH_T, -1, FLASH_T)
        o = _flash_t(q.reshape(b, t, -1), k_cat.reshape(b, t_k, n_k), 0, vt, nh=A_HEADS_PER_STEP,
                     dq=A_QK_PAD, dv=A_V, n_hblk=A_HEADS // A_HEADS_PER_STEP, chunk_causal=True)
    else:
        (kv,) = _mm(lat2, wkv, [BF16])
        kv = kv.reshape(b, t_k, -1)
        o = _flash(q.reshape(b, t, -1), kv, kv, nh=A_HEADS, dq=A_QK_PAD, dv=A_V, n_hblk=1,
                   k_col0=0, v_col0=A_QK_PAD // A_V, tq=512, tk=512, q_off=q_off, chunk_causal=True)
    return o.reshape(b * t, A_HEADS * A_V), ckv.reshape(b, t, -1), kpe.reshape(b, t, -1)


def _band_mixer(xb, b, t, pos0, k_past, v_past, w, i):
    prompt = k_past is None
    assert not prompt or t % BAND_T == 0
    w_qkv = w['b_w_qkv'][i]
    (q,) = _mm(xb, (w_qkv[:, :D_MODEL] * (B_HEAD_DIM ** -0.5 * (LOG2E if prompt else 1.0))).astype(BF16),
               [BF16])
    q = q.reshape(b, t, D_MODEL)
    w_kv = w_qkv[:, D_MODEL:].astype(BF16)
    heads = lambda a, rows: a.reshape(b, rows, B_HEADS, B_HEAD_DIM)
    if prompt:
        nkb = B_WIN // BAND_T + 1
        bias = _band_bias(w['b_rel_bias'][i], B_WIN + np.arange(BAND_T), np.arange(B_WIN + BAND_T))
        (kvb,) = _mm(xb, w_kv, [BF16])
        kvb = kvb.reshape(b, t, 2 * D_MODEL)
        vt = _proj_t(xb, w_qkv[:, 2 * D_MODEL:], BAND_T).reshape(b, t // BAND_T, D_MODEL, BAND_T)
        o = _band_t(q, kvb, vt, LOG2E * bias.transpose(0, 2, 1), nkb=nkb)
        keep = min(B_WIN, t)
        x_tail = xb.reshape(b, t, D_MODEL)[:, t - keep:].reshape(b * keep, D_MODEL)
        (kv_tail,) = _mm(x_tail, w_kv, [F32])
        k_new, v_new = heads(kv_tail[:, :D_MODEL], keep), heads(kv_tail[:, D_MODEL:], keep)
    else:
        kv32, kvb = _mm(xb, w_kv, [F32, BF16])
        k32, v32 = heads(kv32[:, :D_MODEL], t), heads(kv32[:, D_MODEL:], t)
        p_len = k_past.shape[1]
        kvb = kvb.reshape(b, t, 2 * D_MODEL)
        q_pos = pos0 + np.arange(t)
        k_pos = np.concatenate([np.arange(pos0 - p_len, pos0), q_pos])
        bias = _band_bias(w['b_rel_bias'][i], q_pos, k_pos)
        o = _step_attn(q, k_past.reshape(b, p_len, D_MODEL), v_past.reshape(b, p_len, D_MODEL),
                       kvb[:, :, :D_MODEL], kvb[:, :, D_MODEL:], bias, B_HEADS)
        k_new, v_new = k32, v32
    return o.reshape(b * t, D_MODEL), k_new, v_new


C_AUG = LANES


def _aug_kernel(x_ref, w_ref, f_ref, p_ref, b_ref, o_ref):
    acc = (jnp.dot(x_ref[...], w_ref[...], preferred_element_type=F32) + b_ref[...]
           + jnp.dot(f_ref[...], p_ref[...], preferred_element_type=F32))
    o_ref[...] = acc.astype(BF16)


def _aug_tables():
    n = C_HEADS * C_AUG
    place = np.zeros((LANES, 2 * n), np.float32)
    ones = np.zeros((1, 2 * n), np.float32)
    for h in range(C_HEADS):
        base = h * C_AUG + C_HEAD_DIM
        for piece in range(N_PIECES):
            ones[0, base + piece] = 1.0
            place[piece * PIECE_LANES + h, base + N_PIECES + piece] = 1.0
            place[piece * PIECE_LANES + h, n + base + piece] = -1.0
            ones[0, n + base + N_PIECES + piece] = 1.0
    return jnp.asarray(place, BF16), jnp.asarray(ones)


def _aug_qk(xb, w_aug, pieces, tm=512, tn=1024):
    m = xb.shape[0]
    n = w_aug.shape[1]
    tm = _row_tile(m, tm)
    place, ones = _aug_tables()
    row = lambda i, j: (i, 0)
    return pl.pallas_call(
        _aug_kernel,
        grid=(m // tm, n // tn),
        in_specs=[pl.BlockSpec((tm, D_MODEL), row), pl.BlockSpec((D_MODEL, tn), lambda i, j: (0, j)),
                  pl.BlockSpec((tm, LANES), row), pl.BlockSpec((LANES, tn), lambda i, j: (0, j)),
                  pl.BlockSpec((1, tn), lambda i, j: (0, j))],
        out_specs=pl.BlockSpec((tm, tn), lambda i, j: (i, j)),
        out_shape=jax.ShapeDtypeStruct((m, n), BF16),
        compiler_params=_params("parallel", "parallel"),
        name="aug_qk",
    )(xb, w_aug, pieces, place, ones)


def _pad_heads(w, scale):
    w = (w * scale).reshape(D_MODEL, C_HEADS, C_HEAD_DIM)
    return jnp.pad(w, ((0, 0), (0, 0), (0, C_AUG - C_HEAD_DIM))).reshape(D_MODEL, C_HEADS * C_AUG)


def _fox_mixer(xb, b, t, k_past, v_past, lf_past, w, i):
    resident = k_past is None
    assert not resident or t % FLASH_T == 0
    w_qkv = w['c_w_qkv'][i]
    k32, *kb = _mm(xb, w_qkv[:, D_MODEL:2 * D_MODEL].astype(BF16), [F32] if resident else [F32, BF16])
    v32, *vb = _mm(xb, w_qkv[:, 2 * D_MODEL:].astype(BF16), [F32] if resident else [F32, BF16])
    k32 = k32.reshape(b, t, C_HEADS, C_HEAD_DIM)
    v32 = v32.reshape(b, t, C_HEADS, C_HEAD_DIM)
    w_f = jnp.zeros((D_MODEL, LANES), F32).at[:, :C_HEADS].set(w['c_w_f'][i]).astype(BF16)
    b_f = jnp.zeros((1, LANES), F32).at[0, :C_HEADS].set(w['c_b_f'][i])
    log_f = _logf(xb, w_f, b_f).reshape(b, t, LANES)
    lf_all = log_f
    if not resident:
        lf_all = jnp.concatenate([jnp.pad(lf_past, ((0, 0), (0, 0), (0, LANES - C_HEADS))), log_f], axis=1)
    t_k = lf_all.shape[1]
    tc = 256
    t_pad = -(-t_k // tc) * tc
    f_cum, f_packed = _cumsum(jnp.pad(lf_all, ((0, 0), (0, t_pad - t_k), (0, 0))), tc)
    nh = C_HEADS_PER_STEP
    n_hblk = C_HEADS // nh
    q_scale = C_HEAD_DIM ** -0.5
    if resident:
        w_aug = jnp.concatenate([_pad_heads(w_qkv[:, :D_MODEL], q_scale * LOG2E),
                                 _pad_heads(w_qkv[:, D_MODEL:2 * D_MODEL], 1.0)], axis=1).astype(BF16)
        qk_aug = _aug_qk(xb, w_aug, f_packed.reshape(b * t, LANES)).reshape(b, t, -1)
        n_q = C_HEADS * C_AUG
        vt = _proj_t(xb, w_qkv[:, 2 * D_MODEL:], FLASH_T, C_HEAD_DIM).reshape(b, t // FLASH_T, -1, FLASH_T)
        o = _flash_t(qk_aug, qk_aug, n_q // (nh * C_AUG), vt,
                     nh=nh, dq=C_AUG, dv=C_HEAD_DIM, n_hblk=n_hblk, chunk_causal=False)
        return o.reshape(b * t, D_MODEL), k32, v32, log_f[:, :, :C_HEADS]
    (q,) = _mm(xb, (w_qkv[:, :D_MODEL] * q_scale).astype(BF16), [BF16])
    p_len = k_past.shape[1]
    causal = np.where(np.tril(np.ones((t, t), bool)), 0.0, NEG_INF).astype(np.float32)
    bias = jnp.asarray(np.concatenate([np.zeros((t, p_len), np.float32), causal], axis=1)[None])
    o = _step_attn(q.reshape(b, t, D_MODEL), k_past.reshape(b, p_len, D_MODEL),
                   v_past.reshape(b, p_len, D_MODEL), kb[0].reshape(b, t, D_MODEL),
                   vb[0].reshape(b, t, D_MODEL), bias, C_HEADS, f_cum=f_cum[:, :t_k, :C_HEADS])
    return o.reshape(b * t, D_MODEL), k32, v32, log_f[:, :, :C_HEADS]


def _conv_ffn(xb, b, t, conv_past, w, i):
    tab = jnp.concatenate([w['f_conv_w'][i], w['f_conv_b'][i][None],
                           jnp.zeros((CONV_ROWS - CONV_W - 1, 2 * D_FF), F32)], axis=0)
    if conv_past is None:
        past = jnp.zeros((b, CONV_ROWS, 2 * D_FF), F32)
    else:
        past = jnp.pad(conv_past, ((0, 0), (CONV_ROWS - (CONV_W - 1), 0), (0, 0)))
    act, tail = _ffn_up(xb.reshape(b, t, D_MODEL), w['f_w_up_bf16'][i], tab, past)
    return act.reshape(b * t, D_FF), tail[:, CONV_ROWS - (CONV_W - 1):]


def _trunk(x, pos0, past, w):
    b, t, _ = x.shape
    pos = pos0 + jnp.arange(t)
    xf = x.reshape(b * t, D_MODEL)
    xb = xf.astype(BF16)
    outs = {n: [] for n in ('a_ckv', 'a_kpe', 'b_k', 'b_v', 'c_k', 'c_v', 'c_logf', 'ffn_conv')}
    ia = ib = ic = 0
    get = lambda name, j: None if past is None else past[name][j]
    for i in range(DEPTH):
        kind = i % N_MIXERS
        if kind == 0:
            o, ckv, kpe = _mla_mixer(xb, b, t, pos, get('a_ckv', ia), get('a_kpe', ia), w, ia)
            outs['a_ckv'].append(ckv)
            outs['a_kpe'].append(kpe)
            w_o = w['a_w_o_bf16'][ia]
            ia += 1
        elif kind == 1:
            o, kb, vb = _band_mixer(xb, b, t, pos0, get('b_k', ib), get('b_v', ib), w, ib)
            outs['b_k'].append(kb)
            outs['b_v'].append(vb)
            w_o = w['b_w_o_bf16'][ib]
            ib += 1
        else:
            o, kc, vc, lf = _fox_mixer(xb, b, t, get('c_k', ic), get('c_v', ic), get('c_logf', ic), w, ic)
            outs['c_k'].append(kc)
            outs['c_v'].append(vc)
            outs['c_logf'].append(lf)
            w_o = w['c_w_o_bf16'][ic]
            ic += 1
        xf, xb = _mm_res_ln(o, w_o, xf, w['ln1_g'][i], w['ln1_b'][i])
        act, conv_state = _conv_ffn(xb, b, t, get('ffn_conv', i), w, i)
        outs['ffn_conv'].append(conv_state)
        xf, xb = _mm_res_ln(act, w['f_w_down_bf16'][i], xf, w['ln2_g'][i], w['ln2_b'][i])
    return xf.reshape(b, t, D_MODEL), {n: jnp.stack(v) for n, v in outs.items()}


def kernel(x_prompt, x_sample, cache_a_ckv, cache_a_kpe, cache_b_k, cache_b_v, cache_c_k, cache_c_v,
           cache_c_logf, state_ffn_conv, a_w_dq, a_g_q, a_w_uq, a_w_dkv, a_g_kv, a_w_kr, a_w_uk, a_w_uv,
           a_w_o, b_w_qkv, b_rel_bias, b_w_o, c_w_qkv, c_w_f, c_b_f, c_w_o, f_w_up, f_conv_w, f_conv_b,
           f_w_down, ln1_g, ln1_b, ln2_g, ln2_b):
    w = dict(a_w_dq=a_w_dq, a_g_q=a_g_q, a_w_uq=a_w_uq, a_w_dkv=a_w_dkv, a_g_kv=a_g_kv, a_w_kr=a_w_kr,
             a_w_uk=a_w_uk, a_w_uv=a_w_uv, a_w_o=a_w_o, b_w_qkv=b_w_qkv, b_rel_bias=b_rel_bias, b_w_o=b_w_o,
             c_w_qkv=c_w_qkv, c_w_f=c_w_f, c_b_f=c_b_f, c_w_o=c_w_o, f_w_up=f_w_up, f_conv_w=f_conv_w,
             f_conv_b=f_conv_b, f_w_down=f_w_down, ln1_g=ln1_g, ln1_b=ln1_b, ln2_g=ln2_g, ln2_b=ln2_b)
    past = dict(a_ckv=cache_a_ckv, a_kpe=cache_a_kpe, b_k=cache_b_k, b_v=cache_b_v, c_k=cache_c_k,
                c_v=cache_c_v, c_logf=cache_c_logf, ffn_conv=state_ffn_conv)
    past_len = cache_a_ckv.shape[2]
    for name in ('f_w_up', 'f_w_down', 'a_w_o', 'b_w_o', 'c_w_o'):
        w[name + '_bf16'] = _to_bf16(w[name])
    y_prompt, p = _trunk(x_prompt, 0, None, w)
    y_sample, s = _trunk(x_sample, past_len, past, w)
    names = ('a_ckv', 'a_kpe', 'b_k', 'b_v', 'c_k', 'c_v', 'c_logf', 'ffn_conv')
    return (y_prompt, y_sample) + tuple(p[n] for n in names) + tuple(s[n] for n in names)
```

```python
import functools
import math

import numpy as np
import jax
import jax.numpy as jnp
from jax import lax
from jax.experimental import pallas as pl
from jax.experimental.pallas import tpu as pltpu

F32 = jnp.float32
BF16 = jnp.bfloat16

D_MODEL = 1024
DEPTH = 4
CHUNK = 64
N_MIXERS = 3

A_HEADS = 8
A_Q_LORA = 384
A_KV_LORA = 256
A_NOPE = 128
A_ROPE = 64
A_V = 128
A_QK_PAD = 256
A_LAT_PAD = 384
ROPE_THETA = 10000.0

B_HEADS = 16
B_HEAD_DIM = D_MODEL // B_HEADS
B_LEFT_CHUNKS = 8
B_WIN = B_LEFT_CHUNKS * CHUNK
B_REL_CLIP = 128

C_HEADS = 16
C_HEAD_DIM = D_MODEL // C_HEADS

D_FF = 2816
CONV_W = 3

ALPHA = (2.0 * DEPTH) ** 0.25
LN_EPS = 1e-5
RMS_EPS = 1e-6
NEG_INF = -1e30

LANES = 128
SUBLANES = 8
VMEM_LIMIT_BYTES = 48 * 2 ** 20


def _params(*sem):
    return pltpu.CompilerParams(dimension_semantics=sem, vmem_limit_bytes=VMEM_LIMIT_BYTES)


def _row_tile(m, tm):
    while m % tm:
        tm //= 2
    assert tm % SUBLANES == 0, (m, tm)
    return tm


def _mm_kernel(x_ref, w_ref, *o_refs):
    acc = jnp.dot(x_ref[...], w_ref[...], preferred_element_type=F32)
    for o_ref in o_refs:
        o_ref[...] = acc.astype(o_ref.dtype)


def _mm(x, w, out_dtypes, tm=512, tn=1024):
    m, k = x.shape
    n = w.shape[1]
    tm, tn = _row_tile(m, tm), min(tn, n)
    return pl.pallas_call(
        _mm_kernel,
        grid=(m // tm, n // tn),
        in_specs=[pl.BlockSpec((tm, k), lambda i, j: (i, 0)),
                  pl.BlockSpec((k, tn), lambda i, j: (0, j))],
        out_specs=[pl.BlockSpec((tm, tn), lambda i, j: (i, j)) for _ in out_dtypes],
        out_shape=[jax.ShapeDtypeStruct((m, n), d) for d in out_dtypes],
        compiler_params=_params("parallel", "parallel"),
        name="mm",
    )(x, w)


ONES_ROWS = 16


def _proj_t_kernel(wt_ref, b_ref, x_ref, o_ref):
    o_ref[...] = (lax.dot_general(wt_ref[...], x_ref[...], (((1,), (1,)), ((), ())),
                                  preferred_element_type=F32) + b_ref[...]).astype(o_ref.dtype)


def _proj_t(x, w, t_blk, head_dim=None, tm=512):
    m, k = x.shape
    n = w.shape[1]
    bias = jnp.zeros((n, 1), F32)
    if head_dim is not None:
        heads = n // head_dim
        w = jnp.pad(w.reshape(k, heads, head_dim), ((0, 0), (0, 0), (0, ONES_ROWS)))
        bias = jnp.pad(jnp.zeros((heads, head_dim, 1), F32), ((0, 0), (0, ONES_ROWS), (0, 0)),
                       constant_values=1.0)
        n = heads * (head_dim + ONES_ROWS)
        w, bias = w.reshape(k, n), bias.reshape(n, 1)
    tm = min(tm, t_blk)
    per = t_blk // tm
    return pl.pallas_call(
        _proj_t_kernel,
        grid=(m // tm,),
        in_specs=[pl.BlockSpec((n, k), lambda i: (0, 0)), pl.BlockSpec((n, 1), lambda i: (0, 0)),
                  pl.BlockSpec((tm, k), lambda i: (i, 0))],
        out_specs=pl.BlockSpec((None, n, tm), lambda i: (i // per, 0, i % per)),
        out_shape=jax.ShapeDtypeStruct((m // t_blk, n, t_blk), BF16),
        compiler_params=_params("parallel"),
        name="proj_t",
    )(w.T.astype(BF16), bias, x)


def _cast_kernel(x_ref, o_ref):
    o_ref[...] = x_ref[...].astype(o_ref.dtype)


def _to_bf16(w, tr=256):
    n_l, r, c = w.shape
    tr = _row_tile(r, tr)
    spec = pl.BlockSpec((None, tr, c), lambda l, i: (l, i, 0))
    return pl.pallas_call(
        _cast_kernel,
        grid=(n_l, r // tr),
        in_specs=[spec],
        out_specs=spec,
        out_shape=jax.ShapeDtypeStruct(w.shape, BF16),
        compiler_params=_params("parallel", "parallel"),
        name="to_bf16",
    )(w)


LN_ROWS = 128


def _mm_res_ln_kernel(a_ref, w_ref, x_ref, g_ref, b_ref, of_ref, ob_ref):
    tm = a_ref.shape[0]
    pieces = [slice(r, min(r + LN_ROWS, tm)) for r in range(0, tm, LN_ROWS)]

    def mm(rows):
        return jnp.dot(a_ref[rows, :], w_ref[...], preferred_element_type=F32)

    def ln(rows, acc):
        y = ALPHA * x_ref[rows, :] + acc
        mu = jnp.mean(y, axis=-1, keepdims=True)
        d = y - mu
        var = jnp.mean(d * d, axis=-1, keepdims=True)
        out = d * lax.rsqrt(var + LN_EPS) * g_ref[...] + b_ref[...]
        of_ref[rows, :] = out
        ob_ref[rows, :] = out.astype(BF16)

    acc = mm(pieces[0])
    for i, rows in enumerate(pieces):
        nxt = mm(pieces[i + 1]) if i + 1 < len(pieces) else None
        ln(rows, acc)
        acc = nxt


def _mm_res_ln(a, w, x, g, b, tm=512):
    m, k = a.shape
    n = w.shape[1]
    tm = _row_tile(m, tm)
    row = lambda i: (i, 0)
    fixed = lambda i: (0, 0)
    return pl.pallas_call(
        _mm_res_ln_kernel,
        grid=(m // tm,),
        in_specs=[pl.BlockSpec((tm, k), row), pl.BlockSpec((k, n), fixed),
                  pl.BlockSpec((tm, n), row), pl.BlockSpec((1, n), fixed),
                  pl.BlockSpec((1, n), fixed)],
        out_specs=[pl.BlockSpec((tm, n), row), pl.BlockSpec((tm, n), row)],
        out_shape=[jax.ShapeDtypeStruct((m, n), F32), jax.ShapeDtypeStruct((m, n), BF16)],
        compiler_params=_params("parallel"),
        name="mm_res_ln",
    )(a, w, x, g.reshape(1, n), b.reshape(1, n))


_W1_CQ = (0, A_Q_LORA)
_W1_CKV = (A_Q_LORA, A_Q_LORA + A_KV_LORA)
_W1_KR = (_W1_CKV[1], _W1_CKV[1] + A_ROPE)
_W1_KRS = (_W1_CKV[1] + LANES, _W1_CKV[1] + LANES + A_ROPE)
_W1_COLS = _W1_CKV[1] + 2 * LANES


def _rms(v, g):
    return v * lax.rsqrt(jnp.mean(v * v, axis=-1, keepdims=True) + RMS_EPS) * g


def _mla_proj_kernel(x_ref, w1_ref, gq_ref, gkv_ref, wq_ref, cq_ref, sq_ref, ck_ref, sk_ref,
                     q_ref, lat_ref, ckv_ref, kpe_ref):
    y = jnp.dot(x_ref[...], w1_ref[...], preferred_element_type=F32)
    cq = _rms(y[:, _W1_CQ[0]:_W1_CQ[1]], gq_ref[...]).astype(BF16)
    ckv = _rms(y[:, _W1_CKV[0]:_W1_CKV[1]], gkv_ref[...])
    kpe = y[:, _W1_KR[0]:_W1_KR[1]] * ck_ref[...] + y[:, _W1_KRS[0]:_W1_KRS[1]] * sk_ref[...]
    ckv_ref[...] = ckv
    kpe_ref[...] = kpe
    lat_ref[:, 0:A_KV_LORA] = ckv.astype(BF16)
    lat_ref[:, A_KV_LORA:A_KV_LORA + A_ROPE] = kpe.astype(BF16)
    lat_ref[:, A_KV_LORA + A_ROPE:] = jnp.zeros(
        (lat_ref.shape[0], A_LAT_PAD - A_KV_LORA - A_ROPE), BF16)
    sw0 = A_HEADS * A_QK_PAD
    for h in range(A_HEADS):
        lo, hi = h * A_QK_PAD, (h + 1) * A_QK_PAD
        qp = jnp.dot(cq, wq_ref[:, lo:hi], preferred_element_type=F32)
        qs = jnp.dot(cq, wq_ref[:, sw0 + lo:sw0 + hi], preferred_element_type=F32)
        q_ref[:, lo:hi] = (qp * cq_ref[...] + qs * sq_ref[...]).astype(BF16)


def _mla_proj(xb, w1, gq, gkv, wq, cosq, sinq, cosk, sink, tm=512):
    m = xb.shape[0]
    tm = _row_tile(m, tm)
    row = lambda i: (i, 0)
    fixed = lambda i: (0, 0)
    nq = A_HEADS * A_QK_PAD
    return pl.pallas_call(
        _mla_proj_kernel,
        grid=(m // tm,),
        in_specs=[pl.BlockSpec((tm, D_MODEL), row), pl.BlockSpec(w1.shape, fixed),
                  pl.BlockSpec((1, A_Q_LORA), fixed), pl.BlockSpec((1, A_KV_LORA), fixed),
                  pl.BlockSpec(wq.shape, fixed),
                  pl.BlockSpec((tm, A_QK_PAD), row), pl.BlockSpec((tm, A_QK_PAD), row),
                  pl.BlockSpec((tm, A_ROPE), row), pl.BlockSpec((tm, A_ROPE), row)],
        out_specs=[pl.BlockSpec((tm, nq), row), pl.BlockSpec((tm, A_LAT_PAD), row),
                   pl.BlockSpec((tm, A_KV_LORA), row), pl.BlockSpec((tm, A_ROPE), row)],
        out_shape=[jax.ShapeDtypeStruct((m, nq), BF16), jax.ShapeDtypeStruct((m, A_LAT_PAD), BF16),
                   jax.ShapeDtypeStruct((m, A_KV_LORA), F32), jax.ShapeDtypeStruct((m, A_ROPE), F32)],
        compiler_params=_params("parallel"),
        name="mla_proj",
    )(xb, w1, gq.reshape(1, -1), gkv.reshape(1, -1), wq, cosq, sinq, cosk, sink)


def _mla_step_kernel(q_ref, lat_ref, m_ref, wuv_ref, bias_ref, o_ref):
    nt = (((1,), (1,)), ((), ()))
    lat = lat_ref[...]
    ckv = lat_ref[:, 0:A_KV_LORA]
    heads = range(A_HEADS)
    qf = [jnp.dot(q_ref[:, h * A_QK_PAD:(h + 1) * A_QK_PAD], m_ref[h],
                  preferred_element_type=F32).astype(BF16) for h in heads]
    s = [lax.dot_general(qf[h], lat, nt, preferred_element_type=F32) + bias_ref[0] for h in heads]
    p = [jnp.exp(s[h] - jnp.max(s[h], axis=-1, keepdims=True)) for h in heads]
    o_lat = [jnp.dot(p[h].astype(BF16), ckv, preferred_element_type=F32) for h in heads]
    for h in heads:
        o = jnp.dot(o_lat[h].astype(BF16), wuv_ref[:, h * A_V:(h + 1) * A_V], preferred_element_type=F32)
        l = jnp.sum(p[h], axis=-1, keepdims=True)
        o_ref[:, h * A_V:(h + 1) * A_V] = (o / l).astype(o_ref.dtype)


def _mla_step(q, lat, w_uk, w_uv, bias):
    b, t, nq = q.shape
    t_k = lat.shape[1]
    absorb = jnp.zeros((A_HEADS, A_QK_PAD, A_LAT_PAD), F32)
    absorb = absorb.at[:, :A_NOPE, :A_KV_LORA].set(jnp.transpose(w_uk, (1, 2, 0)))
    absorb = absorb.at[:, A_NOPE:A_NOPE + A_ROPE, A_KV_LORA:A_KV_LORA + A_ROPE].set(
        jnp.eye(A_ROPE, dtype=F32))
    stream = lambda bi: (bi, 0, 0)
    return pl.pallas_call(
        _mla_step_kernel,
        grid=(b,),
        in_specs=[pl.BlockSpec((None, t, nq), stream), pl.BlockSpec((None, t_k, A_LAT_PAD), stream),
                  pl.BlockSpec(absorb.shape, lambda bi: (0, 0, 0)),
                  pl.BlockSpec((A_KV_LORA, A_HEADS * A_V), lambda bi: (0, 0)),
                  pl.BlockSpec(bias.shape, lambda bi: (0, 0, 0))],
        out_specs=pl.BlockSpec((None, t, A_HEADS * A_V), stream),
        out_shape=jax.ShapeDtypeStruct((b, t, A_HEADS * A_V), BF16),
        compiler_params=_params("parallel"),
        name="mla_step",
    )(q, lat, absorb.astype(BF16), w_uv.reshape(A_KV_LORA, A_HEADS * A_V).astype(BF16), bias)


FLASH_T = 1024
FLASH_QC = 256
FLASH_KC = 256
FLASH_AHEAD = 8
A_HEADS_PER_STEP = 2
C_HEADS_PER_STEP = 4
LOG2E = math.log2(math.e)


def _flash_t_kernel(q_ref, k_ref, vt_ref, o_ref, m_s, acc_s, *, nh, dq, dv, t, chunk_causal):
    iq = pl.program_id(2)
    dva = dv + ONES_ROWS
    m_s[...] = jnp.full(m_s.shape, -jnp.inf, F32)
    acc_s[...] = jnp.zeros(acc_s.shape, F32)

    qc, kc = FLASH_QC, FLASH_KC

    def run(blocks):
        def n_keys(masked, g, c):
            return min(kc, (c + 1) * qc - g * kc) if masked else kc

        chains = [(j, masked, g, h, c) for j, masked in blocks
                  for g in range(t // kc) for h in range(nh) for c in range(t // qc)
                  if n_keys(masked, g, c) > 0]

        def qk(j, masked, g, h, c):
            row0 = pl.multiple_of(j * t, t) + g * kc
            return lax.dot_general(k_ref[pl.ds(row0, n_keys(masked, g, c)), h * dq:(h + 1) * dq],
                                   q_ref[c * qc:(c + 1) * qc, h * dq:(h + 1) * dq],
                                   (((1,), (1,)), ((), ())), preferred_element_type=F32)

        def softmax_pv(j, masked, g, h, c, s):
            cols = slice(c * qc, (c + 1) * qc)
            nk = n_keys(masked, g, c)
            if masked and g * kc + nk > c * qc:
                k_pos = lax.broadcasted_iota(jnp.int32, (nk, qc), 0) + g * kc
                q_pos = lax.broadcasted_iota(jnp.int32, (nk, qc), 1) + c * qc
                if chunk_causal:
                    shift = CHUNK.bit_length() - 1
                    mask = jnp.right_shift(k_pos, shift) <= jnp.right_shift(q_pos, shift)
                else:
                    mask = k_pos <= q_pos
                s = jnp.where(mask, s, NEG_INF)
            m_prev = m_s[h, :, cols]
            m_new = jnp.maximum(m_prev, jnp.max(s, axis=0, keepdims=True))
            alpha = jnp.exp2(m_prev - m_new)
            p = jnp.exp2(s - m_new)
            acc_s[h, :, cols] = alpha * acc_s[h, :, cols] + jnp.dot(
                vt_ref[j, h * dva:(h + 1) * dva, g * kc:g * kc + nk], p.astype(BF16),
                preferred_element_type=F32)
            m_s[h, :, cols] = m_new

        pending = [qk(*chain) for chain in chains[:FLASH_AHEAD]]
        for i, chain in enumerate(chains):
            if i + FLASH_AHEAD < len(chains):
                pending.append(qk(*chains[i + FLASH_AHEAD]))
            softmax_pv(*chain, pending.pop(0))

    def block_pair(jj, carry):
        run([(2 * jj, False), (2 * jj + 1, False)])
        return carry

    lax.fori_loop(0, iq // 2, block_pair, 0)

    @pl.when(iq % 2 == 1)
    def _odd():
        run([(iq - 1, False)])

    run([(iq, True)])
    out = jnp.concatenate([acc_s[h, 0:dv] / acc_s[h, dv:dv + 1] for h in range(nh)], axis=0)
    o_ref[...] = out.T.astype(o_ref.dtype)


def _flash_t(q, k, k_col0, vt, *, nh, dq, dv, n_hblk, chunk_causal):
    b, t_all = q.shape[0], q.shape[1]
    t = FLASH_T
    nblk = t_all // t
    kern = functools.partial(_flash_t_kernel, nh=nh, dq=dq, dv=dv, t=t, chunk_causal=chunk_causal)
    return pl.pallas_call(
        kern,
        grid=(b, n_hblk, nblk),
        in_specs=[pl.BlockSpec((None, t, nh * dq), lambda bi, h, i: (bi, i, h)),
                  pl.BlockSpec((None, t_all, nh * dq), lambda bi, h, i: (bi, 0, k_col0 + h),
                               pipeline_mode=pl.Buffered(1)),
                  pl.BlockSpec((None, nblk, nh * (dv + ONES_ROWS), t), lambda bi, h, i: (bi, 0, h, 0),
                               pipeline_mode=pl.Buffered(1))],
        out_specs=pl.BlockSpec((None, t, nh * dv), lambda bi, h, i: (bi, i, h)),
        out_shape=jax.ShapeDtypeStruct((b, t_all, n_hblk * nh * dv), BF16),
        scratch_shapes=[pltpu.VMEM((nh, 1, t), F32), pltpu.VMEM((nh, dv + ONES_ROWS, t), F32)],
        compiler_params=_params("parallel", "parallel", "parallel"),
        name="flash_t_chunk" if chunk_causal else "flash_t_frame",
    )(q, k, vt)


B_HEADS_PER_STEP = LANES // B_HEAD_DIM


def _step_attn_kernel(*refs, heads, dh, forget):
    if forget:
        q_ref, kp_ref, vp_ref, kn_ref, vn_ref, bp_ref, bn_ref, fq_ref, fkp_ref, fkn_ref, o_ref = refs
    else:
        q_ref, kp_ref, vp_ref, kn_ref, vn_ref, bp_ref, bn_ref, o_ref = refs
    nt = (((1,), (1,)), ((), ()))
    n_bias = bp_ref.shape[0]
    for h in range(heads):
        cols = slice(h * dh, (h + 1) * dh)
        q = q_ref[:, cols]
        sp = lax.dot_general(q, kp_ref[:, cols].astype(BF16), nt, preferred_element_type=F32)
        sn = lax.dot_general(q, kn_ref[:, cols], nt, preferred_element_type=F32)
        sp = sp + bp_ref[h % n_bias]
        sn = sn + bn_ref[h % n_bias]
        if forget:
            fq = fq_ref[:, h:h + 1]
            sp = sp + fq - fkp_ref[h:h + 1, :]
            sn = sn + fq - fkn_ref[h:h + 1, :]
        m = jnp.maximum(jnp.max(sp, axis=-1, keepdims=True), jnp.max(sn, axis=-1, keepdims=True))
        pp = jnp.exp(sp - m)
        pn = jnp.exp(sn - m)
        l = jnp.sum(pp, axis=-1, keepdims=True) + jnp.sum(pn, axis=-1, keepdims=True)
        o = (jnp.dot(pp.astype(BF16), vp_ref[:, cols].astype(BF16), preferred_element_type=F32)
             + jnp.dot(pn.astype(BF16), vn_ref[:, cols], preferred_element_type=F32))
        o_ref[:, cols] = (o / l).astype(o_ref.dtype)


def _band_bias(rel_bias, q_pos, k_pos):
    nq, nk = len(q_pos), len(k_pos)
    assert (np.diff(q_pos) == 1).all() and (np.diff(k_pos) == 1).all()
    m = np.arange(nq + nk - 1)
    u = rel_bias[:, np.clip(q_pos[0] - k_pos[0] + nq - 1 - m, -B_REL_CLIP, B_REL_CLIP) + B_REL_CLIP]
    period = nq + nk
    w = jnp.concatenate([u[:, nq - 1:], jnp.zeros((u.shape[0], 1), u.dtype), u[:, :nq - 1]], axis=1)
    skew = jnp.tile(w, (1, nq))[:, :nq * (period - 1)].reshape(-1, nq, period - 1)[:, :, :nk]
    qc = q_pos[:, None] // CHUNK
    kc = k_pos[None, :] // CHUNK
    mask = (kc <= qc) & (kc >= qc - B_LEFT_CHUNKS) & (k_pos[None, :] >= 0)
    return jnp.where(jnp.asarray(mask)[None], skew, NEG_INF).astype(F32)


def _step_attn(q, k_past, v_past, k_new, v_new, bias, heads, f_cum=None):
    b, t, d = q.shape
    p = k_past.shape[1]
    stream = lambda bi: (bi, 0, 0)
    fixed = lambda bi: (0, 0, 0)
    args = [q, k_past, v_past, k_new, v_new, bias[:, :, :p], bias[:, :, p:]]
    in_specs = [pl.BlockSpec((None, t, d), stream), pl.BlockSpec((None, p, d), stream),
                pl.BlockSpec((None, p, d), stream), pl.BlockSpec((None, t, d), stream),
                pl.BlockSpec((None, t, d), stream),
                pl.BlockSpec((bias.shape[0], t, p), fixed), pl.BlockSpec((bias.shape[0], t, t), fixed)]
    if f_cum is not None:
        fq = jnp.pad(f_cum[:, p:], ((0, 0), (0, 0), (0, LANES - heads)))
        fk = f_cum.transpose(0, 2, 1)
        args += [fq, fk[:, :, :p], fk[:, :, p:]]
        in_specs += [pl.BlockSpec((None, t, LANES), stream), pl.BlockSpec((None, heads, p), stream),
                     pl.BlockSpec((None, heads, t), stream)]
    return pl.pallas_call(
        functools.partial(_step_attn_kernel, heads=heads, dh=d // heads, forget=f_cum is not None),
        grid=(b,),
        in_specs=in_specs,
        out_specs=pl.BlockSpec((None, t, d), stream),
        out_shape=jax.ShapeDtypeStruct((b, t, d), BF16),
        compiler_params=_params("parallel"),
        name="step_attn",
    )(*args)


BAND_T = 4 * CHUNK


def _band_t_kernel(*refs, nkb, t):
    q_ref = refs[0]
    k_refs = refs[1:1 + nkb]
    vt_refs = refs[1 + nkb:1 + 2 * nkb]
    bias_ref, o_ref = refs[1 + 2 * nkb], refs[2 + 2 * nkb]
    iq = pl.program_id(1)
    dh = B_HEAD_DIM

    def qk(h):
        cols = slice(h * dh, (h + 1) * dh)
        parts = []
        for j in range(nkb):
            s = lax.dot_general(k_refs[j][:, cols], q_ref[:, cols], (((1,), (1,)), ((), ())),
                                preferred_element_type=F32)
            if j < nkb - 1:
                s = jnp.where(iq >= nkb - 1 - j, s, NEG_INF)
            parts.append(s)
        return jnp.concatenate(parts, axis=0) + bias_ref[h]

    def softmax_pv(h, s):
        p = jnp.exp2(s - jnp.max(s, axis=0, keepdims=True))
        l = jnp.sum(p, axis=0, keepdims=True)
        pb = p.astype(BF16)
        o = functools.reduce(lambda a, c: a + c, [
            jnp.dot(vt_refs[j][h * dh:(h + 1) * dh, :], pb[j * t:(j + 1) * t], preferred_element_type=F32)
            for j in range(nkb)])
        return o / l

    heads = list(range(B_HEADS))
    pending = [qk(h) for h in heads[:FLASH_AHEAD]]
    outs = []
    for h in heads:
        if h + FLASH_AHEAD < B_HEADS:
            pending.append(qk(h + FLASH_AHEAD))
        outs.append(softmax_pv(h, pending.pop(0)))
        if len(outs) == B_HEADS_PER_STEP:
            c0 = (h + 1 - B_HEADS_PER_STEP) * dh
            o_ref[:, c0:c0 + LANES] = jnp.concatenate(outs, axis=0).T.astype(o_ref.dtype)
            outs = []


def _band_t(q, kv, vt, bias_t, *, nkb):
    b, t_all = q.shape[0], q.shape[1]
    t = BAND_T

    def back(j):
        return nkb - 1 - j

    k_specs = [pl.BlockSpec((None, t, D_MODEL), lambda bi, i, j=j: (bi, jnp.maximum(i - back(j), 0), 0))
               for j in range(nkb)]
    vt_specs = [pl.BlockSpec((None, None, D_MODEL, t),
                             lambda bi, i, j=j: (bi, jnp.maximum(i - back(j), 0), 0, 0))
                for j in range(nkb)]
    return pl.pallas_call(
        functools.partial(_band_t_kernel, nkb=nkb, t=t),
        grid=(b, t_all // t),
        in_specs=([pl.BlockSpec((None, t, D_MODEL), lambda bi, i: (bi, i, 0))] + k_specs + vt_specs
                  + [pl.BlockSpec(bias_t.shape, lambda bi, i: (0, 0, 0), pipeline_mode=pl.Buffered(1))]),
        out_specs=pl.BlockSpec((None, t, D_MODEL), lambda bi, i: (bi, i, 0)),
        out_shape=jax.ShapeDtypeStruct((b, t_all, D_MODEL), BF16),
        compiler_params=_params("parallel", "parallel"),
        name="band_t",
    )(q, *([kv] * nkb), *([vt] * nkb), bias_t)


def _logf_kernel(x_ref, w_ref, b_ref, o_ref):
    z = jnp.dot(x_ref[...], w_ref[...], preferred_element_type=F32) + b_ref[...]
    o_ref[...] = -(jnp.maximum(-z, 0.0) + jnp.log1p(jnp.exp(-jnp.abs(z))))


def _logf(xb, w_pad, b_pad, tm=512):
    m = xb.shape[0]
    tm = _row_tile(m, tm)
    return pl.pallas_call(
        _logf_kernel,
        grid=(m // tm,),
        in_specs=[pl.BlockSpec((tm, D_MODEL), lambda i: (i, 0)),
                  pl.BlockSpec((D_MODEL, LANES), lambda i: (0, 0)),
                  pl.BlockSpec((1, LANES), lambda i: (0, 0))],
        out_specs=pl.BlockSpec((tm, LANES), lambda i: (i, 0)),
        out_shape=jax.ShapeDtypeStruct((m, LANES), F32),
        compiler_params=_params("parallel"),
        name="logf",
    )(xb, w_pad, b_pad)


def _split3(x):
    hi = x.astype(BF16)
    r = x - hi.astype(F32)
    mid = r.astype(BF16)
    lo = (r - mid.astype(F32)).astype(BF16)
    return hi, mid, lo


N_PIECES = 3
PIECE_LANES = 16


def _cumsum_kernel(x_ref, o_ref, pk_ref, carry, *, tc):
    @pl.when(pl.program_id(1) == 0)
    def _():
        carry[...] = jnp.zeros(carry.shape, F32)

    tri = (lax.broadcasted_iota(jnp.int32, (tc, tc), 0)
           >= lax.broadcasted_iota(jnp.int32, (tc, tc), 1)).astype(BF16)
    c = functools.reduce(lambda a, b: a + b, [jnp.dot(tri, piece, preferred_element_type=F32)
                                              for piece in _split3(x_ref[...])])
    out = c + carry[0:1, :]
    o_ref[...] = out
    src = lax.broadcasted_iota(jnp.int32, (LANES, LANES), 0)
    dst = lax.broadcasted_iota(jnp.int32, (LANES, LANES), 1)
    packed = functools.reduce(lambda a, b: a + b, [
        jnp.dot(piece, ((dst == src + k * PIECE_LANES) & (src < PIECE_LANES)).astype(BF16),
                preferred_element_type=F32)
        for k, piece in enumerate(_split3(out * LOG2E))])
    pk_ref[...] = packed.astype(BF16)
    carry[...] = jnp.broadcast_to(out[tc - 1:tc, :], carry.shape)


def _cumsum(x, tc=256):
    b, t, _ = x.shape
    spec = pl.BlockSpec((None, tc, LANES), lambda bi, i: (bi, i, 0))
    return pl.pallas_call(
        functools.partial(_cumsum_kernel, tc=tc),
        grid=(b, t // tc),
        in_specs=[spec],
        out_specs=[spec] * 2,
        out_shape=[jax.ShapeDtypeStruct(x.shape, F32), jax.ShapeDtypeStruct(x.shape, BF16)],
        scratch_shapes=[pltpu.VMEM((SUBLANES, LANES), F32)],
        compiler_params=_params("parallel", "arbitrary"),
        name="cumsum",
    )(x)


FFN_CHUNK = 256
CONV_ROWS = SUBLANES


def _ffn_up_kernel(x_ref, w_ref, c_ref, p_ref, act_ref, s_ref, halo, *, tm):
    @pl.when(pl.program_id(1) == 0)
    def _():
        halo[...] = p_ref[...]

    x = x_ref[...]
    n_chunks = D_FF // FFN_CHUNK
    groups = tm // CONV_ROWS
    row = lax.broadcasted_iota(jnp.int32, (groups, CONV_ROWS, FFN_CHUNK), 1)

    def up(c, half):
        col = half * D_FF + c * FFN_CHUNK
        return jnp.dot(x, w_ref[:, col:col + FFN_CHUNK], preferred_element_type=F32)

    def conv(h, c, half):
        cols = slice(half * D_FF + c * FFN_CHUNK, half * D_FF + (c + 1) * FFN_CHUNK)
        ext = jnp.concatenate([halo[:, cols], h], axis=0).reshape(groups + 1, CONV_ROWS, FFN_CHUNK)
        hc = c_ref[CONV_W:CONV_W + 1, cols] + c_ref[CONV_W - 1:CONV_W, cols] * h
        for s in range(1, CONV_W):
            rot = pltpu.roll(ext, s, axis=1)
            shifted = jnp.where(row < s, rot[:groups], rot[1:]).reshape(tm, FFN_CHUNK)
            hc = hc + c_ref[CONV_W - 1 - s:CONV_W - s, cols] * shifted
        tail = h[tm - CONV_ROWS:tm]
        halo[:, cols] = tail
        s_ref[:, cols] = tail
        return hc

    pending = [(up(0, 0), up(0, 1))]
    for c in range(n_chunks):
        if c + 1 < n_chunks:
            pending.append((up(c + 1, 0), up(c + 1, 1)))
        ha, hg = pending.pop(0)
        a = conv(ha, c, 0)
        g = conv(hg, c, 1)
        act_ref[:, c * FFN_CHUNK:(c + 1) * FFN_CHUNK] = (g * jax.nn.sigmoid(g) * a).astype(BF16)


def _ffn_up(xb, w_up, conv_tab, past, tm=256):
    b, t, _ = xb.shape
    tm = _row_tile(t, tm)
    fixed = lambda bi, ti: (0, 0)
    return pl.pallas_call(
        functools.partial(_ffn_up_kernel, tm=tm),
        grid=(b, t // tm),
        in_specs=[pl.BlockSpec((None, tm, D_MODEL), lambda bi, ti: (bi, ti, 0)),
                  pl.BlockSpec((D_MODEL, 2 * D_FF), fixed),
                  pl.BlockSpec((CONV_ROWS, 2 * D_FF), fixed),
                  pl.BlockSpec((None, CONV_ROWS, 2 * D_FF), lambda bi, ti: (bi, 0, 0))],
        out_specs=[pl.BlockSpec((None, tm, D_FF), lambda bi, ti: (bi, ti, 0)),
                   pl.BlockSpec((None, CONV_ROWS, 2 * D_FF), lambda bi, ti: (bi, 0, 0))],
        out_shape=[jax.ShapeDtypeStruct((b, t, D_FF), BF16),
                   jax.ShapeDtypeStruct((b, CONV_ROWS, 2 * D_FF), F32)],
        scratch_shapes=[pltpu.VMEM((CONV_ROWS, 2 * D_FF), F32)],
        compiler_params=_params("parallel", "arbitrary"),
        name="ffn_up",
    )(xb, w_up, conv_tab, past)


def _rope_tables(pos, batch, q_scale):
    half = A_ROPE // 2
    inv_freq = ROPE_THETA ** (-jnp.arange(half, dtype=F32) / half)
    ang = pos.astype(F32)[:, None] * inv_freq
    cos, sin = jnp.cos(ang), jnp.sin(ang)
    cosk = jnp.concatenate([cos, cos], axis=-1)
    sink = jnp.concatenate([sin, sin], axis=-1)
    t = pos.shape[0]
    pad = jnp.zeros((t, A_QK_PAD - A_NOPE - A_ROPE), F32)
    cosq = q_scale * jnp.concatenate([jnp.ones((t, A_NOPE), F32), cosk, pad], axis=-1)
    sinq = q_scale * jnp.concatenate([jnp.zeros((t, A_NOPE), F32), sink, pad], axis=-1)
    return tuple(jnp.tile(a, (batch, 1)) for a in (cosq, sinq, cosk, sink))


def _swap_halves(w):
    half = w.shape[-1] // 2
    return jnp.concatenate([-w[..., half:], w[..., :half]], axis=-1)


def _mla_weights(w_dq, w_dkv, w_kr, w_uq, w_uk, w_uv):
    zc = jnp.zeros((D_MODEL, LANES - A_ROPE), F32)
    w1 = jnp.concatenate([w_dq, w_dkv, w_kr, zc, _swap_halves(w_kr), zc], axis=1).astype(BF16)
    wq = w_uq.reshape(A_Q_LORA, A_HEADS, A_NOPE + A_ROPE)
    nope, rope = wq[..., :A_NOPE], wq[..., A_NOPE:]
    zpad = jnp.zeros((A_Q_LORA, A_HEADS, A_QK_PAD - A_NOPE - A_ROPE), F32)
    w_cat = jnp.concatenate([nope, rope, zpad], axis=-1).reshape(A_Q_LORA, -1)
    w_sw = jnp.concatenate([jnp.zeros_like(nope), _swap_halves(rope), zpad], axis=-1).reshape(A_Q_LORA, -1)
    wq2 = jnp.concatenate([w_cat, w_sw], axis=1).astype(BF16)
    wk = jnp.zeros((A_LAT_PAD, A_HEADS, A_QK_PAD), F32)
    wk = wk.at[:A_KV_LORA, :, :A_NOPE].set(w_uk)
    eye = jnp.broadcast_to(jnp.eye(A_ROPE, dtype=F32)[:, None, :], (A_ROPE, A_HEADS, A_ROPE))
    wk = wk.at[A_KV_LORA:A_KV_LORA + A_ROPE, :, A_NOPE:A_NOPE + A_ROPE].set(eye)
    wv = jnp.zeros((A_LAT_PAD, A_HEADS * A_V), F32).at[:A_KV_LORA].set(w_uv.reshape(A_KV_LORA, -1))
    wkv = jnp.concatenate([wk.reshape(A_LAT_PAD, -1), wv], axis=1).astype(BF16)
    return w1, wq2, wkv


def _mla_mixer(xb, b, t, pos, ckv_past, kpe_past, w, i):
    w1, wq2, wkv = _mla_weights(w['a_w_dq'][i], w['a_w_dkv'][i], w['a_w_kr'][i], w['a_w_uq'][i],
                                w['a_w_uk'][i], w['a_w_uv'][i])
    prompt = ckv_past is None
    assert not prompt or t % FLASH_T == 0
    q_scale = (A_NOPE + A_ROPE) ** -0.5 * (LOG2E if prompt else 1.0)
    q, lat, ckv, kpe = _mla_proj(xb, w1, w['a_g_q'][i], w['a_g_kv'][i], wq2,
                                 *_rope_tables(pos, b, q_scale))
    lat = lat.reshape(b, t, A_LAT_PAD)
    n_k = A_HEADS * A_QK_PAD
    if prompt:
        lat2 = lat.reshape(b * t, A_LAT_PAD)
        (k_cat,) = _mm(lat2, wkv[:, :n_k], [BF16])
        vt = _proj_t(lat2, wkv[:, n_k:], FLASH_T, A_V).reshape(b, t // FLASH_T, -1, FLASH_T)
        o = _flash_t(q.reshape(b, t, -1), k_cat.reshape(b, t, n_k), 0, vt, nh=A_HEADS_PER_STEP,
                     dq=A_QK_PAD, dv=A_V, n_hblk=A_HEADS // A_HEADS_PER_STEP, chunk_causal=True)
    else:
        p_len = ckv_past.shape[1]
        past = jnp.concatenate(
            [ckv_past, kpe_past, jnp.zeros((b, p_len, A_LAT_PAD - A_KV_LORA - A_ROPE), F32)], axis=-1)
        lat = jnp.concatenate([past.astype(BF16), lat], axis=1)
        q_pos = p_len + np.arange(t)
        k_pos = np.arange(p_len + t)
        visible = (k_pos[None, :] // CHUNK) <= (q_pos[:, None] // CHUNK)
        bias = jnp.asarray(np.where(visible, 0.0, NEG_INF).astype(np.float32)[None])
        o = _mla_step(q.reshape(b, t, -1), lat, w['a_w_uk'][i], w['a_w_uv'][i], bias)
    return o.reshape(b * t, A_HEADS * A_V), ckv.reshape(b, t, -1), kpe.reshape(b, t, -1)


def _band_mixer(xb, b, t, pos0, k_past, v_past, w, i):
    prompt = k_past is None
    assert not prompt or t % BAND_T == 0
    w_qkv = w['b_w_qkv'][i]
    (q,) = _mm(xb, (w_qkv[:, :D_MODEL] * (B_HEAD_DIM ** -0.5 * (LOG2E if prompt else 1.0))).astype(BF16),
               [BF16])
    q = q.reshape(b, t, D_MODEL)
    w_kv = w_qkv[:, D_MODEL:].astype(BF16)
    heads = lambda a, rows: a.reshape(b, rows, B_HEADS, B_HEAD_DIM)
    if prompt:
        nkb = B_WIN // BAND_T + 1
        bias = _band_bias(w['b_rel_bias'][i], B_WIN + np.arange(BAND_T), np.arange(B_WIN + BAND_T))
        (kvb,) = _mm(xb, w_kv, [BF16])
        kvb = kvb.reshape(b, t, 2 * D_MODEL)
        vt = _proj_t(xb, w_qkv[:, 2 * D_MODEL:], BAND_T).reshape(b, t // BAND_T, D_MODEL, BAND_T)
        o = _band_t(q, kvb, vt, LOG2E * bias.transpose(0, 2, 1), nkb=nkb)
        keep = min(B_WIN, t)
        x_tail = xb.reshape(b, t, D_MODEL)[:, t - keep:].reshape(b * keep, D_MODEL)
        (kv_tail,) = _mm(x_tail, w_kv, [F32])
        k_new, v_new = heads(kv_tail[:, :D_MODEL], keep), heads(kv_tail[:, D_MODEL:], keep)
    else:
        kv32, kvb = _mm(xb, w_kv, [F32, BF16])
        k32, v32 = heads(kv32[:, :D_MODEL], t), heads(kv32[:, D_MODEL:], t)
        p_len = k_past.shape[1]
        kvb = kvb.reshape(b, t, 2 * D_MODEL)
        q_pos = pos0 + np.arange(t)
        k_pos = np.concatenate([np.arange(pos0 - p_len, pos0), q_pos])
        bias = _band_bias(w['b_rel_bias'][i], q_pos, k_pos)
        o = _step_attn(q, k_past.reshape(b, p_len, D_MODEL), v_past.reshape(b, p_len, D_MODEL),
                       kvb[:, :, :D_MODEL], kvb[:, :, D_MODEL:], bias, B_HEADS)
        k_new, v_new = k32, v32
    return o.reshape(b * t, D_MODEL), k_new, v_new


C_AUG = LANES


def _aug_kernel(x_ref, w_ref, f_ref, p_ref, b_ref, o_ref):
    acc = (jnp.dot(x_ref[...], w_ref[...], preferred_element_type=F32) + b_ref[...]
           + jnp.dot(f_ref[...], p_ref[...], preferred_element_type=F32))
    o_ref[...] = acc.astype(BF16)


def _aug_tables():
    n = C_HEADS * C_AUG
    place = np.zeros((LANES, 2 * n), np.float32)
    ones = np.zeros((1, 2 * n), np.float32)
    for h in range(C_HEADS):
        base = h * C_AUG + C_HEAD_DIM
        for piece in range(N_PIECES):
            ones[0, base + piece] = 1.0
            place[piece * PIECE_LANES + h, base + N_PIECES + piece] = 1.0
            place[piece * PIECE_LANES + h, n + base + piece] = -1.0
            ones[0, n + base + N_PIECES + piece] = 1.0
    return jnp.asarray(place, BF16), jnp.asarray(ones)


def _aug_qk(xb, w_aug, pieces, tm=512, tn=1024):
    m = xb.shape[0]
    n = w_aug.shape[1]
    tm = _row_tile(m, tm)
    place, ones = _aug_tables()
    row = lambda i, j: (i, 0)
    return pl.pallas_call(
        _aug_kernel,
        grid=(m // tm, n // tn),
        in_specs=[pl.BlockSpec((tm, D_MODEL), row), pl.BlockSpec((D_MODEL, tn), lambda i, j: (0, j)),
                  pl.BlockSpec((tm, LANES), row), pl.BlockSpec((LANES, tn), lambda i, j: (0, j)),
                  pl.BlockSpec((1, tn), lambda i, j: (0, j))],
        out_specs=pl.BlockSpec((tm, tn), lambda i, j: (i, j)),
        out_shape=jax.ShapeDtypeStruct((m, n), BF16),
        compiler_params=_params("parallel", "parallel"),
        name="aug_qk",
    )(xb, w_aug, pieces, place, ones)


def _pad_heads(w, scale):
    w = (w * scale).reshape(D_MODEL, C_HEADS, C_HEAD_DIM)
    return jnp.pad(w, ((0, 0), (0, 0), (0, C_AUG - C_HEAD_DIM))).reshape(D_MODEL, C_HEADS * C_AUG)


def _fox_mixer(xb, b, t, k_past, v_past, lf_past, w, i):
    resident = k_past is None
    assert not resident or t % FLASH_T == 0
    w_qkv = w['c_w_qkv'][i]
    k32, *kb = _mm(xb, w_qkv[:, D_MODEL:2 * D_MODEL].astype(BF16), [F32] if resident else [F32, BF16])
    v32, *vb = _mm(xb, w_qkv[:, 2 * D_MODEL:].astype(BF16), [F32] if resident else [F32, BF16])
    k32 = k32.reshape(b, t, C_HEADS, C_HEAD_DIM)
    v32 = v32.reshape(b, t, C_HEADS, C_HEAD_DIM)
    w_f = jnp.zeros((D_MODEL, LANES), F32).at[:, :C_HEADS].set(w['c_w_f'][i]).astype(BF16)
    b_f = jnp.zeros((1, LANES), F32).at[0, :C_HEADS].set(w['c_b_f'][i])
    log_f = _logf(xb, w_f, b_f).reshape(b, t, LANES)
    lf_all = log_f
    if not resident:
        lf_all = jnp.concatenate([jnp.pad(lf_past, ((0, 0), (0, 0), (0, LANES - C_HEADS))), log_f], axis=1)
    t_k = lf_all.shape[1]
    tc = 256
    t_pad = -(-t_k // tc) * tc
    f_cum, f_packed = _cumsum(jnp.pad(lf_all, ((0, 0), (0, t_pad - t_k), (0, 0))), tc)
    nh = C_HEADS_PER_STEP
    n_hblk = C_HEADS // nh
    q_scale = C_HEAD_DIM ** -0.5
    if resident:
        w_aug = jnp.concatenate([_pad_heads(w_qkv[:, :D_MODEL], q_scale * LOG2E),
                                 _pad_heads(w_qkv[:, D_MODEL:2 * D_MODEL], 1.0)], axis=1).astype(BF16)
        qk_aug = _aug_qk(xb, w_aug, f_packed.reshape(b * t, LANES)).reshape(b, t, -1)
        n_q = C_HEADS * C_AUG
        vt = _proj_t(xb, w_qkv[:, 2 * D_MODEL:], FLASH_T, C_HEAD_DIM).reshape(b, t // FLASH_T, -1, FLASH_T)
        o = _flash_t(qk_aug, qk_aug, n_q // (nh * C_AUG), vt,
                     nh=nh, dq=C_AUG, dv=C_HEAD_DIM, n_hblk=n_hblk, chunk_causal=False)
        return o.reshape(b * t, D_MODEL), k32, v32, log_f[:, :, :C_HEADS]
    (q,) = _mm(xb, (w_qkv[:, :D_MODEL] * q_scale).astype(BF16), [BF16])
    p_len = k_past.shape[1]
    causal = np.where(np.tril(np.ones((t, t), bool)), 0.0, NEG_INF).astype(np.float32)
    bias = jnp.asarray(np.concatenate([np.zeros((t, p_len), np.float32), causal], axis=1)[None])
    o = _step_attn(q.reshape(b, t, D_MODEL), k_past.reshape(b, p_len, D_MODEL),
                   v_past.reshape(b, p_len, D_MODEL), kb[0].reshape(b, t, D_MODEL),
                   vb[0].reshape(b, t, D_MODEL), bias, C_HEADS, f_cum=f_cum[:, :t_k, :C_HEADS])
    return o.reshape(b * t, D_MODEL), k32, v32, log_f[:, :, :C_HEADS]


def _conv_ffn(xb, b, t, conv_past, w, i):
    tab = jnp.concatenate([w['f_conv_w'][i], w['f_conv_b'][i][None],
                           jnp.zeros((CONV_ROWS - CONV_W - 1, 2 * D_FF), F32)], axis=0)
    if conv_past is None:
        past = jnp.zeros((b, CONV_ROWS, 2 * D_FF), F32)
    else:
        past = jnp.pad(conv_past, ((0, 0), (CONV_ROWS - (CONV_W - 1), 0), (0, 0)))
    act, tail = _ffn_up(xb.reshape(b, t, D_MODEL), w['f_w_up_bf16'][i], tab, past)
    return act.reshape(b * t, D_FF), tail[:, CONV_ROWS - (CONV_W - 1):]


def _trunk(x, pos0, past, w):
    b, t, _ = x.shape
    pos = pos0 + jnp.arange(t)
    xf = x.reshape(b * t, D_MODEL)
    xb = xf.astype(BF16)
    outs = {n: [] for n in ('a_ckv', 'a_kpe', 'b_k', 'b_v', 'c_k', 'c_v', 'c_logf', 'ffn_conv')}
    ia = ib = ic = 0
    get = lambda name, j: None if past is None else past[name][j]
    for i in range(DEPTH):
        kind = i % N_MIXERS
        if kind == 0:
            o, ckv, kpe = _mla_mixer(xb, b, t, pos, get('a_ckv', ia), get('a_kpe', ia), w, ia)
            outs['a_ckv'].append(ckv)
            outs['a_kpe'].append(kpe)
            w_o = w['a_w_o_bf16'][ia]
            ia += 1
        elif kind == 1:
            o, kb, vb = _band_mixer(xb, b, t, pos0, get('b_k', ib), get('b_v', ib), w, ib)
            outs['b_k'].append(kb)
            outs['b_v'].append(vb)
            w_o = w['b_w_o_bf16'][ib]
            ib += 1
        else:
            o, kc, vc, lf = _fox_mixer(xb, b, t, get('c_k', ic), get('c_v', ic), get('c_logf', ic), w, ic)
            outs['c_k'].append(kc)
            outs['c_v'].append(vc)
            outs['c_logf'].append(lf)
            w_o = w['c_w_o_bf16'][ic]
            ic += 1
        xf, xb = _mm_res_ln(o, w_o, xf, w['ln1_g'][i], w['ln1_b'][i])
        act, conv_state = _conv_ffn(xb, b, t, get('ffn_conv', i), w, i)
        outs['ffn_conv'].append(conv_state)
        xf, xb = _mm_res_ln(act, w['f_w_down_bf16'][i], xf, w['ln2_g'][i], w['ln2_b'][i])
    return xf.reshape(b, t, D_MODEL), {n: jnp.stack(v) for n, v in outs.items()}


def kernel(x_prompt, x_sample, cache_a_ckv, cache_a_kpe, cache_b_k, cache_b_v, cache_c_k, cache_c_v,
           cache_c_logf, state_ffn_conv, a_w_dq, a_g_q, a_w_uq, a_w_dkv, a_g_kv, a_w_kr, a_w_uk, a_w_uv,
           a_w_o, b_w_qkv, b_rel_bias, b_w_o, c_w_qkv, c_w_f, c_b_f, c_w_o, f_w_up, f_conv_w, f_conv_b,
           f_w_down, ln1_g, ln1_b, ln2_g, ln2_b):
    w = dict(a_w_dq=a_w_dq, a_g_q=a_g_q, a_w_uq=a_w_uq, a_w_dkv=a_w_dkv, a_g_kv=a_g_kv, a_w_kr=a_w_kr,
             a_w_uk=a_w_uk, a_w_uv=a_w_uv, a_w_o=a_w_o, b_w_qkv=b_w_qkv, b_rel_bias=b_rel_bias, b_w_o=b_w_o,
             c_w_qkv=c_w_qkv, c_w_f=c_w_f, c_b_f=c_b_f, c_w_o=c_w_o, f_w_up=f_w_up, f_conv_w=f_conv_w,
             f_conv_b=f_conv_b, f_w_down=f_w_down, ln1_g=ln1_g, ln1_b=ln1_b, ln2_g=ln2_g, ln2_b=ln2_b)
    past = dict(a_ckv=cache_a_ckv, a_kpe=cache_a_kpe, b_k=cache_b_k, b_v=cache_b_v, c_k=cache_c_k,
                c_v=cache_c_v, c_logf=cache_c_logf, ffn_conv=state_ffn_conv)
    past_len = cache_a_ckv.shape[2]
    for name in ('f_w_up', 'f_w_down', 'a_w_o', 'b_w_o', 'c_w_o'):
        w[name + '_bf16'] = _to_bf16(w[name])
    y_prompt, p = _trunk(x_prompt, 0, None, w)
    y_sample, s = _trunk(x_sample, past_len, past, w)
    names = ('a_ckv', 'a_kpe', 'b_k', 'b_v', 'c_k', 'c_v', 'c_logf', 'ffn_conv')
    return (y_prompt, y_sample) + tuple(p[n] for n in names) + tuple(s[n] for n in names)
```

```python
import functools
import math

import numpy as np
import jax
import jax.numpy as jnp
from jax import lax
from jax.experimental import pallas as pl
from jax.experimental.pallas import tpu as pltpu

F32 = jnp.float32
BF16 = jnp.bfloat16

D_MODEL = 1024
DEPTH = 4
CHUNK = 64
N_MIXERS = 3

A_HEADS = 8
A_Q_LORA = 384
A_KV_LORA = 256
A_NOPE = 128
A_ROPE = 64
A_V = 128
A_QK_PAD = 256
A_LAT_PAD = 384
ROPE_THETA = 10000.0

B_HEADS = 16
B_HEAD_DIM = D_MODEL // B_HEADS
B_LEFT_CHUNKS = 8
B_WIN = B_LEFT_CHUNKS * CHUNK
B_REL_CLIP = 128

C_HEADS = 16
C_HEAD_DIM = D_MODEL // C_HEADS

D_FF = 2816
CONV_W = 3

ALPHA = (2.0 * DEPTH) ** 0.25
LN_EPS = 1e-5
RMS_EPS = 1e-6
NEG_INF = -1e30

LANES = 128
SUBLANES = 8
VMEM_LIMIT_BYTES = 48 * 2 ** 20


def _params(*sem):
    return pltpu.CompilerParams(dimension_semantics=sem, vmem_limit_bytes=VMEM_LIMIT_BYTES)


def _row_tile(m, tm):
    while m % tm:
        tm //= 2
    assert tm % SUBLANES == 0, (m, tm)
    return tm


def _mm_kernel(x_ref, w_ref, *o_refs):
    acc = jnp.dot(x_ref[...], w_ref[...], preferred_element_type=F32)
    for o_ref in o_refs:
        o_ref[...] = acc.astype(o_ref.dtype)


def _mm(x, w, out_dtypes, tm=512, tn=1024):
    m, k = x.shape
    n = w.shape[1]
    tm, tn = _row_tile(m, tm), min(tn, n)
    return pl.pallas_call(
        _mm_kernel,
        grid=(m // tm, n // tn),
        in_specs=[pl.BlockSpec((tm, k), lambda i, j: (i, 0)),
                  pl.BlockSpec((k, tn), lambda i, j: (0, j))],
        out_specs=[pl.BlockSpec((tm, tn), lambda i, j: (i, j)) for _ in out_dtypes],
        out_shape=[jax.ShapeDtypeStruct((m, n), d) for d in out_dtypes],
        compiler_params=_params("parallel", "parallel"),
        name="mm",
    )(x, w)


ONES_ROWS = 16


def _proj_t_kernel(wt_ref, b_ref, x_ref, o_ref):
    o_ref[...] = (lax.dot_general(wt_ref[...], x_ref[...], (((1,), (1,)), ((), ())),
                                  preferred_element_type=F32) + b_ref[...]).astype(o_ref.dtype)


def _proj_t(x, w, t_blk, head_dim=None, tm=512):
    m, k = x.shape
    n = w.shape[1]
    bias = jnp.zeros((n, 1), F32)
    if head_dim is not None:
        heads = n // head_dim
        w = jnp.pad(w.reshape(k, heads, head_dim), ((0, 0), (0, 0), (0, ONES_ROWS)))
        bias = jnp.pad(jnp.zeros((heads, head_dim, 1), F32), ((0, 0), (0, ONES_ROWS), (0, 0)),
                       constant_values=1.0)
        n = heads * (head_dim + ONES_ROWS)
        w, bias = w.reshape(k, n), bias.reshape(n, 1)
    tm = min(tm, t_blk)
    per = t_blk // tm
    return pl.pallas_call(
        _proj_t_kernel,
        grid=(m // tm,),
        in_specs=[pl.BlockSpec((n, k), lambda i: (0, 0)), pl.BlockSpec((n, 1), lambda i: (0, 0)),
                  pl.BlockSpec((tm, k), lambda i: (i, 0))],
        out_specs=pl.BlockSpec((None, n, tm), lambda i: (i // per, 0, i % per)),
        out_shape=jax.ShapeDtypeStruct((m // t_blk, n, t_blk), BF16),
        compiler_params=_params("parallel"),
        name="proj_t",
    )(w.T.astype(BF16), bias, x)


def _cast_kernel(x_ref, o_ref):
    o_ref[...] = x_ref[...].astype(o_ref.dtype)


def _to_bf16(w, tr=256):
    n_l, r, c = w.shape
    tr = _row_tile(r, tr)
    spec = pl.BlockSpec((None, tr, c), lambda l, i: (l, i, 0))
    return pl.pallas_call(
        _cast_kernel,
        grid=(n_l, r // tr),
        in_specs=[spec],
        out_specs=spec,
        out_shape=jax.ShapeDtypeStruct(w.shape, BF16),
        compiler_params=_params("parallel", "parallel"),
        name="to_bf16",
    )(w)


LN_ROWS = 128


def _mm_res_ln_kernel(a_ref, w_ref, x_ref, g_ref, b_ref, of_ref, ob_ref):
    tm = a_ref.shape[0]
    pieces = [slice(r, min(r + LN_ROWS, tm)) for r in range(0, tm, LN_ROWS)]

    def mm(rows):
        return jnp.dot(a_ref[rows, :], w_ref[...], preferred_element_type=F32)

    def ln(rows, acc):
        y = ALPHA * x_ref[rows, :] + acc
        mu = jnp.mean(y, axis=-1, keepdims=True)
        d = y - mu
        var = jnp.mean(d * d, axis=-1, keepdims=True)
        out = d * lax.rsqrt(var + LN_EPS) * g_ref[...] + b_ref[...]
        of_ref[rows, :] = out
        ob_ref[rows, :] = out.astype(BF16)

    acc = mm(pieces[0])
    for i, rows in enumerate(pieces):
        nxt = mm(pieces[i + 1]) if i + 1 < len(pieces) else None
        ln(rows, acc)
        acc = nxt


def _mm_res_ln(a, w, x, g, b, tm=512):
    m, k = a.shape
    n = w.shape[1]
    tm = _row_tile(m, tm)
    row = lambda i: (i, 0)
    fixed = lambda i: (0, 0)
    return pl.pallas_call(
        _mm_res_ln_kernel,
        grid=(m // tm,),
        in_specs=[pl.BlockSpec((tm, k), row), pl.BlockSpec((k, n), fixed),
                  pl.BlockSpec((tm, n), row), pl.BlockSpec((1, n), fixed),
                  pl.BlockSpec((1, n), fixed)],
        out_specs=[pl.BlockSpec((tm, n), row), pl.BlockSpec((tm, n), row)],
        out_shape=[jax.ShapeDtypeStruct((m, n), F32), jax.ShapeDtypeStruct((m, n), BF16)],
        compiler_params=_params("parallel"),
        name="mm_res_ln",
    )(a, w, x, g.reshape(1, n), b.reshape(1, n))


_W1_CQ = (0, A_Q_LORA)
_W1_CKV = (A_Q_LORA, A_Q_LORA + A_KV_LORA)
_W1_KR = (_W1_CKV[1], _W1_CKV[1] + A_ROPE)
_W1_KRS = (_W1_CKV[1] + LANES, _W1_CKV[1] + LANES + A_ROPE)
_W1_COLS = _W1_CKV[1] + 2 * LANES


def _rms(v, g):
    return v * lax.rsqrt(jnp.mean(v * v, axis=-1, keepdims=True) + RMS_EPS) * g


def _mla_proj_kernel(x_ref, w1_ref, gq_ref, gkv_ref, wq_ref, cq_ref, sq_ref, ck_ref, sk_ref,
                     q_ref, lat_ref, ckv_ref, kpe_ref):
    y = jnp.dot(x_ref[...], w1_ref[...], preferred_element_type=F32)
    cq = _rms(y[:, _W1_CQ[0]:_W1_CQ[1]], gq_ref[...]).astype(BF16)
    ckv = _rms(y[:, _W1_CKV[0]:_W1_CKV[1]], gkv_ref[...])
    kpe = y[:, _W1_KR[0]:_W1_KR[1]] * ck_ref[...] + y[:, _W1_KRS[0]:_W1_KRS[1]] * sk_ref[...]
    ckv_ref[...] = ckv
    kpe_ref[...] = kpe
    lat_ref[:, 0:A_KV_LORA] = ckv.astype(BF16)
    lat_ref[:, A_KV_LORA:A_KV_LORA + A_ROPE] = kpe.astype(BF16)
    lat_ref[:, A_KV_LORA + A_ROPE:] = jnp.zeros(
        (lat_ref.shape[0], A_LAT_PAD - A_KV_LORA - A_ROPE), BF16)
    sw0 = A_HEADS * A_QK_PAD
    for h in range(A_HEADS):
        lo, hi = h * A_QK_PAD, (h + 1) * A_QK_PAD
        qp = jnp.dot(cq, wq_ref[:, lo:hi], preferred_element_type=F32)
        qs = jnp.dot(cq, wq_ref[:, sw0 + lo:sw0 + hi], preferred_element_type=F32)
        q_ref[:, lo:hi] = (qp * cq_ref[...] + qs * sq_ref[...]).astype(BF16)


def _mla_proj(xb, w1, gq, gkv, wq, cosq, sinq, cosk, sink, tm=512):
    m = xb.shape[0]
    tm = _row_tile(m, tm)
    row = lambda i: (i, 0)
    fixed = lambda i: (0, 0)
    nq = A_HEADS * A_QK_PAD
    return pl.pallas_call(
        _mla_proj_kernel,
        grid=(m // tm,),
        in_specs=[pl.BlockSpec((tm, D_MODEL), row), pl.BlockSpec(w1.shape, fixed),
                  pl.BlockSpec((1, A_Q_LORA), fixed), pl.BlockSpec((1, A_KV_LORA), fixed),
                  pl.BlockSpec(wq.shape, fixed),
                  pl.BlockSpec((tm, A_QK_PAD), row), pl.BlockSpec((tm, A_QK_PAD), row),
                  pl.BlockSpec((tm, A_ROPE), row), pl.BlockSpec((tm, A_ROPE), row)],
        out_specs=[pl.BlockSpec((tm, nq), row), pl.BlockSpec((tm, A_LAT_PAD), row),
                   pl.BlockSpec((tm, A_KV_LORA), row), pl.BlockSpec((tm, A_ROPE), row)],
        out_shape=[jax.ShapeDtypeStruct((m, nq), BF16), jax.ShapeDtypeStruct((m, A_LAT_PAD), BF16),
                   jax.ShapeDtypeStruct((m, A_KV_LORA), F32), jax.ShapeDtypeStruct((m, A_ROPE), F32)],
        compiler_params=_params("parallel"),
        name="mla_proj",
    )(xb, w1, gq.reshape(1, -1), gkv.reshape(1, -1), wq, cosq, sinq, cosk, sink)


def _mla_step_kernel(q_ref, lat_ref, m_ref, wuv_ref, bias_ref, o_ref):
    nt = (((1,), (1,)), ((), ()))
    lat = lat_ref[...]
    ckv = lat_ref[:, 0:A_KV_LORA]
    heads = range(A_HEADS)
    qf = [jnp.dot(q_ref[:, h * A_QK_PAD:(h + 1) * A_QK_PAD], m_ref[h],
                  preferred_element_type=F32).astype(BF16) for h in heads]
    s = [lax.dot_general(qf[h], lat, nt, preferred_element_type=F32) + bias_ref[0] for h in heads]
    p = [jnp.exp(s[h] - jnp.max(s[h], axis=-1, keepdims=True)) for h in heads]
    o_lat = [jnp.dot(p[h].astype(BF16), ckv, preferred_element_type=F32) for h in heads]
    for h in heads:
        o = jnp.dot(o_lat[h].astype(BF16), wuv_ref[:, h * A_V:(h + 1) * A_V], preferred_element_type=F32)
        l = jnp.sum(p[h], axis=-1, keepdims=True)
        o_ref[:, h * A_V:(h + 1) * A_V] = (o / l).astype(o_ref.dtype)


def _mla_step(q, lat, w_uk, w_uv, bias):
    b, t, nq = q.shape
    t_k = lat.shape[1]
    absorb = jnp.zeros((A_HEADS, A_QK_PAD, A_LAT_PAD), F32)
    absorb = absorb.at[:, :A_NOPE, :A_KV_LORA].set(jnp.transpose(w_uk, (1, 2, 0)))
    absorb = absorb.at[:, A_NOPE:A_NOPE + A_ROPE, A_KV_LORA:A_KV_LORA + A_ROPE].set(
        jnp.eye(A_ROPE, dtype=F32))
    stream = lambda bi: (bi, 0, 0)
    return pl.pallas_call(
        _mla_step_kernel,
        grid=(b,),
        in_specs=[pl.BlockSpec((None, t, nq), stream), pl.BlockSpec((None, t_k, A_LAT_PAD), stream),
                  pl.BlockSpec(absorb.shape, lambda bi: (0, 0, 0)),
                  pl.BlockSpec((A_KV_LORA, A_HEADS * A_V), lambda bi: (0, 0)),
                  pl.BlockSpec(bias.shape, lambda bi: (0, 0, 0))],
        out_specs=pl.BlockSpec((None, t, A_HEADS * A_V), stream),
        out_shape=jax.ShapeDtypeStruct((b, t, A_HEADS * A_V), BF16),
        compiler_params=_params("parallel"),
        name="mla_step",
    )(q, lat, absorb.astype(BF16), w_uv.reshape(A_KV_LORA, A_HEADS * A_V).astype(BF16), bias)


FLASH_T = 1024
FLASH_QC = 256
FLASH_KC = 256
FLASH_AHEAD = 8
A_HEADS_PER_STEP = 2
C_HEADS_PER_STEP = 4
LOG2E = math.log2(math.e)


def _flash_t_kernel(q_ref, k_ref, vt_ref, o_ref, m_s, acc_s, *, nh, dq, dv, t, chunk_causal):
    iq = pl.program_id(2)
    dva = dv + ONES_ROWS
    m_s[...] = jnp.full(m_s.shape, -jnp.inf, F32)
    acc_s[...] = jnp.zeros(acc_s.shape, F32)

    qc, kc = FLASH_QC, FLASH_KC

    def run(blocks):
        def n_keys(masked, g, c):
            return min(kc, (c + 1) * qc - g * kc) if masked else kc

        chains = [(j, masked, g, h, c) for j, masked in blocks
                  for g in range(t // kc) for h in range(nh) for c in range(t // qc)
                  if n_keys(masked, g, c) > 0]

        def qk(j, masked, g, h, c):
            row0 = pl.multiple_of(j * t, t) + g * kc
            return lax.dot_general(k_ref[pl.ds(row0, n_keys(masked, g, c)), h * dq:(h + 1) * dq],
                                   q_ref[c * qc:(c + 1) * qc, h * dq:(h + 1) * dq],
                                   (((1,), (1,)), ((), ())), preferred_element_type=F32)

        def softmax_pv(j, masked, g, h, c, s):
            cols = slice(c * qc, (c + 1) * qc)
            nk = n_keys(masked, g, c)
            if masked and g * kc + nk > c * qc:
                k_pos = lax.broadcasted_iota(jnp.int32, (nk, qc), 0) + g * kc
                q_pos = lax.broadcasted_iota(jnp.int32, (nk, qc), 1) + c * qc
                if chunk_causal:
                    shift = CHUNK.bit_length() - 1
                    mask = jnp.right_shift(k_pos, shift) <= jnp.right_shift(q_pos, shift)
                else:
                    mask = k_pos <= q_pos
                s = jnp.where(mask, s, NEG_INF)
            m_prev = m_s[h, :, cols]
            m_new = jnp.maximum(m_prev, jnp.max(s, axis=0, keepdims=True))
            alpha = jnp.exp2(m_prev - m_new)
            p = jnp.exp2(s - m_new)
            acc_s[h, :, cols] = alpha * acc_s[h, :, cols] + jnp.dot(
                vt_ref[j, h * dva:(h + 1) * dva, g * kc:g * kc + nk], p.astype(BF16),
                preferred_element_type=F32)
            m_s[h, :, cols] = m_new

        pending = [qk(*chain) for chain in chains[:FLASH_AHEAD]]
        for i, chain in enumerate(chains):
            if i + FLASH_AHEAD < len(chains):
                pending.append(qk(*chains[i + FLASH_AHEAD]))
            softmax_pv(*chain, pending.pop(0))

    def block_pair(jj, carry):
        run([(2 * jj, False), (2 * jj + 1, False)])
        return carry

    lax.fori_loop(0, iq // 2, block_pair, 0)

    @pl.when(iq % 2 == 1)
    def _odd():
        run([(iq - 1, False)])

    run([(iq, True)])
    out = jnp.concatenate([acc_s[h, 0:dv] / acc_s[h, dv:dv + 1] for h in range(nh)], axis=0)
    o_ref[...] = out.T.astype(o_ref.dtype)


def _flash_t(q, k, k_col0, vt, *, nh, dq, dv, n_hblk, chunk_causal):
    b, t_all = q.shape[0], q.shape[1]
    t = FLASH_T
    nblk = t_all // t
    kern = functools.partial(_flash_t_kernel, nh=nh, dq=dq, dv=dv, t=t, chunk_causal=chunk_causal)
    return pl.pallas_call(
        kern,
        grid=(b, n_hblk, nblk),
        in_specs=[pl.BlockSpec((None, t, nh * dq), lambda bi, h, i: (bi, i, h)),
                  pl.BlockSpec((None, t_all, nh * dq), lambda bi, h, i: (bi, 0, k_col0 + h),
                               pipeline_mode=pl.Buffered(1)),
                  pl.BlockSpec((None, nblk, nh * (dv + ONES_ROWS), t), lambda bi, h, i: (bi, 0, h, 0),
                               pipeline_mode=pl.Buffered(1))],
        out_specs=pl.BlockSpec((None, t, nh * dv), lambda bi, h, i: (bi, i, h)),
        out_shape=jax.ShapeDtypeStruct((b, t_all, n_hblk * nh * dv), BF16),
        scratch_shapes=[pltpu.VMEM((nh, 1, t), F32), pltpu.VMEM((nh, dv + ONES_ROWS, t), F32)],
        compiler_params=_params("parallel", "parallel", "parallel"),
        name="flash_t_chunk" if chunk_causal else "flash_t_frame",
    )(q, k, vt)


B_HEADS_PER_STEP = LANES // B_HEAD_DIM


STEP_AHEAD = 4


def _step_attn_kernel(*refs, heads, dh, forget):
    if forget:
        q_ref, kp_ref, vp_ref, kn_ref, vn_ref, bp_ref, bn_ref, fq_ref, fkp_ref, fkn_ref, o_ref = refs
    else:
        q_ref, kp_ref, vp_ref, kn_ref, vn_ref, bp_ref, bn_ref, o_ref = refs
    nt = (((1,), (1,)), ((), ()))
    n_bias = bp_ref.shape[0]
    cols = [slice(h * dh, (h + 1) * dh) for h in range(heads)]

    def scores(h):
        q = q_ref[:, cols[h]]
        sp = lax.dot_general(q, kp_ref[:, cols[h]].astype(BF16), nt, preferred_element_type=F32)
        sn = lax.dot_general(q, kn_ref[:, cols[h]], nt, preferred_element_type=F32)
        sp = sp + bp_ref[h % n_bias]
        sn = sn + bn_ref[h % n_bias]
        if forget:
            fq = fq_ref[:, h:h + 1]
            sp = sp + fq - fkp_ref[h:h + 1, :]
            sn = sn + fq - fkn_ref[h:h + 1, :]
        return sp, sn

    def finish(h, sp, sn):
        m = jnp.maximum(jnp.max(sp, axis=-1, keepdims=True), jnp.max(sn, axis=-1, keepdims=True))
        pp = jnp.exp(sp - m)
        pn = jnp.exp(sn - m)
        l = jnp.sum(pp, axis=-1, keepdims=True) + jnp.sum(pn, axis=-1, keepdims=True)
        o = (jnp.dot(pp.astype(BF16), vp_ref[:, cols[h]].astype(BF16), preferred_element_type=F32)
             + jnp.dot(pn.astype(BF16), vn_ref[:, cols[h]], preferred_element_type=F32))
        o_ref[:, cols[h]] = (o / l).astype(o_ref.dtype)

    pending = [scores(h) for h in range(min(STEP_AHEAD, heads))]
    for h in range(heads):
        if h + STEP_AHEAD < heads:
            pending.append(scores(h + STEP_AHEAD))
        finish(h, *pending.pop(0))


def _band_bias(rel_bias, q_pos, k_pos):
    nq, nk = len(q_pos), len(k_pos)
    assert (np.diff(q_pos) == 1).all() and (np.diff(k_pos) == 1).all()
    m = np.arange(nq + nk - 1)
    u = rel_bias[:, np.clip(q_pos[0] - k_pos[0] + nq - 1 - m, -B_REL_CLIP, B_REL_CLIP) + B_REL_CLIP]
    period = nq + nk
    w = jnp.concatenate([u[:, nq - 1:], jnp.zeros((u.shape[0], 1), u.dtype), u[:, :nq - 1]], axis=1)
    skew = jnp.tile(w, (1, nq))[:, :nq * (period - 1)].reshape(-1, nq, period - 1)[:, :, :nk]
    qc = q_pos[:, None] // CHUNK
    kc = k_pos[None, :] // CHUNK
    mask = (kc <= qc) & (kc >= qc - B_LEFT_CHUNKS) & (k_pos[None, :] >= 0)
    return jnp.where(jnp.asarray(mask)[None], skew, NEG_INF).astype(F32)


def _step_attn(q, k_past, v_past, k_new, v_new, bias, heads, f_cum=None):
    b, t, d = q.shape
    p = k_past.shape[1]
    stream = lambda bi: (bi, 0, 0)
    fixed = lambda bi: (0, 0, 0)
    args = [q, k_past, v_past, k_new, v_new, bias[:, :, :p], bias[:, :, p:]]
    in_specs = [pl.BlockSpec((None, t, d), stream), pl.BlockSpec((None, p, d), stream),
                pl.BlockSpec((None, p, d), stream), pl.BlockSpec((None, t, d), stream),
                pl.BlockSpec((None, t, d), stream),
                pl.BlockSpec((bias.shape[0], t, p), fixed), pl.BlockSpec((bias.shape[0], t, t), fixed)]
    if f_cum is not None:
        fq = jnp.pad(f_cum[:, p:], ((0, 0), (0, 0), (0, LANES - heads)))
        fk = f_cum.transpose(0, 2, 1)
        args += [fq, fk[:, :, :p], fk[:, :, p:]]
        in_specs += [pl.BlockSpec((None, t, LANES), stream), pl.BlockSpec((None, heads, p), stream),
                     pl.BlockSpec((None, heads, t), stream)]
    return pl.pallas_call(
        functools.partial(_step_attn_kernel, heads=heads, dh=d // heads, forget=f_cum is not None),
        grid=(b,),
        in_specs=in_specs,
        out_specs=pl.BlockSpec((None, t, d), stream),
        out_shape=jax.ShapeDtypeStruct((b, t, d), BF16),
        compiler_params=_params("parallel"),
        name="step_attn",
    )(*args)


BAND_T = 4 * CHUNK


def _band_t_kernel(*refs, nkb, t):
    q_ref = refs[0]
    k_refs = refs[1:1 + nkb]
    vt_refs = refs[1 + nkb:1 + 2 * nkb]
    bias_ref, o_ref = refs[1 + 2 * nkb], refs[2 + 2 * nkb]
    iq = pl.program_id(1)
    dh = B_HEAD_DIM

    def qk(h):
        cols = slice(h * dh, (h + 1) * dh)
        parts = []
        for j in range(nkb):
            s = lax.dot_general(k_refs[j][:, cols], q_ref[:, cols], (((1,), (1,)), ((), ())),
                                preferred_element_type=F32)
            if j < nkb - 1:
                s = jnp.where(iq >= nkb - 1 - j, s, NEG_INF)
            parts.append(s)
        return jnp.concatenate(parts, axis=0) + bias_ref[h]

    def softmax_pv(h, s):
        p = jnp.exp2(s - jnp.max(s, axis=0, keepdims=True))
        l = jnp.sum(p, axis=0, keepdims=True)
        pb = p.astype(BF16)
        o = functools.reduce(lambda a, c: a + c, [
            jnp.dot(vt_refs[j][h * dh:(h + 1) * dh, :], pb[j * t:(j + 1) * t], preferred_element_type=F32)
            for j in range(nkb)])
        return o / l

    heads = list(range(B_HEADS))
    pending = [qk(h) for h in heads[:FLASH_AHEAD]]
    outs = []
    for h in heads:
        if h + FLASH_AHEAD < B_HEADS:
            pending.append(qk(h + FLASH_AHEAD))
        outs.append(softmax_pv(h, pending.pop(0)))
        if len(outs) == B_HEADS_PER_STEP:
            c0 = (h + 1 - B_HEADS_PER_STEP) * dh
            o_ref[:, c0:c0 + LANES] = jnp.concatenate(outs, axis=0).T.astype(o_ref.dtype)
            outs = []


def _band_t(q, kv, vt, bias_t, *, nkb):
    b, t_all = q.shape[0], q.shape[1]
    t = BAND_T

    def back(j):
        return nkb - 1 - j

    k_specs = [pl.BlockSpec((None, t, D_MODEL), lambda bi, i, j=j: (bi, jnp.maximum(i - back(j), 0), 0))
               for j in range(nkb)]
    vt_specs = [pl.BlockSpec((None, None, D_MODEL, t),
                             lambda bi, i, j=j: (bi, jnp.maximum(i - back(j), 0), 0, 0))
                for j in range(nkb)]
    return pl.pallas_call(
        functools.partial(_band_t_kernel, nkb=nkb, t=t),
        grid=(b, t_all // t),
        in_specs=([pl.BlockSpec((None, t, D_MODEL), lambda bi, i: (bi, i, 0))] + k_specs + vt_specs
                  + [pl.BlockSpec(bias_t.shape, lambda bi, i: (0, 0, 0), pipeline_mode=pl.Buffered(1))]),
        out_specs=pl.BlockSpec((None, t, D_MODEL), lambda bi, i: (bi, i, 0)),
        out_shape=jax.ShapeDtypeStruct((b, t_all, D_MODEL), BF16),
        compiler_params=_params("parallel", "parallel"),
        name="band_t",
    )(q, *([kv] * nkb), *([vt] * nkb), bias_t)


def _logf_kernel(x_ref, w_ref, b_ref, o_ref):
    z = jnp.dot(x_ref[...], w_ref[...], preferred_element_type=F32) + b_ref[...]
    o_ref[...] = -(jnp.maximum(-z, 0.0) + jnp.log1p(jnp.exp(-jnp.abs(z))))


def _logf(xb, w_pad, b_pad, tm=512):
    m = xb.shape[0]
    tm = _row_tile(m, tm)
    return pl.pallas_call(
        _logf_kernel,
        grid=(m // tm,),
        in_specs=[pl.BlockSpec((tm, D_MODEL), lambda i: (i, 0)),
                  pl.BlockSpec((D_MODEL, LANES), lambda i: (0, 0)),
                  pl.BlockSpec((1, LANES), lambda i: (0, 0))],
        out_specs=pl.BlockSpec((tm, LANES), lambda i: (i, 0)),
        out_shape=jax.ShapeDtypeStruct((m, LANES), F32),
        compiler_params=_params("parallel"),
        name="logf",
    )(xb, w_pad, b_pad)


def _split3(x):
    hi = x.astype(BF16)
    r = x - hi.astype(F32)
    mid = r.astype(BF16)
    lo = (r - mid.astype(F32)).astype(BF16)
    return hi, mid, lo


N_PIECES = 3
PIECE_LANES = 16


def _cumsum_kernel(x_ref, o_ref, pk_ref, carry, *, tc):
    @pl.when(pl.program_id(1) == 0)
    def _():
        carry[...] = jnp.zeros(carry.shape, F32)

    tri = (lax.broadcasted_iota(jnp.int32, (tc, tc), 0)
           >= lax.broadcasted_iota(jnp.int32, (tc, tc), 1)).astype(BF16)
    c = functools.reduce(lambda a, b: a + b, [jnp.dot(tri, piece, preferred_element_type=F32)
                                              for piece in _split3(x_ref[...])])
    out = c + carry[0:1, :]
    o_ref[...] = out
    src = lax.broadcasted_iota(jnp.int32, (LANES, LANES), 0)
    dst = lax.broadcasted_iota(jnp.int32, (LANES, LANES), 1)
    packed = functools.reduce(lambda a, b: a + b, [
        jnp.dot(piece, ((dst == src + k * PIECE_LANES) & (src < PIECE_LANES)).astype(BF16),
                preferred_element_type=F32)
        for k, piece in enumerate(_split3(out * LOG2E))])
    pk_ref[...] = packed.astype(BF16)
    carry[...] = jnp.broadcast_to(out[tc - 1:tc, :], carry.shape)


def _cumsum(x, tc=256):
    b, t, _ = x.shape
    spec = pl.BlockSpec((None, tc, LANES), lambda bi, i: (bi, i, 0))
    return pl.pallas_call(
        functools.partial(_cumsum_kernel, tc=tc),
        grid=(b, t // tc),
        in_specs=[spec],
        out_specs=[spec] * 2,
        out_shape=[jax.ShapeDtypeStruct(x.shape, F32), jax.ShapeDtypeStruct(x.shape, BF16)],
        scratch_shapes=[pltpu.VMEM((SUBLANES, LANES), F32)],
        compiler_params=_params("parallel", "arbitrary"),
        name="cumsum",
    )(x)


FFN_CHUNK = 256
CONV_ROWS = SUBLANES


def _ffn_up_kernel(x_ref, w_ref, c_ref, p_ref, act_ref, s_ref, halo, *, tm):
    @pl.when(pl.program_id(1) == 0)
    def _():
        halo[...] = p_ref[...]

    x = x_ref[...]
    n_chunks = D_FF // FFN_CHUNK
    groups = tm // CONV_ROWS
    row = lax.broadcasted_iota(jnp.int32, (groups, CONV_ROWS, FFN_CHUNK), 1)

    def up(c, half):
        col = half * D_FF + c * FFN_CHUNK
        return jnp.dot(x, w_ref[:, col:col + FFN_CHUNK], preferred_element_type=F32)

    def conv(h, c, half):
        cols = slice(half * D_FF + c * FFN_CHUNK, half * D_FF + (c + 1) * FFN_CHUNK)
        ext = jnp.concatenate([halo[:, cols], h], axis=0).reshape(groups + 1, CONV_ROWS, FFN_CHUNK)
        hc = c_ref[CONV_W:CONV_W + 1, cols] + c_ref[CONV_W - 1:CONV_W, cols] * h
        for s in range(1, CONV_W):
            rot = pltpu.roll(ext, s, axis=1)
            shifted = jnp.where(row < s, rot[:groups], rot[1:]).reshape(tm, FFN_CHUNK)
            hc = hc + c_ref[CONV_W - 1 - s:CONV_W - s, cols] * shifted
        tail = h[tm - CONV_ROWS:tm]
        halo[:, cols] = tail
        s_ref[:, cols] = tail
        return hc

    pending = [(up(0, 0), up(0, 1))]
    for c in range(n_chunks):
        if c + 1 < n_chunks:
            pending.append((up(c + 1, 0), up(c + 1, 1)))
        ha, hg = pending.pop(0)
        a = conv(ha, c, 0)
        g = conv(hg, c, 1)
        act_ref[:, c * FFN_CHUNK:(c + 1) * FFN_CHUNK] = (g * jax.nn.sigmoid(g) * a).astype(BF16)


def _ffn_up(xb, w_up, conv_tab, past, tm=256):
    b, t, _ = xb.shape
    tm = _row_tile(t, tm)
    fixed = lambda bi, ti: (0, 0)
    return pl.pallas_call(
        functools.partial(_ffn_up_kernel, tm=tm),
        grid=(b, t // tm),
        in_specs=[pl.BlockSpec((None, tm, D_MODEL), lambda bi, ti: (bi, ti, 0)),
                  pl.BlockSpec((D_MODEL, 2 * D_FF), fixed),
                  pl.BlockSpec((CONV_ROWS, 2 * D_FF), fixed),
                  pl.BlockSpec((None, CONV_ROWS, 2 * D_FF), lambda bi, ti: (bi, 0, 0))],
        out_specs=[pl.BlockSpec((None, tm, D_FF), lambda bi, ti: (bi, ti, 0)),
                   pl.BlockSpec((None, CONV_ROWS, 2 * D_FF), lambda bi, ti: (bi, 0, 0))],
        out_shape=[jax.ShapeDtypeStruct((b, t, D_FF), BF16),
                   jax.ShapeDtypeStruct((b, CONV_ROWS, 2 * D_FF), F32)],
        scratch_shapes=[pltpu.VMEM((CONV_ROWS, 2 * D_FF), F32)],
        compiler_params=_params("parallel", "arbitrary"),
        name="ffn_up",
    )(xb, w_up, conv_tab, past)


def _rope_tables(pos, batch, q_scale):
    half = A_ROPE // 2
    inv_freq = ROPE_THETA ** (-jnp.arange(half, dtype=F32) / half)
    ang = pos.astype(F32)[:, None] * inv_freq
    cos, sin = jnp.cos(ang), jnp.sin(ang)
    cosk = jnp.concatenate([cos, cos], axis=-1)
    sink = jnp.concatenate([sin, sin], axis=-1)
    t = pos.shape[0]
    pad = jnp.zeros((t, A_QK_PAD - A_NOPE - A_ROPE), F32)
    cosq = q_scale * jnp.concatenate([jnp.ones((t, A_NOPE), F32), cosk, pad], axis=-1)
    sinq = q_scale * jnp.concatenate([jnp.zeros((t, A_NOPE), F32), sink, pad], axis=-1)
    return tuple(jnp.tile(a, (batch, 1)) for a in (cosq, sinq, cosk, sink))


def _swap_halves(w):
    half = w.shape[-1] // 2
    return jnp.concatenate([-w[..., half:], w[..., :half]], axis=-1)


def _mla_weights(w_dq, w_dkv, w_kr, w_uq, w_uk, w_uv):
    zc = jnp.zeros((D_MODEL, LANES - A_ROPE), F32)
    w1 = jnp.concatenate([w_dq, w_dkv, w_kr, zc, _swap_halves(w_kr), zc], axis=1).astype(BF16)
    wq = w_uq.reshape(A_Q_LORA, A_HEADS, A_NOPE + A_ROPE)
    nope, rope = wq[..., :A_NOPE], wq[..., A_NOPE:]
    zpad = jnp.zeros((A_Q_LORA, A_HEADS, A_QK_PAD - A_NOPE - A_ROPE), F32)
    w_cat = jnp.concatenate([nope, rope, zpad], axis=-1).reshape(A_Q_LORA, -1)
    w_sw = jnp.concatenate([jnp.zeros_like(nope), _swap_halves(rope), zpad], axis=-1).reshape(A_Q_LORA, -1)
    wq2 = jnp.concatenate([w_cat, w_sw], axis=1).astype(BF16)
    wk = jnp.zeros((A_LAT_PAD, A_HEADS, A_QK_PAD), F32)
    wk = wk.at[:A_KV_LORA, :, :A_NOPE].set(w_uk)
    eye = jnp.broadcast_to(jnp.eye(A_ROPE, dtype=F32)[:, None, :], (A_ROPE, A_HEADS, A_ROPE))
    wk = wk.at[A_KV_LORA:A_KV_LORA + A_ROPE, :, A_NOPE:A_NOPE + A_ROPE].set(eye)
    wv = jnp.zeros((A_LAT_PAD, A_HEADS * A_V), F32).at[:A_KV_LORA].set(w_uv.reshape(A_KV_LORA, -1))
    wkv = jnp.concatenate([wk.reshape(A_LAT_PAD, -1), wv], axis=1).astype(BF16)
    return w1, wq2, wkv


def _mla_mixer(xb, b, t, pos, ckv_past, kpe_past, w, i):
    w1, wq2, wkv = _mla_weights(w['a_w_dq'][i], w['a_w_dkv'][i], w['a_w_kr'][i], w['a_w_uq'][i],
                                w['a_w_uk'][i], w['a_w_uv'][i])
    prompt = ckv_past is None
    assert not prompt or t % FLASH_T == 0
    q_scale = (A_NOPE + A_ROPE) ** -0.5 * (LOG2E if prompt else 1.0)
    q, lat, ckv, kpe = _mla_proj(xb, w1, w['a_g_q'][i], w['a_g_kv'][i], wq2,
                                 *_rope_tables(pos, b, q_scale))
    lat = lat.reshape(b, t, A_LAT_PAD)
    n_k = A_HEADS * A_QK_PAD
    if prompt:
        lat2 = lat.reshape(b * t, A_LAT_PAD)
        (k_cat,) = _mm(lat2, wkv[:, :n_k], [BF16])
        vt = _proj_t(lat2, wkv[:, n_k:], FLASH_T, A_V).reshape(b, t // FLASH_T, -1, FLASH_T)
        o = _flash_t(q.reshape(b, t, -1), k_cat.reshape(b, t, n_k), 0, vt, nh=A_HEADS_PER_STEP,
                     dq=A_QK_PAD, dv=A_V, n_hblk=A_HEADS // A_HEADS_PER_STEP, chunk_causal=True)
    else:
        p_len = ckv_past.shape[1]
        past = jnp.concatenate(
            [ckv_past, kpe_past, jnp.zeros((b, p_len, A_LAT_PAD - A_KV_LORA - A_ROPE), F32)], axis=-1)
        lat = jnp.concatenate([past.astype(BF16), lat], axis=1)
        q_pos = p_len + np.arange(t)
        k_pos = np.arange(p_len + t)
        visible = (k_pos[None, :] // CHUNK) <= (q_pos[:, None] // CHUNK)
        bias = jnp.asarray(np.where(visible, 0.0, NEG_INF).astype(np.float32)[None])
        o = _mla_step(q.reshape(b, t, -1), lat, w['a_w_uk'][i], w['a_w_uv'][i], bias)
    return o.reshape(b * t, A_HEADS * A_V), ckv.reshape(b, t, -1), kpe.reshape(b, t, -1)


def _band_mixer(xb, b, t, pos0, k_past, v_past, w, i):
    prompt = k_past is None
    assert not prompt or t % BAND_T == 0
    w_qkv = w['b_w_qkv'][i]
    (q,) = _mm(xb, (w_qkv[:, :D_MODEL] * (B_HEAD_DIM ** -0.5 * (LOG2E if prompt else 1.0))).astype(BF16),
               [BF16])
    q = q.reshape(b, t, D_MODEL)
    w_kv = w_qkv[:, D_MODEL:].astype(BF16)
    heads = lambda a, rows: a.reshape(b, rows, B_HEADS, B_HEAD_DIM)
    if prompt:
        nkb = B_WIN // BAND_T + 1
        bias = _band_bias(w['b_rel_bias'][i], B_WIN + np.arange(BAND_T), np.arange(B_WIN + BAND_T))
        (kvb,) = _mm(xb, w_kv, [BF16])
        kvb = kvb.reshape(b, t, 2 * D_MODEL)
        vt = _proj_t(xb, w_qkv[:, 2 * D_MODEL:], BAND_T).reshape(b, t // BAND_T, D_MODEL, BAND_T)
        o = _band_t(q, kvb, vt, LOG2E * bias.transpose(0, 2, 1), nkb=nkb)
        keep = min(B_WIN, t)
        x_tail = xb.reshape(b, t, D_MODEL)[:, t - keep:].reshape(b * keep, D_MODEL)
        (kv_tail,) = _mm(x_tail, w_kv, [F32])
        k_new, v_new = heads(kv_tail[:, :D_MODEL], keep), heads(kv_tail[:, D_MODEL:], keep)
    else:
        kv32, kvb = _mm(xb, w_kv, [F32, BF16])
        k32, v32 = heads(kv32[:, :D_MODEL], t), heads(kv32[:, D_MODEL:], t)
        p_len = k_past.shape[1]
        kvb = kvb.reshape(b, t, 2 * D_MODEL)
        q_pos = pos0 + np.arange(t)
        k_pos = np.concatenate([np.arange(pos0 - p_len, pos0), q_pos])
        bias = _band_bias(w['b_rel_bias'][i], q_pos, k_pos)
        o = _step_attn(q, k_past.reshape(b, p_len, D_MODEL), v_past.reshape(b, p_len, D_MODEL),
                       kvb[:, :, :D_MODEL], kvb[:, :, D_MODEL:], bias, B_HEADS)
        k_new, v_new = k32, v32
    return o.reshape(b * t, D_MODEL), k_new, v_new


C_AUG = LANES


def _aug_kernel(x_ref, w_ref, f_ref, p_ref, b_ref, o_ref):
    acc = (jnp.dot(x_ref[...], w_ref[...], preferred_element_type=F32) + b_ref[...]
           + jnp.dot(f_ref[...], p_ref[...], preferred_element_type=F32))
    o_ref[...] = acc.astype(BF16)


def _aug_tables():
    n = C_HEADS * C_AUG
    place = np.zeros((LANES, 2 * n), np.float32)
    ones = np.zeros((1, 2 * n), np.float32)
    for h in range(C_HEADS):
        base = h * C_AUG + C_HEAD_DIM
        for piece in range(N_PIECES):
            ones[0, base + piece] = 1.0
            place[piece * PIECE_LANES + h, base + N_PIECES + piece] = 1.0
            place[piece * PIECE_LANES + h, n + base + piece] = -1.0
            ones[0, n + base + N_PIECES + piece] = 1.0
    return jnp.asarray(place, BF16), jnp.asarray(ones)


def _aug_qk(xb, w_aug, pieces, tm=512, tn=1024):
    m = xb.shape[0]
    n = w_aug.shape[1]
    tm = _row_tile(m, tm)
    place, ones = _aug_tables()
    row = lambda i, j: (i, 0)
    return pl.pallas_call(
        _aug_kernel,
        grid=(m // tm, n // tn),
        in_specs=[pl.BlockSpec((tm, D_MODEL), row), pl.BlockSpec((D_MODEL, tn), lambda i, j: (0, j)),
                  pl.BlockSpec((tm, LANES), row), pl.BlockSpec((LANES, tn), lambda i, j: (0, j)),
                  pl.BlockSpec((1, tn), lambda i, j: (0, j))],
        out_specs=pl.BlockSpec((tm, tn), lambda i, j: (i, j)),
        out_shape=jax.ShapeDtypeStruct((m, n), BF16),
        compiler_params=_params("parallel", "parallel"),
        name="aug_qk",
    )(xb, w_aug, pieces, place, ones)


def _pad_heads(w, scale):
    w = (w * scale).reshape(D_MODEL, C_HEADS, C_HEAD_DIM)
    return jnp.pad(w, ((0, 0), (0, 0), (0, C_AUG - C_HEAD_DIM))).reshape(D_MODEL, C_HEADS * C_AUG)


def _fox_mixer(xb, b, t, k_past, v_past, lf_past, w, i):
    resident = k_past is None
    assert not resident or t % FLASH_T == 0
    w_qkv = w['c_w_qkv'][i]
    k32, *kb = _mm(xb, w_qkv[:, D_MODEL:2 * D_MODEL].astype(BF16), [F32] if resident else [F32, BF16])
    v32, *vb = _mm(xb, w_qkv[:, 2 * D_MODEL:].astype(BF16), [F32] if resident else [F32, BF16])
    k32 = k32.reshape(b, t, C_HEADS, C_HEAD_DIM)
    v32 = v32.reshape(b, t, C_HEADS, C_HEAD_DIM)
    w_f = jnp.zeros((D_MODEL, LANES), F32).at[:, :C_HEADS].set(w['c_w_f'][i]).astype(BF16)
    b_f = jnp.zeros((1, LANES), F32).at[0, :C_HEADS].set(w['c_b_f'][i])
    log_f = _logf(xb, w_f, b_f).reshape(b, t, LANES)
    lf_all = log_f
    if not resident:
        lf_all = jnp.concatenate([jnp.pad(lf_past, ((0, 0), (0, 0), (0, LANES - C_HEADS))), log_f], axis=1)
    t_k = lf_all.shape[1]
    tc = 256
    t_pad = -(-t_k // tc) * tc
    f_cum, f_packed = _cumsum(jnp.pad(lf_all, ((0, 0), (0, t_pad - t_k), (0, 0))), tc)
    nh = C_HEADS_PER_STEP
    n_hblk = C_HEADS // nh
    q_scale = C_HEAD_DIM ** -0.5
    if resident:
        w_aug = jnp.concatenate([_pad_heads(w_qkv[:, :D_MODEL], q_scale * LOG2E),
                                 _pad_heads(w_qkv[:, D_MODEL:2 * D_MODEL], 1.0)], axis=1).astype(BF16)
        qk_aug = _aug_qk(xb, w_aug, f_packed.reshape(b * t, LANES)).reshape(b, t, -1)
        n_q = C_HEADS * C_AUG
        vt = _proj_t(xb, w_qkv[:, 2 * D_MODEL:], FLASH_T, C_HEAD_DIM).reshape(b, t // FLASH_T, -1, FLASH_T)
        o = _flash_t(qk_aug, qk_aug, n_q // (nh * C_AUG), vt,
                     nh=nh, dq=C_AUG, dv=C_HEAD_DIM, n_hblk=n_hblk, chunk_causal=False)
        return o.reshape(b * t, D_MODEL), k32, v32, log_f[:, :, :C_HEADS]
    (q,) = _mm(xb, (w_qkv[:, :D_MODEL] * q_scale).astype(BF16), [BF16])
    p_len = k_past.shape[1]
    causal = np.where(np.tril(np.ones((t, t), bool)), 0.0, NEG_INF).astype(np.float32)
    bias = jnp.asarray(np.concatenate([np.zeros((t, p_len), np.float32), causal], axis=1)[None])
    o = _step_attn(q.reshape(b, t, D_MODEL), k_past.reshape(b, p_len, D_MODEL),
                   v_past.reshape(b, p_len, D_MODEL), kb[0].reshape(b, t, D_MODEL),
                   vb[0].reshape(b, t, D_MODEL), bias, C_HEADS, f_cum=f_cum[:, :t_k, :C_HEADS])
    return o.reshape(b * t, D_MODEL), k32, v32, log_f[:, :, :C_HEADS]


def _conv_ffn(xb, b, t, conv_past, w, i):
    tab = jnp.concatenate([w['f_conv_w'][i], w['f_conv_b'][i][None],
                           jnp.zeros((CONV_ROWS - CONV_W - 1, 2 * D_FF), F32)], axis=0)
    if conv_past is None:
        past = jnp.zeros((b, CONV_ROWS, 2 * D_FF), F32)
    else:
        past = jnp.pad(conv_past, ((0, 0), (CONV_ROWS - (CONV_W - 1), 0), (0, 0)))
    act, tail = _ffn_up(xb.reshape(b, t, D_MODEL), w['f_w_up_bf16'][i], tab, past)
    return act.reshape(b * t, D_FF), tail[:, CONV_ROWS - (CONV_W - 1):]


def _trunk(x, pos0, past, w):
    b, t, _ = x.shape
    pos = pos0 + jnp.arange(t)
    xf = x.reshape(b * t, D_MODEL)
    xb = xf.astype(BF16)
    outs = {n: [] for n in ('a_ckv', 'a_kpe', 'b_k', 'b_v', 'c_k', 'c_v', 'c_logf', 'ffn_conv')}
    ia = ib = ic = 0
    get = lambda name, j: None if past is None else past[name][j]
    for i in range(DEPTH):
        kind = i % N_MIXERS
        if kind == 0:
            o, ckv, kpe = _mla_mixer(xb, b, t, pos, get('a_ckv', ia), get('a_kpe', ia), w, ia)
            outs['a_ckv'].append(ckv)
            outs['a_kpe'].append(kpe)
            w_o = w['a_w_o_bf16'][ia]
            ia += 1
        elif kind == 1:
            o, kb, vb = _band_mixer(xb, b, t, pos0, get('b_k', ib), get('b_v', ib), w, ib)
            outs['b_k'].append(kb)
            outs['b_v'].append(vb)
            w_o = w['b_w_o_bf16'][ib]
            ib += 1
        else:
            o, kc, vc, lf = _fox_mixer(xb, b, t, get('c_k', ic), get('c_v', ic), get('c_logf', ic), w, ic)
            outs['c_k'].append(kc)
            outs['c_v'].append(vc)
            outs['c_logf'].append(lf)
            w_o = w['c_w_o_bf16'][ic]
            ic += 1
        xf, xb = _mm_res_ln(o, w_o, xf, w['ln1_g'][i], w['ln1_b'][i])
        act, conv_state = _conv_ffn(xb, b, t, get('ffn_conv', i), w, i)
        outs['ffn_conv'].append(conv_state)
        xf, xb = _mm_res_ln(act, w['f_w_down_bf16'][i], xf, w['ln2_g'][i], w['ln2_b'][i])
    return xf.reshape(b, t, D_MODEL), {n: jnp.stack(v) for n, v in outs.items()}


def kernel(x_prompt, x_sample, cache_a_ckv, cache_a_kpe, cache_b_k, cache_b_v, cache_c_k, cache_c_v,
           cache_c_logf, state_ffn_conv, a_w_dq, a_g_q, a_w_uq, a_w_dkv, a_g_kv, a_w_kr, a_w_uk, a_w_uv,
           a_w_o, b_w_qkv, b_rel_bias, b_w_o, c_w_qkv, c_w_f, c_b_f, c_w_o, f_w_up, f_conv_w, f_conv_b,
           f_w_down, ln1_g, ln1_b, ln2_g, ln2_b):
    w = dict(a_w_dq=a_w_dq, a_g_q=a_g_q, a_w_uq=a_w_uq, a_w_dkv=a_w_dkv, a_g_kv=a_g_kv, a_w_kr=a_w_kr,
             a_w_uk=a_w_uk, a_w_uv=a_w_uv, a_w_o=a_w_o, b_w_qkv=b_w_qkv, b_rel_bias=b_rel_bias, b_w_o=b_w_o,
             c_w_qkv=c_w_qkv, c_w_f=c_w_f, c_b_f=c_b_f, c_w_o=c_w_o, f_w_up=f_w_up, f_conv_w=f_conv_w,
             f_conv_b=f_conv_b, f_w_down=f_w_down, ln1_g=ln1_g, ln1_b=ln1_b, ln2_g=ln2_g, ln2_b=ln2_b)
    past = dict(a_ckv=cache_a_ckv, a_kpe=cache_a_kpe, b_k=cache_b_k, b_v=cache_b_v, c_k=cache_c_k,
                c_v=cache_c_v, c_logf=cache_c_logf, ffn_conv=state_ffn_conv)
    past_len = cache_a_ckv.shape[2]
    for name in ('f_w_up', 'f_w_down', 'a_w_o', 'b_w_o', 'c_w_o'):
        w[name + '_bf16'] = _to_bf16(w[name])
    y_prompt, p = _trunk(x_prompt, 0, None, w)
    y_sample, s = _trunk(x_sample, past_len, past, w)
    names = ('a_ckv', 'a_kpe', 'b_k', 'b_v', 'c_k', 'c_v', 'c_logf', 'ffn_conv')
    return (y_prompt, y_sample) + tuple(p[n] for n in names) + tuple(s[n] for n in names)
```

```python
import functools
import math

import numpy as np
import jax
import jax.numpy as jnp
from jax import lax
from jax.experimental import pallas as pl
from jax.experimental.pallas import tpu as pltpu

F32 = jnp.float32
BF16 = jnp.bfloat16

D_MODEL = 1024
DEPTH = 4
CHUNK = 64
N_MIXERS = 3

A_HEADS = 8
A_Q_LORA = 384
A_KV_LORA = 256
A_NOPE = 128
A_ROPE = 64
A_V = 128
A_QK_PAD = 256
A_LAT_PAD = 384
ROPE_THETA = 10000.0

B_HEADS = 16
B_HEAD_DIM = D_MODEL // B_HEADS
B_LEFT_CHUNKS = 8
B_WIN = B_LEFT_CHUNKS * CHUNK
B_REL_CLIP = 128

C_HEADS = 16
C_HEAD_DIM = D_MODEL // C_HEADS

D_FF = 2816
CONV_W = 3

ALPHA = (2.0 * DEPTH) ** 0.25
LN_EPS = 1e-5
RMS_EPS = 1e-6
NEG_INF = -1e30

LANES = 128
SUBLANES = 8
VMEM_LIMIT_BYTES = 48 * 2 ** 20


def _params(*sem):
    return pltpu.CompilerParams(dimension_semantics=sem, vmem_limit_bytes=VMEM_LIMIT_BYTES)


def _row_tile(m, tm):
    while m % tm:
        tm //= 2
    assert tm % SUBLANES == 0, (m, tm)
    return tm


def _mm_kernel(x_ref, w_ref, *o_refs):
    acc = jnp.dot(x_ref[...], w_ref[...], preferred_element_type=F32)
    for o_ref in o_refs:
        o_ref[...] = acc.astype(o_ref.dtype)


def _mm(x, w, out_dtypes, tm=512, tn=1024):
    m, k = x.shape
    n = w.shape[1]
    tm, tn = _row_tile(m, tm), min(tn, n)
    return pl.pallas_call(
        _mm_kernel,
        grid=(m // tm, n // tn),
        in_specs=[pl.BlockSpec((tm, k), lambda i, j: (i, 0)),
                  pl.BlockSpec((k, tn), lambda i, j: (0, j))],
        out_specs=[pl.BlockSpec((tm, tn), lambda i, j: (i, j)) for _ in out_dtypes],
        out_shape=[jax.ShapeDtypeStruct((m, n), d) for d in out_dtypes],
        compiler_params=_params("parallel", "parallel"),
        name="mm",
    )(x, w)


ONES_ROWS = 16


def _proj_t_kernel(wt_ref, b_ref, x_ref, o_ref):
    o_ref[...] = (lax.dot_general(wt_ref[...], x_ref[...], (((1,), (1,)), ((), ())),
                                  preferred_element_type=F32) + b_ref[...]).astype(o_ref.dtype)


def _proj_t(x, w, t_blk, head_dim=None, tm=512):
    m, k = x.shape
    n = w.shape[1]
    bias = jnp.zeros((n, 1), F32)
    if head_dim is not None:
        heads = n // head_dim
        w = jnp.pad(w.reshape(k, heads, head_dim), ((0, 0), (0, 0), (0, ONES_ROWS)))
        bias = jnp.pad(jnp.zeros((heads, head_dim, 1), F32), ((0, 0), (0, ONES_ROWS), (0, 0)),
                       constant_values=1.0)
        n = heads * (head_dim + ONES_ROWS)
        w, bias = w.reshape(k, n), bias.reshape(n, 1)
    tm = min(tm, t_blk)
    per = t_blk // tm
    return pl.pallas_call(
        _proj_t_kernel,
        grid=(m // tm,),
        in_specs=[pl.BlockSpec((n, k), lambda i: (0, 0)), pl.BlockSpec((n, 1), lambda i: (0, 0)),
                  pl.BlockSpec((tm, k), lambda i: (i, 0))],
        out_specs=pl.BlockSpec((None, n, tm), lambda i: (i // per, 0, i % per)),
        out_shape=jax.ShapeDtypeStruct((m // t_blk, n, t_blk), BF16),
        compiler_params=_params("parallel"),
        name="proj_t",
    )(w.T.astype(BF16), bias, x)


def _cast_kernel(x_ref, o_ref):
    o_ref[...] = x_ref[...].astype(o_ref.dtype)


def _to_bf16(w, tr=256):
    n_l, r, c = w.shape
    tr = _row_tile(r, tr)
    spec = pl.BlockSpec((None, tr, c), lambda l, i: (l, i, 0))
    return pl.pallas_call(
        _cast_kernel,
        grid=(n_l, r // tr),
        in_specs=[spec],
        out_specs=spec,
        out_shape=jax.ShapeDtypeStruct(w.shape, BF16),
        compiler_params=_params("parallel", "parallel"),
        name="to_bf16",
    )(w)


LN_ROWS = 128


def _mm_res_ln_kernel(a_ref, w_ref, x_ref, g_ref, b_ref, of_ref, ob_ref):
    tm = a_ref.shape[0]
    pieces = [slice(r, min(r + LN_ROWS, tm)) for r in range(0, tm, LN_ROWS)]

    def mm(rows):
        return jnp.dot(a_ref[rows, :], w_ref[...], preferred_element_type=F32)

    def ln(rows, acc):
        y = ALPHA * x_ref[rows, :] + acc
        mu = jnp.mean(y, axis=-1, keepdims=True)
        d = y - mu
        var = jnp.mean(d * d, axis=-1, keepdims=True)
        out = d * lax.rsqrt(var + LN_EPS) * g_ref[...] + b_ref[...]
        of_ref[rows, :] = out
        ob_ref[rows, :] = out.astype(BF16)

    acc = mm(pieces[0])
    for i, rows in enumerate(pieces):
        nxt = mm(pieces[i + 1]) if i + 1 < len(pieces) else None
        ln(rows, acc)
        acc = nxt


def _mm_res_ln(a, w, x, g, b, tm=512):
    m, k = a.shape
    n = w.shape[1]
    tm = _row_tile(m, tm)
    row = lambda i: (i, 0)
    fixed = lambda i: (0, 0)
    return pl.pallas_call(
        _mm_res_ln_kernel,
        grid=(m // tm,),
        in_specs=[pl.BlockSpec((tm, k), row), pl.BlockSpec((k, n), fixed),
                  pl.BlockSpec((tm, n), row), pl.BlockSpec((1, n), fixed),
                  pl.BlockSpec((1, n), fixed)],
        out_specs=[pl.BlockSpec((tm, n), row), pl.BlockSpec((tm, n), row)],
        out_shape=[jax.ShapeDtypeStruct((m, n), F32), jax.ShapeDtypeStruct((m, n), BF16)],
        compiler_params=_params("parallel"),
        name="mm_res_ln",
    )(a, w, x, g.reshape(1, n), b.reshape(1, n))


_W1_CQ = (0, A_Q_LORA)
_W1_CKV = (A_Q_LORA, A_Q_LORA + A_KV_LORA)
_W1_KR = (_W1_CKV[1], _W1_CKV[1] + A_ROPE)
_W1_KRS = (_W1_CKV[1] + LANES, _W1_CKV[1] + LANES + A_ROPE)
_W1_COLS = _W1_CKV[1] + 2 * LANES


def _rms(v, g):
    return v * lax.rsqrt(jnp.mean(v * v, axis=-1, keepdims=True) + RMS_EPS) * g


def _mla_proj_kernel(x_ref, w1_ref, gq_ref, gkv_ref, wq_ref, cq_ref, sq_ref, ck_ref, sk_ref,
                     q_ref, lat_ref, ckv_ref, kpe_ref):
    y = jnp.dot(x_ref[...], w1_ref[...], preferred_element_type=F32)
    cq = _rms(y[:, _W1_CQ[0]:_W1_CQ[1]], gq_ref[...]).astype(BF16)
    ckv = _rms(y[:, _W1_CKV[0]:_W1_CKV[1]], gkv_ref[...])
    kpe = y[:, _W1_KR[0]:_W1_KR[1]] * ck_ref[...] + y[:, _W1_KRS[0]:_W1_KRS[1]] * sk_ref[...]
    ckv_ref[...] = ckv
    kpe_ref[...] = kpe
    lat_ref[:, 0:A_KV_LORA] = ckv.astype(BF16)
    lat_ref[:, A_KV_LORA:A_KV_LORA + A_ROPE] = kpe.astype(BF16)
    lat_ref[:, A_KV_LORA + A_ROPE:] = jnp.zeros(
        (lat_ref.shape[0], A_LAT_PAD - A_KV_LORA - A_ROPE), BF16)
    sw0 = A_HEADS * A_QK_PAD
    for h in range(A_HEADS):
        lo, hi = h * A_QK_PAD, (h + 1) * A_QK_PAD
        qp = jnp.dot(cq, wq_ref[:, lo:hi], preferred_element_type=F32)
        qs = jnp.dot(cq, wq_ref[:, sw0 + lo:sw0 + hi], preferred_element_type=F32)
        q_ref[:, lo:hi] = (qp * cq_ref[...] + qs * sq_ref[...]).astype(BF16)


def _mla_proj(xb, w1, gq, gkv, wq, cosq, sinq, cosk, sink, tm=512):
    m = xb.shape[0]
    tm = _row_tile(m, tm)
    row = lambda i: (i, 0)
    fixed = lambda i: (0, 0)
    nq = A_HEADS * A_QK_PAD
    return pl.pallas_call(
        _mla_proj_kernel,
        grid=(m // tm,),
        in_specs=[pl.BlockSpec((tm, D_MODEL), row), pl.BlockSpec(w1.shape, fixed),
                  pl.BlockSpec((1, A_Q_LORA), fixed), pl.BlockSpec((1, A_KV_LORA), fixed),
                  pl.BlockSpec(wq.shape, fixed),
                  pl.BlockSpec((tm, A_QK_PAD), row), pl.BlockSpec((tm, A_QK_PAD), row),
                  pl.BlockSpec((tm, A_ROPE), row), pl.BlockSpec((tm, A_ROPE), row)],
        out_specs=[pl.BlockSpec((tm, nq), row), pl.BlockSpec((tm, A_LAT_PAD), row),
                   pl.BlockSpec((tm, A_KV_LORA), row), pl.BlockSpec((tm, A_ROPE), row)],
        out_shape=[jax.ShapeDtypeStruct((m, nq), BF16), jax.ShapeDtypeStruct((m, A_LAT_PAD), BF16),
                   jax.ShapeDtypeStruct((m, A_KV_LORA), F32), jax.ShapeDtypeStruct((m, A_ROPE), F32)],
        compiler_params=_params("parallel"),
        name="mla_proj",
    )(xb, w1, gq.reshape(1, -1), gkv.reshape(1, -1), wq, cosq, sinq, cosk, sink)


def _mla_step_kernel(q_ref, lat_ref, m_ref, wuv_ref, bias_ref, o_ref):
    nt = (((1,), (1,)), ((), ()))
    lat = lat_ref[...]
    ckv = lat_ref[:, 0:A_KV_LORA]
    heads = range(A_HEADS)
    qf = [jnp.dot(q_ref[:, h * A_QK_PAD:(h + 1) * A_QK_PAD], m_ref[h],
                  preferred_element_type=F32).astype(BF16) for h in heads]
    s = [lax.dot_general(qf[h], lat, nt, preferred_element_type=F32) + bias_ref[0] for h in heads]
    p = [jnp.exp(s[h] - jnp.max(s[h], axis=-1, keepdims=True)) for h in heads]
    o_lat = [jnp.dot(p[h].astype(BF16), ckv, preferred_element_type=F32) for h in heads]
    for h in heads:
        o = jnp.dot(o_lat[h].astype(BF16), wuv_ref[:, h * A_V:(h + 1) * A_V], preferred_element_type=F32)
        l = jnp.sum(p[h], axis=-1, keepdims=True)
        o_ref[:, h * A_V:(h + 1) * A_V] = (o / l).astype(o_ref.dtype)


def _mla_step(q, lat, w_uk, w_uv, bias):
    b, t, nq = q.shape
    t_k = lat.shape[1]
    absorb = jnp.zeros((A_HEADS, A_QK_PAD, A_LAT_PAD), F32)
    absorb = absorb.at[:, :A_NOPE, :A_KV_LORA].set(jnp.transpose(w_uk, (1, 2, 0)))
    absorb = absorb.at[:, A_NOPE:A_NOPE + A_ROPE, A_KV_LORA:A_KV_LORA + A_ROPE].set(
        jnp.eye(A_ROPE, dtype=F32))
    stream = lambda bi: (bi, 0, 0)
    return pl.pallas_call(
        _mla_step_kernel,
        grid=(b,),
        in_specs=[pl.BlockSpec((None, t, nq), stream), pl.BlockSpec((None, t_k, A_LAT_PAD), stream),
                  pl.BlockSpec(absorb.shape, lambda bi: (0, 0, 0)),
                  pl.BlockSpec((A_KV_LORA, A_HEADS * A_V), lambda bi: (0, 0)),
                  pl.BlockSpec(bias.shape, lambda bi: (0, 0, 0))],
        out_specs=pl.BlockSpec((None, t, A_HEADS * A_V), stream),
        out_shape=jax.ShapeDtypeStruct((b, t, A_HEADS * A_V), BF16),
        compiler_params=_params("parallel"),
        name="mla_step",
    )(q, lat, absorb.astype(BF16), w_uv.reshape(A_KV_LORA, A_HEADS * A_V).astype(BF16), bias)


FLASH_T = 1024
FLASH_QC = 256
FLASH_KC = 256
FLASH_AHEAD = 8
A_HEADS_PER_STEP = 2
C_HEADS_PER_STEP = 4
LOG2E = math.log2(math.e)


def _flash_t_kernel(q_ref, k_ref, vt_ref, o_ref, m_s, acc_s, *, nh, dq, dv, t, chunk_causal):
    iq = pl.program_id(2)
    dva = dv + ONES_ROWS
    m_s[...] = jnp.full(m_s.shape, -jnp.inf, F32)
    acc_s[...] = jnp.zeros(acc_s.shape, F32)

    qc, kc = FLASH_QC, FLASH_KC

    def run(blocks):
        def n_keys(masked, g, c):
            return min(kc, (c + 1) * qc - g * kc) if masked else kc

        chains = [(j, masked, g, h, c) for j, masked in blocks
                  for g in range(t // kc) for h in range(nh) for c in range(t // qc)
                  if n_keys(masked, g, c) > 0]

        def qk(j, masked, g, h, c):
            row0 = pl.multiple_of(j * t, t) + g * kc
            return lax.dot_general(k_ref[pl.ds(row0, n_keys(masked, g, c)), h * dq:(h + 1) * dq],
                                   q_ref[c * qc:(c + 1) * qc, h * dq:(h + 1) * dq],
                                   (((1,), (1,)), ((), ())), preferred_element_type=F32)

        def softmax_pv(j, masked, g, h, c, s):
            cols = slice(c * qc, (c + 1) * qc)
            nk = n_keys(masked, g, c)
            if masked and g * kc + nk > c * qc:
                k_pos = lax.broadcasted_iota(jnp.int32, (nk, qc), 0) + g * kc
                q_pos = lax.broadcasted_iota(jnp.int32, (nk, qc), 1) + c * qc
                if chunk_causal:
                    shift = CHUNK.bit_length() - 1
                    mask = jnp.right_shift(k_pos, shift) <= jnp.right_shift(q_pos, shift)
                else:
                    mask = k_pos <= q_pos
                s = jnp.where(mask, s, NEG_INF)
            m_prev = m_s[h, :, cols]
            m_new = jnp.maximum(m_prev, jnp.max(s, axis=0, keepdims=True))
            alpha = jnp.exp2(m_prev - m_new)
            p = jnp.exp2(s - m_new)
            acc_s[h, :, cols] = alpha * acc_s[h, :, cols] + jnp.dot(
                vt_ref[j, h * dva:(h + 1) * dva, g * kc:g * kc + nk], p.astype(BF16),
                preferred_element_type=F32)
            m_s[h, :, cols] = m_new

        pending = [qk(*chain) for chain in chains[:FLASH_AHEAD]]
        for i, chain in enumerate(chains):
            if i + FLASH_AHEAD < len(chains):
                pending.append(qk(*chains[i + FLASH_AHEAD]))
            softmax_pv(*chain, pending.pop(0))

    def block_pair(jj, carry):
        run([(2 * jj, False), (2 * jj + 1, False)])
        return carry

    lax.fori_loop(0, iq // 2, block_pair, 0)

    @pl.when(iq % 2 == 1)
    def _odd():
        run([(iq - 1, False)])

    run([(iq, True)])
    out = jnp.concatenate([acc_s[h, 0:dv] / acc_s[h, dv:dv + 1] for h in range(nh)], axis=0)
    o_ref[...] = out.T.astype(o_ref.dtype)


def _flash_t(q, k, k_col0, vt, *, nh, dq, dv, n_hblk, chunk_causal):
    b, t_all = q.shape[0], q.shape[1]
    t = FLASH_T
    nblk = t_all // t
    kern = functools.partial(_flash_t_kernel, nh=nh, dq=dq, dv=dv, t=t, chunk_causal=chunk_causal)
    return pl.pallas_call(
        kern,
        grid=(b, n_hblk, nblk),
        in_specs=[pl.BlockSpec((None, t, nh * dq), lambda bi, h, i: (bi, i, h)),
                  pl.BlockSpec((None, t_all, nh * dq), lambda bi, h, i: (bi, 0, k_col0 + h),
                               pipeline_mode=pl.Buffered(1)),
                  pl.BlockSpec((None, nblk, nh * (dv + ONES_ROWS), t), lambda bi, h, i: (bi, 0, h, 0),
                               pipeline_mode=pl.Buffered(1))],
        out_specs=pl.BlockSpec((None, t, nh * dv), lambda bi, h, i: (bi, i, h)),
        out_shape=jax.ShapeDtypeStruct((b, t_all, n_hblk * nh * dv), BF16),
        scratch_shapes=[pltpu.VMEM((nh, 1, t), F32), pltpu.VMEM((nh, dv + ONES_ROWS, t), F32)],
        compiler_params=_params("parallel", "parallel", "parallel"),
        name="flash_t_chunk" if chunk_causal else "flash_t_frame",
    )(q, k, vt)


B_HEADS_PER_STEP = LANES // B_HEAD_DIM


STEP_AHEAD = 4


def _step_attn_kernel(*refs, heads, dh, forget):
    if forget:
        q_ref, kp_ref, vp_ref, kn_ref, vn_ref, bp_ref, bn_ref, fq_ref, fkp_ref, fkn_ref, o_ref = refs
    else:
        q_ref, kp_ref, vp_ref, kn_ref, vn_ref, bp_ref, bn_ref, o_ref = refs
    nt = (((1,), (1,)), ((), ()))
    n_bias = bp_ref.shape[0]
    cols = [slice(h * dh, (h + 1) * dh) for h in range(heads)]

    def scores(h):
        q = q_ref[:, cols[h]]
        sp = lax.dot_general(q, kp_ref[:, cols[h]].astype(BF16), nt, preferred_element_type=F32)
        sn = lax.dot_general(q, kn_ref[:, cols[h]], nt, preferred_element_type=F32)
        sp = sp + bp_ref[h % n_bias]
        sn = sn + bn_ref[h % n_bias]
        if forget:
            fq = fq_ref[:, h:h + 1]
            sp = sp + fq - fkp_ref[h:h + 1, :]
            sn = sn + fq - fkn_ref[h:h + 1, :]
        return sp, sn

    def finish(h, sp, sn):
        m = jnp.maximum(jnp.max(sp, axis=-1, keepdims=True), jnp.max(sn, axis=-1, keepdims=True))
        pp = jnp.exp(sp - m)
        pn = jnp.exp(sn - m)
        l = jnp.sum(pp, axis=-1, keepdims=True) + jnp.sum(pn, axis=-1, keepdims=True)
        o = (jnp.dot(pp.astype(BF16), vp_ref[:, cols[h]].astype(BF16), preferred_element_type=F32)
             + jnp.dot(pn.astype(BF16), vn_ref[:, cols[h]], preferred_element_type=F32))
        o_ref[:, cols[h]] = (o / l).astype(o_ref.dtype)

    pending = [scores(h) for h in range(min(STEP_AHEAD, heads))]
    for h in range(heads):
        if h + STEP_AHEAD < heads:
            pending.append(scores(h + STEP_AHEAD))
        finish(h, *pending.pop(0))


def _band_bias(rel_bias, q_pos, k_pos):
    nq, nk = len(q_pos), len(k_pos)
    assert (np.diff(q_pos) == 1).all() and (np.diff(k_pos) == 1).all()
    m = np.arange(nq + nk - 1)
    u = rel_bias[:, np.clip(q_pos[0] - k_pos[0] + nq - 1 - m, -B_REL_CLIP, B_REL_CLIP) + B_REL_CLIP]
    period = nq + nk
    w = jnp.concatenate([u[:, nq - 1:], jnp.zeros((u.shape[0], 1), u.dtype), u[:, :nq - 1]], axis=1)
    skew = jnp.tile(w, (1, nq))[:, :nq * (period - 1)].reshape(-1, nq, period - 1)[:, :, :nk]
    qc = q_pos[:, None] // CHUNK
    kc = k_pos[None, :] // CHUNK
    mask = (kc <= qc) & (kc >= qc - B_LEFT_CHUNKS) & (k_pos[None, :] >= 0)
    return jnp.where(jnp.asarray(mask)[None], skew, NEG_INF).astype(F32)


def _step_attn(q, k_past, v_past, k_new, v_new, bias, heads, f_cum=None):
    b, t, d = q.shape
    p = k_past.shape[1]
    stream = lambda bi: (bi, 0, 0)
    fixed = lambda bi: (0, 0, 0)
    args = [q, k_past, v_past, k_new, v_new, bias[:, :, :p], bias[:, :, p:]]
    in_specs = [pl.BlockSpec((None, t, d), stream), pl.BlockSpec((None, p, d), stream),
                pl.BlockSpec((None, p, d), stream), pl.BlockSpec((None, t, d), stream),
                pl.BlockSpec((None, t, d), stream),
                pl.BlockSpec((bias.shape[0], t, p), fixed), pl.BlockSpec((bias.shape[0], t, t), fixed)]
    if f_cum is not None:
        fq = jnp.pad(f_cum[:, p:], ((0, 0), (0, 0), (0, LANES - heads)))
        fk = f_cum.transpose(0, 2, 1)
        args += [fq, fk[:, :, :p], fk[:, :, p:]]
        in_specs += [pl.BlockSpec((None, t, LANES), stream), pl.BlockSpec((None, heads, p), stream),
                     pl.BlockSpec((None, heads, t), stream)]
    return pl.pallas_call(
        functools.partial(_step_attn_kernel, heads=heads, dh=d // heads, forget=f_cum is not None),
        grid=(b,),
        in_specs=in_specs,
        out_specs=pl.BlockSpec((None, t, d), stream),
        out_shape=jax.ShapeDtypeStruct((b, t, d), BF16),
        compiler_params=_params("parallel"),
        name="step_attn",
    )(*args)


BAND_T = 4 * CHUNK


def _band_t_kernel(*refs, nkb, t):
    q_ref = refs[0]
    k_refs = refs[1:1 + nkb]
    vt_refs = refs[1 + nkb:1 + 2 * nkb]
    bias_ref, o_ref = refs[1 + 2 * nkb], refs[2 + 2 * nkb]
    iq = pl.program_id(1)
    dh = B_HEAD_DIM

    def qk(h):
        cols = slice(h * dh, (h + 1) * dh)
        parts = []
        for j in range(nkb):
            s = lax.dot_general(k_refs[j][:, cols], q_ref[:, cols], (((1,), (1,)), ((), ())),
                                preferred_element_type=F32)
            if j < nkb - 1:
                s = jnp.where(iq >= nkb - 1 - j, s, NEG_INF)
            parts.append(s)
        return jnp.concatenate(parts, axis=0) + bias_ref[h]

    def softmax_pv(h, s):
        p = jnp.exp2(s - jnp.max(s, axis=0, keepdims=True))
        l = jnp.sum(p, axis=0, keepdims=True)
        pb = p.astype(BF16)
        o = functools.reduce(lambda a, c: a + c, [
            jnp.dot(vt_refs[j][h * dh:(h + 1) * dh, :], pb[j * t:(j + 1) * t], preferred_element_type=F32)
            for j in range(nkb)])
        return o / l

    heads = list(range(B_HEADS))
    pending = [qk(h) for h in heads[:FLASH_AHEAD]]
    outs = []
    for h in heads:
        if h + FLASH_AHEAD < B_HEADS:
            pending.append(qk(h + FLASH_AHEAD))
        outs.append(softmax_pv(h, pending.pop(0)))
        if len(outs) == B_HEADS_PER_STEP:
            c0 = (h + 1 - B_HEADS_PER_STEP) * dh
            o_ref[:, c0:c0 + LANES] = jnp.concatenate(outs, axis=0).T.astype(o_ref.dtype)
            outs = []


def _band_t(q, kv, vt, bias_t, *, nkb):
    b, t_all = q.shape[0], q.shape[1]
    t = BAND_T

    def back(j):
        return nkb - 1 - j

    k_specs = [pl.BlockSpec((None, t, D_MODEL), lambda bi, i, j=j: (bi, jnp.maximum(i - back(j), 0), 0))
               for j in range(nkb)]
    vt_specs = [pl.BlockSpec((None, None, D_MODEL, t),
                             lambda bi, i, j=j: (bi, jnp.maximum(i - back(j), 0), 0, 0))
                for j in range(nkb)]
    return pl.pallas_call(
        functools.partial(_band_t_kernel, nkb=nkb, t=t),
        grid=(b, t_all // t),
        in_specs=([pl.BlockSpec((None, t, D_MODEL), lambda bi, i: (bi, i, 0))] + k_specs + vt_specs
                  + [pl.BlockSpec(bias_t.shape, lambda bi, i: (0, 0, 0), pipeline_mode=pl.Buffered(1))]),
        out_specs=pl.BlockSpec((None, t, D_MODEL), lambda bi, i: (bi, i, 0)),
        out_shape=jax.ShapeDtypeStruct((b, t_all, D_MODEL), BF16),
        compiler_params=_params("parallel", "parallel"),
        name="band_t",
    )(q, *([kv] * nkb), *([vt] * nkb), bias_t)


def _logf_kernel(x_ref, w_ref, b_ref, o_ref):
    z = jnp.dot(x_ref[...], w_ref[...], preferred_element_type=F32) + b_ref[...]
    o_ref[...] = -(jnp.maximum(-z, 0.0) + jnp.log1p(jnp.exp(-jnp.abs(z))))


def _logf(xb, w_pad, b_pad, tm=512):
    m = xb.shape[0]
    tm = _row_tile(m, tm)
    return pl.pallas_call(
        _logf_kernel,
        grid=(m // tm,),
        in_specs=[pl.BlockSpec((tm, D_MODEL), lambda i: (i, 0)),
                  pl.BlockSpec((D_MODEL, LANES), lambda i: (0, 0)),
                  pl.BlockSpec((1, LANES), lambda i: (0, 0))],
        out_specs=pl.BlockSpec((tm, LANES), lambda i: (i, 0)),
        out_shape=jax.ShapeDtypeStruct((m, LANES), F32),
        compiler_params=_params("parallel"),
        name="logf",
    )(xb, w_pad, b_pad)


def _split3(x):
    hi = x.astype(BF16)
    r = x - hi.astype(F32)
    mid = r.astype(BF16)
    lo = (r - mid.astype(F32)).astype(BF16)
    return hi, mid, lo


N_PIECES = 3
PIECE_LANES = 16


def _cumsum_kernel(x_ref, o_ref, pk_ref, carry, *, tc):
    @pl.when(pl.program_id(1) == 0)
    def _():
        carry[...] = jnp.zeros(carry.shape, F32)

    tri = (lax.broadcasted_iota(jnp.int32, (tc, tc), 0)
           >= lax.broadcasted_iota(jnp.int32, (tc, tc), 1)).astype(BF16)
    c = functools.reduce(lambda a, b: a + b, [jnp.dot(tri, piece, preferred_element_type=F32)
                                              for piece in _split3(x_ref[...])])
    out = c + carry[0:1, :]
    o_ref[...] = out
    src = lax.broadcasted_iota(jnp.int32, (LANES, LANES), 0)
    dst = lax.broadcasted_iota(jnp.int32, (LANES, LANES), 1)
    packed = functools.reduce(lambda a, b: a + b, [
        jnp.dot(piece, ((dst == src + k * PIECE_LANES) & (src < PIECE_LANES)).astype(BF16),
                preferred_element_type=F32)
        for k, piece in enumerate(_split3(out * LOG2E))])
    pk_ref[...] = packed.astype(BF16)
    carry[...] = jnp.broadcast_to(out[tc - 1:tc, :], carry.shape)


CUMSUM_ROWS = 512


def _cumsum(x, tc=CUMSUM_ROWS):
    b, t, _ = x.shape
    spec = pl.BlockSpec((None, tc, LANES), lambda bi, i: (bi, i, 0))
    return pl.pallas_call(
        functools.partial(_cumsum_kernel, tc=tc),
        grid=(b, t // tc),
        in_specs=[spec],
        out_specs=[spec] * 2,
        out_shape=[jax.ShapeDtypeStruct(x.shape, F32), jax.ShapeDtypeStruct(x.shape, BF16)],
        scratch_shapes=[pltpu.VMEM((SUBLANES, LANES), F32)],
        compiler_params=_params("parallel", "arbitrary"),
        name="cumsum",
    )(x)


FFN_CHUNK = 256
CONV_ROWS = SUBLANES


def _ffn_up_kernel(x_ref, w_ref, c_ref, p_ref, act_ref, s_ref, halo, *, tm):
    @pl.when(pl.program_id(1) == 0)
    def _():
        halo[...] = p_ref[...]

    x = x_ref[...]
    n_chunks = D_FF // FFN_CHUNK
    groups = tm // CONV_ROWS
    row = lax.broadcasted_iota(jnp.int32, (groups, CONV_ROWS, FFN_CHUNK), 1)

    def up(c, half):
        col = half * D_FF + c * FFN_CHUNK
        return jnp.dot(x, w_ref[:, col:col + FFN_CHUNK], preferred_element_type=F32)

    def conv(h, c, half):
        cols = slice(half * D_FF + c * FFN_CHUNK, half * D_FF + (c + 1) * FFN_CHUNK)
        ext = jnp.concatenate([halo[:, cols], h], axis=0).reshape(groups + 1, CONV_ROWS, FFN_CHUNK)
        hc = c_ref[CONV_W:CONV_W + 1, cols] + c_ref[CONV_W - 1:CONV_W, cols] * h
        for s in range(1, CONV_W):
            rot = pltpu.roll(ext, s, axis=1)
            shifted = jnp.where(row < s, rot[:groups], rot[1:]).reshape(tm, FFN_CHUNK)
            hc = hc + c_ref[CONV_W - 1 - s:CONV_W - s, cols] * shifted
        tail = h[tm - CONV_ROWS:tm]
        halo[:, cols] = tail
        s_ref[:, cols] = tail
        return hc

    pending = [(up(0, 0), up(0, 1))]
    for c in range(n_chunks):
        if c + 1 < n_chunks:
            pending.append((up(c + 1, 0), up(c + 1, 1)))
        ha, hg = pending.pop(0)
        a = conv(ha, c, 0)
        g = conv(hg, c, 1)
        act_ref[:, c * FFN_CHUNK:(c + 1) * FFN_CHUNK] = (g * jax.nn.sigmoid(g) * a).astype(BF16)


def _ffn_up(xb, w_up, conv_tab, past, tm=256):
    b, t, _ = xb.shape
    tm = _row_tile(t, tm)
    fixed = lambda bi, ti: (0, 0)
    return pl.pallas_call(
        functools.partial(_ffn_up_kernel, tm=tm),
        grid=(b, t // tm),
        in_specs=[pl.BlockSpec((None, tm, D_MODEL), lambda bi, ti: (bi, ti, 0)),
                  pl.BlockSpec((D_MODEL, 2 * D_FF), fixed),
                  pl.BlockSpec((CONV_ROWS, 2 * D_FF), fixed),
                  pl.BlockSpec((None, CONV_ROWS, 2 * D_FF), lambda bi, ti: (bi, 0, 0))],
        out_specs=[pl.BlockSpec((None, tm, D_FF), lambda bi, ti: (bi, ti, 0)),
                   pl.BlockSpec((None, CONV_ROWS, 2 * D_FF), lambda bi, ti: (bi, 0, 0))],
        out_shape=[jax.ShapeDtypeStruct((b, t, D_FF), BF16),
                   jax.ShapeDtypeStruct((b, CONV_ROWS, 2 * D_FF), F32)],
        scratch_shapes=[pltpu.VMEM((CONV_ROWS, 2 * D_FF), F32)],
        compiler_params=_params("parallel", "arbitrary"),
        name="ffn_up",
    )(xb, w_up, conv_tab, past)


def _rope_tables(pos, batch, q_scale):
    half = A_ROPE // 2
    inv_freq = ROPE_THETA ** (-jnp.arange(half, dtype=F32) / half)
    ang = pos.astype(F32)[:, None] * inv_freq
    cos, sin = jnp.cos(ang), jnp.sin(ang)
    cosk = jnp.concatenate([cos, cos], axis=-1)
    sink = jnp.concatenate([sin, sin], axis=-1)
    t = pos.shape[0]
    pad = jnp.zeros((t, A_QK_PAD - A_NOPE - A_ROPE), F32)
    cosq = q_scale * jnp.concatenate([jnp.ones((t, A_NOPE), F32), cosk, pad], axis=-1)
    sinq = q_scale * jnp.concatenate([jnp.zeros((t, A_NOPE), F32), sink, pad], axis=-1)
    return tuple(jnp.tile(a, (batch, 1)) for a in (cosq, sinq, cosk, sink))


def _swap_halves(w):
    half = w.shape[-1] // 2
    return jnp.concatenate([-w[..., half:], w[..., :half]], axis=-1)


def _mla_weights(w_dq, w_dkv, w_kr, w_uq, w_uk, w_uv):
    zc = jnp.zeros((D_MODEL, LANES - A_ROPE), F32)
    w1 = jnp.concatenate([w_dq, w_dkv, w_kr, zc, _swap_halves(w_kr), zc], axis=1).astype(BF16)
    wq = w_uq.reshape(A_Q_LORA, A_HEADS, A_NOPE + A_ROPE)
    nope, rope = wq[..., :A_NOPE], wq[..., A_NOPE:]
    zpad = jnp.zeros((A_Q_LORA, A_HEADS, A_QK_PAD - A_NOPE - A_ROPE), F32)
    w_cat = jnp.concatenate([nope, rope, zpad], axis=-1).reshape(A_Q_LORA, -1)
    w_sw = jnp.concatenate([jnp.zeros_like(nope), _swap_halves(rope), zpad], axis=-1).reshape(A_Q_LORA, -1)
    wq2 = jnp.concatenate([w_cat, w_sw], axis=1).astype(BF16)
    wk = jnp.zeros((A_LAT_PAD, A_HEADS, A_QK_PAD), F32)
    wk = wk.at[:A_KV_LORA, :, :A_NOPE].set(w_uk)
    eye = jnp.broadcast_to(jnp.eye(A_ROPE, dtype=F32)[:, None, :], (A_ROPE, A_HEADS, A_ROPE))
    wk = wk.at[A_KV_LORA:A_KV_LORA + A_ROPE, :, A_NOPE:A_NOPE + A_ROPE].set(eye)
    wv = jnp.zeros((A_LAT_PAD, A_HEADS * A_V), F32).at[:A_KV_LORA].set(w_uv.reshape(A_KV_LORA, -1))
    wkv = jnp.concatenate([wk.reshape(A_LAT_PAD, -1), wv], axis=1).astype(BF16)
    return w1, wq2, wkv


def _mla_mixer(xb, b, t, pos, ckv_past, kpe_past, w, i):
    w1, wq2, wkv = _mla_weights(w['a_w_dq'][i], w['a_w_dkv'][i], w['a_w_kr'][i], w['a_w_uq'][i],
                                w['a_w_uk'][i], w['a_w_uv'][i])
    prompt = ckv_past is None
    assert not prompt or t % FLASH_T == 0
    q_scale = (A_NOPE + A_ROPE) ** -0.5 * (LOG2E if prompt else 1.0)
    q, lat, ckv, kpe = _mla_proj(xb, w1, w['a_g_q'][i], w['a_g_kv'][i], wq2,
                                 *_rope_tables(pos, b, q_scale))
    lat = lat.reshape(b, t, A_LAT_PAD)
    n_k = A_HEADS * A_QK_PAD
    if prompt:
        lat2 = lat.reshape(b * t, A_LAT_PAD)
        (k_cat,) = _mm(lat2, wkv[:, :n_k], [BF16])
        vt = _proj_t(lat2, wkv[:, n_k:], FLASH_T, A_V).reshape(b, t // FLASH_T, -1, FLASH_T)
        o = _flash_t(q.reshape(b, t, -1), k_cat.reshape(b, t, n_k), 0, vt, nh=A_HEADS_PER_STEP,
                     dq=A_QK_PAD, dv=A_V, n_hblk=A_HEADS // A_HEADS_PER_STEP, chunk_causal=True)
    else:
        p_len = ckv_past.shape[1]
        past = jnp.concatenate(
            [ckv_past, kpe_past, jnp.zeros((b, p_len, A_LAT_PAD - A_KV_LORA - A_ROPE), F32)], axis=-1)
        lat = jnp.concatenate([past.astype(BF16), lat], axis=1)
        q_pos = p_len + np.arange(t)
        k_pos = np.arange(p_len + t)
        visible = (k_pos[None, :] // CHUNK) <= (q_pos[:, None] // CHUNK)
        bias = jnp.asarray(np.where(visible, 0.0, NEG_INF).astype(np.float32)[None])
        o = _mla_step(q.reshape(b, t, -1), lat, w['a_w_uk'][i], w['a_w_uv'][i], bias)
    return o.reshape(b * t, A_HEADS * A_V), ckv.reshape(b, t, -1), kpe.reshape(b, t, -1)


def _band_mixer(xb, b, t, pos0, k_past, v_past, w, i):
    prompt = k_past is None
    assert not prompt or t % BAND_T == 0
    w_qkv = w['b_w_qkv'][i]
    (q,) = _mm(xb, (w_qkv[:, :D_MODEL] * (B_HEAD_DIM ** -0.5 * (LOG2E if prompt else 1.0))).astype(BF16),
               [BF16])
    q = q.reshape(b, t, D_MODEL)
    w_kv = w_qkv[:, D_MODEL:].astype(BF16)
    heads = lambda a, rows: a.reshape(b, rows, B_HEADS, B_HEAD_DIM)
    if prompt:
        nkb = B_WIN // BAND_T + 1
        bias = _band_bias(w['b_rel_bias'][i], B_WIN + np.arange(BAND_T), np.arange(B_WIN + BAND_T))
        (kvb,) = _mm(xb, w_kv, [BF16])
        kvb = kvb.reshape(b, t, 2 * D_MODEL)
        vt = _proj_t(xb, w_qkv[:, 2 * D_MODEL:], BAND_T).reshape(b, t // BAND_T, D_MODEL, BAND_T)
        o = _band_t(q, kvb, vt, LOG2E * bias.transpose(0, 2, 1), nkb=nkb)
        keep = min(B_WIN, t)
        x_tail = xb.reshape(b, t, D_MODEL)[:, t - keep:].reshape(b * keep, D_MODEL)
        (kv_tail,) = _mm(x_tail, w_kv, [F32])
        k_new, v_new = heads(kv_tail[:, :D_MODEL], keep), heads(kv_tail[:, D_MODEL:], keep)
    else:
        kv32, kvb = _mm(xb, w_kv, [F32, BF16])
        k32, v32 = heads(kv32[:, :D_MODEL], t), heads(kv32[:, D_MODEL:], t)
        p_len = k_past.shape[1]
        kvb = kvb.reshape(b, t, 2 * D_MODEL)
        q_pos = pos0 + np.arange(t)
        k_pos = np.concatenate([np.arange(pos0 - p_len, pos0), q_pos])
        bias = _band_bias(w['b_rel_bias'][i], q_pos, k_pos)
        o = _step_attn(q, k_past.reshape(b, p_len, D_MODEL), v_past.reshape(b, p_len, D_MODEL),
                       kvb[:, :, :D_MODEL], kvb[:, :, D_MODEL:], bias, B_HEADS)
        k_new, v_new = k32, v32
    return o.reshape(b * t, D_MODEL), k_new, v_new


C_AUG = LANES


def _aug_kernel(x_ref, w_ref, f_ref, p_ref, b_ref, o_ref):
    acc = (jnp.dot(x_ref[...], w_ref[...], preferred_element_type=F32) + b_ref[...]
           + jnp.dot(f_ref[...], p_ref[...], preferred_element_type=F32))
    o_ref[...] = acc.astype(BF16)


def _aug_tables():
    n = C_HEADS * C_AUG
    place = np.zeros((LANES, 2 * n), np.float32)
    ones = np.zeros((1, 2 * n), np.float32)
    for h in range(C_HEADS):
        base = h * C_AUG + C_HEAD_DIM
        for piece in range(N_PIECES):
            ones[0, base + piece] = 1.0
            place[piece * PIECE_LANES + h, base + N_PIECES + piece] = 1.0
            place[piece * PIECE_LANES + h, n + base + piece] = -1.0
            ones[0, n + base + N_PIECES + piece] = 1.0
    return jnp.asarray(place, BF16), jnp.asarray(ones)


def _aug_qk(xb, w_aug, pieces, tm=512, tn=1024):
    m = xb.shape[0]
    n = w_aug.shape[1]
    tm = _row_tile(m, tm)
    place, ones = _aug_tables()
    row = lambda i, j: (i, 0)
    return pl.pallas_call(
        _aug_kernel,
        grid=(m // tm, n // tn),
        in_specs=[pl.BlockSpec((tm, D_MODEL), row), pl.BlockSpec((D_MODEL, tn), lambda i, j: (0, j)),
                  pl.BlockSpec((tm, LANES), row), pl.BlockSpec((LANES, tn), lambda i, j: (0, j)),
                  pl.BlockSpec((1, tn), lambda i, j: (0, j))],
        out_specs=pl.BlockSpec((tm, tn), lambda i, j: (i, j)),
        out_shape=jax.ShapeDtypeStruct((m, n), BF16),
        compiler_params=_params("parallel", "parallel"),
        name="aug_qk",
    )(xb, w_aug, pieces, place, ones)


def _pad_heads(w, scale):
    w = (w * scale).reshape(D_MODEL, C_HEADS, C_HEAD_DIM)
    return jnp.pad(w, ((0, 0), (0, 0), (0, C_AUG - C_HEAD_DIM))).reshape(D_MODEL, C_HEADS * C_AUG)


def _fox_mixer(xb, b, t, k_past, v_past, lf_past, w, i):
    resident = k_past is None
    assert not resident or t % FLASH_T == 0
    w_qkv = w['c_w_qkv'][i]
    k32, *kb = _mm(xb, w_qkv[:, D_MODEL:2 * D_MODEL].astype(BF16), [F32] if resident else [F32, BF16])
    v32, *vb = _mm(xb, w_qkv[:, 2 * D_MODEL:].astype(BF16), [F32] if resident else [F32, BF16])
    k32 = k32.reshape(b, t, C_HEADS, C_HEAD_DIM)
    v32 = v32.reshape(b, t, C_HEADS, C_HEAD_DIM)
    w_f = jnp.zeros((D_MODEL, LANES), F32).at[:, :C_HEADS].set(w['c_w_f'][i]).astype(BF16)
    b_f = jnp.zeros((1, LANES), F32).at[0, :C_HEADS].set(w['c_b_f'][i])
    log_f = _logf(xb, w_f, b_f).reshape(b, t, LANES)
    lf_all = log_f
    if not resident:
        lf_all = jnp.concatenate([jnp.pad(lf_past, ((0, 0), (0, 0), (0, LANES - C_HEADS))), log_f], axis=1)
    t_k = lf_all.shape[1]
    tc = CUMSUM_ROWS
    t_pad = -(-t_k // tc) * tc
    f_cum, f_packed = _cumsum(jnp.pad(lf_all, ((0, 0), (0, t_pad - t_k), (0, 0))), tc)
    nh = C_HEADS_PER_STEP
    n_hblk = C_HEADS // nh
    q_scale = C_HEAD_DIM ** -0.5
    if resident:
        w_aug = jnp.concatenate([_pad_heads(w_qkv[:, :D_MODEL], q_scale * LOG2E),
                                 _pad_heads(w_qkv[:, D_MODEL:2 * D_MODEL], 1.0)], axis=1).astype(BF16)
        qk_aug = _aug_qk(xb, w_aug, f_packed.reshape(b * t, LANES)).reshape(b, t, -1)
        n_q = C_HEADS * C_AUG
        vt = _proj_t(xb, w_qkv[:, 2 * D_MODEL:], FLASH_T, C_HEAD_DIM).reshape(b, t // FLASH_T, -1, FLASH_T)
        o = _flash_t(qk_aug, qk_aug, n_q // (nh * C_AUG), vt,
                     nh=nh, dq=C_AUG, dv=C_HEAD_DIM, n_hblk=n_hblk, chunk_causal=False)
        return o.reshape(b * t, D_MODEL), k32, v32, log_f[:, :, :C_HEADS]
    (q,) = _mm(xb, (w_qkv[:, :D_MODEL] * q_scale).astype(BF16), [BF16])
    p_len = k_past.shape[1]
    causal = np.where(np.tril(np.ones((t, t), bool)), 0.0, NEG_INF).astype(np.float32)
    bias = jnp.asarray(np.concatenate([np.zeros((t, p_len), np.float32), causal], axis=1)[None])
    o = _step_attn(q.reshape(b, t, D_MODEL), k_past.reshape(b, p_len, D_MODEL),
                   v_past.reshape(b, p_len, D_MODEL), kb[0].reshape(b, t, D_MODEL),
                   vb[0].reshape(b, t, D_MODEL), bias, C_HEADS, f_cum=f_cum[:, :t_k, :C_HEADS])
    return o.reshape(b * t, D_MODEL), k32, v32, log_f[:, :, :C_HEADS]


def _conv_ffn(xb, b, t, conv_past, w, i):
    tab = jnp.concatenate([w['f_conv_w'][i], w['f_conv_b'][i][None],
                           jnp.zeros((CONV_ROWS - CONV_W - 1, 2 * D_FF), F32)], axis=0)
    if conv_past is None:
        past = jnp.zeros((b, CONV_ROWS, 2 * D_FF), F32)
    else:
        past = jnp.pad(conv_past, ((0, 0), (CONV_ROWS - (CONV_W - 1), 0), (0, 0)))
    act, tail = _ffn_up(xb.reshape(b, t, D_MODEL), w['f_w_up_bf16'][i], tab, past)
    return act.reshape(b * t, D_FF), tail[:, CONV_ROWS - (CONV_W - 1):]


def _trunk(x, pos0, past, w):
    b, t, _ = x.shape
    pos = pos0 + jnp.arange(t)
    xf = x.reshape(b * t, D_MODEL)
    xb = xf.astype(BF16)
    outs = {n: [] for n in ('a_ckv', 'a_kpe', 'b_k', 'b_v', 'c_k', 'c_v', 'c_logf', 'ffn_conv')}
    ia = ib = ic = 0
    get = lambda name, j: None if past is None else past[name][j]
    for i in range(DEPTH):
        kind = i % N_MIXERS
        if kind == 0:
            o, ckv, kpe = _mla_mixer(xb, b, t, pos, get('a_ckv', ia), get('a_kpe', ia), w, ia)
            outs['a_ckv'].append(ckv)
            outs['a_kpe'].append(kpe)
            w_o = w['a_w_o_bf16'][ia]
            ia += 1
        elif kind == 1:
            o, kb, vb = _band_mixer(xb, b, t, pos0, get('b_k', ib), get('b_v', ib), w, ib)
            outs['b_k'].append(kb)
            outs['b_v'].append(vb)
            w_o = w['b_w_o_bf16'][ib]
            ib += 1
        else:
            o, kc, vc, lf = _fox_mixer(xb, b, t, get('c_k', ic), get('c_v', ic), get('c_logf', ic), w, ic)
            outs['c_k'].append(kc)
            outs['c_v'].append(vc)
            outs['c_logf'].append(lf)
            w_o = w['c_w_o_bf16'][ic]
            ic += 1
        xf, xb = _mm_res_ln(o, w_o, xf, w['ln1_g'][i], w['ln1_b'][i])
        act, conv_state = _conv_ffn(xb, b, t, get('ffn_conv', i), w, i)
        outs['ffn_conv'].append(conv_state)
        xf, xb = _mm_res_ln(act, w['f_w_down_bf16'][i], xf, w['ln2_g'][i], w['ln2_b'][i])
    return xf.reshape(b, t, D_MODEL), {n: jnp.stack(v) for n, v in outs.items()}


def kernel(x_prompt, x_sample, cache_a_ckv, cache_a_kpe, cache_b_k, cache_b_v, cache_c_k, cache_c_v,
           cache_c_logf, state_ffn_conv, a_w_dq, a_g_q, a_w_uq, a_w_dkv, a_g_kv, a_w_kr, a_w_uk, a_w_uv,
           a_w_o, b_w_qkv, b_rel_bias, b_w_o, c_w_qkv, c_w_f, c_b_f, c_w_o, f_w_up, f_conv_w, f_conv_b,
           f_w_down, ln1_g, ln1_b, ln2_g, ln2_b):
    w = dict(a_w_dq=a_w_dq, a_g_q=a_g_q, a_w_uq=a_w_uq, a_w_dkv=a_w_dkv, a_g_kv=a_g_kv, a_w_kr=a_w_kr,
             a_w_uk=a_w_uk, a_w_uv=a_w_uv, a_w_o=a_w_o, b_w_qkv=b_w_qkv, b_rel_bias=b_rel_bias, b_w_o=b_w_o,
             c_w_qkv=c_w_qkv, c_w_f=c_w_f, c_b_f=c_b_f, c_w_o=c_w_o, f_w_up=f_w_up, f_conv_w=f_conv_w,
             f_conv_b=f_conv_b, f_w_down=f_w_down, ln1_g=ln1_g, ln1_b=ln1_b, ln2_g=ln2_g, ln2_b=ln2_b)
    past = dict(a_ckv=cache_a_ckv, a_kpe=cache_a_kpe, b_k=cache_b_k, b_v=cache_b_v, c_k=cache_c_k,
                c_v=cache_c_v, c_logf=cache_c_logf, ffn_conv=state_ffn_conv)
    past_len = cache_a_ckv.shape[2]
    for name in ('f_w_up', 'f_w_down', 'a_w_o', 'b_w_o', 'c_w_o'):
        w[name + '_bf16'] = _to_bf16(w[name])
    y_prompt, p = _trunk(x_prompt, 0, None, w)
    y_sample, s = _trunk(x_sample, past_len, past, w)
    names = ('a_ckv', 'a_kpe', 'b_k', 'b_v', 'c_k', 'c_v', 'c_logf', 'ffn_conv')
    return (y_prompt, y_sample) + tuple(p[n] for n in names) + tuple(s[n] for n in names)
```
